```python
import math
import jax, jax.numpy as jnp
from jax import lax
import numpy as np

D_MODEL = 2048
BATCH = 4
SEQ = 2048
DEPTH = 1

POOL_WIDTH = D_MODEL // 2
POOL_WINDOWS = (2, 4, 8, 16)
POOL_GROUPS = len(POOL_WINDOWS)
POOL_GROUP_DIM = POOL_WIDTH // POOL_GROUPS
SB_HEAD_DIM = 128
SB_HEADS = (D_MODEL // 2) // SB_HEAD_DIM
SB_WIDTH = SB_HEADS * SB_HEAD_DIM
N_BRANCHES = 2
IN_WIDTH = POOL_WIDTH + 3 * SB_WIDTH + N_BRANCHES * D_MODEL
D_FF = 4 * D_MODEL
Q_BLOCK = 128
N_MOD = 6
EPS = 1e-6

kernel_name = "hybrid_pool_stickbreak_gated_block"


def rms_norm(x, w):
    xf = x.astype(jnp.float32)
    y = xf * lax.rsqrt(jnp.mean(jnp.square(xf), axis=-1, keepdims=True) + EPS)
    return (y * w.astype(jnp.float32)).astype(x.dtype)


def multiscale_pool(u, w_pool, pool_scale):
    B, S, _ = u.shape
    uf = u.astype(jnp.float32).reshape(B, S, POOL_GROUPS, POOL_GROUP_DIM)
    cs = jnp.cumsum(uf, axis=1)
    pos = jnp.arange(S, dtype=jnp.int32)
    outs = []
    for g, w in enumerate(POOL_WINDOWS):
        csg = cs[:, :, g]
        lag = jnp.pad(csg, ((0, 0), (w, 0), (0, 0)))[:, :S]
        count = jnp.minimum(pos + 1, w).astype(jnp.float32)[None, :, None]
        outs.append((csg - lag) / count - uf[:, :, g])
    pooled = jnp.stack(outs, axis=2)
    mixed = jnp.einsum('bsgc,gce->bsge', pooled, w_pool.astype(jnp.float32))
    y = mixed.reshape(B, S, POOL_WIDTH) * pool_scale.astype(jnp.float32)
    return y.astype(u.dtype)


def stick_breaking_attention(q, k, v):
    B, S, H, Dh = q.shape
    nb = S // Q_BLOCK
    scale = 1.0 / math.sqrt(Dh)
    kh = k.transpose(0, 2, 1, 3)
    vh = v.transpose(0, 2, 1, 3)
    qb = q.transpose(0, 2, 1, 3).reshape(B, H, nb, Q_BLOCK, Dh).transpose(2, 0, 1, 3, 4)
    starts = jnp.arange(nb, dtype=jnp.int32) * Q_BLOCK
    key_pos = jnp.arange(S, dtype=jnp.int32)

    def block(args):
        q_blk, t0 = args
        z = jnp.einsum('bhqd,bhkd->bhqk', q_blk, kh).astype(jnp.float32) * scale
        q_pos = t0 + jnp.arange(Q_BLOCK, dtype=jnp.int32)
        mask = key_pos[None, :] < q_pos[:, None]
        log_beta = jax.nn.log_sigmoid(z)
        log_1m_beta = log_beta - z
        l = jnp.where(mask, log_1m_beta, 0.0)
        suffix = lax.cumsum(l, axis=3, reverse=True) - l
        a = jnp.where(mask, jnp.exp(log_beta + suffix), 0.0)
        return jnp.einsum('bhqk,bhkd->bhqd', a.astype(vh.dtype), vh)

    out = lax.map(block, (qb, starts))
    return out.transpose(1, 0, 3, 2, 4).reshape(B, S, H * Dh)


def setup_inputs(seed: int = 0) -> dict:
    key = jax.random.key(seed)
    ks = jax.random.split(key, 20)
    f32 = jnp.float32
    L = DEPTH

    def nrm(k, shape, fan_in, gain=1.0):
        return jax.random.normal(k, shape, f32) * (gain * fan_in ** -0.5)

    return {
        "x": jax.random.normal(ks[0], (BATCH, SEQ, D_MODEL), f32),
        "c": jax.random.normal(ks[1], (BATCH, D_MODEL), f32),
        "w_ada": nrm(ks[2], (L, D_MODEL, N_MOD * D_MODEL), D_MODEL, 0.5),
        "b_ada": 0.02 * jax.random.normal(ks[3], (L, N_MOD * D_MODEL), f32),
        "norm1_w": 1.0 + 0.05 * jax.random.normal(ks[4], (L, D_MODEL), f32),
        "w_in": nrm(ks[5], (L, D_MODEL, IN_WIDTH), D_MODEL),
        "q_norm_w": 1.0 + 0.05 * jax.random.normal(ks[6], (L, SB_HEAD_DIM), f32),
        "k_norm_w": 1.0 + 0.05 * jax.random.normal(ks[7], (L, SB_HEAD_DIM), f32),
        "w_pool": nrm(ks[8], (L, POOL_GROUPS, POOL_GROUP_DIM, POOL_GROUP_DIM), POOL_GROUP_DIM),
        "pool_scale": 1.0 + 0.1 * jax.random.normal(ks[9], (L, POOL_WIDTH), f32),
        "w_a_up": nrm(ks[10], (L, POOL_WIDTH, D_MODEL), POOL_WIDTH),
        "w_b_up": nrm(ks[11], (L, SB_WIDTH, D_MODEL), SB_WIDTH),
        "w_o": nrm(ks[12], (L, D_MODEL, D_MODEL), D_MODEL),
        "norm2_w": 1.0 + 0.05 * jax.random.normal(ks[13], (L, D_MODEL), f32),
        "w_ff1": nrm(ks[14], (L, D_MODEL, D_FF), D_MODEL),
        "w_ff2": nrm(ks[15], (L, D_FF, D_MODEL), D_FF),
    }


def reference(x, c, w_ada, b_ada, norm1_w, w_in, q_norm_w, k_norm_w, w_pool, pool_scale,
              w_a_up, w_b_up, w_o, norm2_w, w_ff1, w_ff2):
    B, S, D = x.shape
    split_at = [POOL_WIDTH, POOL_WIDTH + SB_WIDTH, POOL_WIDTH + 2 * SB_WIDTH,
                POOL_WIDTH + 3 * SB_WIDTH, POOL_WIDTH + 3 * SB_WIDTH + D_MODEL]
    for l in range(DEPTH):
        mod = jax.nn.silu(c) @ w_ada[l] + b_ada[l]
        shift1, scale1, gate1, shift2, scale2, gate2 = jnp.split(mod, N_MOD, axis=-1)

        h = rms_norm(x, norm1_w[l]) * (1.0 + scale1[:, None]) + shift1[:, None]
        proj = h @ w_in[l]
        u_pool, q, k, v, g_a, g_b = jnp.split(proj, split_at, axis=-1)

        y_a = multiscale_pool(u_pool, w_pool[l], pool_scale[l]) @ w_a_up[l]

        q = rms_norm(q.reshape(B, S, SB_HEADS, SB_HEAD_DIM), q_norm_w[l])
        k = rms_norm(k.reshape(B, S, SB_HEADS, SB_HEAD_DIM), k_norm_w[l])
        v = v.reshape(B, S, SB_HEADS, SB_HEAD_DIM)
        y_b = stick_breaking_attention(q, k, v) @ w_b_up[l]

        merged = jax.nn.sigmoid(g_a) * y_a + jax.nn.sigmoid(g_b) * y_b
        x = x + gate1[:, None] * (merged @ w_o[l])

        h2 = rms_norm(x, norm2_w[l]) * (1.0 + scale2[:, None]) + shift2[:, None]
        f = jnp.square(jax.nn.relu(h2 @ w_ff1[l])) @ w_ff2[l]
        x = x + gate2[:, None] * f
    return x
```

```python
import functools
import math

import jax
import jax.numpy as jnp
from jax import lax
from jax.experimental import pallas as pl
from jax.experimental.pallas import tpu as pltpu

D_MODEL = 2048
SEQ = 2048
POOL_WIDTH = D_MODEL // 2
POOL_WINDOWS = (2, 4, 8, 16)
POOL_GROUP_DIM = POOL_WIDTH // len(POOL_WINDOWS)
HEAD_DIM = 128
SB_WIDTH = D_MODEL // 2
SB_HEADS = SB_WIDTH // HEAD_DIM
IN_WIDTH = POOL_WIDTH + 3 * SB_WIDTH + 2 * D_MODEL
D_FF = 4 * D_MODEL
N_MOD = 6
EPS = 1e-6

Q_OFF = POOL_WIDTH
K_OFF = Q_OFF + SB_WIDTH
V_OFF = K_OFF + SB_WIDTH
GA_OFF = V_OFF + SB_WIDTH
GB_OFF = GA_OFF + D_MODEL

LANES = 128
VMEM_LIMIT_BYTES = 56 * 1024 * 1024

BF16 = jnp.bfloat16
F32 = jnp.float32


def _params(*semantics):
    return pltpu.CompilerParams(dimension_semantics=semantics,
                                vmem_limit_bytes=VMEM_LIMIT_BYTES)


def _ada_kernel(c_ref, w_ref, b_ref, o_ref):
    c = c_ref[...]
    sc = (c * jax.nn.sigmoid(c)).astype(BF16)
    acc = jnp.dot(sc, w_ref[...].astype(BF16), preferred_element_type=F32)
    o_ref[...] = acc + b_ref[...]


def _ada(c, w_ada, b_ada):
    batch, d = c.shape
    n = w_ada.shape[1]
    tn = 1024
    rows = 8
    c_pad = jnp.zeros((rows, d), F32).at[:batch].set(c)
    out = pl.pallas_call(
        _ada_kernel,
        grid=(n // tn,),
        in_specs=[
            pl.BlockSpec((rows, d), lambda j: (0, 0)),
            pl.BlockSpec((d, tn), lambda j: (0, j)),
            pl.BlockSpec((1, tn), lambda j: (0, j)),
        ],
        out_specs=pl.BlockSpec((rows, tn), lambda j: (0, j)),
        out_shape=jax.ShapeDtypeStruct((rows, n), F32),
        compiler_params=_params("arbitrary"),
        name="ada",
    )(c_pad, w_ada, b_ada.reshape(1, n))
    return out[:batch].reshape(batch, N_MOD, d)


def _norm_modulate(x, norm_w, scale, shift):
    ms = jnp.mean(x * x, axis=-1, keepdims=True)
    y = x * lax.rsqrt(ms + EPS) * norm_w
    return y * (1.0 + scale) + shift


def _head_norm_store(acc, w_ref, o_ref):
    w = w_ref[...]
    for hh in range(acc.shape[1] // HEAD_DIM):
        blk = acc[:, hh * HEAD_DIM:(hh + 1) * HEAD_DIM]
        ms = jnp.mean(blk * blk, axis=-1, keepdims=True)
        o_ref[:, hh * HEAD_DIM:(hh + 1) * HEAD_DIM] = (
            blk * lax.rsqrt(ms + EPS) * w).astype(o_ref.dtype)


def _in_proj_kernel(x_ref, mod_ref, nw_ref, w_ref, qw_ref, kw_ref, o_ref, h_scr, *, tn):
    j = pl.program_id(1)

    @pl.when(j == 0)
    def _():
        h = _norm_modulate(x_ref[...], nw_ref[...], mod_ref[0, 1:2, :], mod_ref[0, 0:1, :])
        h_scr[...] = h.astype(BF16)

    acc = jnp.dot(h_scr[...], w_ref[...], preferred_element_type=F32)
    col = j * tn
    is_q = jnp.logical_and(col >= Q_OFF, col < K_OFF)
    is_k = jnp.logical_and(col >= K_OFF, col < V_OFF)
    is_gate = col >= GA_OFF
    is_plain = jnp.logical_or(col < Q_OFF, jnp.logical_and(col >= V_OFF, col < GA_OFF))

    @pl.when(is_plain)
    def _():
        o_ref[...] = acc.astype(o_ref.dtype)

    @pl.when(is_q)
    def _():
        _head_norm_store(acc, qw_ref, o_ref)

    @pl.when(is_k)
    def _():
        _head_norm_store(acc, kw_ref, o_ref)

    @pl.when(is_gate)
    def _():
        o_ref[...] = jax.nn.sigmoid(acc).astype(o_ref.dtype)


def _in_proj(x2, mod, norm_w, w_bf, q_norm_w, k_norm_w, *, tm, tn):
    m, d = x2.shape
    n = w_bf.shape[1]
    return pl.pallas_call(
        functools.partial(_in_proj_kernel, tn=tn),
        grid=(m // tm, n // tn),
        in_specs=[
            pl.BlockSpec((tm, d), lambda i, j: (i, 0)),
            pl.BlockSpec((1, N_MOD, d), lambda i, j: (i * tm // SEQ, 0, 0)),
            pl.BlockSpec((1, d), lambda i, j: (0, 0)),
            pl.BlockSpec((d, tn), lambda i, j: (0, j)),
            pl.BlockSpec((1, HEAD_DIM), lambda i, j: (0, 0)),
            pl.BlockSpec((1, HEAD_DIM), lambda i, j: (0, 0)),
        ],
        out_specs=pl.BlockSpec((tm, tn), lambda i, j: (i, j)),
        out_shape=jax.ShapeDtypeStruct((m, n), BF16),
        scratch_shapes=[pltpu.VMEM((tm, d), BF16)],
        compiler_params=_params("arbitrary", "arbitrary"),
        name="in_proj",
    )(x2, mod, norm_w.reshape(1, d), w_bf, q_norm_w.reshape(1, HEAD_DIM),
      k_norm_w.reshape(1, HEAD_DIM))


def _ffn1_kernel(x_ref, mod_ref, nw_ref, w_ref, o_ref, h_scr):
    @pl.when(pl.program_id(1) == 0)
    def _():
        h = _norm_modulate(x_ref[...], nw_ref[...], mod_ref[0, 4:5, :], mod_ref[0, 3:4, :])
        h_scr[...] = h.astype(BF16)

    acc = jnp.dot(h_scr[...], w_ref[...], preferred_element_type=F32)
    r = jnp.maximum(acc, 0.0)
    o_ref[...] = (r * r).astype(o_ref.dtype)


def _ffn1(x2, mod, norm_w, w_bf, *, tm, tn):
    m, d = x2.shape
    n = w_bf.shape[1]
    return pl.pallas_call(
        _ffn1_kernel,
        grid=(m // tm, n // tn),
        in_specs=[
            pl.BlockSpec((tm, d), lambda i, j: (i, 0)),
            pl.BlockSpec((1, N_MOD, d), lambda i, j: (i * tm // SEQ, 0, 0)),
            pl.BlockSpec((1, d), lambda i, j: (0, 0)),
            pl.BlockSpec((d, tn), lambda i, j: (0, j)),
        ],
        out_specs=pl.BlockSpec((tm, tn), lambda i, j: (i, j)),
        out_shape=jax.ShapeDtypeStruct((m, n), BF16),
        scratch_shapes=[pltpu.VMEM((tm, d), BF16)],
        compiler_params=_params("arbitrary", "arbitrary"),
        name="ffn1",
    )(x2, mod, norm_w.reshape(1, d), w_bf)


def _pool_kernel(u_ref, w_ref, s_ref, o_ref):
    g = pl.program_id(1)
    u = u_ref[...].astype(F32)
    pos = lax.broadcasted_iota(jnp.int32, u.shape, 0)

    def shifted(a, k):
        return jnp.where(pos >= k, pltpu.roll(a, k, 0), 0.0)

    sums = []
    s = u
    width = 1
    for w in POOL_WINDOWS:
        while width < w:
            s = s + shifted(s, width)
            width *= 2
        sums.append(s)
    win_sum = sums[-1]
    window = jnp.int32(POOL_WINDOWS[-1])
    for i in range(len(POOL_WINDOWS) - 2, -1, -1):
        win_sum = jnp.where(g == i, sums[i], win_sum)
        window = jnp.where(g == i, jnp.int32(POOL_WINDOWS[i]), window)
    count = jnp.minimum(pos + 1, window).astype(F32)
    pooled = win_sum / count - u
    mixed = jnp.dot(pooled.astype(BF16), w_ref[0], preferred_element_type=F32)
    o_ref[...] = (mixed * s_ref[...]).astype(o_ref.dtype)


def _pool(proj, w_pool_bf, pool_scale, batch):
    groups = len(POOL_WINDOWS)
    gd = POOL_GROUP_DIM
    return pl.pallas_call(
        _pool_kernel,
        grid=(batch, groups),
        in_specs=[
            pl.BlockSpec((SEQ, gd), lambda b, g: (b, g)),
            pl.BlockSpec((1, gd, gd), lambda b, g: (g, 0, 0)),
            pl.BlockSpec((1, gd), lambda b, g: (0, g)),
        ],
        out_specs=pl.BlockSpec((SEQ, gd), lambda b, g: (b, g)),
        out_shape=jax.ShapeDtypeStruct((batch * SEQ, POOL_WIDTH), BF16),
        compiler_params=_params("arbitrary", "arbitrary"),
        name="pool",
    )(proj, w_pool_bf, pool_scale.reshape(1, POOL_WIDTH))


def _attn_kernel(q_ref, k_ref, v_ref, o_ref, *, tb):
    qi = pl.program_id(2)
    q = q_ref[...]
    scale = 1.0 / math.sqrt(HEAD_DIM)
    reps = tb // LANES

    r = lax.broadcasted_iota(jnp.int32, (tb, tb + LANES), 0)
    c = lax.broadcasted_iota(jnp.int32, (tb, tb + LANES), 1)
    cum_op = jnp.where(jnp.logical_or(c >= tb, r > c), 1.0, 0.0).astype(BF16)
    tri_r = lax.broadcasted_iota(jnp.int32, (tb, tb), 0)
    tri_c = lax.broadcasted_iota(jnp.int32, (tb, tb), 1)
    causal = tri_c < tri_r

    def block(kb, carry, acc, diagonal):
        start = pl.multiple_of(kb * tb, tb)
        k = k_ref[pl.ds(start, tb), :]
        v = v_ref[pl.ds(start, tb), :]
        z = lax.dot_general(q, k, (((1,), (1,)), ((), ())),
                            preferred_element_type=F32) * scale
        log_beta = jnp.minimum(z, 0.0) - jnp.log1p(jnp.exp(-jnp.abs(z)))
        l = log_beta - z
        if diagonal:
            l = jnp.where(causal, l, 0.0)
        l_hi = l.astype(BF16)
        l_lo = (l - l_hi.astype(F32)).astype(BF16)
        cs = (jnp.dot(l_hi, cum_op, preferred_element_type=F32)
              + jnp.dot(l_lo, cum_op, preferred_element_type=F32))
        suffix = cs[:, :tb] + jnp.concatenate([carry] * reps, axis=1)
        a = jnp.exp(log_beta + suffix)
        if diagonal:
            a = jnp.where(causal, a, 0.0)
        acc = acc + jnp.dot(a.astype(BF16), v, preferred_element_type=F32)
        carry = carry + cs[:, tb:]
        return carry, acc

    carry = jnp.zeros((tb, LANES), F32)
    acc = jnp.zeros((tb, HEAD_DIM), F32)
    carry, acc = block(qi, carry, acc, True)

    def body(it, state):
        return block(qi - 1 - it, state[0], state[1], False)

    carry, acc = lax.fori_loop(0, qi, body, (carry, acc))
    o_ref[...] = acc.astype(o_ref.dtype)


def _attn(proj, batch, *, tb):
    nq = SEQ // tb
    qb, kb, vb = Q_OFF // HEAD_DIM, K_OFF // HEAD_DIM, V_OFF // HEAD_DIM
    return pl.pallas_call(
        functools.partial(_attn_kernel, tb=tb),
        grid=(batch, SB_HEADS, nq),
        in_specs=[
            pl.BlockSpec((tb, HEAD_DIM), lambda b, h, i: (b * nq + i, qb + h)),
            pl.BlockSpec((SEQ, HEAD_DIM), lambda b, h, i: (b, kb + h)),
            pl.BlockSpec((SEQ, HEAD_DIM), lambda b, h, i: (b, vb + h)),
        ],
        out_specs=pl.BlockSpec((tb, HEAD_DIM), lambda b, h, i: (b * nq + i, h)),
        out_shape=jax.ShapeDtypeStruct((batch * SEQ, SB_WIDTH), BF16),
        compiler_params=_params("arbitrary", "arbitrary", "arbitrary"),
        name="attn",
    )(proj, proj, proj)


def _merge_kernel(pa_ref, at_ref, wa_ref, wb_ref, sa_ref, sb_ref, o_ref):
    ya = jnp.dot(pa_ref[...], wa_ref[...], preferred_element_type=F32)
    yb = jnp.dot(at_ref[...], wb_ref[...], preferred_element_type=F32)
    merged = sa_ref[...].astype(F32) * ya + sb_ref[...].astype(F32) * yb
    o_ref[...] = merged.astype(o_ref.dtype)


def _merge(pa, at, wa_bf, wb_bf, proj, *, tm, tn):
    m, kdim = pa.shape
    n = wa_bf.shape[1]
    ga, gb = GA_OFF // tn, GB_OFF // tn
    return pl.pallas_call(
        _merge_kernel,
        grid=(m // tm, n // tn),
        in_specs=[
            pl.BlockSpec((tm, kdim), lambda i, j: (i, 0)),
            pl.BlockSpec((tm, kdim), lambda i, j: (i, 0)),
            pl.BlockSpec((kdim, tn), lambda i, j: (0, j)),
            pl.BlockSpec((kdim, tn), lambda i, j: (0, j)),
            pl.BlockSpec((tm, tn), lambda i, j: (i, ga + j)),
            pl.BlockSpec((tm, tn), lambda i, j: (i, gb + j)),
        ],
        out_specs=pl.BlockSpec((tm, tn), lambda i, j: (i, j)),
        out_shape=jax.ShapeDtypeStruct((m, n), BF16),
        compiler_params=_params("arbitrary", "arbitrary"),
        name="merge",
    )(pa, at, wa_bf, wb_bf, proj, proj)


def _resid_kernel(lhs_ref, w_ref, x_ref, mod_ref, o_ref, *, gate_row):
    acc = jnp.dot(lhs_ref[...], w_ref[...], preferred_element_type=F32)
    o_ref[...] = x_ref[...] + mod_ref[0, gate_row:gate_row + 1, :] * acc


def _resid(lhs, w_bf, x2, mod, *, gate_row, tm, tn, name):
    m, kdim = lhs.shape
    n = w_bf.shape[1]
    return pl.pallas_call(
        functools.partial(_resid_kernel, gate_row=gate_row),
        grid=(m // tm, n // tn),
        in_specs=[
            pl.BlockSpec((tm, kdim), lambda i, j: (i, 0)),
            pl.BlockSpec((kdim, tn), lambda i, j: (0, j)),
            pl.BlockSpec((tm, tn), lambda i, j: (i, j)),
            pl.BlockSpec((1, N_MOD, tn), lambda i, j: (i * tm // SEQ, 0, j)),
        ],
        out_specs=pl.BlockSpec((tm, tn), lambda i, j: (i, j)),
        out_shape=jax.ShapeDtypeStruct((m, n), F32),
        compiler_params=_params("arbitrary", "arbitrary"),
        name=name,
    )(lhs, w_bf, x2, mod)


def kernel(x, c, w_ada, b_ada, norm1_w, w_in, q_norm_w, k_norm_w, w_pool, pool_scale,
           w_a_up, w_b_up, w_o, norm2_w, w_ff1, w_ff2):
    batch, seq, d = x.shape
    assert (seq, d) == (SEQ, D_MODEL) and w_ada.shape[0] == 1
    x2 = x.reshape(batch * seq, d)

    mod = _ada(c, w_ada[0], b_ada[0])
    proj = _in_proj(x2, mod, norm1_w[0], w_in[0].astype(BF16), q_norm_w[0], k_norm_w[0],
                    tm=1024, tn=512)
    pa = _pool(proj, w_pool[0].astype(BF16), pool_scale[0], batch)
    at = _attn(proj, batch, tb=256)
    merged = _merge(pa, at, w_a_up[0].astype(BF16), w_b_up[0].astype(BF16), proj,
                    tm=1024, tn=512)
    x1 = _resid(merged, w_o[0].astype(BF16), x2, mod, gate_row=2, tm=1024, tn=512,
                name="resid_attn")
    a = _ffn1(x1, mod, norm2_w[0], w_ff1[0].astype(BF16), tm=1024, tn=512)
    out = _resid(a, w_ff2[0].astype(BF16), x1, mod, gate_row=5, tm=512, tn=512,
                 name="resid_ffn")
    return out.reshape(batch, seq, d)
```

```python
import functools
import math

import jax
import jax.numpy as jnp
from jax import lax
from jax.experimental import pallas as pl
from jax.experimental.pallas import tpu as pltpu

D_MODEL = 2048
SEQ = 2048
POOL_WIDTH = D_MODEL // 2
POOL_WINDOWS = (2, 4, 8, 16)
POOL_GROUP_DIM = POOL_WIDTH // len(POOL_WINDOWS)
HEAD_DIM = 128
SB_WIDTH = D_MODEL // 2
SB_HEADS = SB_WIDTH // HEAD_DIM
IN_WIDTH = POOL_WIDTH + 3 * SB_WIDTH + 2 * D_MODEL
D_FF = 4 * D_MODEL
N_MOD = 6
EPS = 1e-6

Q_OFF = POOL_WIDTH
K_OFF = Q_OFF + SB_WIDTH
V_OFF = K_OFF + SB_WIDTH
GA_OFF = V_OFF + SB_WIDTH
GB_OFF = GA_OFF + D_MODEL

LANES = 128
VMEM_LIMIT_BYTES = 56 * 1024 * 1024

BF16 = jnp.bfloat16
F32 = jnp.float32


def _params(*semantics):
    return pltpu.CompilerParams(dimension_semantics=semantics,
                                vmem_limit_bytes=VMEM_LIMIT_BYTES)


def _ada_kernel(c_ref, w_ref, b_ref, o_ref):
    c = c_ref[...]
    sc = (c * jax.nn.sigmoid(c)).astype(BF16)
    acc = jnp.dot(sc, w_ref[...].astype(BF16), preferred_element_type=F32)
    o_ref[...] = acc + b_ref[...]


def _ada(c, w_ada, b_ada):
    batch, d = c.shape
    n = w_ada.shape[1]
    tn = 1024
    rows = 8
    c_pad = jnp.zeros((rows, d), F32).at[:batch].set(c)
    out = pl.pallas_call(
        _ada_kernel,
        grid=(n // tn,),
        in_specs=[
            pl.BlockSpec((rows, d), lambda j: (0, 0)),
            pl.BlockSpec((d, tn), lambda j: (0, j)),
            pl.BlockSpec((1, tn), lambda j: (0, j)),
        ],
        out_specs=pl.BlockSpec((rows, tn), lambda j: (0, j)),
        out_shape=jax.ShapeDtypeStruct((rows, n), F32),
        compiler_params=_params("arbitrary"),
        name="ada",
    )(c_pad, w_ada, b_ada.reshape(1, n))
    return out[:batch].reshape(batch, N_MOD, d)


def _norm_modulate(x, norm_w, scale, shift):
    ms = jnp.mean(x * x, axis=-1, keepdims=True)
    y = x * lax.rsqrt(ms + EPS) * norm_w
    return y * (1.0 + scale) + shift


def _head_norm_store(acc, w_ref, o_ref):
    w = w_ref[...]
    for hh in range(acc.shape[1] // HEAD_DIM):
        blk = acc[:, hh * HEAD_DIM:(hh + 1) * HEAD_DIM]
        ms = jnp.mean(blk * blk, axis=-1, keepdims=True)
        o_ref[:, hh * HEAD_DIM:(hh + 1) * HEAD_DIM] = (
            blk * lax.rsqrt(ms + EPS) * w).astype(o_ref.dtype)


def _in_proj_kernel(x_ref, mod_ref, nw_ref, w_ref, qw_ref, kw_ref, o_ref, h_scr, *, tn):
    j = pl.program_id(1)

    @pl.when(j == 0)
    def _():
        h = _norm_modulate(x_ref[...], nw_ref[...], mod_ref[0, 1:2, :], mod_ref[0, 0:1, :])
        h_scr[...] = h.astype(BF16)

    acc = jnp.dot(h_scr[...], w_ref[...], preferred_element_type=F32)
    col = j * tn
    is_q = jnp.logical_and(col >= Q_OFF, col < K_OFF)
    is_k = jnp.logical_and(col >= K_OFF, col < V_OFF)
    is_gate = col >= GA_OFF
    is_plain = jnp.logical_or(col < Q_OFF, jnp.logical_and(col >= V_OFF, col < GA_OFF))

    @pl.when(is_plain)
    def _():
        o_ref[...] = acc.astype(o_ref.dtype)

    @pl.when(is_q)
    def _():
        _head_norm_store(acc, qw_ref, o_ref)

    @pl.when(is_k)
    def _():
        _head_norm_store(acc, kw_ref, o_ref)

    @pl.when(is_gate)
    def _():
        o_ref[...] = jax.nn.sigmoid(acc).astype(o_ref.dtype)


def _in_proj(x2, mod, norm_w, w_bf, q_norm_w, k_norm_w, *, tm, tn):
    m, d = x2.shape
    n = w_bf.shape[1]
    return pl.pallas_call(
        functools.partial(_in_proj_kernel, tn=tn),
        grid=(m // tm, n // tn),
        in_specs=[
            pl.BlockSpec((tm, d), lambda i, j: (i, 0)),
            pl.BlockSpec((1, N_MOD, d), lambda i, j: (i * tm // SEQ, 0, 0)),
            pl.BlockSpec((1, d), lambda i, j: (0, 0)),
            pl.BlockSpec((d, tn), lambda i, j: (0, j)),
            pl.BlockSpec((1, HEAD_DIM), lambda i, j: (0, 0)),
            pl.BlockSpec((1, HEAD_DIM), lambda i, j: (0, 0)),
        ],
        out_specs=pl.BlockSpec((tm, tn), lambda i, j: (i, j)),
        out_shape=jax.ShapeDtypeStruct((m, n), BF16),
        scratch_shapes=[pltpu.VMEM((tm, d), BF16)],
        compiler_params=_params("arbitrary", "arbitrary"),
        name="in_proj",
    )(x2, mod, norm_w.reshape(1, d), w_bf, q_norm_w.reshape(1, HEAD_DIM),
      k_norm_w.reshape(1, HEAD_DIM))


def _ffn1_kernel(x_ref, mod_ref, nw_ref, w_ref, o_ref, h_scr):
    @pl.when(pl.program_id(1) == 0)
    def _():
        h = _norm_modulate(x_ref[...], nw_ref[...], mod_ref[0, 4:5, :], mod_ref[0, 3:4, :])
        h_scr[...] = h.astype(BF16)

    acc = jnp.dot(h_scr[...], w_ref[...], preferred_element_type=F32)
    r = jnp.maximum(acc, 0.0)
    o_ref[...] = (r * r).astype(o_ref.dtype)


def _ffn1(x2, mod, norm_w, w_bf, *, tm, tn):
    m, d = x2.shape
    n = w_bf.shape[1]
    return pl.pallas_call(
        _ffn1_kernel,
        grid=(m // tm, n // tn),
        in_specs=[
            pl.BlockSpec((tm, d), lambda i, j: (i, 0)),
            pl.BlockSpec((1, N_MOD, d), lambda i, j: (i * tm // SEQ, 0, 0)),
            pl.BlockSpec((1, d), lambda i, j: (0, 0)),
            pl.BlockSpec((d, tn), lambda i, j: (0, j)),
        ],
        out_specs=pl.BlockSpec((tm, tn), lambda i, j: (i, j)),
        out_shape=jax.ShapeDtypeStruct((m, n), BF16),
        scratch_shapes=[pltpu.VMEM((tm, d), BF16)],
        compiler_params=_params("arbitrary", "arbitrary"),
        name="ffn1",
    )(x2, mod, norm_w.reshape(1, d), w_bf)


def _pool_kernel(u_ref, w_ref, s_ref, o_ref):
    g = pl.program_id(1)
    u = u_ref[...].astype(F32)
    pos = lax.broadcasted_iota(jnp.int32, u.shape, 0)

    def shifted(a, k):
        return jnp.where(pos >= k, pltpu.roll(a, k, 0), 0.0)

    sums = []
    s = u
    width = 1
    for w in POOL_WINDOWS:
        while width < w:
            s = s + shifted(s, width)
            width *= 2
        sums.append(s)
    win_sum = sums[-1]
    window = jnp.int32(POOL_WINDOWS[-1])
    for i in range(len(POOL_WINDOWS) - 2, -1, -1):
        win_sum = jnp.where(g == i, sums[i], win_sum)
        window = jnp.where(g == i, jnp.int32(POOL_WINDOWS[i]), window)
    count = jnp.minimum(pos + 1, window).astype(F32)
    pooled = win_sum / count - u
    mixed = jnp.dot(pooled.astype(BF16), w_ref[0], preferred_element_type=F32)
    o_ref[...] = (mixed * s_ref[...]).astype(o_ref.dtype)


def _pool(proj, w_pool_bf, pool_scale, batch):
    groups = len(POOL_WINDOWS)
    gd = POOL_GROUP_DIM
    return pl.pallas_call(
        _pool_kernel,
        grid=(batch, groups),
        in_specs=[
            pl.BlockSpec((SEQ, gd), lambda b, g: (b, g)),
            pl.BlockSpec((1, gd, gd), lambda b, g: (g, 0, 0)),
            pl.BlockSpec((1, gd), lambda b, g: (0, g)),
        ],
        out_specs=pl.BlockSpec((SEQ, gd), lambda b, g: (b, g)),
        out_shape=jax.ShapeDtypeStruct((batch * SEQ, POOL_WIDTH), BF16),
        compiler_params=_params("arbitrary", "arbitrary"),
        name="pool",
    )(proj, w_pool_bf, pool_scale.reshape(1, POOL_WIDTH))


EXP_UNDERFLOW = -104.0


def _attn_kernel(q_ref, k_ref, v_ref, o_ref, carry_ref, acc_ref, *, tb, hp):
    qi = pl.program_id(2)
    scale = 1.0 / math.sqrt(HEAD_DIM)
    nseg = tb // LANES

    r = lax.broadcasted_iota(jnp.int32, (2 * LANES, 2 * LANES), 0)
    c = lax.broadcasted_iota(jnp.int32, (2 * LANES, 2 * LANES), 1)
    cum_op = jnp.where(jnp.logical_or(c >= LANES, jnp.bitwise_and(r, LANES - 1) > c),
                       1.0, 0.0).astype(BF16)
    tri_r = lax.broadcasted_iota(jnp.int32, (tb, tb), 0)
    tri_c = lax.broadcasted_iota(jnp.int32, (tb, tb), 1)
    causal = tri_c < tri_r

    def block(kb, diagonal):
        start = pl.multiple_of(kb * tb, tb)
        top = None
        for p in range(hp):
            cols = slice(p * HEAD_DIM, (p + 1) * HEAD_DIM)
            k = k_ref[pl.ds(start, tb), cols]
            v = v_ref[pl.ds(start, tb), cols]
            z = lax.dot_general(q_ref[:, cols], k, (((1,), (1,)), ((), ())),
                                preferred_element_type=F32) * scale
            log_beta = jnp.minimum(z, 0.0) - jnp.log(1.0 + jnp.exp(-jnp.abs(z)))
            l = log_beta - z
            if diagonal:
                l = jnp.where(causal, l, 0.0)
            carry = carry_ref[p]
            a_parts = [None] * nseg
            for sg in range(nseg - 1, -1, -1):
                seg = slice(sg * LANES, (sg + 1) * LANES)
                l_hi = l[:, seg].astype(BF16)
                l_lo = (l[:, seg] - l_hi.astype(F32)).astype(BF16)
                cs = jnp.dot(jnp.concatenate([l_hi, l_lo], axis=1), cum_op,
                             preferred_element_type=F32)
                a_parts[sg] = jnp.exp(log_beta[:, seg] + (cs[:, :LANES] + carry))
                carry = carry + cs[:, LANES:]
            a = jnp.concatenate(a_parts, axis=1)
            if diagonal:
                a = jnp.where(causal, a, 0.0)
            acc_ref[p] += jnp.dot(a.astype(BF16), v, preferred_element_type=F32)
            carry_ref[p] = carry
            top = jnp.max(carry) if top is None else jnp.maximum(top, jnp.max(carry))
        return top

    carry_ref[...] = jnp.zeros_like(carry_ref)
    acc_ref[...] = jnp.zeros_like(acc_ref)
    top = block(qi, True)

    def cond(state):
        kb, top = state
        return jnp.logical_and(kb >= 0, top > EXP_UNDERFLOW)

    def body(state):
        kb, _ = state
        return kb - 1, block(kb, False)

    lax.while_loop(cond, body, (qi - 1, top))
    for p in range(hp):
        o_ref[:, p * HEAD_DIM:(p + 1) * HEAD_DIM] = acc_ref[p].astype(o_ref.dtype)


def _attn(proj, batch, *, tb, hp):
    nq = SEQ // tb
    width = hp * HEAD_DIM
    qb, kb, vb = Q_OFF // width, K_OFF // width, V_OFF // width
    return pl.pallas_call(
        functools.partial(_attn_kernel, tb=tb, hp=hp),
        grid=(batch, SB_HEADS // hp, nq),
        in_specs=[
            pl.BlockSpec((tb, width), lambda b, h, i: (b * nq + i, qb + h)),
            pl.BlockSpec((SEQ, width), lambda b, h, i: (b, kb + h)),
            pl.BlockSpec((SEQ, width), lambda b, h, i: (b, vb + h)),
        ],
        out_specs=pl.BlockSpec((tb, width), lambda b, h, i: (b * nq + i, h)),
        out_shape=jax.ShapeDtypeStruct((batch * SEQ, SB_WIDTH), BF16),
        scratch_shapes=[pltpu.VMEM((hp, tb, LANES), F32),
                        pltpu.VMEM((hp, tb, HEAD_DIM), F32)],
        compiler_params=_params("arbitrary", "arbitrary", "arbitrary"),
        name="attn",
    )(proj, proj, proj)


def _merge_kernel(pa_ref, at_ref, wa_ref, wb_ref, sa_ref, sb_ref, o_ref):
    ya = jnp.dot(pa_ref[...], wa_ref[...], preferred_element_type=F32)
    yb = jnp.dot(at_ref[...], wb_ref[...], preferred_element_type=F32)
    merged = sa_ref[...].astype(F32) * ya + sb_ref[...].astype(F32) * yb
    o_ref[...] = merged.astype(o_ref.dtype)


def _merge(pa, at, wa_bf, wb_bf, proj, *, tm, tn):
    m, kdim = pa.shape
    n = wa_bf.shape[1]
    ga, gb = GA_OFF // tn, GB_OFF // tn
    return pl.pallas_call(
        _merge_kernel,
        grid=(m // tm, n // tn),
        in_specs=[
            pl.BlockSpec((tm, kdim), lambda i, j: (i, 0)),
            pl.BlockSpec((tm, kdim), lambda i, j: (i, 0)),
            pl.BlockSpec((kdim, tn), lambda i, j: (0, j)),
            pl.BlockSpec((kdim, tn), lambda i, j: (0, j)),
            pl.BlockSpec((tm, tn), lambda i, j: (i, ga + j)),
            pl.BlockSpec((tm, tn), lambda i, j: (i, gb + j)),
        ],
        out_specs=pl.BlockSpec((tm, tn), lambda i, j: (i, j)),
        out_shape=jax.ShapeDtypeStruct((m, n), BF16),
        compiler_params=_params("arbitrary", "arbitrary"),
        name="merge",
    )(pa, at, wa_bf, wb_bf, proj, proj)


def _resid_kernel(lhs_ref, w_ref, x_ref, mod_ref, o_ref, *, gate_row):
    acc = jnp.dot(lhs_ref[...], w_ref[...], preferred_element_type=F32)
    o_ref[...] = x_ref[...] + mod_ref[0, gate_row:gate_row + 1, :] * acc


def _resid(lhs, w_bf, x2, mod, *, gate_row, tm, tn, name):
    m, kdim = lhs.shape
    n = w_bf.shape[1]
    return pl.pallas_call(
        functools.partial(_resid_kernel, gate_row=gate_row),
        grid=(m // tm, n // tn),
        in_specs=[
            pl.BlockSpec((tm, kdim), lambda i, j: (i, 0)),
            pl.BlockSpec((kdim, tn), lambda i, j: (0, j)),
            pl.BlockSpec((tm, tn), lambda i, j: (i, j)),
            pl.BlockSpec((1, N_MOD, tn), lambda i, j: (i * tm // SEQ, 0, j)),
        ],
        out_specs=pl.BlockSpec((tm, tn), lambda i, j: (i, j)),
        out_shape=jax.ShapeDtypeStruct((m, n), F32),
        compiler_params=_params("arbitrary", "arbitrary"),
        name=name,
    )(lhs, w_bf, x2, mod)


def kernel(x, c, w_ada, b_ada, norm1_w, w_in, q_norm_w, k_norm_w, w_pool, pool_scale,
           w_a_up, w_b_up, w_o, norm2_w, w_ff1, w_ff2):
    batch, seq, d = x.shape
    assert (seq, d) == (SEQ, D_MODEL) and w_ada.shape[0] == 1
    x2 = x.reshape(batch * seq, d)

    mod = _ada(c, w_ada[0], b_ada[0])
    proj = _in_proj(x2, mod, norm1_w[0], w_in[0].astype(BF16), q_norm_w[0], k_norm_w[0],
                    tm=1024, tn=512)
    pa = _pool(proj, w_pool[0].astype(BF16), pool_scale[0], batch)
    at = _attn(proj, batch, tb=256, hp=2)
    merged = _merge(pa, at, w_a_up[0].astype(BF16), w_b_up[0].astype(BF16), proj,
                    tm=1024, tn=512)
    x1 = _resid(merged, w_o[0].astype(BF16), x2, mod, gate_row=2, tm=1024, tn=512,
                name="resid_attn")
    a = _ffn1(x1, mod, norm2_w[0], w_ff1[0].astype(BF16), tm=1024, tn=512)
    out = _resid(a, w_ff2[0].astype(BF16), x1, mod, gate_row=5, tm=512, tn=512,
                 name="resid_ffn")
    return out.reshape(batch, seq, d)
```

```python
import functools
import math

import jax
import jax.numpy as jnp
from jax import lax
from jax.experimental import pallas as pl
from jax.experimental.pallas import tpu as pltpu

D_MODEL = 2048
SEQ = 2048
POOL_WIDTH = D_MODEL // 2
POOL_WINDOWS = (2, 4, 8, 16)
POOL_GROUP_DIM = POOL_WIDTH // len(POOL_WINDOWS)
HEAD_DIM = 128
SB_WIDTH = D_MODEL // 2
SB_HEADS = SB_WIDTH // HEAD_DIM
IN_WIDTH = POOL_WIDTH + 3 * SB_WIDTH + 2 * D_MODEL
D_FF = 4 * D_MODEL
N_MOD = 6
EPS = 1e-6

Q_OFF = POOL_WIDTH
K_OFF = Q_OFF + SB_WIDTH
V_OFF = K_OFF + SB_WIDTH
GA_OFF = V_OFF + SB_WIDTH
GB_OFF = GA_OFF + D_MODEL

LANES = 128
VMEM_LIMIT_BYTES = 56 * 1024 * 1024

BF16 = jnp.bfloat16
F32 = jnp.float32


def _params(*semantics):
    return pltpu.CompilerParams(dimension_semantics=semantics,
                                vmem_limit_bytes=VMEM_LIMIT_BYTES)


def _ada_kernel(c_ref, w_ref, b_ref, o_ref):
    c = c_ref[...]
    sc = (c * jax.nn.sigmoid(c)).astype(BF16)
    acc = jnp.dot(sc, w_ref[...].astype(BF16), preferred_element_type=F32)
    o_ref[...] = acc + b_ref[...]


def _ada(c, w_ada, b_ada):
    batch, d = c.shape
    n = w_ada.shape[1]
    tn = 1024
    rows = 8
    c_pad = jnp.zeros((rows, d), F32).at[:batch].set(c)
    out = pl.pallas_call(
        _ada_kernel,
        grid=(n // tn,),
        in_specs=[
            pl.BlockSpec((rows, d), lambda j: (0, 0)),
            pl.BlockSpec((d, tn), lambda j: (0, j)),
            pl.BlockSpec((1, tn), lambda j: (0, j)),
        ],
        out_specs=pl.BlockSpec((rows, tn), lambda j: (0, j)),
        out_shape=jax.ShapeDtypeStruct((rows, n), F32),
        compiler_params=_params("arbitrary"),
        name="ada",
    )(c_pad, w_ada, b_ada.reshape(1, n))
    return out[:batch].reshape(batch, N_MOD, d)


def _norm_modulate(x, norm_w, scale, shift):
    ms = jnp.mean(x * x, axis=-1, keepdims=True)
    y = x * lax.rsqrt(ms + EPS) * norm_w
    return y * (1.0 + scale) + shift


def _head_norm_store(acc, w_ref, o_ref):
    w = w_ref[...]
    for hh in range(acc.shape[1] // HEAD_DIM):
        blk = acc[:, hh * HEAD_DIM:(hh + 1) * HEAD_DIM]
        ms = jnp.mean(blk * blk, axis=-1, keepdims=True)
        o_ref[:, hh * HEAD_DIM:(hh + 1) * HEAD_DIM] = (
            blk * lax.rsqrt(ms + EPS) * w).astype(o_ref.dtype)


def _in_proj_kernel(x_ref, mod_ref, nw_ref, w_ref, qw_ref, kw_ref, o_ref, h_scr, *, tn):
    j = pl.program_id(1)

    @pl.when(j == 0)
    def _():
        h = _norm_modulate(x_ref[...], nw_ref[...], mod_ref[0, 1:2, :], mod_ref[0, 0:1, :])
        h_scr[...] = h.astype(BF16)

    acc = jnp.dot(h_scr[...], w_ref[...], preferred_element_type=F32)
    col = j * tn
    is_q = jnp.logical_and(col >= Q_OFF, col < K_OFF)
    is_k = jnp.logical_and(col >= K_OFF, col < V_OFF)
    is_gate = col >= GA_OFF
    is_plain = jnp.logical_or(col < Q_OFF, jnp.logical_and(col >= V_OFF, col < GA_OFF))

    @pl.when(is_plain)
    def _():
        o_ref[...] = acc.astype(o_ref.dtype)

    @pl.when(is_q)
    def _():
        _head_norm_store(acc, qw_ref, o_ref)

    @pl.when(is_k)
    def _():
        _head_norm_store(acc, kw_ref, o_ref)

    @pl.when(is_gate)
    def _():
        o_ref[...] = jax.nn.sigmoid(acc).astype(o_ref.dtype)


def _in_proj(x2, mod, norm_w, w_bf, q_norm_w, k_norm_w, *, tm, tn):
    m, d = x2.shape
    n = w_bf.shape[1]
    return pl.pallas_call(
        functools.partial(_in_proj_kernel, tn=tn),
        grid=(m // tm, n // tn),
        in_specs=[
            pl.BlockSpec((tm, d), lambda i, j: (i, 0)),
            pl.BlockSpec((1, N_MOD, d), lambda i, j: (i * tm // SEQ, 0, 0)),
            pl.BlockSpec((1, d), lambda i, j: (0, 0)),
            pl.BlockSpec((d, tn), lambda i, j: (0, j)),
            pl.BlockSpec((1, HEAD_DIM), lambda i, j: (0, 0)),
            pl.BlockSpec((1, HEAD_DIM), lambda i, j: (0, 0)),
        ],
        out_specs=pl.BlockSpec((tm, tn), lambda i, j: (i, j)),
        out_shape=jax.ShapeDtypeStruct((m, n), BF16),
        scratch_shapes=[pltpu.VMEM((tm, d), BF16)],
        compiler_params=_params("arbitrary", "arbitrary"),
        name="in_proj",
    )(x2, mod, norm_w.reshape(1, d), w_bf, q_norm_w.reshape(1, HEAD_DIM),
      k_norm_w.reshape(1, HEAD_DIM))


FFN_OUT_CHUNK = 512


def _ffn_kernel(h_ref, x1_hbm, mod_ref, w1_ref, w2_ref, o_ref, sem, *, tm):
    i = pl.program_id(0)
    c = pl.program_id(1)

    def x1_copy():
        return pltpu.make_async_copy(x1_hbm.at[pl.ds(i * tm, tm), :], o_ref, sem)

    @pl.when(c == 0)
    def _():
        x1_copy().start()

    a = jnp.dot(h_ref[...], w1_ref[...].astype(BF16), preferred_element_type=F32)
    r = jnp.maximum(a, 0.0)
    act = (r * r).astype(BF16)

    @pl.when(c == 0)
    def _():
        x1_copy().wait()

    d = o_ref.shape[1]
    for n0 in range(0, d, FFN_OUT_CHUNK):
        cols = slice(n0, n0 + FFN_OUT_CHUNK)
        y = jnp.dot(act, w2_ref[:, cols].astype(BF16), preferred_element_type=F32)
        o_ref[:, cols] += mod_ref[0, 5:6, cols] * y


def _ffn(h2, x1, mod, w1, w2, *, tm, tf):
    m, d = h2.shape
    f = w1.shape[1]
    return pl.pallas_call(
        functools.partial(_ffn_kernel, tm=tm),
        grid=(m // tm, f // tf),
        in_specs=[
            pl.BlockSpec((tm, d), lambda i, c: (i, 0)),
            pl.BlockSpec(memory_space=pl.ANY),
            pl.BlockSpec((1, N_MOD, d), lambda i, c: (i * tm // SEQ, 0, 0)),
            pl.BlockSpec((d, tf), lambda i, c: (0, c)),
            pl.BlockSpec((tf, d), lambda i, c: (c, 0)),
        ],
        out_specs=pl.BlockSpec((tm, d), lambda i, c: (i, 0)),
        out_shape=jax.ShapeDtypeStruct((m, d), F32),
        scratch_shapes=[pltpu.SemaphoreType.DMA(())],
        compiler_params=_params("arbitrary", "arbitrary"),
        name="ffn",
    )(h2, x1, mod, w1, w2)


def _pool_kernel(u_ref, w_ref, s_ref, o_ref):
    g = pl.program_id(1)
    u = u_ref[...].astype(F32)
    pos = lax.broadcasted_iota(jnp.int32, u.shape, 0)

    def shifted(a, k):
        return jnp.where(pos >= k, pltpu.roll(a, k, 0), 0.0)

    sums = []
    s = u
    width = 1
    for w in POOL_WINDOWS:
        while width < w:
            s = s + shifted(s, width)
            width *= 2
        sums.append(s)
    win_sum = sums[-1]
    window = jnp.int32(POOL_WINDOWS[-1])
    for i in range(len(POOL_WINDOWS) - 2, -1, -1):
        win_sum = jnp.where(g == i, sums[i], win_sum)
        window = jnp.where(g == i, jnp.int32(POOL_WINDOWS[i]), window)
    count = jnp.minimum(pos + 1, window).astype(F32)
    pooled = win_sum / count - u
    mixed = jnp.dot(pooled.astype(BF16), w_ref[0], preferred_element_type=F32)
    o_ref[...] = (mixed * s_ref[...]).astype(o_ref.dtype)


def _pool(proj, w_pool_bf, pool_scale, batch):
    groups = len(POOL_WINDOWS)
    gd = POOL_GROUP_DIM
    return pl.pallas_call(
        _pool_kernel,
        grid=(batch, groups),
        in_specs=[
            pl.BlockSpec((SEQ, gd), lambda b, g: (b, g)),
            pl.BlockSpec((1, gd, gd), lambda b, g: (g, 0, 0)),
            pl.BlockSpec((1, gd), lambda b, g: (0, g)),
        ],
        out_specs=pl.BlockSpec((SEQ, gd), lambda b, g: (b, g)),
        out_shape=jax.ShapeDtypeStruct((batch * SEQ, POOL_WIDTH), BF16),
        compiler_params=_params("arbitrary", "arbitrary"),
        name="pool",
    )(proj, w_pool_bf, pool_scale.reshape(1, POOL_WIDTH))


EXP_UNDERFLOW = -104.0


def _attn_kernel(q_ref, k_ref, v_ref, o_ref, carry_ref, acc_ref, *, tb, hp):
    qi = pl.program_id(2)
    scale = 1.0 / math.sqrt(HEAD_DIM)
    nseg = tb // LANES

    r = lax.broadcasted_iota(jnp.int32, (2 * LANES, 2 * LANES), 0)
    c = lax.broadcasted_iota(jnp.int32, (2 * LANES, 2 * LANES), 1)
    cum_op = jnp.where(jnp.logical_or(c >= LANES, jnp.bitwise_and(r, LANES - 1) > c),
                       1.0, 0.0).astype(BF16)
    tri_r = lax.broadcasted_iota(jnp.int32, (tb, tb), 0)
    tri_c = lax.broadcasted_iota(jnp.int32, (tb, tb), 1)
    causal = tri_c < tri_r

    def block(kb, diagonal):
        start = pl.multiple_of(kb * tb, tb)
        top = None
        for p in range(hp):
            cols = slice(p * HEAD_DIM, (p + 1) * HEAD_DIM)
            k = k_ref[pl.ds(start, tb), cols]
            v = v_ref[pl.ds(start, tb), cols]
            z = lax.dot_general(q_ref[:, cols], k, (((1,), (1,)), ((), ())),
                                preferred_element_type=F32) * scale
            log_beta = jnp.minimum(z, 0.0) - jnp.log(1.0 + jnp.exp(-jnp.abs(z)))
            l = log_beta - z
            if diagonal:
                l = jnp.where(causal, l, 0.0)
            carry = carry_ref[p]
            a_parts = [None] * nseg
            for sg in range(nseg - 1, -1, -1):
                seg = slice(sg * LANES, (sg + 1) * LANES)
                l_hi = l[:, seg].astype(BF16)
                l_lo = (l[:, seg] - l_hi.astype(F32)).astype(BF16)
                cs = jnp.dot(jnp.concatenate([l_hi, l_lo], axis=1), cum_op,
                             preferred_element_type=F32)
                a_parts[sg] = jnp.exp(log_beta[:, seg] + (cs[:, :LANES] + carry))
                carry = carry + cs[:, LANES:]
            a = jnp.concatenate(a_parts, axis=1)
            if diagonal:
                a = jnp.where(causal, a, 0.0)
            acc_ref[p] += jnp.dot(a.astype(BF16), v, preferred_element_type=F32)
            carry_ref[p] = carry
            top = jnp.max(carry) if top is None else jnp.maximum(top, jnp.max(carry))
        return top

    carry_ref[...] = jnp.zeros_like(carry_ref)
    acc_ref[...] = jnp.zeros_like(acc_ref)
    top = block(qi, True)

    def cond(state):
        kb, top = state
        return jnp.logical_and(kb >= 0, top > EXP_UNDERFLOW)

    def body(state):
        kb, _ = state
        return kb - 1, block(kb, False)

    lax.while_loop(cond, body, (qi - 1, top))
    for p in range(hp):
        o_ref[:, p * HEAD_DIM:(p + 1) * HEAD_DIM] = acc_ref[p].astype(o_ref.dtype)


def _attn(proj, batch, *, tb, hp):
    nq = SEQ // tb
    width = hp * HEAD_DIM
    qb, kb, vb = Q_OFF // width, K_OFF // width, V_OFF // width
    return pl.pallas_call(
        functools.partial(_attn_kernel, tb=tb, hp=hp),
        grid=(batch, SB_HEADS // hp, nq),
        in_specs=[
            pl.BlockSpec((tb, width), lambda b, h, i: (b * nq + i, qb + h)),
            pl.BlockSpec((SEQ, width), lambda b, h, i: (b, kb + h)),
            pl.BlockSpec((SEQ, width), lambda b, h, i: (b, vb + h)),
        ],
        out_specs=pl.BlockSpec((tb, width), lambda b, h, i: (b * nq + i, h)),
        out_shape=jax.ShapeDtypeStruct((batch * SEQ, SB_WIDTH), BF16),
        scratch_shapes=[pltpu.VMEM((hp, tb, LANES), F32),
                        pltpu.VMEM((hp, tb, HEAD_DIM), F32)],
        compiler_params=_params("arbitrary", "arbitrary", "arbitrary"),
        name="attn",
    )(proj, proj, proj)


MIX_CHUNK = 512


def _mix_kernel(pa_ref, at_ref, sa_ref, sb_ref, x_ref, mod_ref, nw_ref, wa_ref, wb_ref, wo_ref,
                x1_ref, h2_ref, merged_scr):
    d = x_ref.shape[1]
    pa = pa_ref[...]
    at = at_ref[...]
    for n0 in range(0, d, MIX_CHUNK):
        cols = slice(n0, n0 + MIX_CHUNK)
        ya = jnp.dot(pa, wa_ref[:, cols], preferred_element_type=F32)
        yb = jnp.dot(at, wb_ref[:, cols], preferred_element_type=F32)
        merged = sa_ref[:, cols].astype(F32) * ya + sb_ref[:, cols].astype(F32) * yb
        merged_scr[:, cols] = merged.astype(BF16)
    merged = merged_scr[...]
    for n0 in range(0, d, MIX_CHUNK):
        cols = slice(n0, n0 + MIX_CHUNK)
        o = jnp.dot(merged, wo_ref[:, cols], preferred_element_type=F32)
        x1_ref[:, cols] = x_ref[:, cols] + mod_ref[0, 2:3, cols] * o
    h2 = _norm_modulate(x1_ref[...], nw_ref[...], mod_ref[0, 4:5, :], mod_ref[0, 3:4, :])
    h2_ref[...] = h2.astype(BF16)


def _mix(pa, at, proj, x2, mod, norm_w, wa_bf, wb_bf, wo_bf, *, tm):
    m, d = x2.shape
    kdim = pa.shape[1]
    ga, gb = GA_OFF // d, GB_OFF // d
    resident = pl.Buffered(1)
    return pl.pallas_call(
        _mix_kernel,
        grid=(m // tm,),
        in_specs=[
            pl.BlockSpec((tm, kdim), lambda i: (i, 0)),
            pl.BlockSpec((tm, kdim), lambda i: (i, 0)),
            pl.BlockSpec((tm, d), lambda i: (i, ga)),
            pl.BlockSpec((tm, d), lambda i: (i, gb)),
            pl.BlockSpec((tm, d), lambda i: (i, 0)),
            pl.BlockSpec((1, N_MOD, d), lambda i: (i * tm // SEQ, 0, 0)),
            pl.BlockSpec((1, d), lambda i: (0, 0)),
            pl.BlockSpec((kdim, d), lambda i: (0, 0), pipeline_mode=resident),
            pl.BlockSpec((kdim, d), lambda i: (0, 0), pipeline_mode=resident),
            pl.BlockSpec((d, d), lambda i: (0, 0), pipeline_mode=resident),
        ],
        out_specs=[pl.BlockSpec((tm, d), lambda i: (i, 0)),
                   pl.BlockSpec((tm, d), lambda i: (i, 0))],
        out_shape=[jax.ShapeDtypeStruct((m, d), F32),
                   jax.ShapeDtypeStruct((m, d), BF16)],
        scratch_shapes=[pltpu.VMEM((tm, d), BF16)],
        compiler_params=_params("arbitrary"),
        name="mix",
    )(pa, at, proj, proj, x2, mod, norm_w.reshape(1, d), wa_bf, wb_bf, wo_bf)


def kernel(x, c, w_ada, b_ada, norm1_w, w_in, q_norm_w, k_norm_w, w_pool, pool_scale,
           w_a_up, w_b_up, w_o, norm2_w, w_ff1, w_ff2):
    batch, seq, d = x.shape
    assert (seq, d) == (SEQ, D_MODEL) and w_ada.shape[0] == 1
    x2 = x.reshape(batch * seq, d)

    mod = _ada(c, w_ada[0], b_ada[0])
    proj = _in_proj(x2, mod, norm1_w[0], w_in[0].astype(BF16), q_norm_w[0], k_norm_w[0],
                    tm=1024, tn=512)
    pa = _pool(proj, w_pool[0].astype(BF16), pool_scale[0], batch)
    at = _attn(proj, batch, tb=256, hp=2)
    x1, h2 = _mix(pa, at, proj, x2, mod, norm2_w[0], w_a_up[0].astype(BF16),
                  w_b_up[0].astype(BF16), w_o[0].astype(BF16), tm=256)
    out = _ffn(h2, x1, mod, w_ff1[0], w_ff2[0], tm=1024, tf=512)
    return out.reshape(batch, seq, d)
```

```python
import functools
import math

import jax
import jax.numpy as jnp
from jax import lax
from jax.experimental import pallas as pl
from jax.experimental.pallas import tpu as pltpu

D_MODEL = 2048
SEQ = 2048
POOL_WIDTH = D_MODEL // 2
POOL_WINDOWS = (2, 4, 8, 16)
POOL_GROUP_DIM = POOL_WIDTH // len(POOL_WINDOWS)
HEAD_DIM = 128
SB_WIDTH = D_MODEL // 2
SB_HEADS = SB_WIDTH // HEAD_DIM
IN_WIDTH = POOL_WIDTH + 3 * SB_WIDTH + 2 * D_MODEL
D_FF = 4 * D_MODEL
N_MOD = 6
EPS = 1e-6

Q_OFF = POOL_WIDTH
K_OFF = Q_OFF + SB_WIDTH
V_OFF = K_OFF + SB_WIDTH
GA_OFF = V_OFF + SB_WIDTH
GB_OFF = GA_OFF + D_MODEL

LANES = 128
VMEM_LIMIT_BYTES = 56 * 1024 * 1024

BF16 = jnp.bfloat16
F32 = jnp.float32


def _params(*semantics):
    return pltpu.CompilerParams(dimension_semantics=semantics,
                                vmem_limit_bytes=VMEM_LIMIT_BYTES)


def _ada_kernel(c_ref, w_ref, b_ref, o_ref):
    c = c_ref[...]
    sc = (c * jax.nn.sigmoid(c)).astype(BF16)
    acc = jnp.dot(sc, w_ref[...].astype(BF16), preferred_element_type=F32)
    o_ref[...] = acc + b_ref[...]


def _ada(c, w_ada, b_ada):
    batch, d = c.shape
    n = w_ada.shape[1]
    tn = 1024
    rows = 8
    c_pad = jnp.zeros((rows, d), F32).at[:batch].set(c)
    out = pl.pallas_call(
        _ada_kernel,
        grid=(n // tn,),
        in_specs=[
            pl.BlockSpec((rows, d), lambda j: (0, 0)),
            pl.BlockSpec((d, tn), lambda j: (0, j)),
            pl.BlockSpec((1, tn), lambda j: (0, j)),
        ],
        out_specs=pl.BlockSpec((rows, tn), lambda j: (0, j)),
        out_shape=jax.ShapeDtypeStruct((rows, n), F32),
        compiler_params=_params("arbitrary"),
        name="ada",
    )(c_pad, w_ada, b_ada.reshape(1, n))
    return out[:batch].reshape(batch, N_MOD, d)


def _norm_modulate(x, norm_w, scale, shift):
    ms = jnp.mean(x * x, axis=-1, keepdims=True)
    y = x * lax.rsqrt(ms + EPS) * norm_w
    return y * (1.0 + scale) + shift


NORM_ROWS = 64


def _head_norm(blk, w_ref):
    ms = jnp.mean(blk * blk, axis=-1, keepdims=True)
    return (blk * lax.rsqrt(ms + EPS) * w_ref[...]).astype(BF16)


def _in_proj_kernel(x_ref, mod_ref, nw_ref, wu_ref, wq_ref, wk_ref, wv_ref, wga_ref, wgb_ref,
                    qw_ref, kw_ref, u_ref, q_ref, k_ref, v_ref, ga_ref, gb_ref, h_scr):
    @pl.when(pl.program_id(1) == 0)
    def _():
        def body(r, carry):
            rows = pl.ds(pl.multiple_of(r * NORM_ROWS, NORM_ROWS), NORM_ROWS)
            h = _norm_modulate(x_ref[rows, :], nw_ref[...], mod_ref[0, 1:2, :], mod_ref[0, 0:1, :])
            h_scr[rows, :] = h.astype(BF16)
            return carry
        lax.fori_loop(0, x_ref.shape[0] // NORM_ROWS, body, 0)

    h = h_scr[...]

    def project(*w_refs):
        w = jnp.concatenate([w_ref[...].astype(BF16) for w_ref in w_refs], axis=1)
        return jnp.dot(h, w, preferred_element_type=F32)

    qk = project(wq_ref, wk_ref)
    q_ref[...] = _head_norm(qk[:, :HEAD_DIM], qw_ref)
    k_ref[...] = _head_norm(qk[:, HEAD_DIM:], kw_ref)
    ga_ref[...] = jax.nn.sigmoid(project(wga_ref)).astype(BF16)
    gb_ref[...] = jax.nn.sigmoid(project(wgb_ref)).astype(BF16)
    uv = project(wu_ref, wv_ref)
    u_ref[...] = uv[:, :HEAD_DIM]
    v_ref[...] = uv[:, HEAD_DIM:].astype(BF16)


def _in_proj(x2, mod, norm_w, w_in, q_norm_w, k_norm_w, *, tm):
    m, d = x2.shape
    hd = HEAD_DIM
    gw = D_MODEL // SB_HEADS
    w_spec = lambda width, off: pl.BlockSpec((d, width), lambda i, j: (0, off // width + j))
    out_spec = lambda width: pl.BlockSpec((tm, width), lambda i, j: (i, j))
    return pl.pallas_call(
        _in_proj_kernel,
        grid=(m // tm, SB_HEADS),
        in_specs=[
            pl.BlockSpec((tm, d), lambda i, j: (i, 0)),
            pl.BlockSpec((1, N_MOD, d), lambda i, j: (i * tm // SEQ, 0, 0)),
            pl.BlockSpec((1, d), lambda i, j: (0, 0)),
            w_spec(hd, 0), w_spec(hd, Q_OFF), w_spec(hd, K_OFF), w_spec(hd, V_OFF),
            w_spec(gw, GA_OFF), w_spec(gw, GB_OFF),
            pl.BlockSpec((1, hd), lambda i, j: (0, 0)),
            pl.BlockSpec((1, hd), lambda i, j: (0, 0)),
        ],
        out_specs=[out_spec(hd), out_spec(hd), out_spec(hd), out_spec(hd),
                   out_spec(gw), out_spec(gw)],
        out_shape=[jax.ShapeDtypeStruct((m, POOL_WIDTH), F32),
                   jax.ShapeDtypeStruct((m, SB_WIDTH), BF16),
                   jax.ShapeDtypeStruct((m, SB_WIDTH), BF16),
                   jax.ShapeDtypeStruct((m, SB_WIDTH), BF16),
                   jax.ShapeDtypeStruct((m, D_MODEL), BF16),
                   jax.ShapeDtypeStruct((m, D_MODEL), BF16)],
        scratch_shapes=[pltpu.VMEM((tm, d), BF16)],
        compiler_params=_params("arbitrary", "arbitrary"),
        name="in_proj",
    )(x2, mod, norm_w.reshape(1, d), w_in, w_in, w_in, w_in, w_in, w_in,
      q_norm_w.reshape(1, hd), k_norm_w.reshape(1, hd))


FFN_OUT_CHUNK = 512


def _ffn_kernel(h_ref, x1_hbm, mod_ref, w1_ref, w2_ref, o_ref, sem, *, tm):
    i = pl.program_id(0)
    c = pl.program_id(1)

    def x1_copy():
        return pltpu.make_async_copy(x1_hbm.at[pl.ds(i * tm, tm), :], o_ref, sem)

    @pl.when(c == 0)
    def _():
        x1_copy().start()

    a = jnp.dot(h_ref[...], w1_ref[...].astype(BF16), preferred_element_type=F32)
    r = jnp.maximum(a, 0.0)
    act = (r * r).astype(BF16)

    @pl.when(c == 0)
    def _():
        x1_copy().wait()

    d = o_ref.shape[1]
    for n0 in range(0, d, FFN_OUT_CHUNK):
        cols = slice(n0, n0 + FFN_OUT_CHUNK)
        y = jnp.dot(act, w2_ref[:, cols].astype(BF16), preferred_element_type=F32)
        o_ref[:, cols] += mod_ref[0, 5:6, cols] * y


def _ffn(h2, x1, mod, w1, w2, *, tm, tf):
    m, d = h2.shape
    f = w1.shape[1]
    return pl.pallas_call(
        functools.partial(_ffn_kernel, tm=tm),
        grid=(m // tm, f // tf),
        in_specs=[
            pl.BlockSpec((tm, d), lambda i, c: (i, 0)),
            pl.BlockSpec(memory_space=pl.ANY),
            pl.BlockSpec((1, N_MOD, d), lambda i, c: (i * tm // SEQ, 0, 0)),
            pl.BlockSpec((d, tf), lambda i, c: (0, c)),
            pl.BlockSpec((tf, d), lambda i, c: (c, 0)),
        ],
        out_specs=pl.BlockSpec((tm, d), lambda i, c: (i, 0)),
        out_shape=jax.ShapeDtypeStruct((m, d), F32),
        scratch_shapes=[pltpu.SemaphoreType.DMA(())],
        compiler_params=_params("arbitrary", "arbitrary"),
        name="ffn",
    )(h2, x1, mod, w1, w2)


def _pool_kernel(u_ref, w_ref, s_ref, o_ref):
    g = pl.program_id(1)
    u = u_ref[...].astype(F32)
    pos = lax.broadcasted_iota(jnp.int32, u.shape, 0)

    def shifted(a, k):
        return jnp.where(pos >= k, pltpu.roll(a, k, 0), 0.0)

    sums = []
    s = u
    width = 1
    for w in POOL_WINDOWS:
        while width < w:
            s = s + shifted(s, width)
            width *= 2
        sums.append(s)
    win_sum = sums[-1]
    window = jnp.int32(POOL_WINDOWS[-1])
    for i in range(len(POOL_WINDOWS) - 2, -1, -1):
        win_sum = jnp.where(g == i, sums[i], win_sum)
        window = jnp.where(g == i, jnp.int32(POOL_WINDOWS[i]), window)
    count = jnp.minimum(pos + 1, window).astype(F32)
    pooled = win_sum / count - u
    mixed = jnp.dot(pooled.astype(BF16), w_ref[0], preferred_element_type=F32)
    o_ref[...] = (mixed * s_ref[...]).astype(o_ref.dtype)


def _pool(u, w_pool_bf, pool_scale, batch):
    groups = len(POOL_WINDOWS)
    gd = POOL_GROUP_DIM
    return pl.pallas_call(
        _pool_kernel,
        grid=(batch, groups),
        in_specs=[
            pl.BlockSpec((SEQ, gd), lambda b, g: (b, g)),
            pl.BlockSpec((1, gd, gd), lambda b, g: (g, 0, 0)),
            pl.BlockSpec((1, gd), lambda b, g: (0, g)),
        ],
        out_specs=pl.BlockSpec((SEQ, gd), lambda b, g: (b, g)),
        out_shape=jax.ShapeDtypeStruct((batch * SEQ, POOL_WIDTH), BF16),
        compiler_params=_params("arbitrary", "arbitrary"),
        name="pool",
    )(u, w_pool_bf, pool_scale.reshape(1, POOL_WIDTH))


EXP_UNDERFLOW = -104.0


def _attn_kernel(q_ref, k_ref, v_ref, o_ref, carry_ref, acc_ref, *, tb, hp):
    qi = pl.program_id(2)
    scale = 1.0 / math.sqrt(HEAD_DIM)
    nseg = tb // LANES

    r = lax.broadcasted_iota(jnp.int32, (2 * LANES, 2 * LANES), 0)
    c = lax.broadcasted_iota(jnp.int32, (2 * LANES, 2 * LANES), 1)
    cum_op = jnp.where(jnp.logical_or(c >= LANES, jnp.bitwise_and(r, LANES - 1) > c),
                       1.0, 0.0).astype(BF16)
    tri_r = lax.broadcasted_iota(jnp.int32, (tb, tb), 0)
    tri_c = lax.broadcasted_iota(jnp.int32, (tb, tb), 1)
    causal = tri_c < tri_r

    def block(kb, diagonal):
        start = pl.multiple_of(kb * tb, tb)
        top = None
        for p in range(hp):
            cols = slice(p * HEAD_DIM, (p + 1) * HEAD_DIM)
            k = k_ref[pl.ds(start, tb), cols]
            v = v_ref[pl.ds(start, tb), cols]
            z = lax.dot_general(q_ref[:, cols], k, (((1,), (1,)), ((), ())),
                                preferred_element_type=F32) * scale
            log_beta = jnp.minimum(z, 0.0) - jnp.log(1.0 + jnp.exp(-jnp.abs(z)))
            l = log_beta - z
            if diagonal:
                l = jnp.where(causal, l, 0.0)
            carry = carry_ref[p]
            a_parts = [None] * nseg
            for sg in range(nseg - 1, -1, -1):
                seg = slice(sg * LANES, (sg + 1) * LANES)
                l_hi = l[:, seg].astype(BF16)
                l_lo = (l[:, seg] - l_hi.astype(F32)).astype(BF16)
                cs = jnp.dot(jnp.concatenate([l_hi, l_lo], axis=1), cum_op,
                             preferred_element_type=F32)
                a_parts[sg] = jnp.exp(log_beta[:, seg] + (cs[:, :LANES] + carry))
                carry = carry + cs[:, LANES:]
            a = jnp.concatenate(a_parts, axis=1)
            if diagonal:
                a = jnp.where(causal, a, 0.0)
            acc_ref[p] += jnp.dot(a.astype(BF16), v, preferred_element_type=F32)
            carry_ref[p] = carry
            top = jnp.max(carry) if top is None else jnp.maximum(top, jnp.max(carry))
        return top

    carry_ref[...] = jnp.zeros_like(carry_ref)
    acc_ref[...] = jnp.zeros_like(acc_ref)
    top = block(qi, True)

    def cond(state):
        kb, top = state
        return jnp.logical_and(kb >= 0, top > EXP_UNDERFLOW)

    def body(state):
        kb, _ = state
        return kb - 1, block(kb, False)

    lax.while_loop(cond, body, (qi - 1, top))
    for p in range(hp):
        o_ref[:, p * HEAD_DIM:(p + 1) * HEAD_DIM] = acc_ref[p].astype(o_ref.dtype)


def _attn(q, k, v, batch, *, tb, hp):
    nq = SEQ // tb
    width = hp * HEAD_DIM
    return pl.pallas_call(
        functools.partial(_attn_kernel, tb=tb, hp=hp),
        grid=(batch, SB_HEADS // hp, nq),
        in_specs=[
            pl.BlockSpec((tb, width), lambda b, h, i: (b * nq + i, h)),
            pl.BlockSpec((SEQ, width), lambda b, h, i: (b, h)),
            pl.BlockSpec((SEQ, width), lambda b, h, i: (b, h)),
        ],
        out_specs=pl.BlockSpec((tb, width), lambda b, h, i: (b * nq + i, h)),
        out_shape=jax.ShapeDtypeStruct((batch * SEQ, SB_WIDTH), BF16),
        scratch_shapes=[pltpu.VMEM((hp, tb, LANES), F32),
                        pltpu.VMEM((hp, tb, HEAD_DIM), F32)],
        compiler_params=_params("arbitrary", "arbitrary", "arbitrary"),
        name="attn",
    )(q, k, v)


MIX_CHUNK = 512


def _mix_kernel(pa_ref, at_ref, sa_ref, sb_ref, x_ref, mod_ref, nw_ref, wa_ref, wb_ref, wo_ref,
                x1_ref, h2_ref, merged_scr):
    d = x_ref.shape[1]
    pa = pa_ref[...]
    at = at_ref[...]
    for n0 in range(0, d, MIX_CHUNK):
        cols = slice(n0, n0 + MIX_CHUNK)
        ya = jnp.dot(pa, wa_ref[:, cols], preferred_element_type=F32)
        yb = jnp.dot(at, wb_ref[:, cols], preferred_element_type=F32)
        merged = sa_ref[:, cols].astype(F32) * ya + sb_ref[:, cols].astype(F32) * yb
        merged_scr[:, cols] = merged.astype(BF16)
    merged = merged_scr[...]
    for n0 in range(0, d, MIX_CHUNK):
        cols = slice(n0, n0 + MIX_CHUNK)
        o = jnp.dot(merged, wo_ref[:, cols], preferred_element_type=F32)
        x1_ref[:, cols] = x_ref[:, cols] + mod_ref[0, 2:3, cols] * o
    h2 = _norm_modulate(x1_ref[...], nw_ref[...], mod_ref[0, 4:5, :], mod_ref[0, 3:4, :])
    h2_ref[...] = h2.astype(BF16)


def _mix(pa, at, sa, sb, x2, mod, norm_w, wa_bf, wb_bf, wo_bf, *, tm):
    m, d = x2.shape
    kdim = pa.shape[1]
    resident = pl.Buffered(1)
    return pl.pallas_call(
        _mix_kernel,
        grid=(m // tm,),
        in_specs=[
            pl.BlockSpec((tm, kdim), lambda i: (i, 0)),
            pl.BlockSpec((tm, kdim), lambda i: (i, 0)),
            pl.BlockSpec((tm, d), lambda i: (i, 0)),
            pl.BlockSpec((tm, d), lambda i: (i, 0)),
            pl.BlockSpec((tm, d), lambda i: (i, 0)),
            pl.BlockSpec((1, N_MOD, d), lambda i: (i * tm // SEQ, 0, 0)),
            pl.BlockSpec((1, d), lambda i: (0, 0)),
            pl.BlockSpec((kdim, d), lambda i: (0, 0), pipeline_mode=resident),
            pl.BlockSpec((kdim, d), lambda i: (0, 0), pipeline_mode=resident),
            pl.BlockSpec((d, d), lambda i: (0, 0), pipeline_mode=resident),
        ],
        out_specs=[pl.BlockSpec((tm, d), lambda i: (i, 0)),
                   pl.BlockSpec((tm, d), lambda i: (i, 0))],
        out_shape=[jax.ShapeDtypeStruct((m, d), F32),
                   jax.ShapeDtypeStruct((m, d), BF16)],
        scratch_shapes=[pltpu.VMEM((tm, d), BF16)],
        compiler_params=_params("arbitrary"),
        name="mix",
    )(pa, at, sa, sb, x2, mod, norm_w.reshape(1, d), wa_bf, wb_bf, wo_bf)


def kernel(x, c, w_ada, b_ada, norm1_w, w_in, q_norm_w, k_norm_w, w_pool, pool_scale,
           w_a_up, w_b_up, w_o, norm2_w, w_ff1, w_ff2):
    batch, seq, d = x.shape
    assert (seq, d) == (SEQ, D_MODEL) and w_ada.shape[0] == 1
    x2 = x.reshape(batch * seq, d)

    mod = _ada(c, w_ada[0], b_ada[0])
    u, q, k, v, sa, sb = _in_proj(x2, mod, norm1_w[0], w_in[0], q_norm_w[0], k_norm_w[0],
                                  tm=1024)
    pa = _pool(u, w_pool[0].astype(BF16), pool_scale[0], batch)
    at = _attn(q, k, v, batch, tb=256, hp=2)
    x1, h2 = _mix(pa, at, sa, sb, x2, mod, norm2_w[0], w_a_up[0].astype(BF16),
                  w_b_up[0].astype(BF16), w_o[0].astype(BF16), tm=256)
    out = _ffn(h2, x1, mod, w_ff1[0], w_ff2[0], tm=1024, tf=512)
    return out.reshape(batch, seq, d)
```

```python
import functools
import math

import jax
import jax.numpy as jnp
from jax import lax
from jax.experimental import pallas as pl
from jax.experimental.pallas import tpu as pltpu

D_MODEL = 2048
SEQ = 2048
POOL_WIDTH = D_MODEL // 2
POOL_WINDOWS = (2, 4, 8, 16)
POOL_GROUP_DIM = POOL_WIDTH // len(POOL_WINDOWS)
HEAD_DIM = 128
SB_WIDTH = D_MODEL // 2
SB_HEADS = SB_WIDTH // HEAD_DIM
IN_WIDTH = POOL_WIDTH + 3 * SB_WIDTH + 2 * D_MODEL
D_FF = 4 * D_MODEL
N_MOD = 6
EPS = 1e-6

Q_OFF = POOL_WIDTH
K_OFF = Q_OFF + SB_WIDTH
V_OFF = K_OFF + SB_WIDTH
GA_OFF = V_OFF + SB_WIDTH
GB_OFF = GA_OFF + D_MODEL

LANES = 128
VMEM_LIMIT_BYTES = 56 * 1024 * 1024

BF16 = jnp.bfloat16
F32 = jnp.float32


def _params(*semantics):
    return pltpu.CompilerParams(dimension_semantics=semantics,
                                vmem_limit_bytes=VMEM_LIMIT_BYTES)


def _ada_kernel(c_ref, w_ref, b_ref, o_ref):
    c = c_ref[...]
    sc = (c * jax.nn.sigmoid(c)).astype(BF16)
    acc = jnp.dot(sc, w_ref[...].astype(BF16), preferred_element_type=F32)
    o_ref[...] = acc + b_ref[...]


def _ada(c, w_ada, b_ada):
    batch, d = c.shape
    n = w_ada.shape[1]
    tn = 1024
    rows = 8
    c_pad = jnp.zeros((rows, d), F32).at[:batch].set(c)
    out = pl.pallas_call(
        _ada_kernel,
        grid=(n // tn,),
        in_specs=[
            pl.BlockSpec((rows, d), lambda j: (0, 0)),
            pl.BlockSpec((d, tn), lambda j: (0, j)),
            pl.BlockSpec((1, tn), lambda j: (0, j)),
        ],
        out_specs=pl.BlockSpec((rows, tn), lambda j: (0, j)),
        out_shape=jax.ShapeDtypeStruct((rows, n), F32),
        compiler_params=_params("arbitrary"),
        name="ada",
    )(c_pad, w_ada, b_ada.reshape(1, n))
    return out[:batch].reshape(batch, N_MOD, d)


def _norm_modulate(x, norm_w, scale, shift):
    ms = jnp.mean(x * x, axis=-1, keepdims=True)
    y = x * lax.rsqrt(ms + EPS) * norm_w
    return y * (1.0 + scale) + shift


NORM_ROWS = 64


def _head_norm(blk, w_ref):
    ms = jnp.mean(blk * blk, axis=-1, keepdims=True)
    return (blk * lax.rsqrt(ms + EPS) * w_ref[...]).astype(BF16)


def _in_proj_kernel(x_ref, mod_ref, nw_ref, wu_ref, wq_ref, wk_ref, wv_ref, wga_ref, wgb_ref,
                    qw_ref, kw_ref, u_ref, q_ref, k_ref, v_ref, ga_ref, gb_ref, h_scr):
    @pl.when(pl.program_id(1) == 0)
    def _():
        def body(r, carry):
            rows = pl.ds(pl.multiple_of(r * NORM_ROWS, NORM_ROWS), NORM_ROWS)
            h = _norm_modulate(x_ref[rows, :], nw_ref[...], mod_ref[0, 1:2, :], mod_ref[0, 0:1, :])
            h_scr[rows, :] = h.astype(BF16)
            return carry
        lax.fori_loop(0, x_ref.shape[0] // NORM_ROWS, body, 0)

    h = h_scr[...]

    def project(*w_refs):
        w = jnp.concatenate([w_ref[...].astype(BF16) for w_ref in w_refs], axis=1)
        return jnp.dot(h, w, preferred_element_type=F32)

    qk = project(wq_ref, wk_ref)
    q_ref[...] = _head_norm(qk[:, :HEAD_DIM], qw_ref)
    k_ref[...] = _head_norm(qk[:, HEAD_DIM:], kw_ref)
    ga_ref[...] = jax.nn.sigmoid(project(wga_ref)).astype(BF16)
    gb_ref[...] = jax.nn.sigmoid(project(wgb_ref)).astype(BF16)
    uv = project(wu_ref, wv_ref)
    u_ref[...] = uv[:, :HEAD_DIM]
    v_ref[...] = uv[:, HEAD_DIM:].astype(BF16)


def _in_proj(x2, mod, norm_w, w_in, q_norm_w, k_norm_w, *, tm):
    m, d = x2.shape
    hd = HEAD_DIM
    gw = D_MODEL // SB_HEADS
    w_spec = lambda width, off: pl.BlockSpec((d, width), lambda i, j: (0, off // width + j))
    out_spec = lambda width: pl.BlockSpec((tm, width), lambda i, j: (i, j))
    return pl.pallas_call(
        _in_proj_kernel,
        grid=(m // tm, SB_HEADS),
        in_specs=[
            pl.BlockSpec((tm, d), lambda i, j: (i, 0)),
            pl.BlockSpec((1, N_MOD, d), lambda i, j: (i * tm // SEQ, 0, 0)),
            pl.BlockSpec((1, d), lambda i, j: (0, 0)),
            w_spec(hd, 0), w_spec(hd, Q_OFF), w_spec(hd, K_OFF), w_spec(hd, V_OFF),
            w_spec(gw, GA_OFF), w_spec(gw, GB_OFF),
            pl.BlockSpec((1, hd), lambda i, j: (0, 0)),
            pl.BlockSpec((1, hd), lambda i, j: (0, 0)),
        ],
        out_specs=[out_spec(hd), out_spec(hd), out_spec(hd), out_spec(hd),
                   out_spec(gw), out_spec(gw)],
        out_shape=[jax.ShapeDtypeStruct((m, POOL_WIDTH), F32),
                   jax.ShapeDtypeStruct((m, SB_WIDTH), BF16),
                   jax.ShapeDtypeStruct((m, SB_WIDTH), BF16),
                   jax.ShapeDtypeStruct((m, SB_WIDTH), BF16),
                   jax.ShapeDtypeStruct((m, D_MODEL), BF16),
                   jax.ShapeDtypeStruct((m, D_MODEL), BF16)],
        scratch_shapes=[pltpu.VMEM((tm, d), BF16)],
        compiler_params=_params("arbitrary", "arbitrary"),
        name="in_proj",
    )(x2, mod, norm_w.reshape(1, d), w_in, w_in, w_in, w_in, w_in, w_in,
      q_norm_w.reshape(1, hd), k_norm_w.reshape(1, hd))


FFN_OUT_CHUNK = 512


def _ffn_kernel(h_ref, x1_hbm, mod_ref, w1_ref, w2_ref, o_ref, sem, *, tm):
    i = pl.program_id(0)
    c = pl.program_id(1)

    def x1_copy():
        return pltpu.make_async_copy(x1_hbm.at[pl.ds(i * tm, tm), :], o_ref, sem)

    @pl.when(c == 0)
    def _():
        x1_copy().start()

    a = jnp.dot(h_ref[...], w1_ref[...].astype(BF16), preferred_element_type=F32)
    r = jnp.maximum(a, 0.0)
    act = (r * r).astype(BF16)

    @pl.when(c == 0)
    def _():
        x1_copy().wait()

    d = o_ref.shape[1]
    for n0 in range(0, d, FFN_OUT_CHUNK):
        cols = slice(n0, n0 + FFN_OUT_CHUNK)
        y = jnp.dot(act, w2_ref[:, cols].astype(BF16), preferred_element_type=F32)
        o_ref[:, cols] += mod_ref[0, 5:6, cols] * y


def _ffn(h2, x1, mod, w1, w2, *, tm, tf):
    m, d = h2.shape
    f = w1.shape[1]
    return pl.pallas_call(
        functools.partial(_ffn_kernel, tm=tm),
        grid=(m // tm, f // tf),
        in_specs=[
            pl.BlockSpec((tm, d), lambda i, c: (i, 0)),
            pl.BlockSpec(memory_space=pl.ANY),
            pl.BlockSpec((1, N_MOD, d), lambda i, c: (i * tm // SEQ, 0, 0)),
            pl.BlockSpec((d, tf), lambda i, c: (0, c)),
            pl.BlockSpec((tf, d), lambda i, c: (c, 0)),
        ],
        out_specs=pl.BlockSpec((tm, d), lambda i, c: (i, 0)),
        out_shape=jax.ShapeDtypeStruct((m, d), F32),
        scratch_shapes=[pltpu.SemaphoreType.DMA(())],
        compiler_params=_params("arbitrary", "arbitrary"),
        name="ffn",
    )(h2, x1, mod, w1, w2)


POOL_PAD = 8
POOL_HEAD = max(POOL_WINDOWS)


def _pool_kernel(u_ref, w_ref, s_ref, o_ref, buf_a, buf_b):
    gd = POOL_GROUP_DIM
    body = pl.ds(POOL_PAD, SEQ)
    buf_a[0:POOL_PAD, :] = jnp.zeros((POOL_PAD, gd), F32)
    buf_b[0:POOL_PAD, :] = jnp.zeros((POOL_PAD, gd), F32)
    pos = lax.broadcasted_iota(jnp.int32, (POOL_HEAD, gd), 0)
    for g, w in enumerate(POOL_WINDOWS):
        cols = slice(g * gd, (g + 1) * gd)
        src, dst = buf_a, buf_b
        src[body, :] = u_ref[:, cols]
        shift = 1
        while shift < w:
            dst[body, :] = src[body, :] + src[pl.ds(POOL_PAD - shift, SEQ), :]
            src, dst = dst, src
            shift *= 2
        dst[body, :] = src[body, :] * (1.0 / w) - u_ref[:, cols]
        count = jnp.minimum(pos + 1, w).astype(F32)
        head = pl.ds(POOL_PAD, POOL_HEAD)
        dst[head, :] = src[head, :] / count - u_ref[0:POOL_HEAD, cols]
        mixed = jnp.dot(dst[body, :].astype(BF16), w_ref[g], preferred_element_type=F32)
        o_ref[:, cols] = (mixed * s_ref[:, cols]).astype(o_ref.dtype)


def _pool(u, w_pool_bf, pool_scale, batch):
    groups = len(POOL_WINDOWS)
    gd = POOL_GROUP_DIM
    return pl.pallas_call(
        _pool_kernel,
        grid=(batch,),
        in_specs=[
            pl.BlockSpec((SEQ, POOL_WIDTH), lambda b: (b, 0)),
            pl.BlockSpec((groups, gd, gd), lambda b: (0, 0, 0)),
            pl.BlockSpec((1, POOL_WIDTH), lambda b: (0, 0)),
        ],
        out_specs=pl.BlockSpec((SEQ, POOL_WIDTH), lambda b: (b, 0)),
        out_shape=jax.ShapeDtypeStruct((batch * SEQ, POOL_WIDTH), BF16),
        scratch_shapes=[pltpu.VMEM((POOL_PAD + SEQ, gd), F32),
                        pltpu.VMEM((POOL_PAD + SEQ, gd), F32)],
        compiler_params=_params("arbitrary"),
        name="pool",
    )(u, w_pool_bf, pool_scale.reshape(1, POOL_WIDTH))


EXP2_UNDERFLOW = -151.0


def _attn_kernel(q_ref, k_ref, v_ref, o_ref, carry_ref, acc_ref, *, tb, hp):
    qi = pl.program_id(2)
    scale = math.log2(math.e) / math.sqrt(HEAD_DIM)
    nseg = tb // LANES

    r = lax.broadcasted_iota(jnp.int32, (2 * LANES, 2 * LANES), 0)
    c = lax.broadcasted_iota(jnp.int32, (2 * LANES, 2 * LANES), 1)
    cum_op = jnp.where(jnp.logical_or(c >= LANES, jnp.bitwise_and(r, LANES - 1) > c),
                       1.0, 0.0).astype(BF16)
    tri_r = lax.broadcasted_iota(jnp.int32, (tb, tb), 0)
    tri_c = lax.broadcasted_iota(jnp.int32, (tb, tb), 1)
    causal = tri_c < tri_r

    def block(kb, diagonal):
        start = pl.multiple_of(kb * tb, tb)
        heads = [slice(p * HEAD_DIM, (p + 1) * HEAD_DIM) for p in range(hp)]
        segs = [slice(sg * LANES, (sg + 1) * LANES) for sg in range(nseg)]
        zs = [lax.dot_general(q_ref[:, cols], k_ref[pl.ds(start, tb), cols],
                              (((1,), (1,)), ((), ())), preferred_element_type=F32)
              for cols in heads]
        log_betas, sums = [], []
        for z in zs:
            z = z * scale
            log_beta = jnp.minimum(z, 0.0) - jnp.log2(1.0 + jnp.exp2(-jnp.abs(z)))
            l = log_beta - z
            if diagonal:
                l = jnp.where(causal, l, 0.0)
            cs = []
            for seg in segs:
                l_hi = l[:, seg].astype(BF16)
                l_lo = (l[:, seg] - l_hi.astype(F32)).astype(BF16)
                cs.append(jnp.dot(jnp.concatenate([l_hi, l_lo], axis=1), cum_op,
                                  preferred_element_type=F32))
            log_betas.append(log_beta)
            sums.append(cs)
        top = None
        for p, cols in enumerate(heads):
            carry = carry_ref[p]
            a_parts = [None] * nseg
            for sg in range(nseg - 1, -1, -1):
                cs = sums[p][sg]
                a_parts[sg] = jnp.exp2(log_betas[p][:, segs[sg]] + (cs[:, :LANES] + carry))
                carry = carry + cs[:, LANES:]
            a = jnp.concatenate(a_parts, axis=1)
            if diagonal:
                a = jnp.where(causal, a, 0.0)
            acc_ref[p] += jnp.dot(a.astype(BF16), v_ref[pl.ds(start, tb), cols],
                                  preferred_element_type=F32)
            carry_ref[p] = carry
            top = jnp.max(carry) if top is None else jnp.maximum(top, jnp.max(carry))
        return top

    carry_ref[...] = jnp.zeros_like(carry_ref)
    acc_ref[...] = jnp.zeros_like(acc_ref)
    top = block(qi, True)

    def cond(state):
        kb, top = state
        return jnp.logical_and(kb >= 0, top > EXP2_UNDERFLOW)

    def body(state):
        kb, _ = state
        return kb - 1, block(kb, False)

    lax.while_loop(cond, body, (qi - 1, top))
    for p in range(hp):
        o_ref[:, p * HEAD_DIM:(p + 1) * HEAD_DIM] = acc_ref[p].astype(o_ref.dtype)


def _attn(q, k, v, batch, *, tb, hp):
    nq = SEQ // tb
    width = hp * HEAD_DIM
    return pl.pallas_call(
        functools.partial(_attn_kernel, tb=tb, hp=hp),
        grid=(batch, SB_HEADS // hp, nq),
        in_specs=[
            pl.BlockSpec((tb, width), lambda b, h, i: (b * nq + i, h)),
            pl.BlockSpec((SEQ, width), lambda b, h, i: (b, h)),
            pl.BlockSpec((SEQ, width), lambda b, h, i: (b, h)),
        ],
        out_specs=pl.BlockSpec((tb, width), lambda b, h, i: (b * nq + i, h)),
        out_shape=jax.ShapeDtypeStruct((batch * SEQ, SB_WIDTH), BF16),
        scratch_shapes=[pltpu.VMEM((hp, tb, LANES), F32),
                        pltpu.VMEM((hp, tb, HEAD_DIM), F32)],
        compiler_params=_params("arbitrary", "arbitrary", "arbitrary"),
        name="attn",
    )(q, k, v)


MIX_CHUNK = 512


def _mix_kernel(pa_ref, at_ref, sa_ref, sb_ref, x_ref, mod_ref, nw_ref, wa_ref, wb_ref, wo_ref,
                x1_ref, h2_ref, merged_scr):
    d = x_ref.shape[1]
    pa = pa_ref[...]
    at = at_ref[...]
    for n0 in range(0, d, MIX_CHUNK):
        cols = slice(n0, n0 + MIX_CHUNK)
        ya = jnp.dot(pa, wa_ref[:, cols], preferred_element_type=F32)
        yb = jnp.dot(at, wb_ref[:, cols], preferred_element_type=F32)
        merged = sa_ref[:, cols].astype(F32) * ya + sb_ref[:, cols].astype(F32) * yb
        merged_scr[:, cols] = merged.astype(BF16)
    merged = merged_scr[...]
    for n0 in range(0, d, MIX_CHUNK):
        cols = slice(n0, n0 + MIX_CHUNK)
        o = jnp.dot(merged, wo_ref[:, cols], preferred_element_type=F32)
        x1_ref[:, cols] = x_ref[:, cols] + mod_ref[0, 2:3, cols] * o
    h2 = _norm_modulate(x1_ref[...], nw_ref[...], mod_ref[0, 4:5, :], mod_ref[0, 3:4, :])
    h2_ref[...] = h2.astype(BF16)


def _mix(pa, at, sa, sb, x2, mod, norm_w, wa_bf, wb_bf, wo_bf, *, tm):
    m, d = x2.shape
    kdim = pa.shape[1]
    resident = pl.Buffered(1)
    return pl.pallas_call(
        _mix_kernel,
        grid=(m // tm,),
        in_specs=[
            pl.BlockSpec((tm, kdim), lambda i: (i, 0)),
            pl.BlockSpec((tm, kdim), lambda i: (i, 0)),
            pl.BlockSpec((tm, d), lambda i: (i, 0)),
            pl.BlockSpec((tm, d), lambda i: (i, 0)),
            pl.BlockSpec((tm, d), lambda i: (i, 0)),
            pl.BlockSpec((1, N_MOD, d), lambda i: (i * tm // SEQ, 0, 0)),
            pl.BlockSpec((1, d), lambda i: (0, 0)),
            pl.BlockSpec((kdim, d), lambda i: (0, 0), pipeline_mode=resident),
            pl.BlockSpec((kdim, d), lambda i: (0, 0), pipeline_mode=resident),
            pl.BlockSpec((d, d), lambda i: (0, 0), pipeline_mode=resident),
        ],
        out_specs=[pl.BlockSpec((tm, d), lambda i: (i, 0)),
                   pl.BlockSpec((tm, d), lambda i: (i, 0))],
        out_shape=[jax.ShapeDtypeStruct((m, d), F32),
                   jax.ShapeDtypeStruct((m, d), BF16)],
        scratch_shapes=[pltpu.VMEM((tm, d), BF16)],
        compiler_params=_params("arbitrary"),
        name="mix",
    )(pa, at, sa, sb, x2, mod, norm_w.reshape(1, d), wa_bf, wb_bf, wo_bf)


def kernel(x, c, w_ada, b_ada, norm1_w, w_in, q_norm_w, k_norm_w, w_pool, pool_scale,
           w_a_up, w_b_up, w_o, norm2_w, w_ff1, w_ff2):
    batch, seq, d = x.shape
    assert (seq, d) == (SEQ, D_MODEL) and w_ada.shape[0] == 1
    x2 = x.reshape(batch * seq, d)

    mod = _ada(c, w_ada[0], b_ada[0])
    u, q, k, v, sa, sb = _in_proj(x2, mod, norm1_w[0], w_in[0], q_norm_w[0], k_norm_w[0],
                                  tm=1024)
    pa = _pool(u, w_pool[0].astype(BF16), pool_scale[0], batch)
    at = _attn(q, k, v, batch, tb=256, hp=8)
    x1, h2 = _mix(pa, at, sa, sb, x2, mod, norm2_w[0], w_a_up[0].astype(BF16),
                  w_b_up[0].astype(BF16), w_o[0].astype(BF16), tm=256)
    out = _ffn(h2, x1, mod, w_ff1[0], w_ff2[0], tm=1024, tf=512)
    return out.reshape(batch, seq, d)
```

```python
import functools
import math

import jax
import jax.numpy as jnp
from jax import lax
from jax.experimental import pallas as pl
from jax.experimental.pallas import tpu as pltpu

D_MODEL = 2048
SEQ = 2048
POOL_WIDTH = D_MODEL // 2
POOL_WINDOWS = (2, 4, 8, 16)
POOL_GROUP_DIM = POOL_WIDTH // len(POOL_WINDOWS)
HEAD_DIM = 128
SB_WIDTH = D_MODEL // 2
SB_HEADS = SB_WIDTH // HEAD_DIM
IN_WIDTH = POOL_WIDTH + 3 * SB_WIDTH + 2 * D_MODEL
D_FF = 4 * D_MODEL
N_MOD = 6
EPS = 1e-6

Q_OFF = POOL_WIDTH
K_OFF = Q_OFF + SB_WIDTH
V_OFF = K_OFF + SB_WIDTH
GA_OFF = V_OFF + SB_WIDTH
GB_OFF = GA_OFF + D_MODEL

LANES = 128
VMEM_LIMIT_BYTES = 56 * 1024 * 1024

BF16 = jnp.bfloat16
F32 = jnp.float32


def _params(*semantics, flags=None):
    return pltpu.CompilerParams(dimension_semantics=semantics,
                                vmem_limit_bytes=VMEM_LIMIT_BYTES, flags=flags)


def _ada_kernel(c_ref, w_ref, b_ref, o_ref):
    c = c_ref[...]
    sc = (c * jax.nn.sigmoid(c)).astype(BF16)
    acc = jnp.dot(sc, w_ref[...].astype(BF16), preferred_element_type=F32)
    o_ref[...] = acc + b_ref[...]


def _ada(c, w_ada, b_ada):
    batch, d = c.shape
    n = w_ada.shape[1]
    tn = 1024
    rows = 8
    c_pad = jnp.zeros((rows, d), F32).at[:batch].set(c)
    out = pl.pallas_call(
        _ada_kernel,
        grid=(n // tn,),
        in_specs=[
            pl.BlockSpec((rows, d), lambda j: (0, 0)),
            pl.BlockSpec((d, tn), lambda j: (0, j)),
            pl.BlockSpec((1, tn), lambda j: (0, j)),
        ],
        out_specs=pl.BlockSpec((rows, tn), lambda j: (0, j)),
        out_shape=jax.ShapeDtypeStruct((rows, n), F32),
        compiler_params=_params("arbitrary"),
        name="ada",
    )(c_pad, w_ada, b_ada.reshape(1, n))
    return out[:batch].reshape(batch, N_MOD, d)


NORM_ROWS = 256


def _norm_modulate_rows(x_ref, o_ref, norm_w, scale, shift):
    def body(r, carry):
        rows = pl.ds(pl.multiple_of(r * NORM_ROWS, NORM_ROWS), NORM_ROWS)
        x = x_ref[rows, :]
        y = x * lax.rsqrt(jnp.mean(x * x, axis=-1, keepdims=True) + EPS) * norm_w
        o_ref[rows, :] = (y * (1.0 + scale) + shift).astype(o_ref.dtype)
        return carry

    lax.fori_loop(0, x_ref.shape[0] // NORM_ROWS, body, 0)


def _head_norm(blk, w_ref):
    ms = jnp.mean(blk * blk, axis=-1, keepdims=True)
    return (blk * lax.rsqrt(ms + EPS) * w_ref[...]).astype(BF16)


def _in_proj_kernel(x_ref, mod_ref, nw_ref, wu_ref, wq_ref, wk_ref, wv_ref, wga_ref, wgb_ref,
                    qw_ref, kw_ref, u_ref, q_ref, k_ref, v_ref, ga_ref, gb_ref, h_scr):
    @pl.when(pl.program_id(1) == 0)
    def _():
        _norm_modulate_rows(x_ref, h_scr, nw_ref[...], mod_ref[0, 1:2, :], mod_ref[0, 0:1, :])

    h = h_scr[...]

    def project(*w_refs):
        w = jnp.concatenate([w_ref[...].astype(BF16) for w_ref in w_refs], axis=1)
        return jnp.dot(h, w, preferred_element_type=F32)

    qk = project(wq_ref, wk_ref)
    q_ref[...] = _head_norm(qk[:, :HEAD_DIM], qw_ref)
    k_ref[...] = _head_norm(qk[:, HEAD_DIM:], kw_ref)
    ga_ref[...] = jax.nn.sigmoid(project(wga_ref)).astype(BF16)
    gb_ref[...] = jax.nn.sigmoid(project(wgb_ref)).astype(BF16)
    uv = project(wu_ref, wv_ref)
    u_ref[...] = uv[:, :HEAD_DIM]
    v_ref[...] = uv[:, HEAD_DIM:].astype(BF16)


def _in_proj(x2, mod, norm_w, w_in, q_norm_w, k_norm_w, *, tm):
    m, d = x2.shape
    hd = HEAD_DIM
    gw = D_MODEL // SB_HEADS
    w_spec = lambda width, off: pl.BlockSpec((d, width), lambda i, j: (0, off // width + j))
    out_spec = lambda width: pl.BlockSpec((tm, width), lambda i, j: (i, j))
    return pl.pallas_call(
        _in_proj_kernel,
        grid=(m // tm, SB_HEADS),
        in_specs=[
            pl.BlockSpec((tm, d), lambda i, j: (i, 0)),
            pl.BlockSpec((1, N_MOD, d), lambda i, j: (i * tm // SEQ, 0, 0)),
            pl.BlockSpec((1, d), lambda i, j: (0, 0)),
            w_spec(hd, 0), w_spec(hd, Q_OFF), w_spec(hd, K_OFF), w_spec(hd, V_OFF),
            w_spec(gw, GA_OFF), w_spec(gw, GB_OFF),
            pl.BlockSpec((1, hd), lambda i, j: (0, 0)),
            pl.BlockSpec((1, hd), lambda i, j: (0, 0)),
        ],
        out_specs=[out_spec(hd), out_spec(hd), out_spec(hd), out_spec(hd),
                   out_spec(gw), out_spec(gw)],
        out_shape=[jax.ShapeDtypeStruct((m, POOL_WIDTH), F32),
                   jax.ShapeDtypeStruct((m, SB_WIDTH), BF16),
                   jax.ShapeDtypeStruct((m, SB_WIDTH), BF16),
                   jax.ShapeDtypeStruct((m, SB_WIDTH), BF16),
                   jax.ShapeDtypeStruct((m, D_MODEL), BF16),
                   jax.ShapeDtypeStruct((m, D_MODEL), BF16)],
        scratch_shapes=[pltpu.VMEM((tm, d), BF16)],
        compiler_params=_params("arbitrary", "arbitrary"),
        name="in_proj",
    )(x2, mod, norm_w.reshape(1, d), w_in, w_in, w_in, w_in, w_in, w_in,
      q_norm_w.reshape(1, hd), k_norm_w.reshape(1, hd))


FFN_OUT_CHUNK = 512


def _ffn_kernel(h_ref, x1_hbm, mod_ref, w1_ref, w2_ref, o_ref, xbuf, sem, *, tm, xrows):
    i = pl.program_id(0)
    c = pl.program_id(1)
    x1_copy = pltpu.make_async_copy(x1_hbm.at[pl.ds(i * tm + c * xrows, xrows), :], xbuf, sem)
    x1_copy.start()

    @pl.when(c == 0)
    def _():
        o_ref[...] = jnp.zeros_like(o_ref)

    a = jnp.dot(h_ref[...], w1_ref[...].astype(BF16), preferred_element_type=F32)
    r = jnp.maximum(a, 0.0)
    act = (r * r).astype(BF16)
    d = o_ref.shape[1]
    for n0 in range(0, d, FFN_OUT_CHUNK):
        cols = slice(n0, n0 + FFN_OUT_CHUNK)
        y = jnp.dot(act, w2_ref[:, cols].astype(BF16), preferred_element_type=F32)
        o_ref[:, cols] += mod_ref[0, 5:6, cols] * y
    x1_copy.wait()
    rows = pl.ds(pl.multiple_of(c * xrows, xrows), xrows)
    o_ref[rows, :] += xbuf[...]


def _ffn(h2, x1, mod, w1, w2, *, tm, tf):
    m, d = h2.shape
    f = w1.shape[1]
    xrows = tm // (f // tf)
    return pl.pallas_call(
        functools.partial(_ffn_kernel, tm=tm, xrows=xrows),
        grid=(m // tm, f // tf),
        in_specs=[
            pl.BlockSpec((tm, d), lambda i, c: (i, 0)),
            pl.BlockSpec(memory_space=pl.ANY),
            pl.BlockSpec((1, N_MOD, d), lambda i, c: (i * tm // SEQ, 0, 0)),
            pl.BlockSpec((d, tf), lambda i, c: (0, c)),
            pl.BlockSpec((tf, d), lambda i, c: (c, 0)),
        ],
        out_specs=pl.BlockSpec((tm, d), lambda i, c: (i, 0)),
        out_shape=jax.ShapeDtypeStruct((m, d), F32),
        scratch_shapes=[pltpu.VMEM((xrows, d), F32), pltpu.SemaphoreType.DMA(())],
        compiler_params=_params("arbitrary", "arbitrary"),
        name="ffn",
    )(h2, x1, mod, w1, w2)


POOL_PAD = 8
POOL_HEAD = max(POOL_WINDOWS)


def _pool_kernel(u_ref, w_ref, s_ref, o_ref, buf_a, buf_b):
    gd = POOL_GROUP_DIM
    body = pl.ds(POOL_PAD, SEQ)
    buf_a[0:POOL_PAD, :] = jnp.zeros((POOL_PAD, gd), F32)
    buf_b[0:POOL_PAD, :] = jnp.zeros((POOL_PAD, gd), F32)
    pos = lax.broadcasted_iota(jnp.int32, (POOL_HEAD, gd), 0)
    for g, w in enumerate(POOL_WINDOWS):
        cols = slice(g * gd, (g + 1) * gd)
        src, dst = buf_a, buf_b
        src[body, :] = u_ref[:, cols]
        shift = 1
        while shift < w:
            dst[body, :] = src[body, :] + src[pl.ds(POOL_PAD - shift, SEQ), :]
            src, dst = dst, src
            shift *= 2
        dst[body, :] = src[body, :] * (1.0 / w) - u_ref[:, cols]
        count = jnp.minimum(pos + 1, w).astype(F32)
        head = pl.ds(POOL_PAD, POOL_HEAD)
        dst[head, :] = src[head, :] / count - u_ref[0:POOL_HEAD, cols]
        mixed = jnp.dot(dst[body, :].astype(BF16), w_ref[g], preferred_element_type=F32)
        o_ref[:, cols] = (mixed * s_ref[:, cols]).astype(o_ref.dtype)


def _pool(u, w_pool_bf, pool_scale, batch):
    groups = len(POOL_WINDOWS)
    gd = POOL_GROUP_DIM
    return pl.pallas_call(
        _pool_kernel,
        grid=(batch,),
        in_specs=[
            pl.BlockSpec((SEQ, POOL_WIDTH), lambda b: (b, 0)),
            pl.BlockSpec((groups, gd, gd), lambda b: (0, 0, 0)),
            pl.BlockSpec((1, POOL_WIDTH), lambda b: (0, 0)),
        ],
        out_specs=pl.BlockSpec((SEQ, POOL_WIDTH), lambda b: (b, 0)),
        out_shape=jax.ShapeDtypeStruct((batch * SEQ, POOL_WIDTH), BF16),
        scratch_shapes=[pltpu.VMEM((POOL_PAD + SEQ, gd), F32),
                        pltpu.VMEM((POOL_PAD + SEQ, gd), F32)],
        compiler_params=_params("arbitrary"),
        name="pool",
    )(u, w_pool_bf, pool_scale.reshape(1, POOL_WIDTH))


EXP2_UNDERFLOW = -151.0


def _attn_kernel(q_ref, k_ref, v_ref, o_ref, carry_ref, acc_ref, *, tb, hp):
    qi = pl.program_id(2)
    scale = math.log2(math.e) / math.sqrt(HEAD_DIM)
    nseg = tb // LANES

    r = lax.broadcasted_iota(jnp.int32, (2 * LANES, 2 * LANES), 0)
    c = lax.broadcasted_iota(jnp.int32, (2 * LANES, 2 * LANES), 1)
    cum_op = jnp.where(jnp.logical_or(c >= LANES, jnp.bitwise_and(r, LANES - 1) > c),
                       1.0, 0.0).astype(BF16)
    tri_r = lax.broadcasted_iota(jnp.int32, (tb, tb), 0)
    tri_c = lax.broadcasted_iota(jnp.int32, (tb, tb), 1)
    causal = tri_c < tri_r

    def block(kb, diagonal):
        start = pl.multiple_of(kb * tb, tb)
        heads = [slice(p * HEAD_DIM, (p + 1) * HEAD_DIM) for p in range(hp)]
        segs = [slice(sg * LANES, (sg + 1) * LANES) for sg in range(nseg)]
        zs = [lax.dot_general(q_ref[:, cols], k_ref[pl.ds(start, tb), cols],
                              (((1,), (1,)), ((), ())), preferred_element_type=F32)
              for cols in heads]
        log_betas, sums = [], []
        for z in zs:
            z = z * scale
            log_beta = jnp.minimum(z, 0.0) - jnp.log2(1.0 + jnp.exp2(-jnp.abs(z)))
            l = log_beta - z
            if diagonal:
                l = jnp.where(causal, l, 0.0)
            cs = []
            for seg in segs:
                l_hi = l[:, seg].astype(BF16)
                l_lo = (l[:, seg] - l_hi.astype(F32)).astype(BF16)
                cs.append(jnp.dot(jnp.concatenate([l_hi, l_lo], axis=1), cum_op,
                                  preferred_element_type=F32))
            log_betas.append(log_beta)
            sums.append(cs)
        top = None
        for p, cols in enumerate(heads):
            carry = carry_ref[p]
            a_parts = [None] * nseg
            for sg in range(nseg - 1, -1, -1):
                cs = sums[p][sg]
                a_parts[sg] = jnp.exp2(log_betas[p][:, segs[sg]] + (cs[:, :LANES] + carry))
                carry = carry + cs[:, LANES:]
            a = jnp.concatenate(a_parts, axis=1)
            if diagonal:
                a = jnp.where(causal, a, 0.0)
            acc_ref[p] += jnp.dot(a.astype(BF16), v_ref[pl.ds(start, tb), cols],
                                  preferred_element_type=F32)
            carry_ref[p] = carry
            top = jnp.max(carry) if top is None else jnp.maximum(top, jnp.max(carry))
        return top

    carry_ref[...] = jnp.zeros_like(carry_ref)
    acc_ref[...] = jnp.zeros_like(acc_ref)
    top = block(qi, True)

    def cond(state):
        kb, top = state
        return jnp.logical_and(kb >= 0, top > EXP2_UNDERFLOW)

    def body(state):
        kb, _ = state
        return kb - 1, block(kb, False)

    lax.while_loop(cond, body, (qi - 1, top))
    for p in range(hp):
        o_ref[:, p * HEAD_DIM:(p + 1) * HEAD_DIM] = acc_ref[p].astype(o_ref.dtype)


def _attn(q, k, v, batch, *, tb, hp):
    nq = SEQ // tb
    width = hp * HEAD_DIM
    return pl.pallas_call(
        functools.partial(_attn_kernel, tb=tb, hp=hp),
        grid=(batch, SB_HEADS // hp, nq),
        in_specs=[
            pl.BlockSpec((tb, width), lambda b, h, i: (b * nq + i, h)),
            pl.BlockSpec((SEQ, width), lambda b, h, i: (b, h)),
            pl.BlockSpec((SEQ, width), lambda b, h, i: (b, h)),
        ],
        out_specs=pl.BlockSpec((tb, width), lambda b, h, i: (b * nq + i, h)),
        out_shape=jax.ShapeDtypeStruct((batch * SEQ, SB_WIDTH), BF16),
        scratch_shapes=[pltpu.VMEM((hp, tb, LANES), F32),
                        pltpu.VMEM((hp, tb, HEAD_DIM), F32)],
        compiler_params=_params("arbitrary", "arbitrary", "arbitrary"),
        name="attn",
    )(q, k, v)


MIX_CHUNK = 512


def _mix_kernel(pa_ref, at_ref, sa_ref, sb_ref, x_ref, mod_ref, nw_ref, wa_ref, wb_ref, wo_ref,
                x1_ref, h2_ref, merged_scr):
    d = x_ref.shape[1]
    pa = pa_ref[...]
    at = at_ref[...]
    for n0 in range(0, d, MIX_CHUNK):
        cols = slice(n0, n0 + MIX_CHUNK)
        ya = jnp.dot(pa, wa_ref[:, cols], preferred_element_type=F32)
        yb = jnp.dot(at, wb_ref[:, cols], preferred_element_type=F32)
        merged = sa_ref[:, cols].astype(F32) * ya + sb_ref[:, cols].astype(F32) * yb
        merged_scr[:, cols] = merged.astype(BF16)
    merged = merged_scr[...]
    for n0 in range(0, d, MIX_CHUNK):
        cols = slice(n0, n0 + MIX_CHUNK)
        o = jnp.dot(merged, wo_ref[:, cols], preferred_element_type=F32)
        x1_ref[:, cols] = x_ref[:, cols] + mod_ref[0, 2:3, cols] * o
    _norm_modulate_rows(x1_ref, h2_ref, nw_ref[...], mod_ref[0, 4:5, :], mod_ref[0, 3:4, :])


def _mix(pa, at, sa, sb, x2, mod, norm_w, wa_bf, wb_bf, wo_bf, *, tm):
    m, d = x2.shape
    kdim = pa.shape[1]
    resident = pl.Buffered(1)
    return pl.pallas_call(
        _mix_kernel,
        grid=(m // tm,),
        in_specs=[
            pl.BlockSpec((tm, kdim), lambda i: (i, 0)),
            pl.BlockSpec((tm, kdim), lambda i: (i, 0)),
            pl.BlockSpec((tm, d), lambda i: (i, 0)),
            pl.BlockSpec((tm, d), lambda i: (i, 0)),
            pl.BlockSpec((tm, d), lambda i: (i, 0)),
            pl.BlockSpec((1, N_MOD, d), lambda i: (i * tm // SEQ, 0, 0)),
            pl.BlockSpec((1, d), lambda i: (0, 0)),
            pl.BlockSpec((kdim, d), lambda i: (0, 0), pipeline_mode=resident),
            pl.BlockSpec((kdim, d), lambda i: (0, 0), pipeline_mode=resident),
            pl.BlockSpec((d, d), lambda i: (0, 0), pipeline_mode=resident),
        ],
        out_specs=[pl.BlockSpec((tm, d), lambda i: (i, 0)),
                   pl.BlockSpec((tm, d), lambda i: (i, 0))],
        out_shape=[jax.ShapeDtypeStruct((m, d), F32),
                   jax.ShapeDtypeStruct((m, d), BF16)],
        scratch_shapes=[pltpu.VMEM((tm, d), BF16)],
        compiler_params=_params("arbitrary"),
        name="mix",
    )(pa, at, sa, sb, x2, mod, norm_w.reshape(1, d), wa_bf, wb_bf, wo_bf)


def kernel(x, c, w_ada, b_ada, norm1_w, w_in, q_norm_w, k_norm_w, w_pool, pool_scale,
           w_a_up, w_b_up, w_o, norm2_w, w_ff1, w_ff2):
    batch, seq, d = x.shape
    assert (seq, d) == (SEQ, D_MODEL) and w_ada.shape[0] == 1
    x2 = x.reshape(batch * seq, d)

    mod = _ada(c, w_ada[0], b_ada[0])
    u, q, k, v, sa, sb = _in_proj(x2, mod, norm1_w[0], w_in[0], q_norm_w[0], k_norm_w[0],
                                  tm=1024)
    pa = _pool(u, w_pool[0].astype(BF16), pool_scale[0], batch)
    at = _attn(q, k, v, batch, tb=256, hp=8)
    x1, h2 = _mix(pa, at, sa, sb, x2, mod, norm2_w[0], w_a_up[0].astype(BF16),
                  w_b_up[0].astype(BF16), w_o[0].astype(BF16), tm=256)
    out = _ffn(h2, x1, mod, w_ff1[0], w_ff2[0], tm=1024, tf=512)
    return out.reshape(batch, seq, d)
```

```python
import functools
import math

import jax
import jax.numpy as jnp
from jax import lax
from jax.experimental import pallas as pl
from jax.experimental.pallas import tpu as pltpu

D_MODEL = 2048
SEQ = 2048
POOL_WIDTH = D_MODEL // 2
POOL_WINDOWS = (2, 4, 8, 16)
POOL_GROUP_DIM = POOL_WIDTH // len(POOL_WINDOWS)
HEAD_DIM = 128
SB_WIDTH = D_MODEL // 2
SB_HEADS = SB_WIDTH // HEAD_DIM
IN_WIDTH = POOL_WIDTH + 3 * SB_WIDTH + 2 * D_MODEL
D_FF = 4 * D_MODEL
N_MOD = 6
EPS = 1e-6

Q_OFF = POOL_WIDTH
K_OFF = Q_OFF + SB_WIDTH
V_OFF = K_OFF + SB_WIDTH
GA_OFF = V_OFF + SB_WIDTH
GB_OFF = GA_OFF + D_MODEL

LANES = 128
VMEM_LIMIT_BYTES = 56 * 1024 * 1024

BF16 = jnp.bfloat16
F32 = jnp.float32


def _params(*semantics, flags=None):
    return pltpu.CompilerParams(dimension_semantics=semantics,
                                vmem_limit_bytes=VMEM_LIMIT_BYTES, flags=flags)


def _ada_kernel(c_ref, w_ref, b_ref, o_ref):
    c = c_ref[...]
    sc = (c * jax.nn.sigmoid(c)).astype(BF16)
    acc = jnp.dot(sc, w_ref[...].astype(BF16), preferred_element_type=F32)
    o_ref[...] = acc + b_ref[...]


def _ada(c, w_ada, b_ada):
    batch, d = c.shape
    n = w_ada.shape[1]
    tn = 1024
    rows = 8
    c_pad = jnp.zeros((rows, d), F32).at[:batch].set(c)
    out = pl.pallas_call(
        _ada_kernel,
        grid=(n // tn,),
        in_specs=[
            pl.BlockSpec((rows, d), lambda j: (0, 0)),
            pl.BlockSpec((d, tn), lambda j: (0, j)),
            pl.BlockSpec((1, tn), lambda j: (0, j)),
        ],
        out_specs=pl.BlockSpec((rows, tn), lambda j: (0, j)),
        out_shape=jax.ShapeDtypeStruct((rows, n), F32),
        compiler_params=_params("arbitrary"),
        name="ada",
    )(c_pad, w_ada, b_ada.reshape(1, n))
    return out[:batch].reshape(batch, N_MOD, d)


NORM_ROWS = 256


def _norm_modulate_rows(x_ref, o_ref, norm_w, scale, shift):
    def body(r, carry):
        rows = pl.ds(pl.multiple_of(r * NORM_ROWS, NORM_ROWS), NORM_ROWS)
        x = x_ref[rows, :]
        y = x * lax.rsqrt(jnp.mean(x * x, axis=-1, keepdims=True) + EPS) * norm_w
        o_ref[rows, :] = (y * (1.0 + scale) + shift).astype(o_ref.dtype)
        return carry

    lax.fori_loop(0, x_ref.shape[0] // NORM_ROWS, body, 0)


def _head_norm(blk, w_ref):
    ms = jnp.mean(blk * blk, axis=-1, keepdims=True)
    return (blk * lax.rsqrt(ms + EPS) * w_ref[...]).astype(BF16)


def _in_proj_kernel(x_ref, mod_ref, nw_ref, wu_ref, wq_ref, wk_ref, wv_ref, wga_ref, wgb_ref,
                    qw_ref, kw_ref, u_ref, q_ref, k_ref, v_ref, ga_ref, gb_ref, h_scr):
    @pl.when(pl.program_id(1) == 0)
    def _():
        _norm_modulate_rows(x_ref, h_scr, nw_ref[...], mod_ref[0, 1:2, :], mod_ref[0, 0:1, :])

    h = h_scr[...]

    def project(*w_refs):
        w = jnp.concatenate([w_ref[...].astype(BF16) for w_ref in w_refs], axis=1)
        return jnp.dot(h, w, preferred_element_type=F32)

    qk = project(wq_ref, wk_ref)
    q_ref[...] = _head_norm(qk[:, :HEAD_DIM], qw_ref)
    k_ref[...] = _head_norm(qk[:, HEAD_DIM:], kw_ref)
    ga_ref[...] = jax.nn.sigmoid(project(wga_ref)).astype(BF16)
    gb_ref[...] = jax.nn.sigmoid(project(wgb_ref)).astype(BF16)
    uv = project(wu_ref, wv_ref)
    u_ref[...] = uv[:, :HEAD_DIM]
    v_ref[...] = uv[:, HEAD_DIM:].astype(BF16)


def _in_proj(x2, mod, norm_w, w_in, q_norm_w, k_norm_w, *, tm):
    m, d = x2.shape
    hd = HEAD_DIM
    gw = D_MODEL // SB_HEADS
    w_spec = lambda width, off: pl.BlockSpec((d, width), lambda i, j: (0, off // width + j))
    out_spec = lambda width: pl.BlockSpec((tm, width), lambda i, j: (i, j))
    return pl.pallas_call(
        _in_proj_kernel,
        grid=(m // tm, SB_HEADS),
        in_specs=[
            pl.BlockSpec((tm, d), lambda i, j: (i, 0)),
            pl.BlockSpec((1, N_MOD, d), lambda i, j: (i * tm // SEQ, 0, 0)),
            pl.BlockSpec((1, d), lambda i, j: (0, 0)),
            w_spec(hd, 0), w_spec(hd, Q_OFF), w_spec(hd, K_OFF), w_spec(hd, V_OFF),
            w_spec(gw, GA_OFF), w_spec(gw, GB_OFF),
            pl.BlockSpec((1, hd), lambda i, j: (0, 0)),
            pl.BlockSpec((1, hd), lambda i, j: (0, 0)),
        ],
        out_specs=[out_spec(hd), out_spec(hd), out_spec(hd), out_spec(hd),
                   out_spec(gw), out_spec(gw)],
        out_shape=[jax.ShapeDtypeStruct((m, POOL_WIDTH), F32),
                   jax.ShapeDtypeStruct((m, SB_WIDTH), BF16),
                   jax.ShapeDtypeStruct((m, SB_WIDTH), BF16),
                   jax.ShapeDtypeStruct((m, SB_WIDTH), BF16),
                   jax.ShapeDtypeStruct((m, D_MODEL), BF16),
                   jax.ShapeDtypeStruct((m, D_MODEL), BF16)],
        scratch_shapes=[pltpu.VMEM((tm, d), BF16)],
        compiler_params=_params("arbitrary", "arbitrary"),
        name="in_proj",
    )(x2, mod, norm_w.reshape(1, d), w_in, w_in, w_in, w_in, w_in, w_in,
      q_norm_w.reshape(1, hd), k_norm_w.reshape(1, hd))


FFN_OUT_CHUNK = 512


def _ffn_kernel(h_ref, x1_hbm, mod_ref, w1_ref, w2_ref, o_ref, xbuf, sem, *, tm, xrows):
    i = pl.program_id(0)
    c = pl.program_id(1)
    x1_copy = pltpu.make_async_copy(x1_hbm.at[pl.ds(i * tm + c * xrows, xrows), :], xbuf, sem)
    x1_copy.start()

    @pl.when(c == 0)
    def _():
        o_ref[...] = jnp.zeros_like(o_ref)

    a = jnp.dot(h_ref[...], w1_ref[...].astype(BF16), preferred_element_type=F32)
    r = jnp.maximum(a, 0.0)
    act = (r * r).astype(BF16)
    for n0 in range(0, o_ref.shape[1], FFN_OUT_CHUNK):
        cols = slice(n0, n0 + FFN_OUT_CHUNK)
        y = jnp.dot(act, w2_ref[:, cols].astype(BF16), preferred_element_type=F32)
        o_ref[:, cols] += mod_ref[0, 5:6, cols] * y
    x1_copy.wait()
    rows = pl.ds(pl.multiple_of(c * xrows, xrows), xrows)
    o_ref[rows, :] += xbuf[...]


def _ffn(h2, x1, mod, w1, w2, *, tm, tf):
    m, d = h2.shape
    f = w1.shape[1]
    xrows = tm // (f // tf)
    return pl.pallas_call(
        functools.partial(_ffn_kernel, tm=tm, xrows=xrows),
        grid=(m // tm, f // tf),
        in_specs=[
            pl.BlockSpec((tm, d), lambda i, c: (i, 0)),
            pl.BlockSpec(memory_space=pl.ANY),
            pl.BlockSpec((1, N_MOD, d), lambda i, c: (i * tm // SEQ, 0, 0)),
            pl.BlockSpec((d, tf), lambda i, c: (0, c)),
            pl.BlockSpec((tf, d), lambda i, c: (c, 0)),
        ],
        out_specs=pl.BlockSpec((tm, d), lambda i, c: (i, 0)),
        out_shape=jax.ShapeDtypeStruct((m, d), F32),
        scratch_shapes=[pltpu.VMEM((xrows, d), F32), pltpu.SemaphoreType.DMA(())],
        compiler_params=_params("arbitrary", "arbitrary"),
        name="ffn",
    )(h2, x1, mod, w1, w2)


POOL_PAD = 8
POOL_HEAD = max(POOL_WINDOWS)


def _pool_kernel(u_ref, w_ref, s_ref, o_ref, buf_a, buf_b):
    gd = POOL_GROUP_DIM
    body = pl.ds(POOL_PAD, SEQ)
    buf_a[0:POOL_PAD, :] = jnp.zeros((POOL_PAD, gd), F32)
    buf_b[0:POOL_PAD, :] = jnp.zeros((POOL_PAD, gd), F32)
    pos = lax.broadcasted_iota(jnp.int32, (POOL_HEAD, gd), 0)
    for g, w in enumerate(POOL_WINDOWS):
        cols = slice(g * gd, (g + 1) * gd)
        src, dst = buf_a, buf_b
        src[body, :] = u_ref[:, cols]
        shift = 1
        while shift < w:
            dst[body, :] = src[body, :] + src[pl.ds(POOL_PAD - shift, SEQ), :]
            src, dst = dst, src
            shift *= 2
        dst[body, :] = src[body, :] * (1.0 / w) - u_ref[:, cols]
        count = jnp.minimum(pos + 1, w).astype(F32)
        head = pl.ds(POOL_PAD, POOL_HEAD)
        dst[head, :] = src[head, :] / count - u_ref[0:POOL_HEAD, cols]
        mixed = jnp.dot(dst[body, :].astype(BF16), w_ref[g], preferred_element_type=F32)
        o_ref[:, cols] = (mixed * s_ref[:, cols]).astype(o_ref.dtype)


def _pool(u, w_pool_bf, pool_scale, batch):
    groups = len(POOL_WINDOWS)
    gd = POOL_GROUP_DIM
    return pl.pallas_call(
        _pool_kernel,
        grid=(batch,),
        in_specs=[
            pl.BlockSpec((SEQ, POOL_WIDTH), lambda b: (b, 0)),
            pl.BlockSpec((groups, gd, gd), lambda b: (0, 0, 0)),
            pl.BlockSpec((1, POOL_WIDTH), lambda b: (0, 0)),
        ],
        out_specs=pl.BlockSpec((SEQ, POOL_WIDTH), lambda b: (b, 0)),
        out_shape=jax.ShapeDtypeStruct((batch * SEQ, POOL_WIDTH), BF16),
        scratch_shapes=[pltpu.VMEM((POOL_PAD + SEQ, gd), F32),
                        pltpu.VMEM((POOL_PAD + SEQ, gd), F32)],
        compiler_params=_params("arbitrary"),
        name="pool",
    )(u, w_pool_bf, pool_scale.reshape(1, POOL_WIDTH))


EXP2_UNDERFLOW = -151.0


def _attn_kernel(q_ref, k_ref, v_ref, *refs, tb, hp, cast_steps):
    n_cast = len(cast_steps)
    w_refs, o_ref, wbf_refs = refs[:n_cast], refs[n_cast], refs[n_cast + 1:2 * n_cast + 1]
    carry_ref, acc_ref = refs[2 * n_cast + 1:]
    step = ((pl.program_id(0) * pl.num_programs(1) + pl.program_id(1)) * pl.num_programs(2)
            + pl.program_id(2))
    first = 0
    for w_ref, wbf_ref, n in zip(w_refs, wbf_refs, cast_steps):
        @pl.when(jnp.logical_and(step >= first, step < first + n))
        def _(w_ref=w_ref, wbf_ref=wbf_ref):
            wbf_ref[...] = w_ref[...].astype(BF16)
        first += n

    qi = pl.program_id(2)
    scale = math.log2(math.e) / math.sqrt(HEAD_DIM)
    nseg = tb // LANES

    r = lax.broadcasted_iota(jnp.int32, (2 * LANES, 2 * LANES), 0)
    c = lax.broadcasted_iota(jnp.int32, (2 * LANES, 2 * LANES), 1)
    cum_op = jnp.where(jnp.logical_or(c >= LANES, jnp.bitwise_and(r, LANES - 1) > c),
                       1.0, 0.0).astype(BF16)
    tri_r = lax.broadcasted_iota(jnp.int32, (tb, tb), 0)
    tri_c = lax.broadcasted_iota(jnp.int32, (tb, tb), 1)
    causal = tri_c < tri_r

    def block(kb, diagonal):
        start = pl.multiple_of(kb * tb, tb)
        heads = [slice(p * HEAD_DIM, (p + 1) * HEAD_DIM) for p in range(hp)]
        segs = [slice(sg * LANES, (sg + 1) * LANES) for sg in range(nseg)]
        zs = [lax.dot_general(q_ref[:, cols], k_ref[pl.ds(start, tb), cols],
                              (((1,), (1,)), ((), ())), preferred_element_type=F32)
              for cols in heads]
        log_betas, sums = [], []
        for z in zs:
            z = z * scale
            log_beta = jnp.minimum(z, 0.0) - jnp.log2(1.0 + jnp.exp2(-jnp.abs(z)))
            l = log_beta - z
            if diagonal:
                l = jnp.where(causal, l, 0.0)
            cs = []
            for seg in segs:
                l_hi = l[:, seg].astype(BF16)
                l_lo = (l[:, seg] - l_hi.astype(F32)).astype(BF16)
                cs.append(jnp.dot(jnp.concatenate([l_hi, l_lo], axis=1), cum_op,
                                  preferred_element_type=F32))
            log_betas.append(log_beta)
            sums.append(cs)
        top = None
        for p, cols in enumerate(heads):
            carry = carry_ref[p]
            a_parts = [None] * nseg
            for sg in range(nseg - 1, -1, -1):
                cs = sums[p][sg]
                a_parts[sg] = jnp.exp2(log_betas[p][:, segs[sg]] + (cs[:, :LANES] + carry))
                carry = carry + cs[:, LANES:]
            a = jnp.concatenate(a_parts, axis=1)
            if diagonal:
                a = jnp.where(causal, a, 0.0)
            acc_ref[p] += jnp.dot(a.astype(BF16), v_ref[pl.ds(start, tb), cols],
                                  preferred_element_type=F32)
            carry_ref[p] = carry
            top = jnp.max(carry) if top is None else jnp.maximum(top, jnp.max(carry))
        return top

    carry_ref[...] = jnp.zeros_like(carry_ref)
    acc_ref[...] = jnp.zeros_like(acc_ref)
    top = block(qi, True)

    def cond(state):
        kb, top = state
        return jnp.logical_and(kb >= 0, top > EXP2_UNDERFLOW)

    def body(state):
        kb, _ = state
        return kb - 1, block(kb, False)

    lax.while_loop(cond, body, (qi - 1, top))
    for p in range(hp):
        o_ref[:, p * HEAD_DIM:(p + 1) * HEAD_DIM] = acc_ref[p].astype(o_ref.dtype)


def _attn(q, k, v, weights, batch, *, tb, hp):
    nq = SEQ // tb
    groups = SB_HEADS // hp
    width = hp * HEAD_DIM
    steps = batch * groups * nq
    cast_rows = sum(w.shape[0] for w in weights) // steps
    cast_steps = tuple(w.shape[0] // cast_rows for w in weights)
    assert sum(cast_steps) == steps and all(w.shape[0] % cast_rows == 0 for w in weights)

    def cast_spec(first, n, cols):
        def index(b, h, i):
            step = (b * groups + h) * nq + i
            return (jnp.clip(step - first, 0, n - 1), 0)
        return pl.BlockSpec((cast_rows, cols), index)

    firsts = [sum(cast_steps[:j]) for j in range(len(weights))]
    cast_specs = [cast_spec(f, n, w.shape[1]) for f, n, w in zip(firsts, cast_steps, weights)]
    outs = pl.pallas_call(
        functools.partial(_attn_kernel, tb=tb, hp=hp, cast_steps=cast_steps),
        grid=(batch, groups, nq),
        in_specs=[
            pl.BlockSpec((tb, width), lambda b, h, i: (b * nq + i, h)),
            pl.BlockSpec((SEQ, width), lambda b, h, i: (b, h)),
            pl.BlockSpec((SEQ, width), lambda b, h, i: (b, h)),
        ] + cast_specs,
        out_specs=[pl.BlockSpec((tb, width), lambda b, h, i: (b * nq + i, h))] + cast_specs,
        out_shape=[jax.ShapeDtypeStruct((batch * SEQ, SB_WIDTH), BF16)]
        + [jax.ShapeDtypeStruct(w.shape, BF16) for w in weights],
        scratch_shapes=[pltpu.VMEM((hp, tb, LANES), F32),
                        pltpu.VMEM((hp, tb, HEAD_DIM), F32)],
        compiler_params=_params("arbitrary", "arbitrary", "arbitrary"),
        name="attn",
    )(q, k, v, *weights)
    return outs[0], outs[1:]


MIX_CHUNK = 512


def _mix_kernel(pa_ref, at_ref, sa_ref, sb_ref, x_ref, mod_ref, nw_ref, wa_ref, wb_ref, wo_ref,
                x1_ref, h2_ref, merged_scr):
    d = x_ref.shape[1]
    pa = pa_ref[...]
    at = at_ref[...]
    for n0 in range(0, d, MIX_CHUNK):
        cols = slice(n0, n0 + MIX_CHUNK)
        ya = jnp.dot(pa, wa_ref[:, cols], preferred_element_type=F32)
        yb = jnp.dot(at, wb_ref[:, cols], preferred_element_type=F32)
        merged = sa_ref[:, cols].astype(F32) * ya + sb_ref[:, cols].astype(F32) * yb
        merged_scr[:, cols] = merged.astype(BF16)
    merged = merged_scr[...]
    for n0 in range(0, d, MIX_CHUNK):
        cols = slice(n0, n0 + MIX_CHUNK)
        o = jnp.dot(merged, wo_ref[:, cols], preferred_element_type=F32)
        x1_ref[:, cols] = x_ref[:, cols] + mod_ref[0, 2:3, cols] * o
    _norm_modulate_rows(x1_ref, h2_ref, nw_ref[...], mod_ref[0, 4:5, :], mod_ref[0, 3:4, :])


def _mix(pa, at, sa, sb, x2, mod, norm_w, wa_bf, wb_bf, wo_bf, *, tm):
    m, d = x2.shape
    kdim = pa.shape[1]
    resident = pl.Buffered(1)
    return pl.pallas_call(
        _mix_kernel,
        grid=(m // tm,),
        in_specs=[
            pl.BlockSpec((tm, kdim), lambda i: (i, 0)),
            pl.BlockSpec((tm, kdim), lambda i: (i, 0)),
            pl.BlockSpec((tm, d), lambda i: (i, 0)),
            pl.BlockSpec((tm, d), lambda i: (i, 0)),
            pl.BlockSpec((tm, d), lambda i: (i, 0)),
            pl.BlockSpec((1, N_MOD, d), lambda i: (i * tm // SEQ, 0, 0)),
            pl.BlockSpec((1, d), lambda i: (0, 0)),
            pl.BlockSpec((kdim, d), lambda i: (0, 0), pipeline_mode=resident),
            pl.BlockSpec((kdim, d), lambda i: (0, 0), pipeline_mode=resident),
            pl.BlockSpec((d, d), lambda i: (0, 0), pipeline_mode=resident),
        ],
        out_specs=[pl.BlockSpec((tm, d), lambda i: (i, 0)),
                   pl.BlockSpec((tm, d), lambda i: (i, 0))],
        out_shape=[jax.ShapeDtypeStruct((m, d), F32),
                   jax.ShapeDtypeStruct((m, d), BF16)],
        scratch_shapes=[pltpu.VMEM((tm, d), BF16)],
        compiler_params=_params("arbitrary"),
        name="mix",
    )(pa, at, sa, sb, x2, mod, norm_w.reshape(1, d), wa_bf, wb_bf, wo_bf)


def kernel(x, c, w_ada, b_ada, norm1_w, w_in, q_norm_w, k_norm_w, w_pool, pool_scale,
           w_a_up, w_b_up, w_o, norm2_w, w_ff1, w_ff2):
    batch, seq, d = x.shape
    assert (seq, d) == (SEQ, D_MODEL) and w_ada.shape[0] == 1
    x2 = x.reshape(batch * seq, d)

    mod = _ada(c, w_ada[0], b_ada[0])
    u, q, k, v, sa, sb = _in_proj(x2, mod, norm1_w[0], w_in[0], q_norm_w[0], k_norm_w[0],
                                  tm=1024)
    pa = _pool(u, w_pool[0].astype(BF16), pool_scale[0], batch)
    at, (wa_bf, wb_bf, wo_bf) = _attn(q, k, v, (w_a_up[0], w_b_up[0], w_o[0]), batch,
                                      tb=256, hp=8)
    x1, h2 = _mix(pa, at, sa, sb, x2, mod, norm2_w[0], wa_bf, wb_bf, wo_bf, tm=256)
    out = _ffn(h2, x1, mod, w_ff1[0], w_ff2[0], tm=1024, tf=512)
    return out.reshape(batch, seq, d)
```

```python
import functools
import math

import jax
import jax.numpy as jnp
from jax import lax
from jax.experimental import pallas as pl
from jax.experimental.pallas import tpu as pltpu

D_MODEL = 2048
SEQ = 2048
POOL_WIDTH = D_MODEL // 2
POOL_WINDOWS = (2, 4, 8, 16)
POOL_GROUP_DIM = POOL_WIDTH // len(POOL_WINDOWS)
HEAD_DIM = 128
SB_WIDTH = D_MODEL // 2
SB_HEADS = SB_WIDTH // HEAD_DIM
IN_WIDTH = POOL_WIDTH + 3 * SB_WIDTH + 2 * D_MODEL
D_FF = 4 * D_MODEL
N_MOD = 6
EPS = 1e-6

Q_OFF = POOL_WIDTH
K_OFF = Q_OFF + SB_WIDTH
V_OFF = K_OFF + SB_WIDTH
GA_OFF = V_OFF + SB_WIDTH
GB_OFF = GA_OFF + D_MODEL

LANES = 128
VMEM_LIMIT_BYTES = 56 * 1024 * 1024

BF16 = jnp.bfloat16
F32 = jnp.float32


def _params(*semantics, flags=None):
    return pltpu.CompilerParams(dimension_semantics=semantics,
                                vmem_limit_bytes=VMEM_LIMIT_BYTES, flags=flags)


def _ada_kernel(c_ref, w_ref, b_ref, o_ref):
    c = c_ref[...]
    sc = (c * jax.nn.sigmoid(c)).astype(BF16)
    acc = jnp.dot(sc, w_ref[...].astype(BF16), preferred_element_type=F32)
    o_ref[...] = acc + b_ref[...]


def _ada(c, w_ada, b_ada):
    batch, d = c.shape
    n = w_ada.shape[1]
    tn = 1024
    rows = 8
    c_pad = jnp.zeros((rows, d), F32).at[:batch].set(c)
    out = pl.pallas_call(
        _ada_kernel,
        grid=(n // tn,),
        in_specs=[
            pl.BlockSpec((rows, d), lambda j: (0, 0)),
            pl.BlockSpec((d, tn), lambda j: (0, j)),
            pl.BlockSpec((1, tn), lambda j: (0, j)),
        ],
        out_specs=pl.BlockSpec((rows, tn), lambda j: (0, j)),
        out_shape=jax.ShapeDtypeStruct((rows, n), F32),
        compiler_params=_params("arbitrary"),
        name="ada",
    )(c_pad, w_ada, b_ada.reshape(1, n))
    return out[:batch].reshape(batch, N_MOD, d)


NORM_ROWS = 256


def _norm_modulate_rows(x_ref, o_ref, norm_w, scale, shift):
    def body(r, carry):
        rows = pl.ds(pl.multiple_of(r * NORM_ROWS, NORM_ROWS), NORM_ROWS)
        x = x_ref[rows, :]
        y = x * lax.rsqrt(jnp.mean(x * x, axis=-1, keepdims=True) + EPS) * norm_w
        o_ref[rows, :] = (y * (1.0 + scale) + shift).astype(o_ref.dtype)
        return carry

    lax.fori_loop(0, x_ref.shape[0] // NORM_ROWS, body, 0)


Q_LOGIT_SCALE = math.log2(math.e) / math.sqrt(HEAD_DIM)


def _head_norm(blk, gain):
    ms = jnp.mean(blk * blk, axis=-1, keepdims=True)
    return (blk * lax.rsqrt(ms + EPS) * gain).astype(BF16)


def _in_proj_kernel(x_ref, mod_ref, nw_ref, wu_ref, wq_ref, wk_ref, wv_ref, wga_ref, wgb_ref,
                    qw_ref, kw_ref, u_ref, q_ref, k_ref, v_ref, ga_ref, gb_ref, h_scr):
    @pl.when(pl.program_id(1) == 0)
    def _():
        _norm_modulate_rows(x_ref, h_scr, nw_ref[...], mod_ref[0, 1:2, :], mod_ref[0, 0:1, :])

    h = h_scr[...]

    def project(*w_refs):
        w = jnp.concatenate([w_ref[...].astype(BF16) for w_ref in w_refs], axis=1)
        return jnp.dot(h, w, preferred_element_type=F32)

    qk = project(wq_ref, wk_ref)
    q_ref[...] = _head_norm(qk[:, :HEAD_DIM], qw_ref[...] * Q_LOGIT_SCALE)
    k_ref[...] = _head_norm(qk[:, HEAD_DIM:], kw_ref[...])
    ga_ref[...] = jax.nn.sigmoid(project(wga_ref)).astype(BF16)
    gb_ref[...] = jax.nn.sigmoid(project(wgb_ref)).astype(BF16)
    uv = project(wu_ref, wv_ref)
    u_ref[...] = uv[:, :HEAD_DIM]
    v_ref[...] = uv[:, HEAD_DIM:].astype(BF16)


def _in_proj(x2, mod, norm_w, w_in, q_norm_w, k_norm_w, *, tm):
    m, d = x2.shape
    hd = HEAD_DIM
    gw = D_MODEL // SB_HEADS
    w_spec = lambda width, off: pl.BlockSpec((d, width), lambda i, j: (0, off // width + j))
    out_spec = lambda width: pl.BlockSpec((tm, width), lambda i, j: (i, j))
    return pl.pallas_call(
        _in_proj_kernel,
        grid=(m // tm, SB_HEADS),
        in_specs=[
            pl.BlockSpec((tm, d), lambda i, j: (i, 0)),
            pl.BlockSpec((1, N_MOD, d), lambda i, j: (i * tm // SEQ, 0, 0)),
            pl.BlockSpec((1, d), lambda i, j: (0, 0)),
            w_spec(hd, 0), w_spec(hd, Q_OFF), w_spec(hd, K_OFF), w_spec(hd, V_OFF),
            w_spec(gw, GA_OFF), w_spec(gw, GB_OFF),
            pl.BlockSpec((1, hd), lambda i, j: (0, 0)),
            pl.BlockSpec((1, hd), lambda i, j: (0, 0)),
        ],
        out_specs=[out_spec(hd), out_spec(hd), out_spec(hd), out_spec(hd),
                   out_spec(gw), out_spec(gw)],
        out_shape=[jax.ShapeDtypeStruct((m, POOL_WIDTH), F32),
                   jax.ShapeDtypeStruct((m, SB_WIDTH), BF16),
                   jax.ShapeDtypeStruct((m, SB_WIDTH), BF16),
                   jax.ShapeDtypeStruct((m, SB_WIDTH), BF16),
                   jax.ShapeDtypeStruct((m, D_MODEL), BF16),
                   jax.ShapeDtypeStruct((m, D_MODEL), BF16)],
        scratch_shapes=[pltpu.VMEM((tm, d), BF16)],
        compiler_params=_params("arbitrary", "arbitrary"),
        name="in_proj",
    )(x2, mod, norm_w.reshape(1, d), w_in, w_in, w_in, w_in, w_in, w_in,
      q_norm_w.reshape(1, hd), k_norm_w.reshape(1, hd))


FFN_OUT_CHUNK = 512


def _ffn_kernel(h_ref, x1_hbm, mod_ref, w1_ref, w2_ref, o_ref, xbuf, sem, *, tm, xrows):
    i = pl.program_id(0)
    c = pl.program_id(1)
    x1_copy = pltpu.make_async_copy(x1_hbm.at[pl.ds(i * tm + c * xrows, xrows), :], xbuf, sem)
    x1_copy.start()

    @pl.when(c == 0)
    def _():
        o_ref[...] = jnp.zeros_like(o_ref)

    a = jnp.dot(h_ref[...], w1_ref[...], preferred_element_type=F32)
    r = jnp.maximum(a, 0.0)
    act = (r * r).astype(BF16)
    for n0 in range(0, o_ref.shape[1], FFN_OUT_CHUNK):
        cols = slice(n0, n0 + FFN_OUT_CHUNK)
        y = jnp.dot(act, w2_ref[:, cols], preferred_element_type=F32)
        o_ref[:, cols] += mod_ref[0, 5:6, cols] * y
    x1_copy.wait()
    rows = pl.ds(pl.multiple_of(c * xrows, xrows), xrows)
    o_ref[rows, :] += xbuf[...]


def _ffn(h2, x1, mod, w1, w2, *, tm, tf):
    m, d = h2.shape
    f = w1.shape[1]
    xrows = tm // (f // tf)
    return pl.pallas_call(
        functools.partial(_ffn_kernel, tm=tm, xrows=xrows),
        grid=(m // tm, f // tf),
        in_specs=[
            pl.BlockSpec((tm, d), lambda i, c: (i, 0)),
            pl.BlockSpec(memory_space=pl.ANY),
            pl.BlockSpec((1, N_MOD, d), lambda i, c: (i * tm // SEQ, 0, 0)),
            pl.BlockSpec((d, tf), lambda i, c: (0, c)),
            pl.BlockSpec((tf, d), lambda i, c: (c, 0)),
        ],
        out_specs=pl.BlockSpec((tm, d), lambda i, c: (i, 0)),
        out_shape=jax.ShapeDtypeStruct((m, d), F32),
        scratch_shapes=[pltpu.VMEM((xrows, d), F32), pltpu.SemaphoreType.DMA(())],
        compiler_params=_params("arbitrary", "arbitrary"),
        name="ffn",
    )(h2, x1, mod, w1, w2)


POOL_PAD = 8
POOL_HEAD = max(POOL_WINDOWS)


def _pool_kernel(u_ref, w_ref, s_ref, o_ref, buf_a, buf_b):
    gd = POOL_GROUP_DIM
    body = pl.ds(POOL_PAD, SEQ)
    buf_a[0:POOL_PAD, :] = jnp.zeros((POOL_PAD, gd), F32)
    buf_b[0:POOL_PAD, :] = jnp.zeros((POOL_PAD, gd), F32)
    pos = lax.broadcasted_iota(jnp.int32, (POOL_HEAD, gd), 0)
    for g, w in enumerate(POOL_WINDOWS):
        cols = slice(g * gd, (g + 1) * gd)
        src, dst = buf_a, buf_b
        src[body, :] = u_ref[:, cols]
        shift = 1
        while shift < w:
            dst[body, :] = src[body, :] + src[pl.ds(POOL_PAD - shift, SEQ), :]
            src, dst = dst, src
            shift *= 2
        dst[body, :] = src[body, :] * (1.0 / w) - u_ref[:, cols]
        count = jnp.minimum(pos + 1, w).astype(F32)
        head = pl.ds(POOL_PAD, POOL_HEAD)
        dst[head, :] = src[head, :] / count - u_ref[0:POOL_HEAD, cols]
        mixed = jnp.dot(dst[body, :].astype(BF16), w_ref[g], preferred_element_type=F32)
        o_ref[:, cols] = (mixed * s_ref[:, cols]).astype(o_ref.dtype)


def _pool(u, w_pool_bf, pool_scale, batch):
    groups = len(POOL_WINDOWS)
    gd = POOL_GROUP_DIM
    return pl.pallas_call(
        _pool_kernel,
        grid=(batch,),
        in_specs=[
            pl.BlockSpec((SEQ, POOL_WIDTH), lambda b: (b, 0)),
            pl.BlockSpec((groups, gd, gd), lambda b: (0, 0, 0)),
            pl.BlockSpec((1, POOL_WIDTH), lambda b: (0, 0)),
        ],
        out_specs=pl.BlockSpec((SEQ, POOL_WIDTH), lambda b: (b, 0)),
        out_shape=jax.ShapeDtypeStruct((batch * SEQ, POOL_WIDTH), BF16),
        scratch_shapes=[pltpu.VMEM((POOL_PAD + SEQ, gd), F32),
                        pltpu.VMEM((POOL_PAD + SEQ, gd), F32)],
        compiler_params=_params("arbitrary"),
        name="pool",
    )(u, w_pool_bf, pool_scale.reshape(1, POOL_WIDTH))


EXP2_UNDERFLOW = -151.0


def _attn_kernel(q_ref, k_ref, v_ref, *refs, tb, hp, n_cast):
    w_refs, o_ref, wbf_refs = refs[:n_cast], refs[n_cast], refs[n_cast + 1:2 * n_cast + 1]
    carry_ref, acc_ref = refs[2 * n_cast + 1:]
    for w_ref, wbf_ref in zip(w_refs, wbf_refs):
        wbf_ref[...] = w_ref[...].astype(BF16)

    qi = pl.program_id(2)
    nseg = tb // LANES

    r = lax.broadcasted_iota(jnp.int32, (2 * LANES, 2 * LANES), 0)
    c = lax.broadcasted_iota(jnp.int32, (2 * LANES, 2 * LANES), 1)
    cum_op = jnp.where(jnp.logical_or(c >= LANES, jnp.bitwise_and(r, LANES - 1) > c),
                       1.0, 0.0).astype(BF16)
    tri_r = lax.broadcasted_iota(jnp.int32, (LANES, LANES), 0)
    tri_c = lax.broadcasted_iota(jnp.int32, (LANES, LANES), 1)
    causal = tri_c < tri_r
    segs = [slice(sg * LANES, (sg + 1) * LANES) for sg in range(nseg)]

    def block(kb, diagonal):
        start = pl.multiple_of(kb * tb, tb)
        if diagonal:
            parts = [(segs[g], g + 1) for g in range(nseg)]
        else:
            parts = [(slice(0, tb), nseg)]
        work = [(p, slice(p * HEAD_DIM, (p + 1) * HEAD_DIM), rows, nk)
                for rows, nk in parts for p in range(hp)]
        zs = [lax.dot_general(q_ref[rows, cols], k_ref[pl.ds(start, nk * LANES), cols],
                              (((1,), (1,)), ((), ())), preferred_element_type=F32)
              for _, cols, rows, nk in work]
        log_betas, sums = [], []
        for z, (_, _, _, nk) in zip(zs, work):
            log_beta = jnp.minimum(z, 0.0) - jnp.log2(1.0 + jnp.exp2(-jnp.abs(z)))
            l = log_beta - z
            cs = []
            for sg in range(nk):
                l_seg = l[:, segs[sg]]
                if diagonal and sg == nk - 1:
                    l_seg = jnp.where(causal, l_seg, 0.0)
                l_hi = l_seg.astype(BF16)
                l_lo = (l_seg - l_hi.astype(F32)).astype(BF16)
                cs.append(jnp.dot(jnp.concatenate([l_hi, l_lo], axis=1), cum_op,
                                  preferred_element_type=F32))
            log_betas.append(log_beta)
            sums.append(cs)
        top = None
        for (p, cols, rows, nk), log_beta, cs in zip(work, log_betas, sums):
            carry = carry_ref[p, rows, :]
            a_parts = [None] * nk
            for sg in range(nk - 1, -1, -1):
                a_seg = jnp.exp2(log_beta[:, segs[sg]] + (cs[sg][:, :LANES] + carry))
                if diagonal and sg == nk - 1:
                    a_seg = jnp.where(causal, a_seg, 0.0)
                a_parts[sg] = a_seg
                carry = carry + cs[sg][:, LANES:]
            a = jnp.concatenate(a_parts, axis=1) if nk > 1 else a_parts[0]
            acc_ref[p, rows, :] += jnp.dot(
                a.astype(BF16), v_ref[pl.ds(start, nk * LANES), cols],
                preferred_element_type=F32)
            carry_ref[p, rows, :] = carry
            top = jnp.max(carry) if top is None else jnp.maximum(top, jnp.max(carry))
        return top

    carry_ref[...] = jnp.zeros_like(carry_ref)
    acc_ref[...] = jnp.zeros_like(acc_ref)
    top = block(qi, True)

    def cond(state):
        kb, top = state
        return jnp.logical_and(kb >= 0, top > EXP2_UNDERFLOW)

    def body(state):
        kb, _ = state
        return kb - 1, block(kb, False)

    lax.while_loop(cond, body, (qi - 1, top))
    for p in range(hp):
        o_ref[:, p * HEAD_DIM:(p + 1) * HEAD_DIM] = acc_ref[p].astype(o_ref.dtype)


def _attn(q, k, v, weights, batch, *, tb, hp):
    nq = SEQ // tb
    groups = SB_HEADS // hp
    width = hp * HEAD_DIM
    steps = batch * groups * nq
    assert all(w.shape[0] % steps == 0 for w in weights)
    cast_specs = [pl.BlockSpec((w.shape[0] // steps, w.shape[1]),
                               lambda b, h, i: ((b * groups + h) * nq + i, 0))
                  for w in weights]
    outs = pl.pallas_call(
        functools.partial(_attn_kernel, tb=tb, hp=hp, n_cast=len(weights)),
        grid=(batch, groups, nq),
        in_specs=[
            pl.BlockSpec((tb, width), lambda b, h, i: (b * nq + i, h)),
            pl.BlockSpec((SEQ, width), lambda b, h, i: (b, h)),
            pl.BlockSpec((SEQ, width), lambda b, h, i: (b, h)),
        ] + cast_specs,
        out_specs=[pl.BlockSpec((tb, width), lambda b, h, i: (b * nq + i, h))] + cast_specs,
        out_shape=[jax.ShapeDtypeStruct((batch * SEQ, SB_WIDTH), BF16)]
        + [jax.ShapeDtypeStruct(w.shape, BF16) for w in weights],
        scratch_shapes=[pltpu.VMEM((hp, tb, LANES), F32),
                        pltpu.VMEM((hp, tb, HEAD_DIM), F32)],
        compiler_params=_params("arbitrary", "arbitrary", "arbitrary"),
        name="attn",
    )(q, k, v, *weights)
    return outs[0], outs[1:]


MIX_CHUNK = 512


def _mix_kernel(pa_ref, at_ref, sa_ref, sb_ref, x_ref, mod_ref, nw_ref, wa_ref, wb_ref, wo_ref,
                x1_ref, h2_ref, merged_scr):
    d = x_ref.shape[1]
    pa = pa_ref[...]
    at = at_ref[...]
    for n0 in range(0, d, MIX_CHUNK):
        cols = slice(n0, n0 + MIX_CHUNK)
        ya = jnp.dot(pa, wa_ref[:, cols], preferred_element_type=F32)
        yb = jnp.dot(at, wb_ref[:, cols], preferred_element_type=F32)
        merged = sa_ref[:, cols].astype(F32) * ya + sb_ref[:, cols].astype(F32) * yb
        merged_scr[:, cols] = merged.astype(BF16)
    merged = merged_scr[...]
    for n0 in range(0, d, MIX_CHUNK):
        cols = slice(n0, n0 + MIX_CHUNK)
        o = jnp.dot(merged, wo_ref[:, cols], preferred_element_type=F32)
        x1_ref[:, cols] = x_ref[:, cols] + mod_ref[0, 2:3, cols] * o
    _norm_modulate_rows(x1_ref, h2_ref, nw_ref[...], mod_ref[0, 4:5, :], mod_ref[0, 3:4, :])


def _mix(pa, at, sa, sb, x2, mod, norm_w, wa_bf, wb_bf, wo_bf, *, tm):
    m, d = x2.shape
    kdim = pa.shape[1]
    resident = pl.Buffered(1)
    return pl.pallas_call(
        _mix_kernel,
        grid=(m // tm,),
        in_specs=[
            pl.BlockSpec((tm, kdim), lambda i: (i, 0)),
            pl.BlockSpec((tm, kdim), lambda i: (i, 0)),
            pl.BlockSpec((tm, d), lambda i: (i, 0)),
            pl.BlockSpec((tm, d), lambda i: (i, 0)),
            pl.BlockSpec((tm, d), lambda i: (i, 0)),
            pl.BlockSpec((1, N_MOD, d), lambda i: (i * tm // SEQ, 0, 0)),
            pl.BlockSpec((1, d), lambda i: (0, 0)),
            pl.BlockSpec((kdim, d), lambda i: (0, 0), pipeline_mode=resident),
            pl.BlockSpec((kdim, d), lambda i: (0, 0), pipeline_mode=resident),
            pl.BlockSpec((d, d), lambda i: (0, 0), pipeline_mode=resident),
        ],
        out_specs=[pl.BlockSpec((tm, d), lambda i: (i, 0)),
                   pl.BlockSpec((tm, d), lambda i: (i, 0))],
        out_shape=[jax.ShapeDtypeStruct((m, d), F32),
                   jax.ShapeDtypeStruct((m, d), BF16)],
        scratch_shapes=[pltpu.VMEM((tm, d), BF16)],
        compiler_params=_params("arbitrary"),
        name="mix",
    )(pa, at, sa, sb, x2, mod, norm_w.reshape(1, d), wa_bf, wb_bf, wo_bf)


def kernel(x, c, w_ada, b_ada, norm1_w, w_in, q_norm_w, k_norm_w, w_pool, pool_scale,
           w_a_up, w_b_up, w_o, norm2_w, w_ff1, w_ff2):
    batch, seq, d = x.shape
    assert (seq, d) == (SEQ, D_MODEL) and w_ada.shape[0] == 1
    x2 = x.reshape(batch * seq, d)

    mod = _ada(c, w_ada[0], b_ada[0])
    u, q, k, v, sa, sb = _in_proj(x2, mod, norm1_w[0], w_in[0], q_norm_w[0], k_norm_w[0],
                                  tm=1024)
    pa = _pool(u, w_pool[0].astype(BF16), pool_scale[0], batch)
    at, (wa_bf, wb_bf, wo_bf, w1_bf, w2_bf) = _attn(
        q, k, v, (w_a_up[0], w_b_up[0], w_o[0], w_ff1[0], w_ff2[0]), batch,
        tb=256, hp=8)
    x1, h2 = _mix(pa, at, sa, sb, x2, mod, norm2_w[0], wa_bf, wb_bf, wo_bf, tm=256)
    out = _ffn(h2, x1, mod, w1_bf, w2_bf, tm=1024, tf=1024)
    return out.reshape(batch, seq, d)
```

```python
import functools
import math

import jax
import jax.numpy as jnp
from jax import lax
from jax.experimental import pallas as pl
from jax.experimental.pallas import tpu as pltpu

D_MODEL = 2048
SEQ = 2048
POOL_WIDTH = D_MODEL // 2
POOL_WINDOWS = (2, 4, 8, 16)
POOL_GROUP_DIM = POOL_WIDTH // len(POOL_WINDOWS)
HEAD_DIM = 128
SB_WIDTH = D_MODEL // 2
SB_HEADS = SB_WIDTH // HEAD_DIM
IN_WIDTH = POOL_WIDTH + 3 * SB_WIDTH + 2 * D_MODEL
D_FF = 4 * D_MODEL
N_MOD = 6
EPS = 1e-6

Q_OFF = POOL_WIDTH
K_OFF = Q_OFF + SB_WIDTH
V_OFF = K_OFF + SB_WIDTH
GA_OFF = V_OFF + SB_WIDTH
GB_OFF = GA_OFF + D_MODEL

LANES = 128
VMEM_LIMIT_BYTES = 56 * 1024 * 1024

BF16 = jnp.bfloat16
F32 = jnp.float32


def _params(*semantics, flags=None):
    return pltpu.CompilerParams(dimension_semantics=semantics,
                                vmem_limit_bytes=VMEM_LIMIT_BYTES, flags=flags)


def _ada_kernel(c_ref, w_ref, b_ref, o_ref):
    c = c_ref[...]
    sc = (c * jax.nn.sigmoid(c)).astype(BF16)
    acc = jnp.dot(sc, w_ref[...].astype(BF16), preferred_element_type=F32)
    o_ref[...] = acc + b_ref[...]


def _ada(c, w_ada, b_ada):
    batch, d = c.shape
    n = w_ada.shape[1]
    tn = 1024
    rows = 8
    c_pad = jnp.zeros((rows, d), F32).at[:batch].set(c)
    out = pl.pallas_call(
        _ada_kernel,
        grid=(n // tn,),
        in_specs=[
            pl.BlockSpec((rows, d), lambda j: (0, 0)),
            pl.BlockSpec((d, tn), lambda j: (0, j)),
            pl.BlockSpec((1, tn), lambda j: (0, j)),
        ],
        out_specs=pl.BlockSpec((rows, tn), lambda j: (0, j)),
        out_shape=jax.ShapeDtypeStruct((rows, n), F32),
        compiler_params=_params("arbitrary"),
        name="ada",
    )(c_pad, w_ada, b_ada.reshape(1, n))
    return out[:batch].reshape(batch, N_MOD, d)


NORM_ROWS = 256


def _norm_modulate(x, gain, shift):
    y = x * lax.rsqrt(jnp.mean(x * x, axis=-1, keepdims=True) + EPS)
    return (y * gain + shift).astype(BF16)


def _norm_modulate_rows(x_ref, o_ref, norm_w, scale, shift):
    gain = norm_w * (1.0 + scale)

    def body(r, carry):
        rows = pl.ds(pl.multiple_of(r * NORM_ROWS, NORM_ROWS), NORM_ROWS)
        o_ref[rows, :] = _norm_modulate(x_ref[rows, :], gain, shift)
        return carry

    lax.fori_loop(0, x_ref.shape[0] // NORM_ROWS, body, 0)


Q_LOGIT_SCALE = math.log2(math.e) / math.sqrt(HEAD_DIM)


def _head_norm(blk, gain):
    ms = jnp.mean(blk * blk, axis=-1, keepdims=True)
    return (blk * lax.rsqrt(ms + EPS) * gain).astype(BF16)


def _in_proj_kernel(x_hbm, mod_ref, nw_ref, wu_ref, wq_ref, wk_ref, wv_ref, wga_ref, wgb_ref,
                    qw_ref, kw_ref, u_ref, q_ref, k_ref, v_ref, ga_ref, gb_ref,
                    h_scr, xbuf, sems, *, tm):
    i = pl.program_id(0)

    @pl.when(pl.program_id(1) == 0)
    def _():
        chunks = tm // NORM_ROWS
        gain = nw_ref[...] * (1.0 + mod_ref[0, 1:2, :])
        shift = mod_ref[0, 0:1, :]

        def x_copy(r, slot):
            return pltpu.make_async_copy(
                x_hbm.at[pl.ds(i * tm + r * NORM_ROWS, NORM_ROWS), :], xbuf.at[slot],
                sems.at[slot])

        x_copy(0, 0).start()

        def body(r, carry):
            slot = r % 2

            @pl.when(r + 1 < chunks)
            def _():
                x_copy(r + 1, 1 - slot).start()

            x_copy(r, slot).wait()
            rows = pl.ds(pl.multiple_of(r * NORM_ROWS, NORM_ROWS), NORM_ROWS)
            h_scr[rows, :] = _norm_modulate(xbuf[slot], gain, shift)
            return carry

        lax.fori_loop(0, chunks, body, 0)

    h = h_scr[...]

    def project(*w_refs):
        w = jnp.concatenate([w_ref[...].astype(BF16) for w_ref in w_refs], axis=1)
        return jnp.dot(h, w, preferred_element_type=F32)

    qk = project(wq_ref, wk_ref)
    q_ref[...] = _head_norm(qk[:, :HEAD_DIM], qw_ref[...] * Q_LOGIT_SCALE)
    k_ref[...] = _head_norm(qk[:, HEAD_DIM:], kw_ref[...])
    ga_ref[...] = jax.nn.sigmoid(project(wga_ref)).astype(BF16)
    gb_ref[...] = jax.nn.sigmoid(project(wgb_ref)).astype(BF16)
    uv = project(wu_ref, wv_ref)
    u_ref[...] = uv[:, :HEAD_DIM]
    v_ref[...] = uv[:, HEAD_DIM:].astype(BF16)


def _in_proj(x2, mod, norm_w, w_in, q_norm_w, k_norm_w, *, tm):
    m, d = x2.shape
    hd = HEAD_DIM
    gw = D_MODEL // SB_HEADS
    w_spec = lambda width, off: pl.BlockSpec((d, width), lambda i, j: (0, off // width + j))
    out_spec = lambda width: pl.BlockSpec((tm, width), lambda i, j: (i, j))
    return pl.pallas_call(
        functools.partial(_in_proj_kernel, tm=tm),
        grid=(m // tm, SB_HEADS),
        in_specs=[
            pl.BlockSpec(memory_space=pl.ANY),
            pl.BlockSpec((1, N_MOD, d), lambda i, j: (i * tm // SEQ, 0, 0)),
            pl.BlockSpec((1, d), lambda i, j: (0, 0)),
            w_spec(hd, 0), w_spec(hd, Q_OFF), w_spec(hd, K_OFF), w_spec(hd, V_OFF),
            w_spec(gw, GA_OFF), w_spec(gw, GB_OFF),
            pl.BlockSpec((1, hd), lambda i, j: (0, 0)),
            pl.BlockSpec((1, hd), lambda i, j: (0, 0)),
        ],
        out_specs=[out_spec(hd), out_spec(hd), out_spec(hd), out_spec(hd),
                   out_spec(gw), out_spec(gw)],
        out_shape=[jax.ShapeDtypeStruct((m, POOL_WIDTH), F32),
                   jax.ShapeDtypeStruct((m, SB_WIDTH), BF16),
                   jax.ShapeDtypeStruct((m, SB_WIDTH), BF16),
                   jax.ShapeDtypeStruct((m, SB_WIDTH), BF16),
                   jax.ShapeDtypeStruct((m, D_MODEL), BF16),
                   jax.ShapeDtypeStruct((m, D_MODEL), BF16)],
        scratch_shapes=[pltpu.VMEM((tm, d), BF16), pltpu.VMEM((2, NORM_ROWS, d), F32),
                        pltpu.SemaphoreType.DMA((2,))],
        compiler_params=_params("arbitrary", "arbitrary"),
        name="in_proj",
    )(x2, mod, norm_w.reshape(1, d), w_in, w_in, w_in, w_in, w_in, w_in,
      q_norm_w.reshape(1, hd), k_norm_w.reshape(1, hd))


FFN_OUT_CHUNK = 512


def _ffn_kernel(h_ref, x1_hbm, mod_ref, w1_ref, w2_ref, o_ref, xbuf, sem, *, tm, xrows):
    i = pl.program_id(0)
    c = pl.program_id(1)
    x1_copy = pltpu.make_async_copy(x1_hbm.at[pl.ds(i * tm + c * xrows, xrows), :], xbuf, sem)
    x1_copy.start()

    @pl.when(c == 0)
    def _():
        o_ref[...] = jnp.zeros_like(o_ref)

    a = jnp.dot(h_ref[...], w1_ref[...], preferred_element_type=F32)
    r = jnp.maximum(a, 0.0)
    act = (r * r).astype(BF16)
    for n0 in range(0, o_ref.shape[1], FFN_OUT_CHUNK):
        cols = slice(n0, n0 + FFN_OUT_CHUNK)
        y = jnp.dot(act, w2_ref[:, cols], preferred_element_type=F32)
        o_ref[:, cols] += mod_ref[0, 5:6, cols] * y
    x1_copy.wait()
    rows = pl.ds(pl.multiple_of(c * xrows, xrows), xrows)
    o_ref[rows, :] += xbuf[...]


def _ffn(h2, x1, mod, w1, w2, *, tm, tf):
    m, d = h2.shape
    f = w1.shape[1]
    xrows = tm // (f // tf)
    return pl.pallas_call(
        functools.partial(_ffn_kernel, tm=tm, xrows=xrows),
        grid=(m // tm, f // tf),
        in_specs=[
            pl.BlockSpec((tm, d), lambda i, c: (i, 0)),
            pl.BlockSpec(memory_space=pl.ANY),
            pl.BlockSpec((1, N_MOD, d), lambda i, c: (i * tm // SEQ, 0, 0)),
            pl.BlockSpec((d, tf), lambda i, c: (0, c)),
            pl.BlockSpec((tf, d), lambda i, c: (c, 0)),
        ],
        out_specs=pl.BlockSpec((tm, d), lambda i, c: (i, 0)),
        out_shape=jax.ShapeDtypeStruct((m, d), F32),
        scratch_shapes=[pltpu.VMEM((xrows, d), F32), pltpu.SemaphoreType.DMA(())],
        compiler_params=_params("arbitrary", "arbitrary"),
        name="ffn",
    )(h2, x1, mod, w1, w2)


POOL_PAD = 8
POOL_HEAD = max(POOL_WINDOWS)


def _pool_kernel(u_ref, w_ref, s_ref, o_ref, buf_a, buf_b):
    gd = POOL_GROUP_DIM
    body = pl.ds(POOL_PAD, SEQ)
    buf_a[0:POOL_PAD, :] = jnp.zeros((POOL_PAD, gd), F32)
    buf_b[0:POOL_PAD, :] = jnp.zeros((POOL_PAD, gd), F32)
    pos = lax.broadcasted_iota(jnp.int32, (POOL_HEAD, gd), 0)
    for g, w in enumerate(POOL_WINDOWS):
        cols = slice(g * gd, (g + 1) * gd)
        src, dst = buf_a, buf_b
        src[body, :] = u_ref[:, cols]
        shift = 1
        while shift < w:
            dst[body, :] = src[body, :] + src[pl.ds(POOL_PAD - shift, SEQ), :]
            src, dst = dst, src
            shift *= 2
        dst[body, :] = src[body, :] * (1.0 / w) - u_ref[:, cols]
        count = jnp.minimum(pos + 1, w).astype(F32)
        head = pl.ds(POOL_PAD, POOL_HEAD)
        dst[head, :] = src[head, :] / count - u_ref[0:POOL_HEAD, cols]
        mixed = jnp.dot(dst[body, :].astype(BF16), w_ref[g], preferred_element_type=F32)
        o_ref[:, cols] = (mixed * s_ref[:, cols]).astype(o_ref.dtype)


def _pool(u, w_pool_bf, pool_scale, batch):
    groups = len(POOL_WINDOWS)
    gd = POOL_GROUP_DIM
    return pl.pallas_call(
        _pool_kernel,
        grid=(batch,),
        in_specs=[
            pl.BlockSpec((SEQ, POOL_WIDTH), lambda b: (b, 0)),
            pl.BlockSpec((groups, gd, gd), lambda b: (0, 0, 0)),
            pl.BlockSpec((1, POOL_WIDTH), lambda b: (0, 0)),
        ],
        out_specs=pl.BlockSpec((SEQ, POOL_WIDTH), lambda b: (b, 0)),
        out_shape=jax.ShapeDtypeStruct((batch * SEQ, POOL_WIDTH), BF16),
        scratch_shapes=[pltpu.VMEM((POOL_PAD + SEQ, gd), F32),
                        pltpu.VMEM((POOL_PAD + SEQ, gd), F32)],
        compiler_params=_params("arbitrary"),
        name="pool",
    )(u, w_pool_bf, pool_scale.reshape(1, POOL_WIDTH))


EXP2_UNDERFLOW = -151.0


def _attn_kernel(q_ref, k_ref, v_ref, *refs, tb, hp, n_cast):
    w_refs, o_ref, wbf_refs = refs[:n_cast], refs[n_cast], refs[n_cast + 1:2 * n_cast + 1]
    carry_ref, acc_ref = refs[2 * n_cast + 1:]
    for w_ref, wbf_ref in zip(w_refs, wbf_refs):
        wbf_ref[...] = w_ref[...].astype(BF16)

    qi = pl.program_id(2)
    nseg = tb // LANES

    r = lax.broadcasted_iota(jnp.int32, (2 * LANES, 2 * LANES), 0)
    c = lax.broadcasted_iota(jnp.int32, (2 * LANES, 2 * LANES), 1)
    cum_op = jnp.where(jnp.logical_or(c >= LANES, jnp.bitwise_and(r, LANES - 1) > c),
                       1.0, 0.0).astype(BF16)
    tri_r = lax.broadcasted_iota(jnp.int32, (LANES, LANES), 0)
    tri_c = lax.broadcasted_iota(jnp.int32, (LANES, LANES), 1)
    causal = tri_c < tri_r
    segs = [slice(sg * LANES, (sg + 1) * LANES) for sg in range(nseg)]

    def block(kb, diagonal):
        start = pl.multiple_of(kb * tb, tb)
        if diagonal:
            parts = [(segs[g], g + 1) for g in range(nseg)]
        else:
            parts = [(slice(0, tb), nseg)]
        work = [(p, slice(p * HEAD_DIM, (p + 1) * HEAD_DIM), rows, nk)
                for rows, nk in parts for p in range(hp)]
        zs = [lax.dot_general(q_ref[rows, cols], k_ref[pl.ds(start, nk * LANES), cols],
                              (((1,), (1,)), ((), ())), preferred_element_type=F32)
              for _, cols, rows, nk in work]
        log_betas, sums = [], []
        for z, (_, _, _, nk) in zip(zs, work):
            log_beta = jnp.minimum(z, 0.0) - jnp.log2(1.0 + jnp.exp2(-jnp.abs(z)))
            l = log_beta - z
            cs = []
            for sg in range(nk):
                l_seg = l[:, segs[sg]]
                if diagonal and sg == nk - 1:
                    l_seg = jnp.where(causal, l_seg, 0.0)
                l_hi = l_seg.astype(BF16)
                l_lo = (l_seg - l_hi.astype(F32)).astype(BF16)
                cs.append(jnp.dot(jnp.concatenate([l_hi, l_lo], axis=1), cum_op,
                                  preferred_element_type=F32))
            log_betas.append(log_beta)
            sums.append(cs)
        top = None
        for (p, cols, rows, nk), log_beta, cs in zip(work, log_betas, sums):
            carry = carry_ref[p, rows, :]
            a_parts = [None] * nk
            for sg in range(nk - 1, -1, -1):
                a_seg = jnp.exp2(log_beta[:, segs[sg]] + (cs[sg][:, :LANES] + carry))
                if diagonal and sg == nk - 1:
                    a_seg = jnp.where(causal, a_seg, 0.0)
                a_parts[sg] = a_seg
                carry = carry + cs[sg][:, LANES:]
            a = jnp.concatenate(a_parts, axis=1) if nk > 1 else a_parts[0]
            acc_ref[p, rows, :] += jnp.dot(
                a.astype(BF16), v_ref[pl.ds(start, nk * LANES), cols],
                preferred_element_type=F32)
            carry_ref[p, rows, :] = carry
            top = jnp.max(carry) if top is None else jnp.maximum(top, jnp.max(carry))
        return top

    carry_ref[...] = jnp.zeros_like(carry_ref)
    acc_ref[...] = jnp.zeros_like(acc_ref)
    top = block(qi, True)

    def cond(state):
        kb, top = state
        return jnp.logical_and(kb >= 0, top > EXP2_UNDERFLOW)

    def body(state):
        kb, _ = state
        return kb - 1, block(kb, False)

    lax.while_loop(cond, body, (qi - 1, top))
    for p in range(hp):
        o_ref[:, p * HEAD_DIM:(p + 1) * HEAD_DIM] = acc_ref[p].astype(o_ref.dtype)


def _attn(q, k, v, weights, batch, *, tb, hp):
    nq = SEQ // tb
    groups = SB_HEADS // hp
    width = hp * HEAD_DIM
    steps = batch * groups * nq
    assert all(w.shape[0] % steps == 0 for w in weights)
    cast_specs = [pl.BlockSpec((w.shape[0] // steps, w.shape[1]),
                               lambda b, h, i: ((b * groups + h) * nq + i, 0))
                  for w in weights]
    outs = pl.pallas_call(
        functools.partial(_attn_kernel, tb=tb, hp=hp, n_cast=len(weights)),
        grid=(batch, groups, nq),
        in_specs=[
            pl.BlockSpec((tb, width), lambda b, h, i: (b * nq + i, h)),
            pl.BlockSpec((SEQ, width), lambda b, h, i: (b, h)),
            pl.BlockSpec((SEQ, width), lambda b, h, i: (b, h)),
        ] + cast_specs,
        out_specs=[pl.BlockSpec((tb, width), lambda b, h, i: (b * nq + i, h))] + cast_specs,
        out_shape=[jax.ShapeDtypeStruct((batch * SEQ, SB_WIDTH), BF16)]
        + [jax.ShapeDtypeStruct(w.shape, BF16) for w in weights],
        scratch_shapes=[pltpu.VMEM((hp, tb, LANES), F32),
                        pltpu.VMEM((hp, tb, HEAD_DIM), F32)],
        compiler_params=_params("arbitrary", "arbitrary", "arbitrary"),
        name="attn",
    )(q, k, v, *weights)
    return outs[0], outs[1:]


MIX_CHUNK = 512


def _mix_kernel(pa_ref, at_ref, sa_ref, sb_ref, x_ref, mod_ref, nw_ref, wa_ref, wb_ref, wo_ref,
                x1_ref, h2_ref, merged_scr):
    d = x_ref.shape[1]
    pa = pa_ref[...]
    at = at_ref[...]
    for n0 in range(0, d, MIX_CHUNK):
        cols = slice(n0, n0 + MIX_CHUNK)
        ya = jnp.dot(pa, wa_ref[:, cols], preferred_element_type=F32)
        yb = jnp.dot(at, wb_ref[:, cols], preferred_element_type=F32)
        merged = sa_ref[:, cols].astype(F32) * ya + sb_ref[:, cols].astype(F32) * yb
        merged_scr[:, cols] = merged.astype(BF16)
    merged = merged_scr[...]
    for n0 in range(0, d, MIX_CHUNK):
        cols = slice(n0, n0 + MIX_CHUNK)
        o = jnp.dot(merged, wo_ref[:, cols], preferred_element_type=F32)
        x1_ref[:, cols] = x_ref[:, cols] + mod_ref[0, 2:3, cols] * o
    _norm_modulate_rows(x1_ref, h2_ref, nw_ref[...], mod_ref[0, 4:5, :], mod_ref[0, 3:4, :])


def _mix(pa, at, sa, sb, x2, mod, norm_w, wa_bf, wb_bf, wo_bf, *, tm):
    m, d = x2.shape
    kdim = pa.shape[1]
    resident = pl.Buffered(1)
    return pl.pallas_call(
        _mix_kernel,
        grid=(m // tm,),
        in_specs=[
            pl.BlockSpec((tm, kdim), lambda i: (i, 0)),
            pl.BlockSpec((tm, kdim), lambda i: (i, 0)),
            pl.BlockSpec((tm, d), lambda i: (i, 0)),
            pl.BlockSpec((tm, d), lambda i: (i, 0)),
            pl.BlockSpec((tm, d), lambda i: (i, 0)),
            pl.BlockSpec((1, N_MOD, d), lambda i: (i * tm // SEQ, 0, 0)),
            pl.BlockSpec((1, d), lambda i: (0, 0)),
            pl.BlockSpec((kdim, d), lambda i: (0, 0), pipeline_mode=resident),
            pl.BlockSpec((kdim, d), lambda i: (0, 0), pipeline_mode=resident),
            pl.BlockSpec((d, d), lambda i: (0, 0), pipeline_mode=resident),
        ],
        out_specs=[pl.BlockSpec((tm, d), lambda i: (i, 0)),
                   pl.BlockSpec((tm, d), lambda i: (i, 0))],
        out_shape=[jax.ShapeDtypeStruct((m, d), F32),
                   jax.ShapeDtypeStruct((m, d), BF16)],
        scratch_shapes=[pltpu.VMEM((tm, d), BF16)],
        compiler_params=_params("arbitrary"),
        name="mix",
    )(pa, at, sa, sb, x2, mod, norm_w.reshape(1, d), wa_bf, wb_bf, wo_bf)


def kernel(x, c, w_ada, b_ada, norm1_w, w_in, q_norm_w, k_norm_w, w_pool, pool_scale,
           w_a_up, w_b_up, w_o, norm2_w, w_ff1, w_ff2):
    batch, seq, d = x.shape
    assert (seq, d) == (SEQ, D_MODEL) and w_ada.shape[0] == 1
    x2 = x.reshape(batch * seq, d)

    mod = _ada(c, w_ada[0], b_ada[0])
    u, q, k, v, sa, sb = _in_proj(x2, mod, norm1_w[0], w_in[0], q_norm_w[0], k_norm_w[0],
                                  tm=2048)
    pa = _pool(u, w_pool[0].astype(BF16), pool_scale[0], batch)
    at, (wa_bf, wb_bf, wo_bf, w1_bf, w2_bf) = _attn(
        q, k, v, (w_a_up[0], w_b_up[0], w_o[0], w_ff1[0], w_ff2[0]), batch,
        tb=256, hp=8)
    x1, h2 = _mix(pa, at, sa, sb, x2, mod, norm2_w[0], wa_bf, wb_bf, wo_bf, tm=256)
    out = _ffn(h2, x1, mod, w1_bf, w2_bf, tm=1024, tf=1024)
    return out.reshape(batch, seq, d)
```

```python
import functools
import math

import jax
import jax.numpy as jnp
from jax import lax
from jax.experimental import pallas as pl
from jax.experimental.pallas import tpu as pltpu

D_MODEL = 2048
SEQ = 2048
POOL_WIDTH = D_MODEL // 2
POOL_WINDOWS = (2, 4, 8, 16)
POOL_GROUP_DIM = POOL_WIDTH // len(POOL_WINDOWS)
HEAD_DIM = 128
SB_WIDTH = D_MODEL // 2
SB_HEADS = SB_WIDTH // HEAD_DIM
IN_WIDTH = POOL_WIDTH + 3 * SB_WIDTH + 2 * D_MODEL
D_FF = 4 * D_MODEL
N_MOD = 6
EPS = 1e-6

Q_OFF = POOL_WIDTH
K_OFF = Q_OFF + SB_WIDTH
V_OFF = K_OFF + SB_WIDTH
GA_OFF = V_OFF + SB_WIDTH
GB_OFF = GA_OFF + D_MODEL

LANES = 128
VMEM_LIMIT_BYTES = 56 * 1024 * 1024

BF16 = jnp.bfloat16
F32 = jnp.float32


def _params(*semantics, flags=None):
    return pltpu.CompilerParams(dimension_semantics=semantics,
                                vmem_limit_bytes=VMEM_LIMIT_BYTES, flags=flags)


def _ada_kernel(c_ref, w_ref, b_ref, o_ref):
    c = c_ref[...]
    sc = (c * jax.nn.sigmoid(c)).astype(BF16)
    acc = jnp.dot(sc, w_ref[...].astype(BF16), preferred_element_type=F32)
    o_ref[...] = acc + b_ref[...]


def _ada(c, w_ada, b_ada):
    batch, d = c.shape
    n = w_ada.shape[1]
    tn = 1024
    rows = 8
    c_pad = jnp.zeros((rows, d), F32).at[:batch].set(c)
    out = pl.pallas_call(
        _ada_kernel,
        grid=(n // tn,),
        in_specs=[
            pl.BlockSpec((rows, d), lambda j: (0, 0)),
            pl.BlockSpec((d, tn), lambda j: (0, j)),
            pl.BlockSpec((1, tn), lambda j: (0, j)),
        ],
        out_specs=pl.BlockSpec((rows, tn), lambda j: (0, j)),
        out_shape=jax.ShapeDtypeStruct((rows, n), F32),
        compiler_params=_params("arbitrary"),
        name="ada",
    )(c_pad, w_ada, b_ada.reshape(1, n))
    return out[:batch].reshape(batch, N_MOD, d)


NORM_ROWS = 256


def _norm_modulate(x, gain, shift):
    y = x * lax.rsqrt(jnp.mean(x * x, axis=-1, keepdims=True) + EPS)
    return (y * gain + shift).astype(BF16)


def _norm_modulate_rows(x_ref, o_ref, norm_w, scale, shift):
    gain = norm_w * (1.0 + scale)

    def body(r, carry):
        rows = pl.ds(pl.multiple_of(r * NORM_ROWS, NORM_ROWS), NORM_ROWS)
        o_ref[rows, :] = _norm_modulate(x_ref[rows, :], gain, shift)
        return carry

    lax.fori_loop(0, x_ref.shape[0] // NORM_ROWS, body, 0)


Q_LOGIT_SCALE = math.log2(math.e) / math.sqrt(HEAD_DIM)


def _head_norm(blk, gain):
    ms = jnp.mean(blk * blk, axis=-1, keepdims=True)
    return (blk * lax.rsqrt(ms + EPS) * gain).astype(BF16)


IN_PROJ_ROWS = 1024


def _in_proj_kernel(x_hbm, mod_ref, nw_ref, wu_ref, wq_ref, wk_ref, wv_ref, wga_ref, wgb_ref,
                    qw_ref, kw_ref, u_ref, q_ref, k_ref, v_ref, ga_ref, gb_ref,
                    h_scr, xbuf, sems, *, tm):
    i = pl.program_id(0)

    @pl.when(pl.program_id(1) == 0)
    def _():
        chunks = tm // NORM_ROWS
        gain = nw_ref[...] * (1.0 + mod_ref[0, 1:2, :])
        shift = mod_ref[0, 0:1, :]

        def x_copy(r, slot):
            return pltpu.make_async_copy(
                x_hbm.at[pl.ds(i * tm + r * NORM_ROWS, NORM_ROWS), :], xbuf.at[slot],
                sems.at[slot])

        x_copy(0, 0).start()

        def body(r, carry):
            slot = r % 2

            @pl.when(r + 1 < chunks)
            def _():
                x_copy(r + 1, 1 - slot).start()

            x_copy(r, slot).wait()
            rows = pl.ds(pl.multiple_of(r * NORM_ROWS, NORM_ROWS), NORM_ROWS)
            h_scr[rows, :] = _norm_modulate(xbuf[slot], gain, shift)
            return carry

        lax.fori_loop(0, chunks, body, 0)

    def bf16_cols(*w_refs):
        return jnp.concatenate([w_ref[...].astype(BF16) for w_ref in w_refs], axis=1)

    w_qk, w_ga, w_gb, w_uv = (bf16_cols(wq_ref, wk_ref), bf16_cols(wga_ref),
                              bf16_cols(wgb_ref), bf16_cols(wu_ref, wv_ref))
    q_gain = qw_ref[...] * Q_LOGIT_SCALE
    k_gain = kw_ref[...]
    for r0 in range(0, tm, IN_PROJ_ROWS):
        rows = slice(r0, r0 + IN_PROJ_ROWS)
        h = h_scr[rows, :]
        qk = jnp.dot(h, w_qk, preferred_element_type=F32)
        q_ref[rows, :] = _head_norm(qk[:, :HEAD_DIM], q_gain)
        k_ref[rows, :] = _head_norm(qk[:, HEAD_DIM:], k_gain)
        ga = jnp.dot(h, w_ga, preferred_element_type=F32)
        ga_ref[rows, :] = jax.nn.sigmoid(ga).astype(BF16)
        gb = jnp.dot(h, w_gb, preferred_element_type=F32)
        gb_ref[rows, :] = jax.nn.sigmoid(gb).astype(BF16)
        uv = jnp.dot(h, w_uv, preferred_element_type=F32)
        u_ref[rows, :] = uv[:, :HEAD_DIM]
        v_ref[rows, :] = uv[:, HEAD_DIM:].astype(BF16)


def _in_proj(x2, mod, norm_w, w_in, q_norm_w, k_norm_w, *, tm):
    m, d = x2.shape
    hd = HEAD_DIM
    gw = D_MODEL // SB_HEADS
    w_spec = lambda width, off: pl.BlockSpec((d, width), lambda i, j: (0, off // width + j))
    out_spec = lambda width: pl.BlockSpec((tm, width), lambda i, j: (i, j))
    return pl.pallas_call(
        functools.partial(_in_proj_kernel, tm=tm),
        grid=(m // tm, SB_HEADS),
        in_specs=[
            pl.BlockSpec(memory_space=pl.ANY),
            pl.BlockSpec((1, N_MOD, d), lambda i, j: (i * tm // SEQ, 0, 0)),
            pl.BlockSpec((1, d), lambda i, j: (0, 0)),
            w_spec(hd, 0), w_spec(hd, Q_OFF), w_spec(hd, K_OFF), w_spec(hd, V_OFF),
            w_spec(gw, GA_OFF), w_spec(gw, GB_OFF),
            pl.BlockSpec((1, hd), lambda i, j: (0, 0)),
            pl.BlockSpec((1, hd), lambda i, j: (0, 0)),
        ],
        out_specs=[out_spec(hd), out_spec(hd), out_spec(hd), out_spec(hd),
                   out_spec(gw), out_spec(gw)],
        out_shape=[jax.ShapeDtypeStruct((m, POOL_WIDTH), F32),
                   jax.ShapeDtypeStruct((m, SB_WIDTH), BF16),
                   jax.ShapeDtypeStruct((m, SB_WIDTH), BF16),
                   jax.ShapeDtypeStruct((m, SB_WIDTH), BF16),
                   jax.ShapeDtypeStruct((m, D_MODEL), BF16),
                   jax.ShapeDtypeStruct((m, D_MODEL), BF16)],
        scratch_shapes=[pltpu.VMEM((tm, d), BF16), pltpu.VMEM((2, NORM_ROWS, d), F32),
                        pltpu.SemaphoreType.DMA((2,))],
        compiler_params=_params("arbitrary", "arbitrary"),
        name="in_proj",
    )(x2, mod, norm_w.reshape(1, d), w_in, w_in, w_in, w_in, w_in, w_in,
      q_norm_w.reshape(1, hd), k_norm_w.reshape(1, hd))


FFN_OUT_CHUNK = 512


def _ffn_kernel(h_ref, x1_hbm, mod_ref, w1_ref, w2_ref, o_ref, xbuf, sem, *, tm, xrows):
    i = pl.program_id(0)
    c = pl.program_id(1)
    x1_copy = pltpu.make_async_copy(x1_hbm.at[pl.ds(i * tm + c * xrows, xrows), :], xbuf, sem)
    x1_copy.start()

    @pl.when(c == 0)
    def _():
        o_ref[...] = jnp.zeros_like(o_ref)

    a = jnp.dot(h_ref[...], w1_ref[...], preferred_element_type=F32)
    r = jnp.maximum(a, 0.0)
    act = (r * r).astype(BF16)
    for n0 in range(0, o_ref.shape[1], FFN_OUT_CHUNK):
        cols = slice(n0, n0 + FFN_OUT_CHUNK)
        y = jnp.dot(act, w2_ref[:, cols], preferred_element_type=F32)
        o_ref[:, cols] += mod_ref[0, 5:6, cols] * y
    x1_copy.wait()
    rows = pl.ds(pl.multiple_of(c * xrows, xrows), xrows)
    o_ref[rows, :] += xbuf[...]


def _ffn(h2, x1, mod, w1, w2, *, tm, tf):
    m, d = h2.shape
    f = w1.shape[1]
    xrows = tm // (f // tf)
    return pl.pallas_call(
        functools.partial(_ffn_kernel, tm=tm, xrows=xrows),
        grid=(m // tm, f // tf),
        in_specs=[
            pl.BlockSpec((tm, d), lambda i, c: (i, 0)),
            pl.BlockSpec(memory_space=pl.ANY),
            pl.BlockSpec((1, N_MOD, d), lambda i, c: (i * tm // SEQ, 0, 0)),
            pl.BlockSpec((d, tf), lambda i, c: (0, c)),
            pl.BlockSpec((tf, d), lambda i, c: (c, 0)),
        ],
        out_specs=pl.BlockSpec((tm, d), lambda i, c: (i, 0)),
        out_shape=jax.ShapeDtypeStruct((m, d), F32),
        scratch_shapes=[pltpu.VMEM((xrows, d), F32), pltpu.SemaphoreType.DMA(())],
        compiler_params=_params("arbitrary", "arbitrary"),
        name="ffn",
    )(h2, x1, mod, w1, w2)


POOL_PAD = 8
POOL_HEAD = max(POOL_WINDOWS)


def _pool_kernel(u_ref, w_ref, s_ref, o_ref, buf_a, buf_b):
    gd = POOL_GROUP_DIM
    body = pl.ds(POOL_PAD, SEQ)
    buf_a[0:POOL_PAD, :] = jnp.zeros((POOL_PAD, gd), F32)
    buf_b[0:POOL_PAD, :] = jnp.zeros((POOL_PAD, gd), F32)
    pos = lax.broadcasted_iota(jnp.int32, (POOL_HEAD, gd), 0)
    for g, w in enumerate(POOL_WINDOWS):
        cols = slice(g * gd, (g + 1) * gd)
        src, dst = buf_a, buf_b
        src[body, :] = u_ref[:, cols]
        shift = 1
        while shift < w:
            dst[body, :] = src[body, :] + src[pl.ds(POOL_PAD - shift, SEQ), :]
            src, dst = dst, src
            shift *= 2
        dst[body, :] = src[body, :] * (1.0 / w) - u_ref[:, cols]
        count = jnp.minimum(pos + 1, w).astype(F32)
        head = pl.ds(POOL_PAD, POOL_HEAD)
        dst[head, :] = src[head, :] / count - u_ref[0:POOL_HEAD, cols]
        mixed = jnp.dot(dst[body, :].astype(BF16), w_ref[g], preferred_element_type=F32)
        o_ref[:, cols] = (mixed * s_ref[:, cols]).astype(o_ref.dtype)


def _pool(u, w_pool_bf, pool_scale, batch):
    groups = len(POOL_WINDOWS)
    gd = POOL_GROUP_DIM
    return pl.pallas_call(
        _pool_kernel,
        grid=(batch,),
        in_specs=[
            pl.BlockSpec((SEQ, POOL_WIDTH), lambda b: (b, 0)),
            pl.BlockSpec((groups, gd, gd), lambda b: (0, 0, 0)),
            pl.BlockSpec((1, POOL_WIDTH), lambda b: (0, 0)),
        ],
        out_specs=pl.BlockSpec((SEQ, POOL_WIDTH), lambda b: (b, 0)),
        out_shape=jax.ShapeDtypeStruct((batch * SEQ, POOL_WIDTH), BF16),
        scratch_shapes=[pltpu.VMEM((POOL_PAD + SEQ, gd), F32),
                        pltpu.VMEM((POOL_PAD + SEQ, gd), F32)],
        compiler_params=_params("arbitrary"),
        name="pool",
    )(u, w_pool_bf, pool_scale.reshape(1, POOL_WIDTH))


EXP2_UNDERFLOW = -151.0


def _attn_kernel(q_ref, k_ref, v_ref, *refs, tb, hp, n_cast):
    w_refs, o_ref, wbf_refs = refs[:n_cast], refs[n_cast], refs[n_cast + 1:2 * n_cast + 1]
    carry_ref, acc_ref = refs[2 * n_cast + 1:]
    for w_ref, wbf_ref in zip(w_refs, wbf_refs):
        wbf_ref[...] = w_ref[...].astype(BF16)

    qi = pl.program_id(2)
    nseg = tb // LANES

    r = lax.broadcasted_iota(jnp.int32, (2 * LANES, 2 * LANES), 0)
    c = lax.broadcasted_iota(jnp.int32, (2 * LANES, 2 * LANES), 1)
    cum_op = jnp.where(jnp.logical_or(c >= LANES, jnp.bitwise_and(r, LANES - 1) > c),
                       1.0, 0.0).astype(BF16)
    tri_r = lax.broadcasted_iota(jnp.int32, (LANES, LANES), 0)
    tri_c = lax.broadcasted_iota(jnp.int32, (LANES, LANES), 1)
    causal = tri_c < tri_r
    segs = [slice(sg * LANES, (sg + 1) * LANES) for sg in range(nseg)]

    def block(kb, diagonal):
        start = pl.multiple_of(kb * tb, tb)
        if diagonal:
            parts = [(segs[g], g + 1) for g in range(nseg)]
        else:
            parts = [(slice(0, tb), nseg)]
        work = [(p, slice(p * HEAD_DIM, (p + 1) * HEAD_DIM), rows, nk)
                for rows, nk in parts for p in range(hp)]
        zs = [lax.dot_general(q_ref[rows, cols], k_ref[pl.ds(start, nk * LANES), cols],
                              (((1,), (1,)), ((), ())), preferred_element_type=F32)
              for _, cols, rows, nk in work]
        log_betas, sums = [], []
        for z, (_, _, _, nk) in zip(zs, work):
            log_beta = jnp.minimum(z, 0.0) - jnp.log2(1.0 + jnp.exp2(-jnp.abs(z)))
            l = log_beta - z
            cs = []
            for sg in range(nk):
                l_seg = l[:, segs[sg]]
                if diagonal and sg == nk - 1:
                    l_seg = jnp.where(causal, l_seg, 0.0)
                l_hi = l_seg.astype(BF16)
                l_lo = (l_seg - l_hi.astype(F32)).astype(BF16)
                cs.append(jnp.dot(jnp.concatenate([l_hi, l_lo], axis=1), cum_op,
                                  preferred_element_type=F32))
            log_betas.append(log_beta)
            sums.append(cs)
        top = None
        for (p, cols, rows, nk), log_beta, cs in zip(work, log_betas, sums):
            carry = carry_ref[p, rows, :]
            a_parts = [None] * nk
            for sg in range(nk - 1, -1, -1):
                a_seg = jnp.exp2(log_beta[:, segs[sg]] + (cs[sg][:, :LANES] + carry))
                if diagonal and sg == nk - 1:
                    a_seg = jnp.where(causal, a_seg, 0.0)
                a_parts[sg] = a_seg
                carry = carry + cs[sg][:, LANES:]
            a = jnp.concatenate(a_parts, axis=1) if nk > 1 else a_parts[0]
            acc_ref[p, rows, :] += jnp.dot(
                a.astype(BF16), v_ref[pl.ds(start, nk * LANES), cols],
                preferred_element_type=F32)
            carry_ref[p, rows, :] = carry
            top = jnp.max(carry) if top is None else jnp.maximum(top, jnp.max(carry))
        return top

    carry_ref[...] = jnp.zeros_like(carry_ref)
    acc_ref[...] = jnp.zeros_like(acc_ref)
    top = block(qi, True)

    def cond(state):
        kb, top = state
        return jnp.logical_and(kb >= 0, top > EXP2_UNDERFLOW)

    def body(state):
        kb, _ = state
        return kb - 1, block(kb, False)

    lax.while_loop(cond, body, (qi - 1, top))
    for p in range(hp):
        o_ref[:, p * HEAD_DIM:(p + 1) * HEAD_DIM] = acc_ref[p].astype(o_ref.dtype)


def _attn(q, k, v, weights, batch, *, tb, hp):
    nq = SEQ // tb
    groups = SB_HEADS // hp
    width = hp * HEAD_DIM
    steps = batch * groups * nq
    assert all(w.shape[0] % steps == 0 for w in weights)
    cast_specs = [pl.BlockSpec((w.shape[0] // steps, w.shape[1]),
                               lambda b, h, i: ((b * groups + h) * nq + i, 0))
                  for w in weights]
    outs = pl.pallas_call(
        functools.partial(_attn_kernel, tb=tb, hp=hp, n_cast=len(weights)),
        grid=(batch, groups, nq),
        in_specs=[
            pl.BlockSpec((tb, width), lambda b, h, i: (b * nq + i, h)),
            pl.BlockSpec((SEQ, width), lambda b, h, i: (b, h)),
            pl.BlockSpec((SEQ, width), lambda b, h, i: (b, h)),
        ] + cast_specs,
        out_specs=[pl.BlockSpec((tb, width), lambda b, h, i: (b * nq + i, h))] + cast_specs,
        out_shape=[jax.ShapeDtypeStruct((batch * SEQ, SB_WIDTH), BF16)]
        + [jax.ShapeDtypeStruct(w.shape, BF16) for w in weights],
        scratch_shapes=[pltpu.VMEM((hp, tb, LANES), F32),
                        pltpu.VMEM((hp, tb, HEAD_DIM), F32)],
        compiler_params=_params("arbitrary", "arbitrary", "arbitrary"),
        name="attn",
    )(q, k, v, *weights)
    return outs[0], outs[1:]


MIX_CHUNK = 512


def _mix_kernel(pa_ref, at_ref, sa_ref, sb_ref, x_ref, mod_ref, nw_ref, wa_ref, wb_ref, wo_ref,
                x1_ref, h2_ref, merged_scr):
    d = x_ref.shape[1]
    pa = pa_ref[...]
    at = at_ref[...]
    for n0 in range(0, d, MIX_CHUNK):
        cols = slice(n0, n0 + MIX_CHUNK)
        ya = jnp.dot(pa, wa_ref[:, cols], preferred_element_type=F32)
        yb = jnp.dot(at, wb_ref[:, cols], preferred_element_type=F32)
        merged = sa_ref[:, cols].astype(F32) * ya + sb_ref[:, cols].astype(F32) * yb
        merged_scr[:, cols] = merged.astype(BF16)
    merged = merged_scr[...]
    for n0 in range(0, d, MIX_CHUNK):
        cols = slice(n0, n0 + MIX_CHUNK)
        o = jnp.dot(merged, wo_ref[:, cols], preferred_element_type=F32)
        x1_ref[:, cols] = x_ref[:, cols] + mod_ref[0, 2:3, cols] * o
    _norm_modulate_rows(x1_ref, h2_ref, nw_ref[...], mod_ref[0, 4:5, :], mod_ref[0, 3:4, :])


def _mix(pa, at, sa, sb, x2, mod, norm_w, wa_bf, wb_bf, wo_bf, *, tm):
    m, d = x2.shape
    kdim = pa.shape[1]
    resident = pl.Buffered(1)
    return pl.pallas_call(
        _mix_kernel,
        grid=(m // tm,),
        in_specs=[
            pl.BlockSpec((tm, kdim), lambda i: (i, 0)),
            pl.BlockSpec((tm, kdim), lambda i: (i, 0)),
            pl.BlockSpec((tm, d), lambda i: (i, 0)),
            pl.BlockSpec((tm, d), lambda i: (i, 0)),
            pl.BlockSpec((tm, d), lambda i: (i, 0)),
            pl.BlockSpec((1, N_MOD, d), lambda i: (i * tm // SEQ, 0, 0)),
            pl.BlockSpec((1, d), lambda i: (0, 0)),
            pl.BlockSpec((kdim, d), lambda i: (0, 0), pipeline_mode=resident),
            pl.BlockSpec((kdim, d), lambda i: (0, 0), pipeline_mode=resident),
            pl.BlockSpec((d, d), lambda i: (0, 0), pipeline_mode=resident),
        ],
        out_specs=[pl.BlockSpec((tm, d), lambda i: (i, 0)),
                   pl.BlockSpec((tm, d), lambda i: (i, 0))],
        out_shape=[jax.ShapeDtypeStruct((m, d), F32),
                   jax.ShapeDtypeStruct((m, d), BF16)],
        scratch_shapes=[pltpu.VMEM((tm, d), BF16)],
        compiler_params=_params("arbitrary"),
        name="mix",
    )(pa, at, sa, sb, x2, mod, norm_w.reshape(1, d), wa_bf, wb_bf, wo_bf)


def kernel(x, c, w_ada, b_ada, norm1_w, w_in, q_norm_w, k_norm_w, w_pool, pool_scale,
           w_a_up, w_b_up, w_o, norm2_w, w_ff1, w_ff2):
    batch, seq, d = x.shape
    assert (seq, d) == (SEQ, D_MODEL) and w_ada.shape[0] == 1
    x2 = x.reshape(batch * seq, d)

    mod = _ada(c, w_ada[0], b_ada[0])
    u, q, k, v, sa, sb = _in_proj(x2, mod, norm1_w[0], w_in[0], q_norm_w[0], k_norm_w[0],
                                  tm=2048)
    pa = _pool(u, w_pool[0].astype(BF16), pool_scale[0], batch)
    at, (wa_bf, wb_bf, wo_bf, w1_bf, w2_bf) = _attn(
        q, k, v, (w_a_up[0], w_b_up[0], w_o[0], w_ff1[0], w_ff2[0]), batch,
        tb=256, hp=8)
    x1, h2 = _mix(pa, at, sa, sb, x2, mod, norm2_w[0], wa_bf, wb_bf, wo_bf, tm=256)
    out = _ffn(h2, x1, mod, w1_bf, w2_bf, tm=1024, tf=1024)
    return out.reshape(batch, seq, d)
```

```python
import functools
import math

import jax
import jax.numpy as jnp
from jax import lax
from jax.experimental import pallas as pl
from jax.experimental.pallas import tpu as pltpu

D_MODEL = 2048
SEQ = 2048
POOL_WIDTH = D_MODEL // 2
POOL_WINDOWS = (2, 4, 8, 16)
POOL_GROUP_DIM = POOL_WIDTH // len(POOL_WINDOWS)
HEAD_DIM = 128
SB_WIDTH = D_MODEL // 2
SB_HEADS = SB_WIDTH // HEAD_DIM
IN_WIDTH = POOL_WIDTH + 3 * SB_WIDTH + 2 * D_MODEL
D_FF = 4 * D_MODEL
N_MOD = 6
EPS = 1e-6

Q_OFF = POOL_WIDTH
K_OFF = Q_OFF + SB_WIDTH
V_OFF = K_OFF + SB_WIDTH
GA_OFF = V_OFF + SB_WIDTH
GB_OFF = GA_OFF + D_MODEL

LANES = 128
VMEM_LIMIT_BYTES = 56 * 1024 * 1024

BF16 = jnp.bfloat16
F32 = jnp.float32


def _params(*semantics, flags=None):
    return pltpu.CompilerParams(dimension_semantics=semantics,
                                vmem_limit_bytes=VMEM_LIMIT_BYTES, flags=flags)


def _ada_kernel(c_ref, w_ref, b_ref, o_ref):
    c = c_ref[...]
    sc = (c * jax.nn.sigmoid(c)).astype(BF16)
    acc = jnp.dot(sc, w_ref[...].astype(BF16), preferred_element_type=F32)
    o_ref[...] = acc + b_ref[...]


def _ada(c, w_ada, b_ada):
    batch, d = c.shape
    n = w_ada.shape[1]
    tn = 1024
    rows = 8
    c_pad = jnp.zeros((rows, d), F32).at[:batch].set(c)
    out = pl.pallas_call(
        _ada_kernel,
        grid=(n // tn,),
        in_specs=[
            pl.BlockSpec((rows, d), lambda j: (0, 0)),
            pl.BlockSpec((d, tn), lambda j: (0, j)),
            pl.BlockSpec((1, tn), lambda j: (0, j)),
        ],
        out_specs=pl.BlockSpec((rows, tn), lambda j: (0, j)),
        out_shape=jax.ShapeDtypeStruct((rows, n), F32),
        compiler_params=_params("arbitrary"),
        name="ada",
    )(c_pad, w_ada, b_ada.reshape(1, n))
    return out[:batch].reshape(batch, N_MOD, d)


NORM_ROWS = 256


def _norm_modulate(x, gain, shift):
    y = x * lax.rsqrt(jnp.mean(x * x, axis=-1, keepdims=True) + EPS)
    return (y * gain + shift).astype(BF16)


def _norm_modulate_rows(x_ref, o_ref, norm_w, scale, shift):
    gain = norm_w * (1.0 + scale)

    def body(r, carry):
        rows = pl.ds(pl.multiple_of(r * NORM_ROWS, NORM_ROWS), NORM_ROWS)
        o_ref[rows, :] = _norm_modulate(x_ref[rows, :], gain, shift)
        return carry

    lax.fori_loop(0, x_ref.shape[0] // NORM_ROWS, body, 0)


Q_LOGIT_SCALE = math.log2(math.e) / math.sqrt(HEAD_DIM)


def _head_norm(blk, gain):
    ms = jnp.mean(blk * blk, axis=-1, keepdims=True)
    return (blk * lax.rsqrt(ms + EPS) * gain).astype(BF16)


IN_PROJ_ROWS = 1024
X_SLOTS = 4


def _in_proj_kernel(x_hbm, mod_ref, nw_ref, wu_ref, wq_ref, wk_ref, wv_ref, wga_ref, wgb_ref,
                    qw_ref, kw_ref, u_ref, q_ref, k_ref, v_ref, ga_ref, gb_ref,
                    h_scr, xbuf, sems, *, tm):
    i = pl.program_id(0)
    j = pl.program_id(1)
    chunks = tm // NORM_ROWS

    def x_copy(tile, r):
        slot = r % X_SLOTS
        return pltpu.make_async_copy(
            x_hbm.at[pl.ds(tile * tm + r * NORM_ROWS, NORM_ROWS), :], xbuf.at[slot],
            sems.at[slot])

    @pl.when(jnp.logical_and(j == 0, i == 0))
    def _():
        for r in range(X_SLOTS):
            x_copy(0, r).start()

    @pl.when(j == 0)
    def _():
        gain = nw_ref[...] * (1.0 + mod_ref[0, 1:2, :])
        shift = mod_ref[0, 0:1, :]

        def body(r, carry):
            x_copy(i, r).wait()
            rows = pl.ds(pl.multiple_of(r * NORM_ROWS, NORM_ROWS), NORM_ROWS)
            h_scr[rows, :] = _norm_modulate(xbuf[r % X_SLOTS], gain, shift)

            @pl.when(r + X_SLOTS < chunks)
            def _():
                x_copy(i, r + X_SLOTS).start()

            return carry

        lax.fori_loop(0, chunks, body, 0)

    @pl.when(jnp.logical_and(j == pl.num_programs(1) - 1, i + 1 < pl.num_programs(0)))
    def _():
        for r in range(X_SLOTS):
            x_copy(i + 1, r).start()

    def bf16_cols(*w_refs):
        return jnp.concatenate([w_ref[...].astype(BF16) for w_ref in w_refs], axis=1)

    w_qk, w_ga, w_gb, w_uv = (bf16_cols(wq_ref, wk_ref), bf16_cols(wga_ref),
                              bf16_cols(wgb_ref), bf16_cols(wu_ref, wv_ref))
    q_gain = qw_ref[...] * Q_LOGIT_SCALE
    k_gain = kw_ref[...]
    for r0 in range(0, tm, IN_PROJ_ROWS):
        rows = slice(r0, r0 + IN_PROJ_ROWS)
        h = h_scr[rows, :]
        qk = jnp.dot(h, w_qk, preferred_element_type=F32)
        q_ref[rows, :] = _head_norm(qk[:, :HEAD_DIM], q_gain)
        k_ref[rows, :] = _head_norm(qk[:, HEAD_DIM:], k_gain)
        ga = jnp.dot(h, w_ga, preferred_element_type=F32)
        ga_ref[rows, :] = jax.nn.sigmoid(ga).astype(BF16)
        gb = jnp.dot(h, w_gb, preferred_element_type=F32)
        gb_ref[rows, :] = jax.nn.sigmoid(gb).astype(BF16)
        uv = jnp.dot(h, w_uv, preferred_element_type=F32)
        u_ref[rows, :] = uv[:, :HEAD_DIM]
        v_ref[rows, :] = uv[:, HEAD_DIM:].astype(BF16)


def _in_proj(x2, mod, norm_w, w_in, q_norm_w, k_norm_w, *, tm):
    m, d = x2.shape
    hd = HEAD_DIM
    gw = D_MODEL // SB_HEADS
    w_spec = lambda width, off: pl.BlockSpec((d, width), lambda i, j: (0, off // width + j))
    out_spec = lambda width: pl.BlockSpec((tm, width), lambda i, j: (i, j))
    return pl.pallas_call(
        functools.partial(_in_proj_kernel, tm=tm),
        grid=(m // tm, SB_HEADS),
        in_specs=[
            pl.BlockSpec(memory_space=pl.ANY),
            pl.BlockSpec((1, N_MOD, d), lambda i, j: (i * tm // SEQ, 0, 0)),
            pl.BlockSpec((1, d), lambda i, j: (0, 0)),
            w_spec(hd, 0), w_spec(hd, Q_OFF), w_spec(hd, K_OFF), w_spec(hd, V_OFF),
            w_spec(gw, GA_OFF), w_spec(gw, GB_OFF),
            pl.BlockSpec((1, hd), lambda i, j: (0, 0)),
            pl.BlockSpec((1, hd), lambda i, j: (0, 0)),
        ],
        out_specs=[out_spec(hd), out_spec(hd), out_spec(hd), out_spec(hd),
                   out_spec(gw), out_spec(gw)],
        out_shape=[jax.ShapeDtypeStruct((m, POOL_WIDTH), F32),
                   jax.ShapeDtypeStruct((m, SB_WIDTH), BF16),
                   jax.ShapeDtypeStruct((m, SB_WIDTH), BF16),
                   jax.ShapeDtypeStruct((m, SB_WIDTH), BF16),
                   jax.ShapeDtypeStruct((m, D_MODEL), BF16),
                   jax.ShapeDtypeStruct((m, D_MODEL), BF16)],
        scratch_shapes=[pltpu.VMEM((tm, d), BF16), pltpu.VMEM((X_SLOTS, NORM_ROWS, d), F32),
                        pltpu.SemaphoreType.DMA((X_SLOTS,))],
        compiler_params=_params("arbitrary", "arbitrary"),
        name="in_proj",
    )(x2, mod, norm_w.reshape(1, d), w_in, w_in, w_in, w_in, w_in, w_in,
      q_norm_w.reshape(1, hd), k_norm_w.reshape(1, hd))


FFN_OUT_CHUNK = 512


def _ffn_kernel(h_ref, x1_hbm, mod_ref, w1_ref, w2_ref, o_ref, xbuf, sem, *, tm, xrows):
    i = pl.program_id(0)
    c = pl.program_id(1)
    x1_copy = pltpu.make_async_copy(x1_hbm.at[pl.ds(i * tm + c * xrows, xrows), :], xbuf, sem)
    x1_copy.start()

    @pl.when(c == 0)
    def _():
        o_ref[...] = jnp.zeros_like(o_ref)

    a = jnp.dot(h_ref[...], w1_ref[...], preferred_element_type=F32)
    r = jnp.maximum(a, 0.0)
    act = (r * r).astype(BF16)
    for n0 in range(0, o_ref.shape[1], FFN_OUT_CHUNK):
        cols = slice(n0, n0 + FFN_OUT_CHUNK)
        y = jnp.dot(act, w2_ref[:, cols], preferred_element_type=F32)
        o_ref[:, cols] += mod_ref[0, 5:6, cols] * y
    x1_copy.wait()
    rows = pl.ds(pl.multiple_of(c * xrows, xrows), xrows)
    o_ref[rows, :] += xbuf[...]


def _ffn(h2, x1, mod, w1, w2, *, tm, tf):
    m, d = h2.shape
    f = w1.shape[1]
    xrows = tm // (f // tf)
    return pl.pallas_call(
        functools.partial(_ffn_kernel, tm=tm, xrows=xrows),
        grid=(m // tm, f // tf),
        in_specs=[
            pl.BlockSpec((tm, d), lambda i, c: (i, 0)),
            pl.BlockSpec(memory_space=pl.ANY),
            pl.BlockSpec((1, N_MOD, d), lambda i, c: (i * tm // SEQ, 0, 0)),
            pl.BlockSpec((d, tf), lambda i, c: (0, c)),
            pl.BlockSpec((tf, d), lambda i, c: (c, 0)),
        ],
        out_specs=pl.BlockSpec((tm, d), lambda i, c: (i, 0)),
        out_shape=jax.ShapeDtypeStruct((m, d), F32),
        scratch_shapes=[pltpu.VMEM((xrows, d), F32), pltpu.SemaphoreType.DMA(())],
        compiler_params=_params("arbitrary", "arbitrary"),
        name="ffn",
    )(h2, x1, mod, w1, w2)


POOL_PAD = 8
POOL_HEAD = max(POOL_WINDOWS)


def _pool_kernel(u_ref, w_ref, s_ref, o_ref, buf_a, buf_b):
    gd = POOL_GROUP_DIM
    body = pl.ds(POOL_PAD, SEQ)
    buf_a[0:POOL_PAD, :] = jnp.zeros((POOL_PAD, gd), F32)
    buf_b[0:POOL_PAD, :] = jnp.zeros((POOL_PAD, gd), F32)
    pos = lax.broadcasted_iota(jnp.int32, (POOL_HEAD, gd), 0)
    for g, w in enumerate(POOL_WINDOWS):
        cols = slice(g * gd, (g + 1) * gd)
        src, dst = buf_a, buf_b
        src[body, :] = u_ref[:, cols]
        shift = 1
        while shift < w:
            dst[body, :] = src[body, :] + src[pl.ds(POOL_PAD - shift, SEQ), :]
            src, dst = dst, src
            shift *= 2
        dst[body, :] = src[body, :] * (1.0 / w) - u_ref[:, cols]
        count = jnp.minimum(pos + 1, w).astype(F32)
        head = pl.ds(POOL_PAD, POOL_HEAD)
        dst[head, :] = src[head, :] / count - u_ref[0:POOL_HEAD, cols]
        mixed = jnp.dot(dst[body, :].astype(BF16), w_ref[g], preferred_element_type=F32)
        o_ref[:, cols] = (mixed * s_ref[:, cols]).astype(o_ref.dtype)


def _pool(u, w_pool_bf, pool_scale, batch):
    groups = len(POOL_WINDOWS)
    gd = POOL_GROUP_DIM
    return pl.pallas_call(
        _pool_kernel,
        grid=(batch,),
        in_specs=[
            pl.BlockSpec((SEQ, POOL_WIDTH), lambda b: (b, 0)),
            pl.BlockSpec((groups, gd, gd), lambda b: (0, 0, 0)),
            pl.BlockSpec((1, POOL_WIDTH), lambda b: (0, 0)),
        ],
        out_specs=pl.BlockSpec((SEQ, POOL_WIDTH), lambda b: (b, 0)),
        out_shape=jax.ShapeDtypeStruct((batch * SEQ, POOL_WIDTH), BF16),
        scratch_shapes=[pltpu.VMEM((POOL_PAD + SEQ, gd), F32),
                        pltpu.VMEM((POOL_PAD + SEQ, gd), F32)],
        compiler_params=_params("arbitrary"),
        name="pool",
    )(u, w_pool_bf, pool_scale.reshape(1, POOL_WIDTH))


EXP2_UNDERFLOW = -151.0


def _attn_kernel(q_ref, k_ref, v_ref, *refs, tb, hp, n_cast):
    w_refs, o_ref, wbf_refs = refs[:n_cast], refs[n_cast], refs[n_cast + 1:2 * n_cast + 1]
    carry_ref, acc_ref = refs[2 * n_cast + 1:]
    for w_ref, wbf_ref in zip(w_refs, wbf_refs):
        wbf_ref[...] = w_ref[...].astype(BF16)

    qi = pl.program_id(2)
    nseg = tb // LANES

    r = lax.broadcasted_iota(jnp.int32, (2 * LANES, 2 * LANES), 0)
    c = lax.broadcasted_iota(jnp.int32, (2 * LANES, 2 * LANES), 1)
    cum_op = jnp.where(jnp.logical_or(c >= LANES, jnp.bitwise_and(r, LANES - 1) > c),
                       1.0, 0.0).astype(BF16)
    tri_r = lax.broadcasted_iota(jnp.int32, (LANES, LANES), 0)
    tri_c = lax.broadcasted_iota(jnp.int32, (LANES, LANES), 1)
    causal = tri_c < tri_r
    segs = [slice(sg * LANES, (sg + 1) * LANES) for sg in range(nseg)]

    def block(kb, diagonal):
        start = pl.multiple_of(kb * tb, tb)
        if diagonal:
            parts = [(segs[g], g + 1) for g in range(nseg)]
        else:
            parts = [(slice(0, tb), nseg)]
        work = [(p, slice(p * HEAD_DIM, (p + 1) * HEAD_DIM), rows, nk)
                for rows, nk in parts for p in range(hp)]
        zs = [lax.dot_general(q_ref[rows, cols], k_ref[pl.ds(start, nk * LANES), cols],
                              (((1,), (1,)), ((), ())), preferred_element_type=F32)
              for _, cols, rows, nk in work]
        log_betas, sums = [], []
        for z, (_, _, _, nk) in zip(zs, work):
            log_beta = jnp.minimum(z, 0.0) - jnp.log2(1.0 + jnp.exp2(-jnp.abs(z)))
            l = log_beta - z
            cs = []
            for sg in range(nk):
                l_seg = l[:, segs[sg]]
                if diagonal and sg == nk - 1:
                    l_seg = jnp.where(causal, l_seg, 0.0)
                l_hi = l_seg.astype(BF16)
                l_lo = (l_seg - l_hi.astype(F32)).astype(BF16)
                cs.append(jnp.dot(jnp.concatenate([l_hi, l_lo], axis=1), cum_op,
                                  preferred_element_type=F32))
            log_betas.append(log_beta)
            sums.append(cs)
        top = None
        for (p, cols, rows, nk), log_beta, cs in zip(work, log_betas, sums):
            carry = carry_ref[p, rows, :]
            a_parts = [None] * nk
            for sg in range(nk - 1, -1, -1):
                a_seg = jnp.exp2(log_beta[:, segs[sg]] + (cs[sg][:, :LANES] + carry))
                if diagonal and sg == nk - 1:
                    a_seg = jnp.where(causal, a_seg, 0.0)
                a_parts[sg] = a_seg
                carry = carry + cs[sg][:, LANES:]
            a = jnp.concatenate(a_parts, axis=1) if nk > 1 else a_parts[0]
            acc_ref[p, rows, :] += jnp.dot(
                a.astype(BF16), v_ref[pl.ds(start, nk * LANES), cols],
                preferred_element_type=F32)
            carry_ref[p, rows, :] = carry
            top = jnp.max(carry) if top is None else jnp.maximum(top, jnp.max(carry))
        return top

    carry_ref[...] = jnp.zeros_like(carry_ref)
    acc_ref[...] = jnp.zeros_like(acc_ref)
    top = block(qi, True)

    def cond(state):
        kb, top = state
        return jnp.logical_and(kb >= 0, top > EXP2_UNDERFLOW)

    def body(state):
        kb, _ = state
        return kb - 1, block(kb, False)

    lax.while_loop(cond, body, (qi - 1, top))
    for p in range(hp):
        o_ref[:, p * HEAD_DIM:(p + 1) * HEAD_DIM] = acc_ref[p].astype(o_ref.dtype)


def _attn(q, k, v, weights, batch, *, tb, hp):
    nq = SEQ // tb
    groups = SB_HEADS // hp
    width = hp * HEAD_DIM
    steps = batch * groups * nq
    assert all(w.shape[0] % steps == 0 for w in weights)
    cast_specs = [pl.BlockSpec((w.shape[0] // steps, w.shape[1]),
                               lambda b, h, i: ((b * groups + h) * nq + i, 0))
                  for w in weights]
    outs = pl.pallas_call(
        functools.partial(_attn_kernel, tb=tb, hp=hp, n_cast=len(weights)),
        grid=(batch, groups, nq),
        in_specs=[
            pl.BlockSpec((tb, width), lambda b, h, i: (b * nq + i, h)),
            pl.BlockSpec((SEQ, width), lambda b, h, i: (b, h)),
            pl.BlockSpec((SEQ, width), lambda b, h, i: (b, h)),
        ] + cast_specs,
        out_specs=[pl.BlockSpec((tb, width), lambda b, h, i: (b * nq + i, h))] + cast_specs,
        out_shape=[jax.ShapeDtypeStruct((batch * SEQ, SB_WIDTH), BF16)]
        + [jax.ShapeDtypeStruct(w.shape, BF16) for w in weights],
        scratch_shapes=[pltpu.VMEM((hp, tb, LANES), F32),
                        pltpu.VMEM((hp, tb, HEAD_DIM), F32)],
        compiler_params=_params("arbitrary", "arbitrary", "arbitrary"),
        name="attn",
    )(q, k, v, *weights)
    return outs[0], outs[1:]


MIX_CHUNK = 512


def _mix_kernel(pa_ref, at_ref, sa_ref, sb_ref, x_ref, mod_ref, nw_ref, wa_ref, wb_ref, wo_ref,
                x1_ref, h2_ref, merged_scr):
    d = x_ref.shape[1]
    pa = pa_ref[...]
    at = at_ref[...]
    for n0 in range(0, d, MIX_CHUNK):
        cols = slice(n0, n0 + MIX_CHUNK)
        ya = jnp.dot(pa, wa_ref[:, cols], preferred_element_type=F32)
        yb = jnp.dot(at, wb_ref[:, cols], preferred_element_type=F32)
        merged = sa_ref[:, cols].astype(F32) * ya + sb_ref[:, cols].astype(F32) * yb
        merged_scr[:, cols] = merged.astype(BF16)
    merged = merged_scr[...]
    for n0 in range(0, d, MIX_CHUNK):
        cols = slice(n0, n0 + MIX_CHUNK)
        o = jnp.dot(merged, wo_ref[:, cols], preferred_element_type=F32)
        x1_ref[:, cols] = x_ref[:, cols] + mod_ref[0, 2:3, cols] * o
    _norm_modulate_rows(x1_ref, h2_ref, nw_ref[...], mod_ref[0, 4:5, :], mod_ref[0, 3:4, :])


def _mix(pa, at, sa, sb, x2, mod, norm_w, wa_bf, wb_bf, wo_bf, *, tm):
    m, d = x2.shape
    kdim = pa.shape[1]
    resident = pl.Buffered(1)
    return pl.pallas_call(
        _mix_kernel,
        grid=(m // tm,),
        in_specs=[
            pl.BlockSpec((tm, kdim), lambda i: (i, 0)),
            pl.BlockSpec((tm, kdim), lambda i: (i, 0)),
            pl.BlockSpec((tm, d), lambda i: (i, 0)),
            pl.BlockSpec((tm, d), lambda i: (i, 0)),
            pl.BlockSpec((tm, d), lambda i: (i, 0)),
            pl.BlockSpec((1, N_MOD, d), lambda i: (i * tm // SEQ, 0, 0)),
            pl.BlockSpec((1, d), lambda i: (0, 0)),
            pl.BlockSpec((kdim, d), lambda i: (0, 0), pipeline_mode=resident),
            pl.BlockSpec((kdim, d), lambda i: (0, 0), pipeline_mode=resident),
            pl.BlockSpec((d, d), lambda i: (0, 0), pipeline_mode=resident),
        ],
        out_specs=[pl.BlockSpec((tm, d), lambda i: (i, 0)),
                   pl.BlockSpec((tm, d), lambda i: (i, 0))],
        out_shape=[jax.ShapeDtypeStruct((m, d), F32),
                   jax.ShapeDtypeStruct((m, d), BF16)],
        scratch_shapes=[pltpu.VMEM((tm, d), BF16)],
        compiler_params=_params("arbitrary"),
        name="mix",
    )(pa, at, sa, sb, x2, mod, norm_w.reshape(1, d), wa_bf, wb_bf, wo_bf)


def kernel(x, c, w_ada, b_ada, norm1_w, w_in, q_norm_w, k_norm_w, w_pool, pool_scale,
           w_a_up, w_b_up, w_o, norm2_w, w_ff1, w_ff2):
    batch, seq, d = x.shape
    assert (seq, d) == (SEQ, D_MODEL) and w_ada.shape[0] == 1
    x2 = x.reshape(batch * seq, d)

    mod = _ada(c, w_ada[0], b_ada[0])
    u, q, k, v, sa, sb = _in_proj(x2, mod, norm1_w[0], w_in[0], q_norm_w[0], k_norm_w[0],
                                  tm=2048)
    pa = _pool(u, w_pool[0].astype(BF16), pool_scale[0], batch)
    at, (wa_bf, wb_bf, wo_bf, w1_bf, w2_bf) = _attn(
        q, k, v, (w_a_up[0], w_b_up[0], w_o[0], w_ff1[0], w_ff2[0]), batch,
        tb=256, hp=8)
    x1, h2 = _mix(pa, at, sa, sb, x2, mod, norm2_w[0], wa_bf, wb_bf, wo_bf, tm=256)
    out = _ffn(h2, x1, mod, w1_bf, w2_bf, tm=1024, tf=1024)
    return out.reshape(batch, seq, d)
```

```python
import functools
import math

import jax
import jax.numpy as jnp
from jax import lax
from jax.experimental import pallas as pl
from jax.experimental.pallas import tpu as pltpu

D_MODEL = 2048
SEQ = 2048
POOL_WIDTH = D_MODEL // 2
POOL_WINDOWS = (2, 4, 8, 16)
POOL_GROUP_DIM = POOL_WIDTH // len(POOL_WINDOWS)
HEAD_DIM = 128
SB_WIDTH = D_MODEL // 2
SB_HEADS = SB_WIDTH // HEAD_DIM
N_MOD = 6
EPS = 1e-6

Q_OFF = POOL_WIDTH
K_OFF = Q_OFF + SB_WIDTH
V_OFF = K_OFF + SB_WIDTH
GA_OFF = V_OFF + SB_WIDTH
GB_OFF = GA_OFF + D_MODEL

LANES = 128
SUBLANES = 8
V7X_VMEM_BYTES = 64 * 1024 * 1024
VMEM_LIMIT_BYTES = V7X_VMEM_BYTES * 7 // 8

BF16 = jnp.bfloat16
F32 = jnp.float32


def _params(*semantics):
    return pltpu.CompilerParams(dimension_semantics=semantics,
                                vmem_limit_bytes=VMEM_LIMIT_BYTES)


def _ada_kernel(c_ref, w_ref, b_ref, o_ref):
    c = c_ref[...]
    batch = c.shape[0]
    sc = c * jax.nn.sigmoid(c)
    if batch % SUBLANES:
        pad = SUBLANES - batch % SUBLANES
        sc = jnp.concatenate([sc, jnp.zeros((pad, sc.shape[1]), F32)], axis=0)
    acc = jnp.dot(sc.astype(BF16), w_ref[...].astype(BF16), preferred_element_type=F32)
    o_ref[...] = acc[:batch] + b_ref[...]


def _ada(c, w_ada, b_ada):
    batch, d = c.shape
    n = w_ada.shape[1]
    tn = 1024
    out = pl.pallas_call(
        _ada_kernel,
        grid=(n // tn,),
        in_specs=[
            pl.BlockSpec((batch, d), lambda j: (0, 0)),
            pl.BlockSpec((d, tn), lambda j: (0, j)),
            pl.BlockSpec((1, tn), lambda j: (0, j)),
        ],
        out_specs=pl.BlockSpec((batch, tn), lambda j: (0, j)),
        out_shape=jax.ShapeDtypeStruct((batch, n), F32),
        compiler_params=_params("arbitrary"),
        name="ada",
    )(c, w_ada, b_ada.reshape(1, n))
    return out.reshape(batch, N_MOD, d)


NORM_ROWS = 256


def _norm_modulate(x, gain, shift):
    y = x * lax.rsqrt(jnp.mean(x * x, axis=-1, keepdims=True) + EPS)
    return (y * gain + shift).astype(BF16)


def _norm_modulate_rows(x_ref, o_ref, norm_w, scale, shift):
    gain = norm_w * (1.0 + scale)

    def body(r, carry):
        rows = pl.ds(pl.multiple_of(r * NORM_ROWS, NORM_ROWS), NORM_ROWS)
        o_ref[rows, :] = _norm_modulate(x_ref[rows, :], gain, shift)
        return carry

    lax.fori_loop(0, x_ref.shape[0] // NORM_ROWS, body, 0)


Q_LOGIT_SCALE = math.log2(math.e) / math.sqrt(HEAD_DIM)


def _head_norm(blk, gain):
    ms = jnp.mean(blk * blk, axis=-1, keepdims=True)
    return (blk * lax.rsqrt(ms + EPS) * gain).astype(BF16)


IN_PROJ_ROWS = 1024
X_SLOTS = 4


def _in_proj_kernel(x_hbm, mod_ref, nw_ref, wu_ref, wq_ref, wk_ref, wv_ref, wga_ref, wgb_ref,
                    qw_ref, kw_ref, u_ref, q_ref, k_ref, v_ref, ga_ref, gb_ref,
                    h_scr, xbuf, sems, *, tm):
    i = pl.program_id(0)
    j = pl.program_id(1)
    chunks = tm // NORM_ROWS

    def x_copy(tile, r):
        slot = r % X_SLOTS
        return pltpu.make_async_copy(
            x_hbm.at[pl.ds(tile * tm + r * NORM_ROWS, NORM_ROWS), :], xbuf.at[slot],
            sems.at[slot])

    @pl.when(jnp.logical_and(j == 0, i == 0))
    def _():
        for r in range(X_SLOTS):
            x_copy(0, r).start()

    @pl.when(j == 0)
    def _():
        gain = nw_ref[...] * (1.0 + mod_ref[0, 1:2, :])
        shift = mod_ref[0, 0:1, :]

        def body(r, carry):
            x_copy(i, r).wait()
            rows = pl.ds(pl.multiple_of(r * NORM_ROWS, NORM_ROWS), NORM_ROWS)
            h_scr[rows, :] = _norm_modulate(xbuf[r % X_SLOTS], gain, shift)

            @pl.when(r + X_SLOTS < chunks)
            def _():
                x_copy(i, r + X_SLOTS).start()

            return carry

        lax.fori_loop(0, chunks, body, 0)

    @pl.when(jnp.logical_and(j == pl.num_programs(1) - 1, i + 1 < pl.num_programs(0)))
    def _():
        for r in range(X_SLOTS):
            x_copy(i + 1, r).start()

    def bf16_cols(*w_refs):
        return jnp.concatenate([w_ref[...].astype(BF16) for w_ref in w_refs], axis=1)

    w_qk, w_ga, w_gb, w_uv = (bf16_cols(wq_ref, wk_ref), bf16_cols(wga_ref),
                              bf16_cols(wgb_ref), bf16_cols(wu_ref, wv_ref))
    q_gain = qw_ref[...] * Q_LOGIT_SCALE
    k_gain = kw_ref[...]
    for r0 in range(0, tm, IN_PROJ_ROWS):
        rows = slice(r0, r0 + IN_PROJ_ROWS)
        h = h_scr[rows, :]
        qk = jnp.dot(h, w_qk, preferred_element_type=F32)
        q_ref[rows, :] = _head_norm(qk[:, :HEAD_DIM], q_gain)
        k_ref[rows, :] = _head_norm(qk[:, HEAD_DIM:], k_gain)
        ga = jnp.dot(h, w_ga, preferred_element_type=F32)
        ga_ref[rows, :] = jax.nn.sigmoid(ga).astype(BF16)
        gb = jnp.dot(h, w_gb, preferred_element_type=F32)
        gb_ref[rows, :] = jax.nn.sigmoid(gb).astype(BF16)
        uv = jnp.dot(h, w_uv, preferred_element_type=F32)
        u_ref[rows, :] = uv[:, :HEAD_DIM]
        v_ref[rows, :] = uv[:, HEAD_DIM:].astype(BF16)


def _in_proj(x2, mod, norm_w, w_in, q_norm_w, k_norm_w, *, tm):
    m, d = x2.shape
    hd = HEAD_DIM
    gw = D_MODEL // SB_HEADS
    assert POOL_WIDTH == SB_HEADS * hd and D_MODEL == SB_HEADS * gw
    assert m % tm == 0 and SEQ % tm == 0 and tm % IN_PROJ_ROWS == 0
    assert tm % NORM_ROWS == 0 and tm // NORM_ROWS >= X_SLOTS
    w_spec = lambda width, off: pl.BlockSpec((d, width), lambda i, j: (0, off // width + j))
    out_spec = lambda width: pl.BlockSpec((tm, width), lambda i, j: (i, j))
    return pl.pallas_call(
        functools.partial(_in_proj_kernel, tm=tm),
        grid=(m // tm, SB_HEADS),
        in_specs=[
            pl.BlockSpec(memory_space=pl.ANY),
            pl.BlockSpec((1, N_MOD, d), lambda i, j: (i * tm // SEQ, 0, 0)),
            pl.BlockSpec((1, d), lambda i, j: (0, 0)),
            w_spec(hd, 0), w_spec(hd, Q_OFF), w_spec(hd, K_OFF), w_spec(hd, V_OFF),
            w_spec(gw, GA_OFF), w_spec(gw, GB_OFF),
            pl.BlockSpec((1, hd), lambda i, j: (0, 0)),
            pl.BlockSpec((1, hd), lambda i, j: (0, 0)),
        ],
        out_specs=[out_spec(hd), out_spec(hd), out_spec(hd), out_spec(hd),
                   out_spec(gw), out_spec(gw)],
        out_shape=[jax.ShapeDtypeStruct((m, POOL_WIDTH), F32),
                   jax.ShapeDtypeStruct((m, SB_WIDTH), BF16),
                   jax.ShapeDtypeStruct((m, SB_WIDTH), BF16),
                   jax.ShapeDtypeStruct((m, SB_WIDTH), BF16),
                   jax.ShapeDtypeStruct((m, D_MODEL), BF16),
                   jax.ShapeDtypeStruct((m, D_MODEL), BF16)],
        scratch_shapes=[pltpu.VMEM((tm, d), BF16), pltpu.VMEM((X_SLOTS, NORM_ROWS, d), F32),
                        pltpu.SemaphoreType.DMA((X_SLOTS,))],
        compiler_params=_params("arbitrary", "arbitrary"),
        name="in_proj",
    )(x2, mod, norm_w.reshape(1, d), w_in, w_in, w_in, w_in, w_in, w_in,
      q_norm_w.reshape(1, hd), k_norm_w.reshape(1, hd))


FFN_OUT_CHUNK = 512


def _ffn_kernel(h_ref, x1_hbm, mod_ref, w1_ref, w2_ref, o_ref, xbuf, sem, *, tm, xrows):
    i = pl.program_id(0)
    c = pl.program_id(1)
    x1_copy = pltpu.make_async_copy(x1_hbm.at[pl.ds(i * tm + c * xrows, xrows), :], xbuf, sem)
    x1_copy.start()

    @pl.when(c == 0)
    def _():
        o_ref[...] = jnp.zeros_like(o_ref)

    a = jnp.dot(h_ref[...], w1_ref[...], preferred_element_type=F32)
    r = jnp.maximum(a, 0.0)
    act = (r * r).astype(BF16)
    for n0 in range(0, o_ref.shape[1], FFN_OUT_CHUNK):
        cols = slice(n0, n0 + FFN_OUT_CHUNK)
        y = jnp.dot(act, w2_ref[:, cols], preferred_element_type=F32)
        o_ref[:, cols] += mod_ref[0, 5:6, cols] * y
    x1_copy.wait()
    rows = pl.ds(pl.multiple_of(c * xrows, xrows), xrows)
    o_ref[rows, :] += xbuf[...]


def _ffn(h2, x1, mod, w1, w2, *, tm, tf):
    m, d = h2.shape
    f = w1.shape[1]
    assert m % tm == 0 and SEQ % tm == 0 and f % tf == 0 and tm % (f // tf) == 0
    assert d % FFN_OUT_CHUNK == 0
    xrows = tm // (f // tf)
    return pl.pallas_call(
        functools.partial(_ffn_kernel, tm=tm, xrows=xrows),
        grid=(m // tm, f // tf),
        in_specs=[
            pl.BlockSpec((tm, d), lambda i, c: (i, 0)),
            pl.BlockSpec(memory_space=pl.ANY),
            pl.BlockSpec((1, N_MOD, d), lambda i, c: (i * tm // SEQ, 0, 0)),
            pl.BlockSpec((d, tf), lambda i, c: (0, c)),
            pl.BlockSpec((tf, d), lambda i, c: (c, 0)),
        ],
        out_specs=pl.BlockSpec((tm, d), lambda i, c: (i, 0)),
        out_shape=jax.ShapeDtypeStruct((m, d), F32),
        scratch_shapes=[pltpu.VMEM((xrows, d), F32), pltpu.SemaphoreType.DMA(())],
        compiler_params=_params("arbitrary", "arbitrary"),
        name="ffn",
    )(h2, x1, mod, w1, w2)


POOL_PAD = max(POOL_WINDOWS) // 2
assert POOL_PAD % SUBLANES == 0
POOL_HEAD = max(POOL_WINDOWS)


def _pool_kernel(u_ref, w_ref, s_ref, o_ref, buf_a, buf_b):
    gd = POOL_GROUP_DIM
    body = pl.ds(POOL_PAD, SEQ)
    buf_a[0:POOL_PAD, :] = jnp.zeros((POOL_PAD, gd), F32)
    buf_b[0:POOL_PAD, :] = jnp.zeros((POOL_PAD, gd), F32)
    pos = lax.broadcasted_iota(jnp.int32, (POOL_HEAD, gd), 0)
    for g, w in enumerate(POOL_WINDOWS):
        cols = slice(g * gd, (g + 1) * gd)
        src, dst = buf_a, buf_b
        src[body, :] = u_ref[:, cols]
        shift = 1
        while shift < w:
            dst[body, :] = src[body, :] + src[pl.ds(POOL_PAD - shift, SEQ), :]
            src, dst = dst, src
            shift *= 2
        dst[body, :] = src[body, :] * (1.0 / w) - u_ref[:, cols]
        count = jnp.minimum(pos + 1, w).astype(F32)
        head = pl.ds(POOL_PAD, POOL_HEAD)
        dst[head, :] = src[head, :] / count - u_ref[0:POOL_HEAD, cols]
        mixed = jnp.dot(dst[body, :].astype(BF16), w_ref[g].astype(BF16),
                        preferred_element_type=F32)
        o_ref[:, cols] = (mixed * s_ref[:, cols]).astype(o_ref.dtype)


def _pool(u, w_pool, pool_scale, batch):
    groups = len(POOL_WINDOWS)
    gd = POOL_GROUP_DIM
    return pl.pallas_call(
        _pool_kernel,
        grid=(batch,),
        in_specs=[
            pl.BlockSpec((SEQ, POOL_WIDTH), lambda b: (b, 0)),
            pl.BlockSpec((groups, gd, gd), lambda b: (0, 0, 0)),
            pl.BlockSpec((1, POOL_WIDTH), lambda b: (0, 0)),
        ],
        out_specs=pl.BlockSpec((SEQ, POOL_WIDTH), lambda b: (b, 0)),
        out_shape=jax.ShapeDtypeStruct((batch * SEQ, POOL_WIDTH), BF16),
        scratch_shapes=[pltpu.VMEM((POOL_PAD + SEQ, gd), F32),
                        pltpu.VMEM((POOL_PAD + SEQ, gd), F32)],
        compiler_params=_params("arbitrary"),
        name="pool",
    )(u, w_pool, pool_scale.reshape(1, POOL_WIDTH))


EXP2_UNDERFLOW = -151.0


def _attn_kernel(q_ref, k_ref, v_ref, *refs, tb, hp, off_segs, n_cast):
    w_refs, o_ref, wbf_refs = refs[:n_cast], refs[n_cast], refs[n_cast + 1:2 * n_cast + 1]
    carry_ref, acc_ref = refs[2 * n_cast + 1:]
    for w_ref, wbf_ref in zip(w_refs, wbf_refs):
        wbf_ref[...] = w_ref[...].astype(BF16)

    qi = pl.program_id(2)
    nseg = tb // LANES

    r = lax.broadcasted_iota(jnp.int32, (LANES, 2 * LANES), 0)
    c = lax.broadcasted_iota(jnp.int32, (LANES, 2 * LANES), 1)
    cum_op = jnp.where(jnp.logical_or(c >= LANES, r > c), 1.0, 0.0).astype(BF16)
    tri_r = lax.broadcasted_iota(jnp.int32, (LANES, LANES), 0)
    tri_c = lax.broadcasted_iota(jnp.int32, (LANES, LANES), 1)
    causal = tri_c < tri_r
    segs = [slice(sg * LANES, (sg + 1) * LANES) for sg in range(nseg)]

    def block(kb, diagonal):
        if diagonal:
            start = pl.multiple_of(kb * tb, tb)
            parts = [(segs[g], g + 1) for g in range(nseg)]
        else:
            start = pl.multiple_of(kb * (off_segs * LANES), off_segs * LANES)
            parts = [(slice(0, tb), off_segs)]
        work = [(p, slice(p * HEAD_DIM, (p + 1) * HEAD_DIM), rows, nk)
                for rows, nk in parts for p in range(hp)]
        zs = [lax.dot_general(q_ref[rows, cols], k_ref[pl.ds(start, nk * LANES), cols],
                              (((1,), (1,)), ((), ())), preferred_element_type=F32)
              for _, cols, rows, nk in work]
        log_betas, sums = [], []
        for z, (_, _, _, nk) in zip(zs, work):
            log_beta = jnp.minimum(z, 0.0) - jnp.log2(1.0 + jnp.exp2(-jnp.abs(z)))
            l = log_beta - z
            cs = []
            for sg in range(nk):
                l_seg = l[:, segs[sg]]
                if diagonal and sg == nk - 1:
                    l_seg = jnp.where(causal, l_seg, 0.0)
                cs.append(jnp.dot(l_seg.astype(BF16), cum_op, preferred_element_type=F32))
            log_betas.append(log_beta)
            sums.append(cs)
        top = None
        for (p, cols, rows, nk), log_beta, cs in zip(work, log_betas, sums):
            carry = carry_ref[p, rows, :]
            a_parts = [None] * nk
            for sg in range(nk - 1, -1, -1):
                a_seg = jnp.exp2(log_beta[:, segs[sg]] + (cs[sg][:, :LANES] + carry))
                if diagonal and sg == nk - 1:
                    a_seg = jnp.where(causal, a_seg, 0.0)
                a_parts[sg] = a_seg
                carry = carry + cs[sg][:, LANES:]
            a = jnp.concatenate(a_parts, axis=1) if nk > 1 else a_parts[0]
            acc_ref[p, rows, :] += jnp.dot(
                a.astype(BF16), v_ref[pl.ds(start, nk * LANES), cols],
                preferred_element_type=F32)
            carry_ref[p, rows, :] = carry
            top = jnp.max(carry) if top is None else jnp.maximum(top, jnp.max(carry))
        return top

    carry_ref[...] = jnp.zeros_like(carry_ref)
    acc_ref[...] = jnp.zeros_like(acc_ref)
    top = block(qi, True)

    def cond(state):
        kb, top = state
        return jnp.logical_and(kb >= 0, top > EXP2_UNDERFLOW)

    def body(state):
        kb, _ = state
        return kb - 1, block(kb, False)

    lax.while_loop(cond, body, (qi * (nseg // off_segs) - 1, top))
    for p in range(hp):
        o_ref[:, p * HEAD_DIM:(p + 1) * HEAD_DIM] = acc_ref[p].astype(o_ref.dtype)


def _attn(q, k, v, weights, batch, *, tb, hp, off_segs):
    nq = SEQ // tb
    groups = SB_HEADS // hp
    width = hp * HEAD_DIM
    steps = batch * groups * nq
    assert all(w.shape[0] % steps == 0 for w in weights)
    cast_specs = [pl.BlockSpec((w.shape[0] // steps, w.shape[1]),
                               lambda b, h, i: ((b * groups + h) * nq + i, 0))
                  for w in weights]
    outs = pl.pallas_call(
        functools.partial(_attn_kernel, tb=tb, hp=hp, off_segs=off_segs, n_cast=len(weights)),
        grid=(batch, groups, nq),
        in_specs=[
            pl.BlockSpec((tb, width), lambda b, h, i: (b * nq + i, h)),
            pl.BlockSpec((SEQ, width), lambda b, h, i: (b, h)),
            pl.BlockSpec((SEQ, width), lambda b, h, i: (b, h)),
        ] + cast_specs,
        out_specs=[pl.BlockSpec((tb, width), lambda b, h, i: (b * nq + i, h))] + cast_specs,
        out_shape=[jax.ShapeDtypeStruct((batch * SEQ, SB_WIDTH), BF16)]
        + [jax.ShapeDtypeStruct(w.shape, BF16) for w in weights],
        scratch_shapes=[pltpu.VMEM((hp, tb, LANES), F32),
                        pltpu.VMEM((hp, tb, HEAD_DIM), F32)],
        compiler_params=_params("arbitrary", "arbitrary", "arbitrary"),
        name="attn",
    )(q, k, v, *weights)
    return outs[0], outs[1:]


MIX_CHUNK = 512


def _mix_kernel(pa_ref, at_ref, sa_ref, sb_ref, x_ref, mod_ref, nw_ref, wa_ref, wb_ref, wo_ref,
                x1_ref, h2_ref, merged_scr):
    d = x_ref.shape[1]
    pa = pa_ref[...]
    at = at_ref[...]
    for n0 in range(0, d, MIX_CHUNK):
        cols = slice(n0, n0 + MIX_CHUNK)
        ya = jnp.dot(pa, wa_ref[:, cols], preferred_element_type=F32)
        yb = jnp.dot(at, wb_ref[:, cols], preferred_element_type=F32)
        merged = sa_ref[:, cols].astype(F32) * ya + sb_ref[:, cols].astype(F32) * yb
        merged_scr[:, cols] = merged.astype(BF16)
    merged = merged_scr[...]
    for n0 in range(0, d, MIX_CHUNK):
        cols = slice(n0, n0 + MIX_CHUNK)
        o = jnp.dot(merged, wo_ref[:, cols], preferred_element_type=F32)
        x1_ref[:, cols] = x_ref[:, cols] + mod_ref[0, 2:3, cols] * o
    _norm_modulate_rows(x1_ref, h2_ref, nw_ref[...], mod_ref[0, 4:5, :], mod_ref[0, 3:4, :])


def _mix(pa, at, sa, sb, x2, mod, norm_w, wa_bf, wb_bf, wo_bf, *, tm):
    m, d = x2.shape
    kdim = pa.shape[1]
    resident = pl.Buffered(1)
    return pl.pallas_call(
        _mix_kernel,
        grid=(m // tm,),
        in_specs=[
            pl.BlockSpec((tm, kdim), lambda i: (i, 0)),
            pl.BlockSpec((tm, kdim), lambda i: (i, 0)),
            pl.BlockSpec((tm, d), lambda i: (i, 0)),
            pl.BlockSpec((tm, d), lambda i: (i, 0)),
            pl.BlockSpec((tm, d), lambda i: (i, 0)),
            pl.BlockSpec((1, N_MOD, d), lambda i: (i * tm // SEQ, 0, 0)),
            pl.BlockSpec((1, d), lambda i: (0, 0)),
            pl.BlockSpec((kdim, d), lambda i: (0, 0), pipeline_mode=resident),
            pl.BlockSpec((kdim, d), lambda i: (0, 0), pipeline_mode=resident),
            pl.BlockSpec((d, d), lambda i: (0, 0), pipeline_mode=resident),
        ],
        out_specs=[pl.BlockSpec((tm, d), lambda i: (i, 0)),
                   pl.BlockSpec((tm, d), lambda i: (i, 0))],
        out_shape=[jax.ShapeDtypeStruct((m, d), F32),
                   jax.ShapeDtypeStruct((m, d), BF16)],
        scratch_shapes=[pltpu.VMEM((tm, d), BF16)],
        compiler_params=_params("arbitrary"),
        name="mix",
    )(pa, at, sa, sb, x2, mod, norm_w.reshape(1, d), wa_bf, wb_bf, wo_bf)


def kernel(x, c, w_ada, b_ada, norm1_w, w_in, q_norm_w, k_norm_w, w_pool, pool_scale,
           w_a_up, w_b_up, w_o, norm2_w, w_ff1, w_ff2):
    batch, seq, d = x.shape
    assert (seq, d) == (SEQ, D_MODEL) and w_ada.shape[0] == 1
    x2 = x.reshape(batch * seq, d)

    mod = _ada(c, w_ada[0], b_ada[0])
    u, q, k, v, sa, sb = _in_proj(x2, mod, norm1_w[0], w_in[0], q_norm_w[0], k_norm_w[0],
                                  tm=2048)
    pa = _pool(u, w_pool[0], pool_scale[0], batch)
    at, (wa_bf, wb_bf, wo_bf, w1_bf, w2_bf) = _attn(
        q, k, v, (w_a_up[0], w_b_up[0], w_o[0], w_ff1[0], w_ff2[0]), batch,
        tb=256, hp=8, off_segs=1)
    x1, h2 = _mix(pa, at, sa, sb, x2, mod, norm2_w[0], wa_bf, wb_bf, wo_bf, tm=256)
    out = _ffn(h2, x1, mod, w1_bf, w2_bf, tm=1024, tf=1024)
    return out.reshape(batch, seq, d)
```

```python
import functools
import math

import jax
import jax.numpy as jnp
from jax import lax
from jax.experimental import pallas as pl
from jax.experimental.pallas import tpu as pltpu

D_MODEL = 2048
SEQ = 2048
POOL_WIDTH = D_MODEL // 2
POOL_WINDOWS = (2, 4, 8, 16)
POOL_GROUP_DIM = POOL_WIDTH // len(POOL_WINDOWS)
HEAD_DIM = 128
SB_WIDTH = D_MODEL // 2
SB_HEADS = SB_WIDTH // HEAD_DIM
N_MOD = 6
EPS = 1e-6

Q_OFF = POOL_WIDTH
K_OFF = Q_OFF + SB_WIDTH
V_OFF = K_OFF + SB_WIDTH
GA_OFF = V_OFF + SB_WIDTH
GB_OFF = GA_OFF + D_MODEL

LANES = 128
SUBLANES = 8
V7X_VMEM_BYTES = 64 * 1024 * 1024
VMEM_LIMIT_BYTES = V7X_VMEM_BYTES * 7 // 8

BF16 = jnp.bfloat16
F32 = jnp.float32


def _params(*semantics):
    return pltpu.CompilerParams(dimension_semantics=semantics,
                                vmem_limit_bytes=VMEM_LIMIT_BYTES)


ADA_STREAMS = 4


def _ada_kernel(c_ref, b_ref, *refs):
    w_refs, o_ref = refs[:-1], refs[-1]
    c = c_ref[...]
    batch = c.shape[0]
    sc = c * jax.nn.sigmoid(c)
    if batch % SUBLANES:
        pad = SUBLANES - batch % SUBLANES
        sc = jnp.concatenate([sc, jnp.zeros((pad, sc.shape[1]), F32)], axis=0)
    sc = sc.astype(BF16)
    rows = w_refs[0].shape[0]
    acc = b_ref[...]
    for s, w_ref in enumerate(w_refs):
        acc = acc + jnp.dot(sc[:, s * rows:(s + 1) * rows], w_ref[...].astype(BF16),
                            preferred_element_type=F32)[:batch]
    o_ref[...] = acc


def _ada(c, w_ada, b_ada):
    batch, d = c.shape
    n = w_ada.shape[1]
    tn = 1024
    assert n % tn == 0 and d % ADA_STREAMS == 0
    slab = d // ADA_STREAMS
    out = pl.pallas_call(
        _ada_kernel,
        grid=(n // tn,),
        in_specs=[
            pl.BlockSpec((batch, d), lambda j: (0, 0)),
            pl.BlockSpec((1, tn), lambda j: (0, j)),
        ] + [pl.BlockSpec((slab, tn), functools.partial(lambda j, s: (s, j), s=s))
             for s in range(ADA_STREAMS)],
        out_specs=pl.BlockSpec((batch, tn), lambda j: (0, j)),
        out_shape=jax.ShapeDtypeStruct((batch, n), F32),
        compiler_params=_params("arbitrary"),
        name="ada",
    )(c, b_ada.reshape(1, n), *([w_ada] * ADA_STREAMS))
    return out.reshape(batch, N_MOD, d)


NORM_ROWS = 256


def _norm_modulate(x, gain, shift):
    y = x * lax.rsqrt(jnp.mean(x * x, axis=-1, keepdims=True) + EPS)
    return (y * gain + shift).astype(BF16)


def _norm_modulate_rows(x_ref, o_ref, norm_w, scale, shift):
    gain = norm_w * (1.0 + scale)

    def body(r, carry):
        rows = pl.ds(pl.multiple_of(r * NORM_ROWS, NORM_ROWS), NORM_ROWS)
        o_ref[rows, :] = _norm_modulate(x_ref[rows, :], gain, shift)
        return carry

    lax.fori_loop(0, x_ref.shape[0] // NORM_ROWS, body, 0)


Q_LOGIT_SCALE = math.log2(math.e) / math.sqrt(HEAD_DIM)


def _head_norm(blk, gain):
    ms = jnp.mean(blk * blk, axis=-1, keepdims=True)
    return (blk * lax.rsqrt(ms + EPS) * gain).astype(BF16)


IN_PROJ_ROWS = 1024
X_SLOTS = 4


def _in_proj_kernel(x_hbm, mod_ref, nw_ref, wu_ref, wq_ref, wk_ref, wv_ref, wga_ref, wgb_ref,
                    qw_ref, kw_ref, u_ref, q_ref, k_ref, v_ref, ga_ref, gb_ref,
                    h_scr, xbuf, sems, *, tm):
    i = pl.program_id(0)
    j = pl.program_id(1)
    chunks = tm // NORM_ROWS

    def x_copy(tile, r):
        slot = r % X_SLOTS
        return pltpu.make_async_copy(
            x_hbm.at[pl.ds(tile * tm + r * NORM_ROWS, NORM_ROWS), :], xbuf.at[slot],
            sems.at[slot])

    @pl.when(jnp.logical_and(j == 0, i == 0))
    def _():
        for r in range(X_SLOTS):
            x_copy(0, r).start()

    @pl.when(j == 0)
    def _():
        gain = nw_ref[...] * (1.0 + mod_ref[0, 1:2, :])
        shift = mod_ref[0, 0:1, :]

        def body(r, carry):
            x_copy(i, r).wait()
            rows = pl.ds(pl.multiple_of(r * NORM_ROWS, NORM_ROWS), NORM_ROWS)
            h_scr[rows, :] = _norm_modulate(xbuf[r % X_SLOTS], gain, shift)

            @pl.when(r + X_SLOTS < chunks)
            def _():
                x_copy(i, r + X_SLOTS).start()

            return carry

        lax.fori_loop(0, chunks, body, 0)

    @pl.when(jnp.logical_and(j == pl.num_programs(1) - 1, i + 1 < pl.num_programs(0)))
    def _():
        for r in range(X_SLOTS):
            x_copy(i + 1, r).start()

    def bf16_cols(*w_refs):
        return jnp.concatenate([w_ref[...].astype(BF16) for w_ref in w_refs], axis=1)

    w_qk, w_ga, w_gb, w_uv = (bf16_cols(wq_ref, wk_ref), bf16_cols(wga_ref),
                              bf16_cols(wgb_ref), bf16_cols(wu_ref, wv_ref))
    q_gain = qw_ref[...] * Q_LOGIT_SCALE
    k_gain = kw_ref[...]
    for r0 in range(0, tm, IN_PROJ_ROWS):
        rows = slice(r0, r0 + IN_PROJ_ROWS)
        h = h_scr[rows, :]
        qk = jnp.dot(h, w_qk, preferred_element_type=F32)
        q_ref[rows, :] = _head_norm(qk[:, :HEAD_DIM], q_gain)
        k_ref[rows, :] = _head_norm(qk[:, HEAD_DIM:], k_gain)
        ga = jnp.dot(h, w_ga, preferred_element_type=F32)
        ga_ref[rows, :] = jax.nn.sigmoid(ga).astype(BF16)
        gb = jnp.dot(h, w_gb, preferred_element_type=F32)
        gb_ref[rows, :] = jax.nn.sigmoid(gb).astype(BF16)
        uv = jnp.dot(h, w_uv, preferred_element_type=F32)
        u_ref[rows, :] = uv[:, :HEAD_DIM]
        v_ref[rows, :] = uv[:, HEAD_DIM:].astype(BF16)


def _in_proj(x2, mod, norm_w, w_in, q_norm_w, k_norm_w, *, tm):
    m, d = x2.shape
    hd = HEAD_DIM
    gw = D_MODEL // SB_HEADS
    assert POOL_WIDTH == SB_HEADS * hd and D_MODEL == SB_HEADS * gw
    assert m % tm == 0 and SEQ % tm == 0 and tm % IN_PROJ_ROWS == 0
    assert tm % NORM_ROWS == 0 and tm // NORM_ROWS >= X_SLOTS
    w_spec = lambda width, off: pl.BlockSpec((d, width), lambda i, j: (0, off // width + j))
    out_spec = lambda width: pl.BlockSpec((tm, width), lambda i, j: (i, j))
    return pl.pallas_call(
        functools.partial(_in_proj_kernel, tm=tm),
        grid=(m // tm, SB_HEADS),
        in_specs=[
            pl.BlockSpec(memory_space=pl.ANY),
            pl.BlockSpec((1, N_MOD, d), lambda i, j: (i * tm // SEQ, 0, 0)),
            pl.BlockSpec((1, d), lambda i, j: (0, 0)),
            w_spec(hd, 0), w_spec(hd, Q_OFF), w_spec(hd, K_OFF), w_spec(hd, V_OFF),
            w_spec(gw, GA_OFF), w_spec(gw, GB_OFF),
            pl.BlockSpec((1, hd), lambda i, j: (0, 0)),
            pl.BlockSpec((1, hd), lambda i, j: (0, 0)),
        ],
        out_specs=[out_spec(hd), out_spec(hd), out_spec(hd), out_spec(hd),
                   out_spec(gw), out_spec(gw)],
        out_shape=[jax.ShapeDtypeStruct((m, POOL_WIDTH), F32),
                   jax.ShapeDtypeStruct((m, SB_WIDTH), BF16),
                   jax.ShapeDtypeStruct((m, SB_WIDTH), BF16),
                   jax.ShapeDtypeStruct((m, SB_WIDTH), BF16),
                   jax.ShapeDtypeStruct((m, D_MODEL), BF16),
                   jax.ShapeDtypeStruct((m, D_MODEL), BF16)],
        scratch_shapes=[pltpu.VMEM((tm, d), BF16), pltpu.VMEM((X_SLOTS, NORM_ROWS, d), F32),
                        pltpu.SemaphoreType.DMA((X_SLOTS,))],
        compiler_params=_params("arbitrary", "arbitrary"),
        name="in_proj",
    )(x2, mod, norm_w.reshape(1, d), w_in, w_in, w_in, w_in, w_in, w_in,
      q_norm_w.reshape(1, hd), k_norm_w.reshape(1, hd))


FFN_OUT_CHUNK = 512


def _ffn_kernel(h_ref, x1_hbm, mod_ref, w1_ref, w2_ref, o_ref, xbuf, sem, *, tm, xrows):
    i = pl.program_id(0)
    c = pl.program_id(1)
    x1_copy = pltpu.make_async_copy(x1_hbm.at[pl.ds(i * tm + c * xrows, xrows), :], xbuf, sem)
    x1_copy.start()

    @pl.when(c == 0)
    def _():
        o_ref[...] = jnp.zeros_like(o_ref)

    a = jnp.dot(h_ref[...], w1_ref[...], preferred_element_type=F32)
    r = jnp.maximum(a, 0.0)
    act = (r * r).astype(BF16)
    for n0 in range(0, o_ref.shape[1], FFN_OUT_CHUNK):
        cols = slice(n0, n0 + FFN_OUT_CHUNK)
        y = jnp.dot(act, w2_ref[:, cols], preferred_element_type=F32)
        o_ref[:, cols] += mod_ref[0, 5:6, cols] * y
    x1_copy.wait()
    rows = pl.ds(pl.multiple_of(c * xrows, xrows), xrows)
    o_ref[rows, :] += xbuf[...]


def _ffn(h2, x1, mod, w1, w2, *, tm, tf):
    m, d = h2.shape
    f = w1.shape[1]
    assert m % tm == 0 and SEQ % tm == 0 and f % tf == 0 and tm % (f // tf) == 0
    assert d % FFN_OUT_CHUNK == 0
    xrows = tm // (f // tf)
    return pl.pallas_call(
        functools.partial(_ffn_kernel, tm=tm, xrows=xrows),
        grid=(m // tm, f // tf),
        in_specs=[
            pl.BlockSpec((tm, d), lambda i, c: (i, 0)),
            pl.BlockSpec(memory_space=pl.ANY),
            pl.BlockSpec((1, N_MOD, d), lambda i, c: (i * tm // SEQ, 0, 0)),
            pl.BlockSpec((d, tf), lambda i, c: (0, c)),
            pl.BlockSpec((tf, d), lambda i, c: (c, 0)),
        ],
        out_specs=pl.BlockSpec((tm, d), lambda i, c: (i, 0)),
        out_shape=jax.ShapeDtypeStruct((m, d), F32),
        scratch_shapes=[pltpu.VMEM((xrows, d), F32), pltpu.SemaphoreType.DMA(())],
        compiler_params=_params("arbitrary", "arbitrary"),
        name="ffn",
    )(h2, x1, mod, w1, w2)


POOL_PAD = max(POOL_WINDOWS) // 2
assert POOL_PAD % SUBLANES == 0
POOL_HEAD = max(POOL_WINDOWS)


def _pool_kernel(u_ref, w_ref, s_ref, o_ref, buf_a, buf_b):
    gd = POOL_GROUP_DIM
    body = pl.ds(POOL_PAD, SEQ)
    buf_a[0:POOL_PAD, :] = jnp.zeros((POOL_PAD, gd), F32)
    buf_b[0:POOL_PAD, :] = jnp.zeros((POOL_PAD, gd), F32)
    pos = lax.broadcasted_iota(jnp.int32, (POOL_HEAD, gd), 0)
    for g, w in enumerate(POOL_WINDOWS):
        cols = slice(g * gd, (g + 1) * gd)
        src, dst = buf_a, buf_b
        src[body, :] = u_ref[:, cols]
        shift = 1
        while shift < w:
            dst[body, :] = src[body, :] + src[pl.ds(POOL_PAD - shift, SEQ), :]
            src, dst = dst, src
            shift *= 2
        dst[body, :] = src[body, :] * (1.0 / w) - u_ref[:, cols]
        count = jnp.minimum(pos + 1, w).astype(F32)
        head = pl.ds(POOL_PAD, POOL_HEAD)
        dst[head, :] = src[head, :] / count - u_ref[0:POOL_HEAD, cols]
        mixed = jnp.dot(dst[body, :].astype(BF16), w_ref[g].astype(BF16),
                        preferred_element_type=F32)
        o_ref[:, cols] = (mixed * s_ref[:, cols]).astype(o_ref.dtype)


def _pool(u, w_pool, pool_scale, batch):
    groups = len(POOL_WINDOWS)
    gd = POOL_GROUP_DIM
    return pl.pallas_call(
        _pool_kernel,
        grid=(batch,),
        in_specs=[
            pl.BlockSpec((SEQ, POOL_WIDTH), lambda b: (b, 0)),
            pl.BlockSpec((groups, gd, gd), lambda b: (0, 0, 0)),
            pl.BlockSpec((1, POOL_WIDTH), lambda b: (0, 0)),
        ],
        out_specs=pl.BlockSpec((SEQ, POOL_WIDTH), lambda b: (b, 0)),
        out_shape=jax.ShapeDtypeStruct((batch * SEQ, POOL_WIDTH), BF16),
        scratch_shapes=[pltpu.VMEM((POOL_PAD + SEQ, gd), F32),
                        pltpu.VMEM((POOL_PAD + SEQ, gd), F32)],
        compiler_params=_params("arbitrary"),
        name="pool",
    )(u, w_pool, pool_scale.reshape(1, POOL_WIDTH))


EXP2_UNDERFLOW = -151.0


def _attn_kernel(q_ref, k_ref, v_ref, *refs, tb, hp, n_cast):
    w_refs, o_ref, wbf_refs = refs[:n_cast], refs[n_cast], refs[n_cast + 1:2 * n_cast + 1]
    carry_ref, acc_ref = refs[2 * n_cast + 1:]
    for w_ref, wbf_ref in zip(w_refs, wbf_refs):
        wbf_ref[...] = w_ref[...].astype(BF16)

    qi = pl.program_id(2)
    nseg = tb // LANES

    r = lax.broadcasted_iota(jnp.int32, (LANES, 2 * LANES), 0)
    c = lax.broadcasted_iota(jnp.int32, (LANES, 2 * LANES), 1)
    cum_op = jnp.where(jnp.logical_or(c >= LANES, r > c), 1.0, 0.0).astype(BF16)
    tri_r = lax.broadcasted_iota(jnp.int32, (LANES, LANES), 0)
    tri_c = lax.broadcasted_iota(jnp.int32, (LANES, LANES), 1)
    causal = tri_c < tri_r
    segs = [slice(sg * LANES, (sg + 1) * LANES) for sg in range(nseg)]

    def block(kb, diagonal):
        start = pl.multiple_of(kb * tb, tb)
        if diagonal:
            parts = [(segs[g], g + 1) for g in range(nseg)]
        else:
            parts = [(slice(0, tb), nseg)]
        work = [(p, slice(p * HEAD_DIM, (p + 1) * HEAD_DIM), rows, nk)
                for rows, nk in parts for p in range(hp)]
        zs = [lax.dot_general(q_ref[rows, cols], k_ref[pl.ds(start, nk * LANES), cols],
                              (((1,), (1,)), ((), ())), preferred_element_type=F32)
              for _, cols, rows, nk in work]
        log_betas, sums = [], []
        for z, (_, _, _, nk) in zip(zs, work):
            log_beta = jnp.minimum(z, 0.0) - jnp.log2(1.0 + jnp.exp2(-jnp.abs(z)))
            l = log_beta - z
            cs = []
            for sg in range(nk):
                l_seg = l[:, segs[sg]]
                if diagonal and sg == nk - 1:
                    l_seg = jnp.where(causal, l_seg, 0.0)
                cs.append(jnp.dot(l_seg.astype(BF16), cum_op, preferred_element_type=F32))
            log_betas.append(log_beta)
            sums.append(cs)
        top = None
        for (p, cols, rows, nk), log_beta, cs in zip(work, log_betas, sums):
            carry = carry_ref[p, rows, :]
            a_parts = [None] * nk
            for sg in range(nk - 1, -1, -1):
                a_seg = jnp.exp2(log_beta[:, segs[sg]] + (cs[sg][:, :LANES] + carry))
                if diagonal and sg == nk - 1:
                    a_seg = jnp.where(causal, a_seg, 0.0)
                a_parts[sg] = a_seg
                carry = carry + cs[sg][:, LANES:]
            a = jnp.concatenate(a_parts, axis=1) if nk > 1 else a_parts[0]
            acc_ref[p, rows, :] += jnp.dot(
                a.astype(BF16), v_ref[pl.ds(start, nk * LANES), cols],
                preferred_element_type=F32)
            carry_ref[p, rows, :] = carry
            top = jnp.max(carry) if top is None else jnp.maximum(top, jnp.max(carry))
        return top

    carry_ref[...] = jnp.zeros_like(carry_ref)
    acc_ref[...] = jnp.zeros_like(acc_ref)
    top = block(qi, True)

    def cond(state):
        kb, top = state
        return jnp.logical_and(kb >= 0, top > EXP2_UNDERFLOW)

    def body(state):
        kb, _ = state
        return kb - 1, block(kb, False)

    lax.while_loop(cond, body, (qi - 1, top))
    for p in range(hp):
        o_ref[:, p * HEAD_DIM:(p + 1) * HEAD_DIM] = acc_ref[p].astype(o_ref.dtype)


def _attn(q, k, v, weights, batch, *, tb, hp):
    nq = SEQ // tb
    groups = SB_HEADS // hp
    width = hp * HEAD_DIM
    steps = batch * groups * nq
    assert all(w.shape[0] % steps == 0 for w in weights)
    cast_specs = [pl.BlockSpec((w.shape[0] // steps, w.shape[1]),
                               lambda b, h, i: ((b * groups + h) * nq + i, 0))
                  for w in weights]
    outs = pl.pallas_call(
        functools.partial(_attn_kernel, tb=tb, hp=hp, n_cast=len(weights)),
        grid=(batch, groups, nq),
        in_specs=[
            pl.BlockSpec((tb, width), lambda b, h, i: (b * nq + i, h)),
            pl.BlockSpec((SEQ, width), lambda b, h, i: (b, h)),
            pl.BlockSpec((SEQ, width), lambda b, h, i: (b, h)),
        ] + cast_specs,
        out_specs=[pl.BlockSpec((tb, width), lambda b, h, i: (b * nq + i, h))] + cast_specs,
        out_shape=[jax.ShapeDtypeStruct((batch * SEQ, SB_WIDTH), BF16)]
        + [jax.ShapeDtypeStruct(w.shape, BF16) for w in weights],
        scratch_shapes=[pltpu.VMEM((hp, tb, LANES), F32),
                        pltpu.VMEM((hp, tb, HEAD_DIM), F32)],
        compiler_params=_params("arbitrary", "arbitrary", "arbitrary"),
        name="attn",
    )(q, k, v, *weights)
    return outs[0], outs[1:]


MIX_CHUNK = 512


def _mix_kernel(pa_ref, at_ref, sa_ref, sb_ref, x_ref, mod_ref, nw_ref, wa_ref, wb_ref, wo_ref,
                x1_ref, h2_ref, merged_scr):
    d = x_ref.shape[1]
    pa = pa_ref[...]
    at = at_ref[...]
    for n0 in range(0, d, MIX_CHUNK):
        cols = slice(n0, n0 + MIX_CHUNK)
        ya = jnp.dot(pa, wa_ref[:, cols], preferred_element_type=F32)
        yb = jnp.dot(at, wb_ref[:, cols], preferred_element_type=F32)
        merged = sa_ref[:, cols].astype(F32) * ya + sb_ref[:, cols].astype(F32) * yb
        merged_scr[:, cols] = merged.astype(BF16)
    merged = merged_scr[...]
    for n0 in range(0, d, MIX_CHUNK):
        cols = slice(n0, n0 + MIX_CHUNK)
        o = jnp.dot(merged, wo_ref[:, cols], preferred_element_type=F32)
        x1_ref[:, cols] = x_ref[:, cols] + mod_ref[0, 2:3, cols] * o
    _norm_modulate_rows(x1_ref, h2_ref, nw_ref[...], mod_ref[0, 4:5, :], mod_ref[0, 3:4, :])


def _mix(pa, at, sa, sb, x2, mod, norm_w, wa_bf, wb_bf, wo_bf, *, tm):
    m, d = x2.shape
    kdim = pa.shape[1]
    resident = pl.Buffered(1)
    return pl.pallas_call(
        _mix_kernel,
        grid=(m // tm,),
        in_specs=[
            pl.BlockSpec((tm, kdim), lambda i: (i, 0)),
            pl.BlockSpec((tm, kdim), lambda i: (i, 0)),
            pl.BlockSpec((tm, d), lambda i: (i, 0)),
            pl.BlockSpec((tm, d), lambda i: (i, 0)),
            pl.BlockSpec((tm, d), lambda i: (i, 0)),
            pl.BlockSpec((1, N_MOD, d), lambda i: (i * tm // SEQ, 0, 0)),
            pl.BlockSpec((1, d), lambda i: (0, 0)),
            pl.BlockSpec((kdim, d), lambda i: (0, 0), pipeline_mode=resident),
            pl.BlockSpec((kdim, d), lambda i: (0, 0), pipeline_mode=resident),
            pl.BlockSpec((d, d), lambda i: (0, 0), pipeline_mode=resident),
        ],
        out_specs=[pl.BlockSpec((tm, d), lambda i: (i, 0)),
                   pl.BlockSpec((tm, d), lambda i: (i, 0))],
        out_shape=[jax.ShapeDtypeStruct((m, d), F32),
                   jax.ShapeDtypeStruct((m, d), BF16)],
        scratch_shapes=[pltpu.VMEM((tm, d), BF16)],
        compiler_params=_params("arbitrary"),
        name="mix",
    )(pa, at, sa, sb, x2, mod, norm_w.reshape(1, d), wa_bf, wb_bf, wo_bf)


def kernel(x, c, w_ada, b_ada, norm1_w, w_in, q_norm_w, k_norm_w, w_pool, pool_scale,
           w_a_up, w_b_up, w_o, norm2_w, w_ff1, w_ff2):
    batch, seq, d = x.shape
    assert (seq, d) == (SEQ, D_MODEL) and w_ada.shape[0] == 1
    x2 = x.reshape(batch * seq, d)

    mod = _ada(c, w_ada[0], b_ada[0])
    u, q, k, v, sa, sb = _in_proj(x2, mod, norm1_w[0], w_in[0], q_norm_w[0], k_norm_w[0],
                                  tm=2048)
    pa = _pool(u, w_pool[0], pool_scale[0], batch)
    at, (wa_bf, wb_bf, wo_bf, w1_bf, w2_bf) = _attn(
        q, k, v, (w_a_up[0], w_b_up[0], w_o[0], w_ff1[0], w_ff2[0]), batch,
        tb=256, hp=8)
    x1, h2 = _mix(pa, at, sa, sb, x2, mod, norm2_w[0], wa_bf, wb_bf, wo_bf, tm=256)
    out = _ffn(h2, x1, mod, w1_bf, w2_bf, tm=1024, tf=1024)
    return out.reshape(batch, seq, d)
```

```python
import functools
import math

import jax
import jax.numpy as jnp
from jax import lax
from jax.experimental import pallas as pl
from jax.experimental.pallas import tpu as pltpu

D_MODEL = 2048
SEQ = 2048
POOL_WIDTH = D_MODEL // 2
POOL_WINDOWS = (2, 4, 8, 16)
POOL_GROUP_DIM = POOL_WIDTH // len(POOL_WINDOWS)
HEAD_DIM = 128
SB_WIDTH = D_MODEL // 2
SB_HEADS = SB_WIDTH // HEAD_DIM
N_MOD = 6
EPS = 1e-6

Q_OFF = POOL_WIDTH
K_OFF = Q_OFF + SB_WIDTH
V_OFF = K_OFF + SB_WIDTH
GA_OFF = V_OFF + SB_WIDTH
GB_OFF = GA_OFF + D_MODEL

LANES = 128
SUBLANES = 8
V7X_VMEM_BYTES = 64 * 1024 * 1024
VMEM_LIMIT_BYTES = V7X_VMEM_BYTES * 7 // 8

BF16 = jnp.bfloat16
F32 = jnp.float32


def _params(*semantics):
    return pltpu.CompilerParams(dimension_semantics=semantics,
                                vmem_limit_bytes=VMEM_LIMIT_BYTES)


ADA_STREAMS = 8


def _ada_kernel(c_ref, b_ref, *refs):
    w_refs, o_ref = refs[:-1], refs[-1]
    c = c_ref[...]
    batch = c.shape[0]
    sc = c * jax.nn.sigmoid(c)
    if batch % SUBLANES:
        pad = SUBLANES - batch % SUBLANES
        sc = jnp.concatenate([sc, jnp.zeros((pad, sc.shape[1]), F32)], axis=0)
    sc = sc.astype(BF16)
    rows = w_refs[0].shape[0]
    acc = b_ref[...]
    for s, w_ref in enumerate(w_refs):
        acc = acc + jnp.dot(sc[:, s * rows:(s + 1) * rows], w_ref[...].astype(BF16),
                            preferred_element_type=F32)[:batch]
    o_ref[...] = acc


def _ada(c, w_ada, b_ada):
    batch, d = c.shape
    n = w_ada.shape[1]
    tn = 1024
    assert n % tn == 0 and d % ADA_STREAMS == 0
    slab = d // ADA_STREAMS
    out = pl.pallas_call(
        _ada_kernel,
        grid=(n // tn,),
        in_specs=[
            pl.BlockSpec((batch, d), lambda j: (0, 0)),
            pl.BlockSpec((1, tn), lambda j: (0, j)),
        ] + [pl.BlockSpec((slab, tn), functools.partial(lambda j, s: (s, j), s=s))
             for s in range(ADA_STREAMS)],
        out_specs=pl.BlockSpec((batch, tn), lambda j: (0, j)),
        out_shape=jax.ShapeDtypeStruct((batch, n), F32),
        compiler_params=_params("arbitrary"),
        name="ada",
    )(c, b_ada.reshape(1, n), *([w_ada] * ADA_STREAMS))
    return out.reshape(batch, N_MOD, d)


NORM_ROWS = 256


def _norm_modulate(x, gain, shift):
    y = x * lax.rsqrt(jnp.mean(x * x, axis=-1, keepdims=True) + EPS)
    return (y * gain + shift).astype(BF16)


def _norm_modulate_rows(x_ref, o_ref, norm_w, scale, shift):
    gain = norm_w * (1.0 + scale)

    def body(r, carry):
        rows = pl.ds(pl.multiple_of(r * NORM_ROWS, NORM_ROWS), NORM_ROWS)
        o_ref[rows, :] = _norm_modulate(x_ref[rows, :], gain, shift)
        return carry

    lax.fori_loop(0, x_ref.shape[0] // NORM_ROWS, body, 0)


Q_LOGIT_SCALE = math.log2(math.e) / math.sqrt(HEAD_DIM)


def _head_norm(blk, gain):
    ms = jnp.mean(blk * blk, axis=-1, keepdims=True)
    return (blk * lax.rsqrt(ms + EPS) * gain).astype(BF16)


IN_PROJ_ROWS = 1024
X_SLOTS = 4


def _in_proj_kernel(x_hbm, mod_ref, nw_ref, wu_ref, wq_ref, wk_ref, wv_ref, wga_ref, wgb_ref,
                    qw_ref, kw_ref, u_ref, q_ref, k_ref, v_ref, ga_ref, gb_ref,
                    h_scr, xbuf, sems, *, tm):
    i = pl.program_id(0)
    j = pl.program_id(1)
    chunks = tm // NORM_ROWS

    def x_copy(tile, r):
        slot = r % X_SLOTS
        return pltpu.make_async_copy(
            x_hbm.at[pl.ds(tile * tm + r * NORM_ROWS, NORM_ROWS), :], xbuf.at[slot],
            sems.at[slot])

    @pl.when(jnp.logical_and(j == 0, i == 0))
    def _():
        for r in range(X_SLOTS):
            x_copy(0, r).start()

    @pl.when(j == 0)
    def _():
        gain = nw_ref[...] * (1.0 + mod_ref[0, 1:2, :])
        shift = mod_ref[0, 0:1, :]

        def body(r, carry):
            x_copy(i, r).wait()
            rows = pl.ds(pl.multiple_of(r * NORM_ROWS, NORM_ROWS), NORM_ROWS)
            h_scr[rows, :] = _norm_modulate(xbuf[r % X_SLOTS], gain, shift)

            @pl.when(r + X_SLOTS < chunks)
            def _():
                x_copy(i, r + X_SLOTS).start()

            return carry

        lax.fori_loop(0, chunks, body, 0)

    @pl.when(jnp.logical_and(j == pl.num_programs(1) - 1, i + 1 < pl.num_programs(0)))
    def _():
        for r in range(X_SLOTS):
            x_copy(i + 1, r).start()

    def bf16_cols(*w_refs):
        return jnp.concatenate([w_ref[...].astype(BF16) for w_ref in w_refs], axis=1)

    w_qk, w_ga, w_gb, w_uv = (bf16_cols(wq_ref, wk_ref), bf16_cols(wga_ref),
                              bf16_cols(wgb_ref), bf16_cols(wu_ref, wv_ref))
    q_gain = qw_ref[...] * Q_LOGIT_SCALE
    k_gain = kw_ref[...]
    for r0 in range(0, tm, IN_PROJ_ROWS):
        rows = slice(r0, r0 + IN_PROJ_ROWS)
        h = h_scr[rows, :]
        qk = jnp.dot(h, w_qk, preferred_element_type=F32)
        q_ref[rows, :] = _head_norm(qk[:, :HEAD_DIM], q_gain)
        k_ref[rows, :] = _head_norm(qk[:, HEAD_DIM:], k_gain)
        ga = jnp.dot(h, w_ga, preferred_element_type=F32)
        ga_ref[rows, :] = jax.nn.sigmoid(ga).astype(BF16)
        gb = jnp.dot(h, w_gb, preferred_element_type=F32)
        gb_ref[rows, :] = jax.nn.sigmoid(gb).astype(BF16)
        uv = jnp.dot(h, w_uv, preferred_element_type=F32)
        u_ref[rows, :] = uv[:, :HEAD_DIM]
        v_ref[rows, :] = uv[:, HEAD_DIM:].astype(BF16)


def _in_proj(x2, mod, norm_w, w_in, q_norm_w, k_norm_w, *, tm):
    m, d = x2.shape
    hd = HEAD_DIM
    gw = D_MODEL // SB_HEADS
    assert POOL_WIDTH == SB_HEADS * hd and D_MODEL == SB_HEADS * gw
    assert m % tm == 0 and SEQ % tm == 0 and tm % IN_PROJ_ROWS == 0
    assert tm % NORM_ROWS == 0 and tm // NORM_ROWS >= X_SLOTS
    w_spec = lambda width, off: pl.BlockSpec((d, width), lambda i, j: (0, off // width + j))
    out_spec = lambda width: pl.BlockSpec((tm, width), lambda i, j: (i, j))
    return pl.pallas_call(
        functools.partial(_in_proj_kernel, tm=tm),
        grid=(m // tm, SB_HEADS),
        in_specs=[
            pl.BlockSpec(memory_space=pl.ANY),
            pl.BlockSpec((1, N_MOD, d), lambda i, j: (i * tm // SEQ, 0, 0)),
            pl.BlockSpec((1, d), lambda i, j: (0, 0)),
            w_spec(hd, 0), w_spec(hd, Q_OFF), w_spec(hd, K_OFF), w_spec(hd, V_OFF),
            w_spec(gw, GA_OFF), w_spec(gw, GB_OFF),
            pl.BlockSpec((1, hd), lambda i, j: (0, 0)),
            pl.BlockSpec((1, hd), lambda i, j: (0, 0)),
        ],
        out_specs=[out_spec(hd), out_spec(hd), out_spec(hd), out_spec(hd),
                   out_spec(gw), out_spec(gw)],
        out_shape=[jax.ShapeDtypeStruct((m, POOL_WIDTH), F32),
                   jax.ShapeDtypeStruct((m, SB_WIDTH), BF16),
                   jax.ShapeDtypeStruct((m, SB_WIDTH), BF16),
                   jax.ShapeDtypeStruct((m, SB_WIDTH), BF16),
                   jax.ShapeDtypeStruct((m, D_MODEL), BF16),
                   jax.ShapeDtypeStruct((m, D_MODEL), BF16)],
        scratch_shapes=[pltpu.VMEM((tm, d), BF16), pltpu.VMEM((X_SLOTS, NORM_ROWS, d), F32),
                        pltpu.SemaphoreType.DMA((X_SLOTS,))],
        compiler_params=_params("arbitrary", "arbitrary"),
        name="in_proj",
    )(x2, mod, norm_w.reshape(1, d), w_in, w_in, w_in, w_in, w_in, w_in,
      q_norm_w.reshape(1, hd), k_norm_w.reshape(1, hd))


FFN_OUT_CHUNK = 512


def _ffn_kernel(h_ref, x1_hbm, mod_ref, w1_ref, w2_ref, o_ref, xbuf, sem, *, tm, xrows):
    i = pl.program_id(0)
    c = pl.program_id(1)
    x1_copy = pltpu.make_async_copy(x1_hbm.at[pl.ds(i * tm + c * xrows, xrows), :], xbuf, sem)
    x1_copy.start()

    @pl.when(c == 0)
    def _():
        o_ref[...] = jnp.zeros_like(o_ref)

    a = jnp.dot(h_ref[...], w1_ref[...], preferred_element_type=F32)
    r = jnp.maximum(a, 0.0)
    act = (r * r).astype(BF16)
    for n0 in range(0, o_ref.shape[1], FFN_OUT_CHUNK):
        cols = slice(n0, n0 + FFN_OUT_CHUNK)
        y = jnp.dot(act, w2_ref[:, cols], preferred_element_type=F32)
        o_ref[:, cols] += mod_ref[0, 5:6, cols] * y
    x1_copy.wait()
    rows = pl.ds(pl.multiple_of(c * xrows, xrows), xrows)
    o_ref[rows, :] += xbuf[...]


def _ffn(h2, x1, mod, w1, w2, *, tm, tf):
    m, d = h2.shape
    f = w1.shape[1]
    assert m % tm == 0 and SEQ % tm == 0 and f % tf == 0 and tm % (f // tf) == 0
    assert d % FFN_OUT_CHUNK == 0
    xrows = tm // (f // tf)
    return pl.pallas_call(
        functools.partial(_ffn_kernel, tm=tm, xrows=xrows),
        grid=(m // tm, f // tf),
        in_specs=[
            pl.BlockSpec((tm, d), lambda i, c: (i, 0)),
            pl.BlockSpec(memory_space=pl.ANY),
            pl.BlockSpec((1, N_MOD, d), lambda i, c: (i * tm // SEQ, 0, 0)),
            pl.BlockSpec((d, tf), lambda i, c: (0, c)),
            pl.BlockSpec((tf, d), lambda i, c: (c, 0)),
        ],
        out_specs=pl.BlockSpec((tm, d), lambda i, c: (i, 0)),
        out_shape=jax.ShapeDtypeStruct((m, d), F32),
        scratch_shapes=[pltpu.VMEM((xrows, d), F32), pltpu.SemaphoreType.DMA(())],
        compiler_params=_params("arbitrary", "arbitrary"),
        name="ffn",
    )(h2, x1, mod, w1, w2)


POOL_PAD = max(POOL_WINDOWS) // 2
assert POOL_PAD % SUBLANES == 0
POOL_HEAD = max(POOL_WINDOWS)


def _pool_kernel(w_ref, s_ref, *refs):
    groups = len(POOL_WINDOWS)
    u_refs, (o_ref, buf_a, buf_b) = refs[:groups], refs[groups:]
    gd = POOL_GROUP_DIM
    body = pl.ds(POOL_PAD, SEQ)
    buf_a[0:POOL_PAD, :] = jnp.zeros((POOL_PAD, gd), F32)
    buf_b[0:POOL_PAD, :] = jnp.zeros((POOL_PAD, gd), F32)
    pos = lax.broadcasted_iota(jnp.int32, (POOL_HEAD, gd), 0)
    for g, w in enumerate(POOL_WINDOWS):
        cols = slice(g * gd, (g + 1) * gd)
        u_ref = u_refs[g]
        src, dst = buf_a, buf_b
        src[body, :] = u_ref[...]
        shift = 1
        while shift < w:
            dst[body, :] = src[body, :] + src[pl.ds(POOL_PAD - shift, SEQ), :]
            src, dst = dst, src
            shift *= 2
        dst[body, :] = src[body, :] * (1.0 / w) - u_ref[...]
        count = jnp.minimum(pos + 1, w).astype(F32)
        head = pl.ds(POOL_PAD, POOL_HEAD)
        dst[head, :] = src[head, :] / count - u_ref[0:POOL_HEAD, :]
        mixed = jnp.dot(dst[body, :].astype(BF16), w_ref[g].astype(BF16),
                        preferred_element_type=F32)
        o_ref[:, cols] = (mixed * s_ref[:, cols]).astype(o_ref.dtype)


def _pool(u, w_pool, pool_scale, batch):
    groups = len(POOL_WINDOWS)
    gd = POOL_GROUP_DIM
    return pl.pallas_call(
        _pool_kernel,
        grid=(batch,),
        in_specs=[
            pl.BlockSpec((groups, gd, gd), lambda b: (0, 0, 0)),
            pl.BlockSpec((1, POOL_WIDTH), lambda b: (0, 0)),
        ] + [pl.BlockSpec((SEQ, gd), functools.partial(lambda b, g: (b, g), g=g))
             for g in range(groups)],
        out_specs=pl.BlockSpec((SEQ, POOL_WIDTH), lambda b: (b, 0)),
        out_shape=jax.ShapeDtypeStruct((batch * SEQ, POOL_WIDTH), BF16),
        scratch_shapes=[pltpu.VMEM((POOL_PAD + SEQ, gd), F32),
                        pltpu.VMEM((POOL_PAD + SEQ, gd), F32)],
        compiler_params=_params("arbitrary"),
        name="pool",
    )(w_pool, pool_scale.reshape(1, POOL_WIDTH), *([u] * groups))


EXP2_UNDERFLOW = -151.0


def _attn_kernel(q_ref, k_ref, v_ref, *refs, tb, hp, n_cast):
    w_refs, o_ref, wbf_refs = refs[:n_cast], refs[n_cast], refs[n_cast + 1:2 * n_cast + 1]
    carry_ref, acc_ref = refs[2 * n_cast + 1:]
    for w_ref, wbf_ref in zip(w_refs, wbf_refs):
        wbf_ref[...] = w_ref[...].astype(BF16)

    qi = pl.program_id(2)
    nseg = tb // LANES

    r = lax.broadcasted_iota(jnp.int32, (LANES, 2 * LANES), 0)
    c = lax.broadcasted_iota(jnp.int32, (LANES, 2 * LANES), 1)
    cum_op = jnp.where(jnp.logical_or(c >= LANES, r > c), 1.0, 0.0).astype(BF16)
    tri_r = lax.broadcasted_iota(jnp.int32, (LANES, LANES), 0)
    tri_c = lax.broadcasted_iota(jnp.int32, (LANES, LANES), 1)
    causal = tri_c < tri_r
    segs = [slice(sg * LANES, (sg + 1) * LANES) for sg in range(nseg)]

    def block(kb, diagonal):
        start = pl.multiple_of(kb * tb, tb)
        if diagonal:
            parts = [(segs[g], g + 1) for g in range(nseg)]
        else:
            parts = [(slice(0, tb), nseg)]
        work = [(p, slice(p * HEAD_DIM, (p + 1) * HEAD_DIM), rows, nk)
                for rows, nk in parts for p in range(hp)]
        zs = [lax.dot_general(q_ref[rows, cols], k_ref[pl.ds(start, nk * LANES), cols],
                              (((1,), (1,)), ((), ())), preferred_element_type=F32)
              for _, cols, rows, nk in work]
        log_betas, sums = [], []
        for z, (_, _, _, nk) in zip(zs, work):
            log_beta = jnp.minimum(z, 0.0) - jnp.log2(1.0 + jnp.exp2(-jnp.abs(z)))
            l = log_beta - z
            cs = []
            for sg in range(nk):
                l_seg = l[:, segs[sg]]
                if diagonal and sg == nk - 1:
                    l_seg = jnp.where(causal, l_seg, 0.0)
                cs.append(jnp.dot(l_seg.astype(BF16), cum_op, preferred_element_type=F32))
            log_betas.append(log_beta)
            sums.append(cs)
        top = None
        for (p, cols, rows, nk), log_beta, cs in zip(work, log_betas, sums):
            carry = carry_ref[p, rows, :]
            a_parts = [None] * nk
            for sg in range(nk - 1, -1, -1):
                a_seg = jnp.exp2(log_beta[:, segs[sg]] + (cs[sg][:, :LANES] + carry))
                if diagonal and sg == nk - 1:
                    a_seg = jnp.where(causal, a_seg, 0.0)
                a_parts[sg] = a_seg
                carry = carry + cs[sg][:, LANES:]
            a = jnp.concatenate(a_parts, axis=1) if nk > 1 else a_parts[0]
            acc_ref[p, rows, :] += jnp.dot(
                a.astype(BF16), v_ref[pl.ds(start, nk * LANES), cols],
                preferred_element_type=F32)
            carry_ref[p, rows, :] = carry
            top = jnp.max(carry) if top is None else jnp.maximum(top, jnp.max(carry))
        return top

    carry_ref[...] = jnp.zeros_like(carry_ref)
    acc_ref[...] = jnp.zeros_like(acc_ref)
    top = block(qi, True)

    def cond(state):
        kb, top = state
        return jnp.logical_and(kb >= 0, top > EXP2_UNDERFLOW)

    def body(state):
        kb, _ = state
        return kb - 1, block(kb, False)

    lax.while_loop(cond, body, (qi - 1, top))
    for p in range(hp):
        o_ref[:, p * HEAD_DIM:(p + 1) * HEAD_DIM] = acc_ref[p].astype(o_ref.dtype)


def _attn(q, k, v, weights, batch, *, tb, hp):
    nq = SEQ // tb
    groups = SB_HEADS // hp
    width = hp * HEAD_DIM
    steps = batch * groups * nq
    assert all(w.shape[0] % steps == 0 for w in weights)
    cast_specs = [pl.BlockSpec((w.shape[0] // steps, w.shape[1]),
                               lambda b, h, i: ((b * groups + h) * nq + i, 0))
                  for w in weights]
    outs = pl.pallas_call(
        functools.partial(_attn_kernel, tb=tb, hp=hp, n_cast=len(weights)),
        grid=(batch, groups, nq),
        in_specs=[
            pl.BlockSpec((tb, width), lambda b, h, i: (b * nq + i, h)),
            pl.BlockSpec((SEQ, width), lambda b, h, i: (b, h)),
            pl.BlockSpec((SEQ, width), lambda b, h, i: (b, h)),
        ] + cast_specs,
        out_specs=[pl.BlockSpec((tb, width), lambda b, h, i: (b * nq + i, h))] + cast_specs,
        out_shape=[jax.ShapeDtypeStruct((batch * SEQ, SB_WIDTH), BF16)]
        + [jax.ShapeDtypeStruct(w.shape, BF16) for w in weights],
        scratch_shapes=[pltpu.VMEM((hp, tb, LANES), F32),
                        pltpu.VMEM((hp, tb, HEAD_DIM), F32)],
        compiler_params=_params("arbitrary", "arbitrary", "arbitrary"),
        name="attn",
    )(q, k, v, *weights)
    return outs[0], outs[1:]


MIX_CHUNK = 512


def _mix_kernel(pa_ref, at_ref, sa_ref, sb_ref, x_ref, mod_ref, nw_ref, wa_ref, wb_ref, wo_ref,
                x1_ref, h2_ref, merged_scr):
    d = x_ref.shape[1]
    pa = pa_ref[...]
    at = at_ref[...]
    for n0 in range(0, d, MIX_CHUNK):
        cols = slice(n0, n0 + MIX_CHUNK)
        ya = jnp.dot(pa, wa_ref[:, cols], preferred_element_type=F32)
        yb = jnp.dot(at, wb_ref[:, cols], preferred_element_type=F32)
        merged = sa_ref[:, cols].astype(F32) * ya + sb_ref[:, cols].astype(F32) * yb
        merged_scr[:, cols] = merged.astype(BF16)
    merged = merged_scr[...]
    for n0 in range(0, d, MIX_CHUNK):
        cols = slice(n0, n0 + MIX_CHUNK)
        o = jnp.dot(merged, wo_ref[:, cols], preferred_element_type=F32)
        x1_ref[:, cols] = x_ref[:, cols] + mod_ref[0, 2:3, cols] * o
    _norm_modulate_rows(x1_ref, h2_ref, nw_ref[...], mod_ref[0, 4:5, :], mod_ref[0, 3:4, :])


def _mix(pa, at, sa, sb, x2, mod, norm_w, wa_bf, wb_bf, wo_bf, *, tm):
    m, d = x2.shape
    kdim = pa.shape[1]
    resident = pl.Buffered(1)
    return pl.pallas_call(
        _mix_kernel,
        grid=(m // tm,),
        in_specs=[
            pl.BlockSpec((tm, kdim), lambda i: (i, 0)),
            pl.BlockSpec((tm, kdim), lambda i: (i, 0)),
            pl.BlockSpec((tm, d), lambda i: (i, 0)),
            pl.BlockSpec((tm, d), lambda i: (i, 0)),
            pl.BlockSpec((tm, d), lambda i: (i, 0)),
            pl.BlockSpec((1, N_MOD, d), lambda i: (i * tm // SEQ, 0, 0)),
            pl.BlockSpec((1, d), lambda i: (0, 0)),
            pl.BlockSpec((kdim, d), lambda i: (0, 0), pipeline_mode=resident),
            pl.BlockSpec((kdim, d), lambda i: (0, 0), pipeline_mode=resident),
            pl.BlockSpec((d, d), lambda i: (0, 0), pipeline_mode=resident),
        ],
        out_specs=[pl.BlockSpec((tm, d), lambda i: (i, 0)),
                   pl.BlockSpec((tm, d), lambda i: (i, 0))],
        out_shape=[jax.ShapeDtypeStruct((m, d), F32),
                   jax.ShapeDtypeStruct((m, d), BF16)],
        scratch_shapes=[pltpu.VMEM((tm, d), BF16)],
        compiler_params=_params("arbitrary"),
        name="mix",
    )(pa, at, sa, sb, x2, mod, norm_w.reshape(1, d), wa_bf, wb_bf, wo_bf)


def kernel(x, c, w_ada, b_ada, norm1_w, w_in, q_norm_w, k_norm_w, w_pool, pool_scale,
           w_a_up, w_b_up, w_o, norm2_w, w_ff1, w_ff2):
    batch, seq, d = x.shape
    assert (seq, d) == (SEQ, D_MODEL) and w_ada.shape[0] == 1
    x2 = x.reshape(batch * seq, d)

    mod = _ada(c, w_ada[0], b_ada[0])
    u, q, k, v, sa, sb = _in_proj(x2, mod, norm1_w[0], w_in[0], q_norm_w[0], k_norm_w[0],
                                  tm=2048)
    pa = _pool(u, w_pool[0], pool_scale[0], batch)
    at, (wa_bf, wb_bf, wo_bf, w1_bf, w2_bf) = _attn(
        q, k, v, (w_a_up[0], w_b_up[0], w_o[0], w_ff1[0], w_ff2[0]), batch,
        tb=256, hp=8)
    x1, h2 = _mix(pa, at, sa, sb, x2, mod, norm2_w[0], wa_bf, wb_bf, wo_bf, tm=256)
    out = _ffn(h2, x1, mod, w1_bf, w2_bf, tm=1024, tf=1024)
    return out.reshape(batch, seq, d)
```

```python
import functools
import math

import jax
import jax.numpy as jnp
from jax import lax
from jax.experimental import pallas as pl
from jax.experimental.pallas import tpu as pltpu

D_MODEL = 2048
SEQ = 2048
POOL_WIDTH = D_MODEL // 2
POOL_WINDOWS = (2, 4, 8, 16)
POOL_GROUP_DIM = POOL_WIDTH // len(POOL_WINDOWS)
HEAD_DIM = 128
SB_WIDTH = D_MODEL // 2
SB_HEADS = SB_WIDTH // HEAD_DIM
N_MOD = 6
EPS = 1e-6

Q_OFF = POOL_WIDTH
K_OFF = Q_OFF + SB_WIDTH
V_OFF = K_OFF + SB_WIDTH
GA_OFF = V_OFF + SB_WIDTH
GB_OFF = GA_OFF + D_MODEL

LANES = 128
SUBLANES = 8
V7X_VMEM_BYTES = 64 * 1024 * 1024
VMEM_LIMIT_BYTES = V7X_VMEM_BYTES * 7 // 8

BF16 = jnp.bfloat16
F32 = jnp.float32


def _params(*semantics):
    return pltpu.CompilerParams(dimension_semantics=semantics,
                                vmem_limit_bytes=VMEM_LIMIT_BYTES)


ADA_STREAMS = 4


def _ada_kernel(c_ref, b_ref, *refs):
    w_refs, o_ref = refs[:-1], refs[-1]
    c = c_ref[...]
    batch = c.shape[0]
    sc = c * jax.nn.sigmoid(c)
    if batch % SUBLANES:
        pad = SUBLANES - batch % SUBLANES
        sc = jnp.concatenate([sc, jnp.zeros((pad, sc.shape[1]), F32)], axis=0)
    sc = sc.astype(BF16)
    rows = w_refs[0].shape[0]
    acc = b_ref[...]
    for s, w_ref in enumerate(w_refs):
        acc = acc + jnp.dot(sc[:, s * rows:(s + 1) * rows], w_ref[...].astype(BF16),
                            preferred_element_type=F32)[:batch]
    o_ref[...] = acc


def _ada(c, w_ada, b_ada):
    batch, d = c.shape
    n = w_ada.shape[1]
    tn = 1024
    assert n % tn == 0 and d % ADA_STREAMS == 0
    slab = d // ADA_STREAMS
    out = pl.pallas_call(
        _ada_kernel,
        grid=(n // tn,),
        in_specs=[
            pl.BlockSpec((batch, d), lambda j: (0, 0)),
            pl.BlockSpec((1, tn), lambda j: (0, j)),
        ] + [pl.BlockSpec((slab, tn), functools.partial(lambda j, s: (s, j), s=s))
             for s in range(ADA_STREAMS)],
        out_specs=pl.BlockSpec((batch, tn), lambda j: (0, j)),
        out_shape=jax.ShapeDtypeStruct((batch, n), F32),
        compiler_params=_params("arbitrary"),
        name="ada",
    )(c, b_ada.reshape(1, n), *([w_ada] * ADA_STREAMS))
    return out.reshape(batch, N_MOD, d)


NORM_ROWS = 256


def _norm_modulate(x, gain, shift):
    y = x * lax.rsqrt(jnp.mean(x * x, axis=-1, keepdims=True) + EPS)
    return (y * gain + shift).astype(BF16)


def _norm_modulate_rows(x_ref, o_ref, norm_w, scale, shift):
    gain = norm_w * (1.0 + scale)

    def body(r, carry):
        rows = pl.ds(pl.multiple_of(r * NORM_ROWS, NORM_ROWS), NORM_ROWS)
        o_ref[rows, :] = _norm_modulate(x_ref[rows, :], gain, shift)
        return carry

    lax.fori_loop(0, x_ref.shape[0] // NORM_ROWS, body, 0)


Q_LOGIT_SCALE = math.log2(math.e) / math.sqrt(HEAD_DIM)


def _head_norm(blk, gain):
    ms = jnp.mean(blk * blk, axis=-1, keepdims=True)
    return (blk * lax.rsqrt(ms + EPS) * gain).astype(BF16)


IN_PROJ_ROWS = 1024
X_SLOTS = 4


def _in_proj_kernel(x_hbm, mod_ref, nw_ref, wu_ref, wq_ref, wk_ref, wv_ref, wga_ref, wgb_ref,
                    qw_ref, kw_ref, u_ref, q_ref, k_ref, v_ref, ga_ref, gb_ref,
                    h_scr, xbuf, sems, *, tm):
    i = pl.program_id(0)
    j = pl.program_id(1)
    chunks = tm // NORM_ROWS

    def x_copy(tile, r):
        slot = r % X_SLOTS
        return pltpu.make_async_copy(
            x_hbm.at[pl.ds(tile * tm + r * NORM_ROWS, NORM_ROWS), :], xbuf.at[slot],
            sems.at[slot])

    @pl.when(jnp.logical_and(j == 0, i == 0))
    def _():
        for r in range(X_SLOTS):
            x_copy(0, r).start()

    @pl.when(j == 0)
    def _():
        gain = nw_ref[...] * (1.0 + mod_ref[0, 1:2, :])
        shift = mod_ref[0, 0:1, :]

        def body(r, carry):
            x_copy(i, r).wait()
            rows = pl.ds(pl.multiple_of(r * NORM_ROWS, NORM_ROWS), NORM_ROWS)
            h_scr[rows, :] = _norm_modulate(xbuf[r % X_SLOTS], gain, shift)

            @pl.when(r + X_SLOTS < chunks)
            def _():
                x_copy(i, r + X_SLOTS).start()

            return carry

        lax.fori_loop(0, chunks, body, 0)

    @pl.when(jnp.logical_and(j == pl.num_programs(1) - 1, i + 1 < pl.num_programs(0)))
    def _():
        for r in range(X_SLOTS):
            x_copy(i + 1, r).start()

    def bf16_cols(*w_refs):
        return jnp.concatenate([w_ref[...].astype(BF16) for w_ref in w_refs], axis=1)

    w_qk, w_ga, w_gb, w_uv = (bf16_cols(wq_ref, wk_ref), bf16_cols(wga_ref),
                              bf16_cols(wgb_ref), bf16_cols(wu_ref, wv_ref))
    q_gain = qw_ref[...] * Q_LOGIT_SCALE
    k_gain = kw_ref[...]
    for r0 in range(0, tm, IN_PROJ_ROWS):
        rows = slice(r0, r0 + IN_PROJ_ROWS)
        h = h_scr[rows, :]
        qk = jnp.dot(h, w_qk, preferred_element_type=F32)
        q_ref[rows, :] = _head_norm(qk[:, :HEAD_DIM], q_gain)
        k_ref[rows, :] = _head_norm(qk[:, HEAD_DIM:], k_gain)
        ga = jnp.dot(h, w_ga, preferred_element_type=F32)
        ga_ref[rows, :] = jax.nn.sigmoid(ga).astype(BF16)
        gb = jnp.dot(h, w_gb, preferred_element_type=F32)
        gb_ref[rows, :] = jax.nn.sigmoid(gb).astype(BF16)
        uv = jnp.dot(h, w_uv, preferred_element_type=F32)
        u_ref[rows, :] = uv[:, :HEAD_DIM].astype(BF16)
        v_ref[rows, :] = uv[:, HEAD_DIM:].astype(BF16)


def _in_proj(x2, mod, norm_w, w_in, q_norm_w, k_norm_w, *, tm):
    m, d = x2.shape
    hd = HEAD_DIM
    gw = D_MODEL // SB_HEADS
    assert POOL_WIDTH == SB_HEADS * hd and D_MODEL == SB_HEADS * gw
    assert m % tm == 0 and SEQ % tm == 0 and tm % IN_PROJ_ROWS == 0
    assert tm % NORM_ROWS == 0 and tm // NORM_ROWS >= X_SLOTS
    w_spec = lambda width, off: pl.BlockSpec((d, width), lambda i, j: (0, off // width + j))
    out_spec = lambda width: pl.BlockSpec((tm, width), lambda i, j: (i, j))
    return pl.pallas_call(
        functools.partial(_in_proj_kernel, tm=tm),
        grid=(m // tm, SB_HEADS),
        in_specs=[
            pl.BlockSpec(memory_space=pl.ANY),
            pl.BlockSpec((1, N_MOD, d), lambda i, j: (i * tm // SEQ, 0, 0)),
            pl.BlockSpec((1, d), lambda i, j: (0, 0)),
            w_spec(hd, 0), w_spec(hd, Q_OFF), w_spec(hd, K_OFF), w_spec(hd, V_OFF),
            w_spec(gw, GA_OFF), w_spec(gw, GB_OFF),
            pl.BlockSpec((1, hd), lambda i, j: (0, 0)),
            pl.BlockSpec((1, hd), lambda i, j: (0, 0)),
        ],
        out_specs=[out_spec(hd), out_spec(hd), out_spec(hd), out_spec(hd),
                   out_spec(gw), out_spec(gw)],
        out_shape=[jax.ShapeDtypeStruct((m, POOL_WIDTH), BF16),
                   jax.ShapeDtypeStruct((m, SB_WIDTH), BF16),
                   jax.ShapeDtypeStruct((m, SB_WIDTH), BF16),
                   jax.ShapeDtypeStruct((m, SB_WIDTH), BF16),
                   jax.ShapeDtypeStruct((m, D_MODEL), BF16),
                   jax.ShapeDtypeStruct((m, D_MODEL), BF16)],
        scratch_shapes=[pltpu.VMEM((tm, d), BF16), pltpu.VMEM((X_SLOTS, NORM_ROWS, d), F32),
                        pltpu.SemaphoreType.DMA((X_SLOTS,))],
        compiler_params=_params("arbitrary", "arbitrary"),
        name="in_proj",
    )(x2, mod, norm_w.reshape(1, d), w_in, w_in, w_in, w_in, w_in, w_in,
      q_norm_w.reshape(1, hd), k_norm_w.reshape(1, hd))


FFN_OUT_CHUNK = 512


def _ffn_kernel(h_ref, x1_hbm, mod_ref, w1_ref, w2_ref, o_ref, xbuf, sem, *, tm, xrows):
    i = pl.program_id(0)
    c = pl.program_id(1)
    x1_copy = pltpu.make_async_copy(x1_hbm.at[pl.ds(i * tm + c * xrows, xrows), :], xbuf, sem)
    x1_copy.start()

    @pl.when(c == 0)
    def _():
        o_ref[...] = jnp.zeros_like(o_ref)

    a = jnp.dot(h_ref[...], w1_ref[...], preferred_element_type=F32)
    r = jnp.maximum(a, 0.0)
    act = (r * r).astype(BF16)
    for n0 in range(0, o_ref.shape[1], FFN_OUT_CHUNK):
        cols = slice(n0, n0 + FFN_OUT_CHUNK)
        y = jnp.dot(act, w2_ref[:, cols], preferred_element_type=F32)
        o_ref[:, cols] += mod_ref[0, 5:6, cols] * y
    x1_copy.wait()
    rows = pl.ds(pl.multiple_of(c * xrows, xrows), xrows)
    o_ref[rows, :] += xbuf[...]


def _ffn(h2, x1, mod, w1, w2, *, tm, tf):
    m, d = h2.shape
    f = w1.shape[1]
    assert m % tm == 0 and SEQ % tm == 0 and f % tf == 0 and tm % (f // tf) == 0
    assert d % FFN_OUT_CHUNK == 0
    xrows = tm // (f // tf)
    return pl.pallas_call(
        functools.partial(_ffn_kernel, tm=tm, xrows=xrows),
        grid=(m // tm, f // tf),
        in_specs=[
            pl.BlockSpec((tm, d), lambda i, c: (i, 0)),
            pl.BlockSpec(memory_space=pl.ANY),
            pl.BlockSpec((1, N_MOD, d), lambda i, c: (i * tm // SEQ, 0, 0)),
            pl.BlockSpec((d, tf), lambda i, c: (0, c)),
            pl.BlockSpec((tf, d), lambda i, c: (c, 0)),
        ],
        out_specs=pl.BlockSpec((tm, d), lambda i, c: (i, 0)),
        out_shape=jax.ShapeDtypeStruct((m, d), F32),
        scratch_shapes=[pltpu.VMEM((xrows, d), F32), pltpu.SemaphoreType.DMA(())],
        compiler_params=_params("arbitrary", "arbitrary"),
        name="ffn",
    )(h2, x1, mod, w1, w2)


POOL_ROWS = 256
POOL_HEAD = max(POOL_WINDOWS)
assert POOL_HEAD % (2 * SUBLANES) == 0 and all(w & (w - 1) == 0 for w in POOL_WINDOWS)


def _pool_head_exact(u_head, w):
    pos = lax.broadcasted_iota(jnp.int32, u_head.shape, 0)
    win_sum = u_head
    for k in range(1, w):
        win_sum = win_sum + jnp.where(pos >= k, pltpu.roll(u_head, k, 0), 0.0)
    count = jnp.minimum(pos + 1, w).astype(F32)
    return win_sum / count - u_head


def _pool_kernel(u_ref, w_ref, s_ref, o_ref):
    gd = POOL_GROUP_DIM
    tt = lax.broadcasted_iota(jnp.int32, (POOL_ROWS + POOL_HEAD, POOL_ROWS), 0)
    jj = lax.broadcasted_iota(jnp.int32, (POOL_ROWS + POOL_HEAD, POOL_ROWS), 1)
    for g, w in enumerate(POOL_WINDOWS):
        cols = slice(g * gd, (g + 1) * gd)
        inside = jnp.where(jnp.logical_and(jj <= tt, jj > tt - w), 1.0 / w, 0.0)
        inside = inside - jnp.where(jj == tt, 1.0, 0.0)
        spill = jnp.where(jj > tt - w, 1.0 / w, 0.0)
        band = jnp.where(tt < POOL_ROWS, inside, spill).astype(BF16)
        carry = None
        pooled = []
        for r0 in range(0, SEQ, POOL_ROWS):
            res = jnp.dot(band, u_ref[r0:r0 + POOL_ROWS, cols], preferred_element_type=F32)
            if r0 == 0:
                head = _pool_head_exact(u_ref[0:POOL_HEAD, cols].astype(F32), w)
            else:
                head = res[:POOL_HEAD] + carry
            pooled += [head.astype(BF16), res[POOL_HEAD:POOL_ROWS].astype(BF16)]
            carry = res[POOL_ROWS:]
        mixed = jnp.dot(jnp.concatenate(pooled, axis=0), w_ref[g].astype(BF16),
                        preferred_element_type=F32)
        o_ref[:, cols] = (mixed * s_ref[:, cols]).astype(o_ref.dtype)


def _pool(u, w_pool, pool_scale, batch):
    groups = len(POOL_WINDOWS)
    gd = POOL_GROUP_DIM
    assert SEQ % POOL_ROWS == 0
    return pl.pallas_call(
        _pool_kernel,
        grid=(batch,),
        in_specs=[
            pl.BlockSpec((SEQ, POOL_WIDTH), lambda b: (b, 0)),
            pl.BlockSpec((groups, gd, gd), lambda b: (0, 0, 0)),
            pl.BlockSpec((1, POOL_WIDTH), lambda b: (0, 0)),
        ],
        out_specs=pl.BlockSpec((SEQ, POOL_WIDTH), lambda b: (b, 0)),
        out_shape=jax.ShapeDtypeStruct((batch * SEQ, POOL_WIDTH), BF16),
        compiler_params=_params("arbitrary"),
        name="pool",
    )(u, w_pool, pool_scale.reshape(1, POOL_WIDTH))


EXP2_UNDERFLOW = -151.0


def _attn_kernel(q_ref, k_ref, v_ref, *refs, tb, hp, n_cast):
    w_refs, o_ref, wbf_refs = refs[:n_cast], refs[n_cast], refs[n_cast + 1:2 * n_cast + 1]
    carry_ref, acc_ref = refs[2 * n_cast + 1:]
    for w_ref, wbf_ref in zip(w_refs, wbf_refs):
        wbf_ref[...] = w_ref[...].astype(BF16)

    qi = pl.program_id(2)
    nseg = tb // LANES

    r = lax.broadcasted_iota(jnp.int32, (LANES, 2 * LANES), 0)
    c = lax.broadcasted_iota(jnp.int32, (LANES, 2 * LANES), 1)
    cum_op = jnp.where(jnp.logical_or(c >= LANES, r > c), 1.0, 0.0).astype(BF16)
    tri_r = lax.broadcasted_iota(jnp.int32, (LANES, LANES), 0)
    tri_c = lax.broadcasted_iota(jnp.int32, (LANES, LANES), 1)
    causal = tri_c < tri_r
    segs = [slice(sg * LANES, (sg + 1) * LANES) for sg in range(nseg)]

    def block(kb, diagonal):
        start = pl.multiple_of(kb * tb, tb)
        if diagonal:
            parts = [(segs[g], g + 1) for g in range(nseg)]
        else:
            parts = [(slice(0, tb), nseg)]
        work = [(p, slice(p * HEAD_DIM, (p + 1) * HEAD_DIM), rows, nk)
                for rows, nk in parts for p in range(hp)]
        zs = [lax.dot_general(q_ref[rows, cols], k_ref[pl.ds(start, nk * LANES), cols],
                              (((1,), (1,)), ((), ())), preferred_element_type=F32)
              for _, cols, rows, nk in work]
        log_betas, sums = [], []
        for z, (_, _, _, nk) in zip(zs, work):
            log_beta = jnp.minimum(z, 0.0) - jnp.log2(1.0 + jnp.exp2(-jnp.abs(z)))
            l = log_beta - z
            cs = []
            for sg in range(nk):
                l_seg = l[:, segs[sg]]
                if diagonal and sg == nk - 1:
                    l_seg = jnp.where(causal, l_seg, 0.0)
                cs.append(jnp.dot(l_seg.astype(BF16), cum_op, preferred_element_type=F32))
            log_betas.append(log_beta)
            sums.append(cs)
        top = None
        for (p, cols, rows, nk), log_beta, cs in zip(work, log_betas, sums):
            carry = carry_ref[p, rows, :]
            a_parts = [None] * nk
            for sg in range(nk - 1, -1, -1):
                a_seg = jnp.exp2(log_beta[:, segs[sg]] + (cs[sg][:, :LANES] + carry))
                if diagonal and sg == nk - 1:
                    a_seg = jnp.where(causal, a_seg, 0.0)
                a_parts[sg] = a_seg
                carry = carry + cs[sg][:, LANES:]
            a = jnp.concatenate(a_parts, axis=1) if nk > 1 else a_parts[0]
            acc_ref[p, rows, :] += jnp.dot(
                a.astype(BF16), v_ref[pl.ds(start, nk * LANES), cols],
                preferred_element_type=F32)
            carry_ref[p, rows, :] = carry
            top = jnp.max(carry) if top is None else jnp.maximum(top, jnp.max(carry))
        return top

    carry_ref[...] = jnp.zeros_like(carry_ref)
    acc_ref[...] = jnp.zeros_like(acc_ref)
    top = block(qi, True)

    def cond(state):
        kb, top = state
        return jnp.logical_and(kb >= 0, top > EXP2_UNDERFLOW)

    def body(state):
        kb, _ = state
        return kb - 1, block(kb, False)

    lax.while_loop(cond, body, (qi - 1, top))
    for p in range(hp):
        o_ref[:, p * HEAD_DIM:(p + 1) * HEAD_DIM] = acc_ref[p].astype(o_ref.dtype)


def _attn(q, k, v, weights, batch, *, tb, hp):
    nq = SEQ // tb
    groups = SB_HEADS // hp
    width = hp * HEAD_DIM
    steps = batch * groups * nq
    assert all(w.shape[0] % steps == 0 for w in weights)
    cast_specs = [pl.BlockSpec((w.shape[0] // steps, w.shape[1]),
                               lambda b, h, i: ((b * groups + h) * nq + i, 0))
                  for w in weights]
    outs = pl.pallas_call(
        functools.partial(_attn_kernel, tb=tb, hp=hp, n_cast=len(weights)),
        grid=(batch, groups, nq),
        in_specs=[
            pl.BlockSpec((tb, width), lambda b, h, i: (b * nq + i, h)),
            pl.BlockSpec((SEQ, width), lambda b, h, i: (b, h)),
            pl.BlockSpec((SEQ, width), lambda b, h, i: (b, h)),
        ] + cast_specs,
        out_specs=[pl.BlockSpec((tb, width), lambda b, h, i: (b * nq + i, h))] + cast_specs,
        out_shape=[jax.ShapeDtypeStruct((batch * SEQ, SB_WIDTH), BF16)]
        + [jax.ShapeDtypeStruct(w.shape, BF16) for w in weights],
        scratch_shapes=[pltpu.VMEM((hp, tb, LANES), F32),
                        pltpu.VMEM((hp, tb, HEAD_DIM), F32)],
        compiler_params=_params("arbitrary", "arbitrary", "arbitrary"),
        name="attn",
    )(q, k, v, *weights)
    return outs[0], outs[1:]


MIX_CHUNK = 256


def _mix_kernel(pa_ref, at_ref, sa_ref, sb_ref, x_ref, mod_ref, nw_ref, wa_ref, wb_ref, wo_ref,
                x1_ref, h2_ref, merged_scr):
    d = x_ref.shape[1]
    pa = pa_ref[...]
    at = at_ref[...]
    for n0 in range(0, d, MIX_CHUNK):
        cols = slice(n0, n0 + MIX_CHUNK)
        ya = jnp.dot(pa, wa_ref[:, cols], preferred_element_type=F32)
        yb = jnp.dot(at, wb_ref[:, cols], preferred_element_type=F32)
        merged = sa_ref[:, cols].astype(F32) * ya + sb_ref[:, cols].astype(F32) * yb
        merged_scr[:, cols] = merged.astype(BF16)
    merged = merged_scr[...]
    for n0 in range(0, d, MIX_CHUNK):
        cols = slice(n0, n0 + MIX_CHUNK)
        o = jnp.dot(merged, wo_ref[:, cols], preferred_element_type=F32)
        x1_ref[:, cols] = x_ref[:, cols] + mod_ref[0, 2:3, cols] * o
    _norm_modulate_rows(x1_ref, h2_ref, nw_ref[...], mod_ref[0, 4:5, :], mod_ref[0, 3:4, :])


def _mix(pa, at, sa, sb, x2, mod, norm_w, wa_bf, wb_bf, wo_bf, *, tm):
    m, d = x2.shape
    kdim = pa.shape[1]
    resident = pl.Buffered(1)
    return pl.pallas_call(
        _mix_kernel,
        grid=(m // tm,),
        in_specs=[
            pl.BlockSpec((tm, kdim), lambda i: (i, 0)),
            pl.BlockSpec((tm, kdim), lambda i: (i, 0)),
            pl.BlockSpec((tm, d), lambda i: (i, 0)),
            pl.BlockSpec((tm, d), lambda i: (i, 0)),
            pl.BlockSpec((tm, d), lambda i: (i, 0)),
            pl.BlockSpec((1, N_MOD, d), lambda i: (i * tm // SEQ, 0, 0)),
            pl.BlockSpec((1, d), lambda i: (0, 0)),
            pl.BlockSpec((kdim, d), lambda i: (0, 0), pipeline_mode=resident),
            pl.BlockSpec((kdim, d), lambda i: (0, 0), pipeline_mode=resident),
            pl.BlockSpec((d, d), lambda i: (0, 0), pipeline_mode=resident),
        ],
        out_specs=[pl.BlockSpec((tm, d), lambda i: (i, 0)),
                   pl.BlockSpec((tm, d), lambda i: (i, 0))],
        out_shape=[jax.ShapeDtypeStruct((m, d), F32),
                   jax.ShapeDtypeStruct((m, d), BF16)],
        scratch_shapes=[pltpu.VMEM((tm, d), BF16)],
        compiler_params=_params("arbitrary"),
        name="mix",
    )(pa, at, sa, sb, x2, mod, norm_w.reshape(1, d), wa_bf, wb_bf, wo_bf)


def kernel(x, c, w_ada, b_ada, norm1_w, w_in, q_norm_w, k_norm_w, w_pool, pool_scale,
           w_a_up, w_b_up, w_o, norm2_w, w_ff1, w_ff2):
    batch, seq, d = x.shape
    assert (seq, d) == (SEQ, D_MODEL) and w_ada.shape[0] == 1
    x2 = x.reshape(batch * seq, d)

    mod = _ada(c, w_ada[0], b_ada[0])
    u, q, k, v, sa, sb = _in_proj(x2, mod, norm1_w[0], w_in[0], q_norm_w[0], k_norm_w[0],
                                  tm=2048)
    pa = _pool(u, w_pool[0], pool_scale[0], batch)
    at, (wa_bf, wb_bf, wo_bf, w1_bf, w2_bf) = _attn(
        q, k, v, (w_a_up[0], w_b_up[0], w_o[0], w_ff1[0], w_ff2[0]), batch,
        tb=256, hp=8)
    x1, h2 = _mix(pa, at, sa, sb, x2, mod, norm2_w[0], wa_bf, wb_bf, wo_bf, tm=256)
    out = _ffn(h2, x1, mod, w1_bf, w2_bf, tm=1024, tf=1024)
    return out.reshape(batch, seq, d)
```

```python
import functools
import math

import jax
import jax.numpy as jnp
from jax import lax
from jax.experimental import pallas as pl
from jax.experimental.pallas import tpu as pltpu

D_MODEL = 2048
SEQ = 2048
POOL_WIDTH = D_MODEL // 2
POOL_WINDOWS = (2, 4, 8, 16)
POOL_GROUP_DIM = POOL_WIDTH // len(POOL_WINDOWS)
HEAD_DIM = 128
SB_WIDTH = D_MODEL // 2
SB_HEADS = SB_WIDTH // HEAD_DIM
N_MOD = 6
EPS = 1e-6

Q_OFF = POOL_WIDTH
K_OFF = Q_OFF + SB_WIDTH
V_OFF = K_OFF + SB_WIDTH
GA_OFF = V_OFF + SB_WIDTH
GB_OFF = GA_OFF + D_MODEL

LANES = 128
SUBLANES = 8
V7X_VMEM_BYTES = 64 * 1024 * 1024
VMEM_LIMIT_BYTES = V7X_VMEM_BYTES * 7 // 8

BF16 = jnp.bfloat16
F32 = jnp.float32


def _params(*semantics):
    return pltpu.CompilerParams(dimension_semantics=semantics,
                                vmem_limit_bytes=VMEM_LIMIT_BYTES)


ADA_STREAMS = 4


def _ada_kernel(c_ref, b_ref, *refs):
    w_refs, o_ref = refs[:-1], refs[-1]
    c = c_ref[...]
    batch = c.shape[0]
    sc = c * jax.nn.sigmoid(c)
    if batch % SUBLANES:
        pad = SUBLANES - batch % SUBLANES
        sc = jnp.concatenate([sc, jnp.zeros((pad, sc.shape[1]), F32)], axis=0)
    sc = sc.astype(BF16)
    rows = w_refs[0].shape[0]
    acc = b_ref[...]
    for s, w_ref in enumerate(w_refs):
        acc = acc + jnp.dot(sc[:, s * rows:(s + 1) * rows], w_ref[...].astype(BF16),
                            preferred_element_type=F32)[:batch]
    o_ref[...] = acc


def _ada(c, w_ada, b_ada):
    batch, d = c.shape
    n = w_ada.shape[1]
    tn = 1024
    assert n % tn == 0 and d % ADA_STREAMS == 0
    slab = d // ADA_STREAMS
    out = pl.pallas_call(
        _ada_kernel,
        grid=(n // tn,),
        in_specs=[
            pl.BlockSpec((batch, d), lambda j: (0, 0)),
            pl.BlockSpec((1, tn), lambda j: (0, j)),
        ] + [pl.BlockSpec((slab, tn), functools.partial(lambda j, s: (s, j), s=s))
             for s in range(ADA_STREAMS)],
        out_specs=pl.BlockSpec((batch, tn), lambda j: (0, j)),
        out_shape=jax.ShapeDtypeStruct((batch, n), F32),
        compiler_params=_params("arbitrary"),
        name="ada",
    )(c, b_ada.reshape(1, n), *([w_ada] * ADA_STREAMS))
    return out.reshape(batch, N_MOD, d)


NORM_ROWS = 256


def _norm_modulate(x, gain, shift):
    y = x * lax.rsqrt(jnp.mean(x * x, axis=-1, keepdims=True) + EPS)
    return (y * gain + shift).astype(BF16)


def _norm_modulate_rows(x_ref, o_ref, norm_w, scale, shift):
    gain = norm_w * (1.0 + scale)

    def body(r, carry):
        rows = pl.ds(pl.multiple_of(r * NORM_ROWS, NORM_ROWS), NORM_ROWS)
        o_ref[rows, :] = _norm_modulate(x_ref[rows, :], gain, shift)
        return carry

    lax.fori_loop(0, x_ref.shape[0] // NORM_ROWS, body, 0)


Q_LOGIT_SCALE = math.log2(math.e) / math.sqrt(HEAD_DIM)


def _head_norm(blk, gain):
    ms = jnp.mean(blk * blk, axis=-1, keepdims=True)
    return (blk * lax.rsqrt(ms + EPS) * gain).astype(BF16)


IN_PROJ_ROWS = 1024
X_SLOTS = 4


def _in_proj_kernel(x_hbm, mod_ref, nw_ref, wu_ref, wq_ref, wk_ref, wv_ref, wga_ref, wgb_ref,
                    qw_ref, kw_ref, u_ref, q_ref, k_ref, v_ref, ga_ref, gb_ref,
                    h_scr, xbuf, sems, *, tm):
    i = pl.program_id(0)
    j = pl.program_id(1)
    chunks = tm // NORM_ROWS

    def x_copy(tile, r):
        slot = r % X_SLOTS
        return pltpu.make_async_copy(
            x_hbm.at[pl.ds(tile * tm + r * NORM_ROWS, NORM_ROWS), :], xbuf.at[slot],
            sems.at[slot])

    @pl.when(jnp.logical_and(j == 0, i == 0))
    def _():
        for r in range(X_SLOTS):
            x_copy(0, r).start()

    @pl.when(j == 0)
    def _():
        gain = nw_ref[...] * (1.0 + mod_ref[0, 1:2, :])
        shift = mod_ref[0, 0:1, :]

        def body(r, carry):
            x_copy(i, r).wait()
            rows = pl.ds(pl.multiple_of(r * NORM_ROWS, NORM_ROWS), NORM_ROWS)
            h_scr[rows, :] = _norm_modulate(xbuf[r % X_SLOTS], gain, shift)

            @pl.when(r + X_SLOTS < chunks)
            def _():
                x_copy(i, r + X_SLOTS).start()

            return carry

        lax.fori_loop(0, chunks, body, 0)

    @pl.when(jnp.logical_and(j == pl.num_programs(1) - 1, i + 1 < pl.num_programs(0)))
    def _():
        for r in range(X_SLOTS):
            x_copy(i + 1, r).start()

    def bf16_cols(*w_refs):
        return jnp.concatenate([w_ref[...].astype(BF16) for w_ref in w_refs], axis=1)

    w_qk, w_ga, w_gb, w_uv = (bf16_cols(wq_ref, wk_ref), bf16_cols(wga_ref),
                              bf16_cols(wgb_ref), bf16_cols(wu_ref, wv_ref))
    q_gain = qw_ref[...] * Q_LOGIT_SCALE
    k_gain = kw_ref[...]
    for r0 in range(0, tm, IN_PROJ_ROWS):
        rows = slice(r0, r0 + IN_PROJ_ROWS)
        h = h_scr[rows, :]
        qk = jnp.dot(h, w_qk, preferred_element_type=F32)
        q_ref[rows, :] = _head_norm(qk[:, :HEAD_DIM], q_gain)
        k_ref[rows, :] = _head_norm(qk[:, HEAD_DIM:], k_gain)
        ga = jnp.dot(h, w_ga, preferred_element_type=F32)
        ga_ref[rows, :] = jax.nn.sigmoid(ga).astype(BF16)
        gb = jnp.dot(h, w_gb, preferred_element_type=F32)
        gb_ref[rows, :] = jax.nn.sigmoid(gb).astype(BF16)
        uv = jnp.dot(h, w_uv, preferred_element_type=F32)
        u_ref[rows, :] = uv[:, :HEAD_DIM].astype(BF16)
        v_ref[rows, :] = uv[:, HEAD_DIM:].astype(BF16)


def _in_proj(x2, mod, norm_w, w_in, q_norm_w, k_norm_w, *, tm):
    m, d = x2.shape
    hd = HEAD_DIM
    gw = D_MODEL // SB_HEADS
    assert POOL_WIDTH == SB_HEADS * hd and D_MODEL == SB_HEADS * gw
    assert m % tm == 0 and SEQ % tm == 0 and tm % IN_PROJ_ROWS == 0
    assert tm % NORM_ROWS == 0 and tm // NORM_ROWS >= X_SLOTS
    w_spec = lambda width, off: pl.BlockSpec((d, width), lambda i, j: (0, off // width + j))
    out_spec = lambda width: pl.BlockSpec((tm, width), lambda i, j: (i, j))
    return pl.pallas_call(
        functools.partial(_in_proj_kernel, tm=tm),
        grid=(m // tm, SB_HEADS),
        in_specs=[
            pl.BlockSpec(memory_space=pl.ANY),
            pl.BlockSpec((1, N_MOD, d), lambda i, j: (i * tm // SEQ, 0, 0)),
            pl.BlockSpec((1, d), lambda i, j: (0, 0)),
            w_spec(hd, 0), w_spec(hd, Q_OFF), w_spec(hd, K_OFF), w_spec(hd, V_OFF),
            w_spec(gw, GA_OFF), w_spec(gw, GB_OFF),
            pl.BlockSpec((1, hd), lambda i, j: (0, 0)),
            pl.BlockSpec((1, hd), lambda i, j: (0, 0)),
        ],
        out_specs=[out_spec(hd), out_spec(hd), out_spec(hd), out_spec(hd),
                   out_spec(gw), out_spec(gw)],
        out_shape=[jax.ShapeDtypeStruct((m, POOL_WIDTH), BF16),
                   jax.ShapeDtypeStruct((m, SB_WIDTH), BF16),
                   jax.ShapeDtypeStruct((m, SB_WIDTH), BF16),
                   jax.ShapeDtypeStruct((m, SB_WIDTH), BF16),
                   jax.ShapeDtypeStruct((m, D_MODEL), BF16),
                   jax.ShapeDtypeStruct((m, D_MODEL), BF16)],
        scratch_shapes=[pltpu.VMEM((tm, d), BF16), pltpu.VMEM((X_SLOTS, NORM_ROWS, d), F32),
                        pltpu.SemaphoreType.DMA((X_SLOTS,))],
        compiler_params=_params("arbitrary", "arbitrary"),
        name="in_proj",
    )(x2, mod, norm_w.reshape(1, d), w_in, w_in, w_in, w_in, w_in, w_in,
      q_norm_w.reshape(1, hd), k_norm_w.reshape(1, hd))


FFN_OUT_CHUNK = 512


def _ffn_kernel(h_ref, x1_hbm, mod_ref, w1_ref, w2_ref, o_ref, xbuf, sem, *, tm, xrows):
    i = pl.program_id(0)
    c = pl.program_id(1)
    x1_copy = pltpu.make_async_copy(x1_hbm.at[pl.ds(i * tm + c * xrows, xrows), :], xbuf, sem)
    x1_copy.start()

    @pl.when(c == 0)
    def _():
        o_ref[...] = jnp.zeros_like(o_ref)

    a = jnp.dot(h_ref[...], w1_ref[...], preferred_element_type=F32)
    r = jnp.maximum(a, 0.0)
    act = (r * r).astype(BF16)
    for n0 in range(0, o_ref.shape[1], FFN_OUT_CHUNK):
        cols = slice(n0, n0 + FFN_OUT_CHUNK)
        y = jnp.dot(act, w2_ref[:, cols], preferred_element_type=F32)
        o_ref[:, cols] += mod_ref[0, 5:6, cols] * y
    x1_copy.wait()
    rows = pl.ds(pl.multiple_of(c * xrows, xrows), xrows)
    o_ref[rows, :] += xbuf[...]


def _ffn(h2, x1, mod, w1, w2, *, tm, tf):
    m, d = h2.shape
    f = w1.shape[1]
    assert m % tm == 0 and SEQ % tm == 0 and f % tf == 0 and tm % (f // tf) == 0
    assert d % FFN_OUT_CHUNK == 0
    xrows = tm // (f // tf)
    return pl.pallas_call(
        functools.partial(_ffn_kernel, tm=tm, xrows=xrows),
        grid=(m // tm, f // tf),
        in_specs=[
            pl.BlockSpec((tm, d), lambda i, c: (i, 0)),
            pl.BlockSpec(memory_space=pl.ANY),
            pl.BlockSpec((1, N_MOD, d), lambda i, c: (i * tm // SEQ, 0, 0)),
            pl.BlockSpec((d, tf), lambda i, c: (0, c)),
            pl.BlockSpec((tf, d), lambda i, c: (c, 0)),
        ],
        out_specs=pl.BlockSpec((tm, d), lambda i, c: (i, 0)),
        out_shape=jax.ShapeDtypeStruct((m, d), F32),
        scratch_shapes=[pltpu.VMEM((xrows, d), F32), pltpu.SemaphoreType.DMA(())],
        compiler_params=_params("arbitrary", "arbitrary"),
        name="ffn",
    )(h2, x1, mod, w1, w2)


POOL_ROWS = 256
POOL_HEAD = max(POOL_WINDOWS)
assert POOL_HEAD % (2 * SUBLANES) == 0 and all(w & (w - 1) == 0 for w in POOL_WINDOWS)


def _pool_head_exact(u_head, w):
    pos = lax.broadcasted_iota(jnp.int32, u_head.shape, 0)
    win_sum = u_head
    for k in range(1, w):
        win_sum = win_sum + jnp.where(pos >= k, pltpu.roll(u_head, k, 0), 0.0)
    count = jnp.minimum(pos + 1, w).astype(F32)
    return win_sum / count - u_head


def _pool_bands():
    tt = jnp.arange(POOL_ROWS + POOL_HEAD)[:, None]
    jj = jnp.arange(POOL_ROWS)[None, :]
    bands = []
    for w in POOL_WINDOWS:
        inside = jnp.where(jnp.logical_and(jj <= tt, jj > tt - w), 1.0 / w, 0.0)
        inside = inside - jnp.where(jj == tt, 1.0, 0.0)
        spill = jnp.where(jj > tt - w, 1.0 / w, 0.0)
        bands.append(jnp.where(tt < POOL_ROWS, inside, spill))
    return jnp.stack(bands).astype(BF16)


def _pool_rows(first, u_ref, band_ref, w_ref, s_ref, o_ref, carry_ref, head_ref):
    gd = POOL_GROUP_DIM
    for g in range(len(POOL_WINDOWS)):
        cols = slice(g * gd, (g + 1) * gd)
        res = jnp.dot(band_ref[g], u_ref[:, cols], preferred_element_type=F32)
        head = jnp.where(first, head_ref[g], res[:POOL_HEAD] + carry_ref[g])
        carry_ref[g] = res[POOL_ROWS:]
        pooled = jnp.concatenate([head, res[POOL_HEAD:POOL_ROWS]], axis=0).astype(BF16)
        mixed = jnp.dot(pooled, w_ref[g].astype(BF16), preferred_element_type=F32)
        o_ref[:, cols] = (mixed * s_ref[:, cols]).astype(o_ref.dtype)


EXP2_UNDERFLOW = -151.0


def _attn_kernel(q_ref, k_ref, v_ref, u_ref, band_ref, wp_ref, ps_ref, *refs, tb, hp, n_cast):
    w_refs, (o_ref, pa_ref) = refs[:n_cast], refs[n_cast:n_cast + 2]
    wbf_refs = refs[n_cast + 2:2 * n_cast + 2]
    carry_ref, acc_ref, pool_carry, pool_head = refs[2 * n_cast + 2:]
    for w_ref, wbf_ref in zip(w_refs, wbf_refs):
        wbf_ref[...] = w_ref[...].astype(BF16)

    qi = pl.program_id(2)
    nseg = tb // LANES
    first = qi == 0

    @pl.when(first)
    def _():
        pool_carry[...] = jnp.zeros_like(pool_carry)
        for g, w in enumerate(POOL_WINDOWS):
            cols = slice(g * POOL_GROUP_DIM, (g + 1) * POOL_GROUP_DIM)
            pool_head[g] = _pool_head_exact(u_ref[0:POOL_HEAD, cols].astype(F32), w)

    def pool_rows():
        _pool_rows(first, u_ref, band_ref, wp_ref, ps_ref, pa_ref, pool_carry, pool_head)

    r = lax.broadcasted_iota(jnp.int32, (LANES, 2 * LANES), 0)
    c = lax.broadcasted_iota(jnp.int32, (LANES, 2 * LANES), 1)
    cum_op = jnp.where(jnp.logical_or(c >= LANES, r > c), 1.0, 0.0).astype(BF16)
    tri_r = lax.broadcasted_iota(jnp.int32, (LANES, LANES), 0)
    tri_c = lax.broadcasted_iota(jnp.int32, (LANES, LANES), 1)
    causal = tri_c < tri_r
    segs = [slice(sg * LANES, (sg + 1) * LANES) for sg in range(nseg)]

    def block(kb, diagonal):
        start = pl.multiple_of(kb * tb, tb)
        if diagonal:
            parts = [(segs[g], g + 1) for g in range(nseg)]
        else:
            parts = [(slice(0, tb), nseg)]
        work = [(p, slice(p * HEAD_DIM, (p + 1) * HEAD_DIM), rows, nk)
                for rows, nk in parts for p in range(hp)]
        zs = [lax.dot_general(q_ref[rows, cols], k_ref[pl.ds(start, nk * LANES), cols],
                              (((1,), (1,)), ((), ())), preferred_element_type=F32)
              for _, cols, rows, nk in work]
        if diagonal:
            pool_rows()
        log_betas, sums = [], []
        for z, (_, _, _, nk) in zip(zs, work):
            log_beta = jnp.minimum(z, 0.0) - jnp.log2(1.0 + jnp.exp2(-jnp.abs(z)))
            l = log_beta - z
            cs = []
            for sg in range(nk):
                l_seg = l[:, segs[sg]]
                if diagonal and sg == nk - 1:
                    l_seg = jnp.where(causal, l_seg, 0.0)
                cs.append(jnp.dot(l_seg.astype(BF16), cum_op, preferred_element_type=F32))
            log_betas.append(log_beta)
            sums.append(cs)
        top = None
        for (p, cols, rows, nk), log_beta, cs in zip(work, log_betas, sums):
            carry = carry_ref[p, rows, :]
            a_parts = [None] * nk
            for sg in range(nk - 1, -1, -1):
                a_seg = jnp.exp2(log_beta[:, segs[sg]] + (cs[sg][:, :LANES] + carry))
                if diagonal and sg == nk - 1:
                    a_seg = jnp.where(causal, a_seg, 0.0)
                a_parts[sg] = a_seg
                carry = carry + cs[sg][:, LANES:]
            a = jnp.concatenate(a_parts, axis=1) if nk > 1 else a_parts[0]
            acc_ref[p, rows, :] += jnp.dot(
                a.astype(BF16), v_ref[pl.ds(start, nk * LANES), cols],
                preferred_element_type=F32)
            carry_ref[p, rows, :] = carry
            top = jnp.max(carry) if top is None else jnp.maximum(top, jnp.max(carry))
        return top

    carry_ref[...] = jnp.zeros_like(carry_ref)
    acc_ref[...] = jnp.zeros_like(acc_ref)
    top = block(qi, True)

    def cond(state):
        kb, top = state
        return jnp.logical_and(kb >= 0, top > EXP2_UNDERFLOW)

    def body(state):
        kb, _ = state
        return kb - 1, block(kb, False)

    lax.while_loop(cond, body, (qi - 1, top))
    for p in range(hp):
        o_ref[:, p * HEAD_DIM:(p + 1) * HEAD_DIM] = acc_ref[p].astype(o_ref.dtype)


def _attn(q, k, v, u, w_pool, pool_scale, weights, batch, *, tb, hp):
    nq = SEQ // tb
    width = hp * HEAD_DIM
    steps = batch * nq
    groups = len(POOL_WINDOWS)
    gd = POOL_GROUP_DIM
    assert hp == SB_HEADS and tb == POOL_ROWS
    assert all(w.shape[0] % steps == 0 for w in weights)
    rows_spec = lambda cols: pl.BlockSpec((tb, cols), lambda b, h, i: (b * nq + i, 0))
    seq_spec = pl.BlockSpec((SEQ, width), lambda b, h, i: (b, 0))
    whole = lambda shape: pl.BlockSpec(shape, lambda b, h, i: (0,) * len(shape))
    cast_specs = [pl.BlockSpec((w.shape[0] // steps, w.shape[1]),
                               lambda b, h, i: (b * nq + i, 0))
                  for w in weights]
    bands = _pool_bands()
    outs = pl.pallas_call(
        functools.partial(_attn_kernel, tb=tb, hp=hp, n_cast=len(weights)),
        grid=(batch, 1, nq),
        in_specs=[rows_spec(width), seq_spec, seq_spec, rows_spec(POOL_WIDTH),
                  whole(bands.shape), whole(w_pool.shape), whole((1, POOL_WIDTH))] + cast_specs,
        out_specs=[rows_spec(width), rows_spec(POOL_WIDTH)] + cast_specs,
        out_shape=[jax.ShapeDtypeStruct((batch * SEQ, SB_WIDTH), BF16),
                   jax.ShapeDtypeStruct((batch * SEQ, POOL_WIDTH), BF16)]
        + [jax.ShapeDtypeStruct(w.shape, BF16) for w in weights],
        scratch_shapes=[pltpu.VMEM((hp, tb, LANES), F32),
                        pltpu.VMEM((hp, tb, HEAD_DIM), F32),
                        pltpu.VMEM((groups, POOL_HEAD, gd), F32),
                        pltpu.VMEM((groups, POOL_HEAD, gd), F32)],
        compiler_params=_params("arbitrary", "arbitrary", "arbitrary"),
        name="attn",
    )(q, k, v, u, bands, w_pool, pool_scale.reshape(1, POOL_WIDTH), *weights)
    return outs[0], outs[1], outs[2:]


MIX_CHUNK = 256


def _mix_kernel(pa_ref, at_ref, sa_ref, sb_ref, x_ref, mod_ref, nw_ref, wa_ref, wb_ref, wo_ref,
                x1_ref, h2_ref, merged_scr):
    d = x_ref.shape[1]
    pa = pa_ref[...]
    at = at_ref[...]
    for n0 in range(0, d, MIX_CHUNK):
        cols = slice(n0, n0 + MIX_CHUNK)
        ya = jnp.dot(pa, wa_ref[:, cols], preferred_element_type=F32)
        yb = jnp.dot(at, wb_ref[:, cols], preferred_element_type=F32)
        merged = sa_ref[:, cols].astype(F32) * ya + sb_ref[:, cols].astype(F32) * yb
        merged_scr[:, cols] = merged.astype(BF16)
    merged = merged_scr[...]
    for n0 in range(0, d, MIX_CHUNK):
        cols = slice(n0, n0 + MIX_CHUNK)
        o = jnp.dot(merged, wo_ref[:, cols], preferred_element_type=F32)
        x1_ref[:, cols] = x_ref[:, cols] + mod_ref[0, 2:3, cols] * o
    _norm_modulate_rows(x1_ref, h2_ref, nw_ref[...], mod_ref[0, 4:5, :], mod_ref[0, 3:4, :])


def _mix(pa, at, sa, sb, x2, mod, norm_w, wa_bf, wb_bf, wo_bf, *, tm):
    m, d = x2.shape
    kdim = pa.shape[1]
    resident = pl.Buffered(1)
    return pl.pallas_call(
        _mix_kernel,
        grid=(m // tm,),
        in_specs=[
            pl.BlockSpec((tm, kdim), lambda i: (i, 0)),
            pl.BlockSpec((tm, kdim), lambda i: (i, 0)),
            pl.BlockSpec((tm, d), lambda i: (i, 0)),
            pl.BlockSpec((tm, d), lambda i: (i, 0)),
            pl.BlockSpec((tm, d), lambda i: (i, 0)),
            pl.BlockSpec((1, N_MOD, d), lambda i: (i * tm // SEQ, 0, 0)),
            pl.BlockSpec((1, d), lambda i: (0, 0)),
            pl.BlockSpec((kdim, d), lambda i: (0, 0), pipeline_mode=resident),
            pl.BlockSpec((kdim, d), lambda i: (0, 0), pipeline_mode=resident),
            pl.BlockSpec((d, d), lambda i: (0, 0), pipeline_mode=resident),
        ],
        out_specs=[pl.BlockSpec((tm, d), lambda i: (i, 0)),
                   pl.BlockSpec((tm, d), lambda i: (i, 0))],
        out_shape=[jax.ShapeDtypeStruct((m, d), F32),
                   jax.ShapeDtypeStruct((m, d), BF16)],
        scratch_shapes=[pltpu.VMEM((tm, d), BF16)],
        compiler_params=_params("arbitrary"),
        name="mix",
    )(pa, at, sa, sb, x2, mod, norm_w.reshape(1, d), wa_bf, wb_bf, wo_bf)


def kernel(x, c, w_ada, b_ada, norm1_w, w_in, q_norm_w, k_norm_w, w_pool, pool_scale,
           w_a_up, w_b_up, w_o, norm2_w, w_ff1, w_ff2):
    batch, seq, d = x.shape
    assert (seq, d) == (SEQ, D_MODEL) and w_ada.shape[0] == 1
    x2 = x.reshape(batch * seq, d)

    mod = _ada(c, w_ada[0], b_ada[0])
    u, q, k, v, sa, sb = _in_proj(x2, mod, norm1_w[0], w_in[0], q_norm_w[0], k_norm_w[0],
                                  tm=2048)
    at, pa, (wa_bf, wb_bf, wo_bf, w1_bf, w2_bf) = _attn(
        q, k, v, u, w_pool[0], pool_scale[0],
        (w_a_up[0], w_b_up[0], w_o[0], w_ff1[0], w_ff2[0]), batch,
        tb=POOL_ROWS, hp=SB_HEADS)
    x1, h2 = _mix(pa, at, sa, sb, x2, mod, norm2_w[0], wa_bf, wb_bf, wo_bf, tm=256)
    out = _ffn(h2, x1, mod, w1_bf, w2_bf, tm=1024, tf=1024)
    return out.reshape(batch, seq, d)
```

```python
import functools
import math

import jax
import jax.numpy as jnp
from jax import lax
from jax.experimental import pallas as pl
from jax.experimental.pallas import tpu as pltpu

D_MODEL = 2048
SEQ = 2048
POOL_WIDTH = D_MODEL // 2
POOL_WINDOWS = (2, 4, 8, 16)
POOL_GROUP_DIM = POOL_WIDTH // len(POOL_WINDOWS)
HEAD_DIM = 128
SB_WIDTH = D_MODEL // 2
SB_HEADS = SB_WIDTH // HEAD_DIM
N_MOD = 6
EPS = 1e-6

Q_OFF = POOL_WIDTH
K_OFF = Q_OFF + SB_WIDTH
V_OFF = K_OFF + SB_WIDTH
GA_OFF = V_OFF + SB_WIDTH
GB_OFF = GA_OFF + D_MODEL

LANES = 128
SUBLANES = 8
V7X_VMEM_BYTES = 64 * 1024 * 1024
VMEM_LIMIT_BYTES = V7X_VMEM_BYTES * 7 // 8

BF16 = jnp.bfloat16
F32 = jnp.float32


def _params(*semantics):
    return pltpu.CompilerParams(dimension_semantics=semantics,
                                vmem_limit_bytes=VMEM_LIMIT_BYTES)


ADA_STREAMS = 4


def _ada_kernel(c_ref, b_ref, *refs):
    w_refs, o_ref = refs[:-1], refs[-1]
    c = c_ref[...]
    batch = c.shape[0]
    sc = c * jax.nn.sigmoid(c)
    if batch % SUBLANES:
        pad = SUBLANES - batch % SUBLANES
        sc = jnp.concatenate([sc, jnp.zeros((pad, sc.shape[1]), F32)], axis=0)
    sc = sc.astype(BF16)
    rows = w_refs[0].shape[0]
    acc = b_ref[...]
    for s, w_ref in enumerate(w_refs):
        acc = acc + jnp.dot(sc[:, s * rows:(s + 1) * rows], w_ref[...].astype(BF16),
                            preferred_element_type=F32)[:batch]
    o_ref[0] = acc


def _ada(c, w_ada, b_ada):
    batch, d = c.shape
    n = w_ada.shape[1]
    tn = 1024
    assert n == N_MOD * d and d % tn == 0 and d % ADA_STREAMS == 0
    slab = d // ADA_STREAMS
    per_chunk = d // tn
    return pl.pallas_call(
        _ada_kernel,
        grid=(n // tn,),
        in_specs=[
            pl.BlockSpec((batch, d), lambda j: (0, 0)),
            pl.BlockSpec((1, tn), lambda j: (0, j)),
        ] + [pl.BlockSpec((slab, tn), functools.partial(lambda j, s: (s, j), s=s))
             for s in range(ADA_STREAMS)],
        out_specs=pl.BlockSpec((1, batch, tn), lambda j: (j // per_chunk, 0, j % per_chunk)),
        out_shape=jax.ShapeDtypeStruct((N_MOD, batch, d), F32),
        compiler_params=_params("arbitrary"),
        name="ada",
    )(c, b_ada.reshape(1, n), *([w_ada] * ADA_STREAMS))


SHIFT1, SCALE1, GATE1, SHIFT2, SCALE2, GATE2 = range(N_MOD)


def _mod_row(mod_ref, k, b):
    return mod_ref[k, pl.ds(b, 1), :]


NORM_ROWS = 256


def _norm_modulate(x, gain, shift):
    y = x * lax.rsqrt(jnp.mean(x * x, axis=-1, keepdims=True) + EPS)
    return (y * gain + shift).astype(BF16)


def _norm_modulate_rows(x_ref, o_ref, norm_w, scale, shift):
    gain = norm_w * (1.0 + scale)

    def body(r, carry):
        rows = pl.ds(pl.multiple_of(r * NORM_ROWS, NORM_ROWS), NORM_ROWS)
        o_ref[rows, :] = _norm_modulate(x_ref[rows, :], gain, shift)
        return carry

    lax.fori_loop(0, x_ref.shape[0] // NORM_ROWS, body, 0)


Q_LOGIT_SCALE = math.log2(math.e) / math.sqrt(HEAD_DIM)


def _sigmoid(x):
    return 0.5 * jnp.tanh(0.5 * x) + 0.5


def _head_norm(blk, gain):
    ms = jnp.mean(blk * blk, axis=-1, keepdims=True)
    return (blk * lax.rsqrt(ms + EPS) * gain).astype(BF16)


IN_PROJ_ROWS = 1024
X_SLOTS = 4


def _in_proj_kernel(x_hbm, mod_ref, nw_ref, wu_ref, wq_ref, wk_ref, wv_ref, wga_ref, wgb_ref,
                    qw_ref, kw_ref, u_ref, q_ref, k_ref, v_ref, ga_ref, gb_ref,
                    h_scr, xbuf, sems, *, tm):
    i = pl.program_id(0)
    j = pl.program_id(1)
    chunks = tm // NORM_ROWS

    def x_copy(tile, r):
        slot = r % X_SLOTS
        return pltpu.make_async_copy(
            x_hbm.at[pl.ds(tile * tm + r * NORM_ROWS, NORM_ROWS), :], xbuf.at[slot],
            sems.at[slot])

    @pl.when(jnp.logical_and(j == 0, i == 0))
    def _():
        for r in range(X_SLOTS):
            x_copy(0, r).start()

    @pl.when(j == 0)
    def _():
        b = i * tm // SEQ
        gain = nw_ref[...] * (1.0 + _mod_row(mod_ref, SCALE1, b))
        shift = _mod_row(mod_ref, SHIFT1, b)

        def body(r, carry):
            x_copy(i, r).wait()
            rows = pl.ds(pl.multiple_of(r * NORM_ROWS, NORM_ROWS), NORM_ROWS)
            h_scr[rows, :] = _norm_modulate(xbuf[r % X_SLOTS], gain, shift)

            @pl.when(r + X_SLOTS < chunks)
            def _():
                x_copy(i, r + X_SLOTS).start()

            return carry

        lax.fori_loop(0, chunks, body, 0)

    @pl.when(jnp.logical_and(j == pl.num_programs(1) - 1, i + 1 < pl.num_programs(0)))
    def _():
        for r in range(X_SLOTS):
            x_copy(i + 1, r).start()

    def bf16_cols(*w_refs):
        return jnp.concatenate([w_ref[...].astype(BF16) for w_ref in w_refs], axis=1)

    w_qk, w_ga, w_gb, w_uv = (bf16_cols(wq_ref, wk_ref), bf16_cols(wga_ref),
                              bf16_cols(wgb_ref), bf16_cols(wu_ref, wv_ref))
    q_gain = qw_ref[...] * Q_LOGIT_SCALE
    k_gain = kw_ref[...]
    for r0 in range(0, tm, IN_PROJ_ROWS):
        rows = slice(r0, r0 + IN_PROJ_ROWS)
        h = h_scr[rows, :]
        qk = jnp.dot(h, w_qk, preferred_element_type=F32)
        q_ref[rows, :] = _head_norm(qk[:, :HEAD_DIM], q_gain)
        k_ref[rows, :] = _head_norm(qk[:, HEAD_DIM:], k_gain)
        ga = jnp.dot(h, w_ga, preferred_element_type=F32)
        ga_ref[rows, :] = _sigmoid(ga).astype(BF16)
        gb = jnp.dot(h, w_gb, preferred_element_type=F32)
        gb_ref[rows, :] = _sigmoid(gb).astype(BF16)
        uv = jnp.dot(h, w_uv, preferred_element_type=F32)
        u_ref[rows, :] = uv[:, :HEAD_DIM].astype(BF16)
        v_ref[rows, :] = uv[:, HEAD_DIM:].astype(BF16)


def _in_proj(x2, mod, norm_w, w_in, q_norm_w, k_norm_w, *, tm):
    m, d = x2.shape
    hd = HEAD_DIM
    gw = D_MODEL // SB_HEADS
    assert POOL_WIDTH == SB_HEADS * hd and D_MODEL == SB_HEADS * gw
    assert m % tm == 0 and SEQ % tm == 0 and tm % IN_PROJ_ROWS == 0
    assert tm % NORM_ROWS == 0 and tm // NORM_ROWS >= X_SLOTS
    w_spec = lambda width, off: pl.BlockSpec((d, width), lambda i, j: (0, off // width + j))
    out_spec = lambda width: pl.BlockSpec((tm, width), lambda i, j: (i, j))
    return pl.pallas_call(
        functools.partial(_in_proj_kernel, tm=tm),
        grid=(m // tm, SB_HEADS),
        in_specs=[
            pl.BlockSpec(memory_space=pl.ANY),
            pl.BlockSpec(mod.shape, lambda i, j: (0, 0, 0)),
            pl.BlockSpec((1, d), lambda i, j: (0, 0)),
            w_spec(hd, 0), w_spec(hd, Q_OFF), w_spec(hd, K_OFF), w_spec(hd, V_OFF),
            w_spec(gw, GA_OFF), w_spec(gw, GB_OFF),
            pl.BlockSpec((1, hd), lambda i, j: (0, 0)),
            pl.BlockSpec((1, hd), lambda i, j: (0, 0)),
        ],
        out_specs=[out_spec(hd), out_spec(hd), out_spec(hd), out_spec(hd),
                   out_spec(gw), out_spec(gw)],
        out_shape=[jax.ShapeDtypeStruct((m, POOL_WIDTH), BF16),
                   jax.ShapeDtypeStruct((m, SB_WIDTH), BF16),
                   jax.ShapeDtypeStruct((m, SB_WIDTH), BF16),
                   jax.ShapeDtypeStruct((m, SB_WIDTH), BF16),
                   jax.ShapeDtypeStruct((m, D_MODEL), BF16),
                   jax.ShapeDtypeStruct((m, D_MODEL), BF16)],
        scratch_shapes=[pltpu.VMEM((tm, d), BF16), pltpu.VMEM((X_SLOTS, NORM_ROWS, d), F32),
                        pltpu.SemaphoreType.DMA((X_SLOTS,))],
        compiler_params=_params("arbitrary", "arbitrary"),
        name="in_proj",
    )(x2, mod, norm_w.reshape(1, d), w_in, w_in, w_in, w_in, w_in, w_in,
      q_norm_w.reshape(1, hd), k_norm_w.reshape(1, hd))


FFN_OUT_CHUNK = 512


def _ffn_kernel(h_ref, x1_hbm, mod_ref, w1_ref, w2_ref, o_ref, xbuf, sem, *, tm, xrows):
    i = pl.program_id(0)
    c = pl.program_id(1)
    x1_copy = pltpu.make_async_copy(x1_hbm.at[pl.ds(i * tm + c * xrows, xrows), :], xbuf, sem)
    x1_copy.start()

    @pl.when(c == 0)
    def _():
        o_ref[...] = jnp.zeros_like(o_ref)

    a = jnp.dot(h_ref[...], w1_ref[...], preferred_element_type=F32)
    r = jnp.maximum(a, 0.0)
    act = (r * r).astype(BF16)
    gate = _mod_row(mod_ref, GATE2, i * tm // SEQ)
    for n0 in range(0, o_ref.shape[1], FFN_OUT_CHUNK):
        cols = slice(n0, n0 + FFN_OUT_CHUNK)
        y = jnp.dot(act, w2_ref[:, cols], preferred_element_type=F32)
        o_ref[:, cols] += gate[:, cols] * y
    x1_copy.wait()
    rows = pl.ds(pl.multiple_of(c * xrows, xrows), xrows)
    o_ref[rows, :] += xbuf[...]


def _ffn(h2, x1, mod, w1, w2, *, tm, tf):
    m, d = h2.shape
    f = w1.shape[1]
    assert m % tm == 0 and SEQ % tm == 0 and f % tf == 0 and tm % (f // tf) == 0
    assert d % FFN_OUT_CHUNK == 0
    xrows = tm // (f // tf)
    return pl.pallas_call(
        functools.partial(_ffn_kernel, tm=tm, xrows=xrows),
        grid=(m // tm, f // tf),
        in_specs=[
            pl.BlockSpec((tm, d), lambda i, c: (i, 0)),
            pl.BlockSpec(memory_space=pl.ANY),
            pl.BlockSpec(mod.shape, lambda i, c: (0, 0, 0)),
            pl.BlockSpec((d, tf), lambda i, c: (0, c)),
            pl.BlockSpec((tf, d), lambda i, c: (c, 0)),
        ],
        out_specs=pl.BlockSpec((tm, d), lambda i, c: (i, 0)),
        out_shape=jax.ShapeDtypeStruct((m, d), F32),
        scratch_shapes=[pltpu.VMEM((xrows, d), F32), pltpu.SemaphoreType.DMA(())],
        compiler_params=_params("arbitrary", "arbitrary"),
        name="ffn",
    )(h2, x1, mod, w1, w2)


POOL_ROWS = 256
POOL_HEAD = max(POOL_WINDOWS)
assert POOL_HEAD % (2 * SUBLANES) == 0 and all(w & (w - 1) == 0 for w in POOL_WINDOWS)


def _pool_head_exact(u_head, w):
    pos = lax.broadcasted_iota(jnp.int32, u_head.shape, 0)
    win_sum = u_head
    for k in range(1, w):
        win_sum = win_sum + jnp.where(pos >= k, pltpu.roll(u_head, k, 0), 0.0)
    count = jnp.minimum(pos + 1, w).astype(F32)
    return win_sum / count - u_head


def _pool_kernel(u_ref, w_ref, s_ref, o_ref):
    gd = POOL_GROUP_DIM
    tt = lax.broadcasted_iota(jnp.int32, (POOL_ROWS + POOL_HEAD, POOL_ROWS), 0)
    jj = lax.broadcasted_iota(jnp.int32, (POOL_ROWS + POOL_HEAD, POOL_ROWS), 1)
    for g, w in enumerate(POOL_WINDOWS):
        cols = slice(g * gd, (g + 1) * gd)
        inside = jnp.where(jnp.logical_and(jj <= tt, jj > tt - w), 1.0 / w, 0.0)
        inside = inside - jnp.where(jj == tt, 1.0, 0.0)
        spill = jnp.where(jj > tt - w, 1.0 / w, 0.0)
        band = jnp.where(tt < POOL_ROWS, inside, spill).astype(BF16)
        carry = None
        pooled = []
        for r0 in range(0, SEQ, POOL_ROWS):
            res = jnp.dot(band, u_ref[r0:r0 + POOL_ROWS, cols], preferred_element_type=F32)
            if r0 == 0:
                head = _pool_head_exact(u_ref[0:POOL_HEAD, cols].astype(F32), w)
            else:
                head = res[:POOL_HEAD] + carry
            pooled += [head.astype(BF16), res[POOL_HEAD:POOL_ROWS].astype(BF16)]
            carry = res[POOL_ROWS:]
        mixed = jnp.dot(jnp.concatenate(pooled, axis=0), w_ref[g].astype(BF16),
                        preferred_element_type=F32)
        o_ref[:, cols] = (mixed * s_ref[:, cols]).astype(o_ref.dtype)


def _pool(u, w_pool, pool_scale, batch):
    groups = len(POOL_WINDOWS)
    gd = POOL_GROUP_DIM
    assert SEQ % POOL_ROWS == 0
    return pl.pallas_call(
        _pool_kernel,
        grid=(batch,),
        in_specs=[
            pl.BlockSpec((SEQ, POOL_WIDTH), lambda b: (b, 0)),
            pl.BlockSpec((groups, gd, gd), lambda b: (0, 0, 0)),
            pl.BlockSpec((1, POOL_WIDTH), lambda b: (0, 0)),
        ],
        out_specs=pl.BlockSpec((SEQ, POOL_WIDTH), lambda b: (b, 0)),
        out_shape=jax.ShapeDtypeStruct((batch * SEQ, POOL_WIDTH), BF16),
        compiler_params=_params("arbitrary"),
        name="pool",
    )(u, w_pool, pool_scale.reshape(1, POOL_WIDTH))


EXP2_UNDERFLOW = -151.0


def _attn_kernel(q_ref, k_ref, v_ref, *refs, tb, hp, n_cast):
    w_refs, o_ref, wbf_refs = refs[:n_cast], refs[n_cast], refs[n_cast + 1:2 * n_cast + 1]
    carry_ref, acc_ref = refs[2 * n_cast + 1:]
    for w_ref, wbf_ref in zip(w_refs, wbf_refs):
        wbf_ref[...] = w_ref[...].astype(BF16)

    qi = pl.program_id(2)
    nseg = tb // LANES

    r = lax.broadcasted_iota(jnp.int32, (LANES, 2 * LANES), 0)
    c = lax.broadcasted_iota(jnp.int32, (LANES, 2 * LANES), 1)
    cum_op = jnp.where(jnp.logical_or(c >= LANES, r > c), 1.0, 0.0).astype(BF16)
    tri_r = lax.broadcasted_iota(jnp.int32, (LANES, LANES), 0)
    tri_c = lax.broadcasted_iota(jnp.int32, (LANES, LANES), 1)
    causal = tri_c < tri_r
    segs = [slice(sg * LANES, (sg + 1) * LANES) for sg in range(nseg)]

    def block(kb, diagonal):
        start = pl.multiple_of(kb * tb, tb)
        if diagonal:
            parts = [(segs[g], g + 1) for g in range(nseg)]
        else:
            parts = [(slice(0, tb), nseg)]
        work = [(p, slice(p * HEAD_DIM, (p + 1) * HEAD_DIM), rows, nk)
                for rows, nk in parts for p in range(hp)]
        zs = [lax.dot_general(q_ref[rows, cols], k_ref[pl.ds(start, nk * LANES), cols],
                              (((1,), (1,)), ((), ())), preferred_element_type=F32)
              for _, cols, rows, nk in work]
        log_betas, sums = [], []
        for z, (_, _, _, nk) in zip(zs, work):
            log_beta = jnp.minimum(z, 0.0) - jnp.log2(1.0 + jnp.exp2(-jnp.abs(z)))
            l = log_beta - z
            cs = []
            for sg in range(nk):
                l_seg = l[:, segs[sg]]
                if diagonal and sg == nk - 1:
                    l_seg = jnp.where(causal, l_seg, 0.0)
                cs.append(jnp.dot(l_seg.astype(BF16), cum_op, preferred_element_type=F32))
            log_betas.append(log_beta)
            sums.append(cs)
        top = None
        for (p, cols, rows, nk), log_beta, cs in zip(work, log_betas, sums):
            carry = carry_ref[p, rows, :]
            a_parts = [None] * nk
            for sg in range(nk - 1, -1, -1):
                a_seg = jnp.exp2(log_beta[:, segs[sg]] + (cs[sg][:, :LANES] + carry))
                if diagonal and sg == nk - 1:
                    a_seg = jnp.where(causal, a_seg, 0.0)
                a_parts[sg] = a_seg
                carry = carry + cs[sg][:, LANES:]
            a = jnp.concatenate(a_parts, axis=1) if nk > 1 else a_parts[0]
            acc_ref[p, rows, :] += jnp.dot(
                a.astype(BF16), v_ref[pl.ds(start, nk * LANES), cols],
                preferred_element_type=F32)
            carry_ref[p, rows, :] = carry
            top = jnp.max(carry) if top is None else jnp.maximum(top, jnp.max(carry))
        return top

    carry_ref[...] = jnp.zeros_like(carry_ref)
    acc_ref[...] = jnp.zeros_like(acc_ref)
    top = block(qi, True)

    def cond(state):
        kb, top = state
        return jnp.logical_and(kb >= 0, top > EXP2_UNDERFLOW)

    def body(state):
        kb, _ = state
        return kb - 1, block(kb, False)

    lax.while_loop(cond, body, (qi - 1, top))
    for p in range(hp):
        o_ref[:, p * HEAD_DIM:(p + 1) * HEAD_DIM] = acc_ref[p].astype(o_ref.dtype)


def _attn(q, k, v, weights, batch, *, tb, hp):
    nq = SEQ // tb
    groups = SB_HEADS // hp
    width = hp * HEAD_DIM
    steps = batch * groups * nq
    assert all(w.shape[0] % steps == 0 for w in weights)
    cast_specs = [pl.BlockSpec((w.shape[0] // steps, w.shape[1]),
                               lambda b, h, i: ((b * groups + h) * nq + i, 0))
                  for w in weights]
    outs = pl.pallas_call(
        functools.partial(_attn_kernel, tb=tb, hp=hp, n_cast=len(weights)),
        grid=(batch, groups, nq),
        in_specs=[
            pl.BlockSpec((tb, width), lambda b, h, i: (b * nq + i, h)),
            pl.BlockSpec((SEQ, width), lambda b, h, i: (b, h)),
            pl.BlockSpec((SEQ, width), lambda b, h, i: (b, h)),
        ] + cast_specs,
        out_specs=[pl.BlockSpec((tb, width), lambda b, h, i: (b * nq + i, h))] + cast_specs,
        out_shape=[jax.ShapeDtypeStruct((batch * SEQ, SB_WIDTH), BF16)]
        + [jax.ShapeDtypeStruct(w.shape, BF16) for w in weights],
        scratch_shapes=[pltpu.VMEM((hp, tb, LANES), F32),
                        pltpu.VMEM((hp, tb, HEAD_DIM), F32)],
        compiler_params=_params("arbitrary", "arbitrary", "arbitrary"),
        name="attn",
    )(q, k, v, *weights)
    return outs[0], outs[1:]


MIX_CHUNK = 256


def _mix_kernel(pa_ref, at_ref, sa_ref, sb_ref, x_ref, mod_ref, nw_ref, wa_ref, wb_ref, wo_ref,
                x1_ref, h2_ref, merged_scr):
    d = x_ref.shape[1]
    pa = pa_ref[...]
    at = at_ref[...]
    for n0 in range(0, d, MIX_CHUNK):
        cols = slice(n0, n0 + MIX_CHUNK)
        ya = jnp.dot(pa, wa_ref[:, cols], preferred_element_type=F32)
        yb = jnp.dot(at, wb_ref[:, cols], preferred_element_type=F32)
        merged = sa_ref[:, cols].astype(F32) * ya + sb_ref[:, cols].astype(F32) * yb
        merged_scr[:, cols] = merged.astype(BF16)
    merged = merged_scr[...]
    b = pl.program_id(0) * x_ref.shape[0] // SEQ
    gate = _mod_row(mod_ref, GATE1, b)
    for n0 in range(0, d, MIX_CHUNK):
        cols = slice(n0, n0 + MIX_CHUNK)
        o = jnp.dot(merged, wo_ref[:, cols], preferred_element_type=F32)
        x1_ref[:, cols] = x_ref[:, cols] + gate[:, cols] * o
    _norm_modulate_rows(x1_ref, h2_ref, nw_ref[...], _mod_row(mod_ref, SCALE2, b),
                        _mod_row(mod_ref, SHIFT2, b))


def _mix(pa, at, sa, sb, x2, mod, norm_w, wa_bf, wb_bf, wo_bf, *, tm):
    m, d = x2.shape
    kdim = pa.shape[1]
    resident = pl.Buffered(1)
    return pl.pallas_call(
        _mix_kernel,
        grid=(m // tm,),
        in_specs=[
            pl.BlockSpec((tm, kdim), lambda i: (i, 0)),
            pl.BlockSpec((tm, kdim), lambda i: (i, 0)),
            pl.BlockSpec((tm, d), lambda i: (i, 0)),
            pl.BlockSpec((tm, d), lambda i: (i, 0)),
            pl.BlockSpec((tm, d), lambda i: (i, 0)),
            pl.BlockSpec(mod.shape, lambda i: (0, 0, 0)),
            pl.BlockSpec((1, d), lambda i: (0, 0)),
            pl.BlockSpec((kdim, d), lambda i: (0, 0), pipeline_mode=resident),
            pl.BlockSpec((kdim, d), lambda i: (0, 0), pipeline_mode=resident),
            pl.BlockSpec((d, d), lambda i: (0, 0), pipeline_mode=resident),
        ],
        out_specs=[pl.BlockSpec((tm, d), lambda i: (i, 0)),
                   pl.BlockSpec((tm, d), lambda i: (i, 0))],
        out_shape=[jax.ShapeDtypeStruct((m, d), F32),
                   jax.ShapeDtypeStruct((m, d), BF16)],
        scratch_shapes=[pltpu.VMEM((tm, d), BF16)],
        compiler_params=_params("arbitrary"),
        name="mix",
    )(pa, at, sa, sb, x2, mod, norm_w.reshape(1, d), wa_bf, wb_bf, wo_bf)


def kernel(x, c, w_ada, b_ada, norm1_w, w_in, q_norm_w, k_norm_w, w_pool, pool_scale,
           w_a_up, w_b_up, w_o, norm2_w, w_ff1, w_ff2):
    batch, seq, d = x.shape
    assert (seq, d) == (SEQ, D_MODEL) and w_ada.shape[0] == 1
    x2 = x.reshape(batch * seq, d)

    mod = _ada(c, w_ada[0], b_ada[0])
    u, q, k, v, sa, sb = _in_proj(x2, mod, norm1_w[0], w_in[0], q_norm_w[0], k_norm_w[0],
                                  tm=2048)
    pa = _pool(u, w_pool[0], pool_scale[0], batch)
    at, (wa_bf, wb_bf, wo_bf, w1_bf, w2_bf) = _attn(
        q, k, v, (w_a_up[0], w_b_up[0], w_o[0], w_ff1[0], w_ff2[0]), batch,
        tb=256, hp=8)
    x1, h2 = _mix(pa, at, sa, sb, x2, mod, norm2_w[0], wa_bf, wb_bf, wo_bf, tm=256)
    out = _ffn(h2, x1, mod, w1_bf, w2_bf, tm=1024, tf=1024)
    return out.reshape(batch, seq, d)
```

```python
import functools
import math

import jax
import jax.numpy as jnp
from jax import lax
from jax.experimental import pallas as pl
from jax.experimental.pallas import tpu as pltpu

D_MODEL = 2048
SEQ = 2048
POOL_WIDTH = D_MODEL // 2
POOL_WINDOWS = (2, 4, 8, 16)
POOL_GROUP_DIM = POOL_WIDTH // len(POOL_WINDOWS)
HEAD_DIM = 128
SB_WIDTH = D_MODEL // 2
SB_HEADS = SB_WIDTH // HEAD_DIM
N_MOD = 6
EPS = 1e-6

Q_OFF = POOL_WIDTH
K_OFF = Q_OFF + SB_WIDTH
V_OFF = K_OFF + SB_WIDTH
GA_OFF = V_OFF + SB_WIDTH
GB_OFF = GA_OFF + D_MODEL

LANES = 128
SUBLANES = 8
V7X_VMEM_BYTES = 64 * 1024 * 1024
VMEM_LIMIT_BYTES = V7X_VMEM_BYTES * 7 // 8

BF16 = jnp.bfloat16
F32 = jnp.float32


def _params(*semantics):
    return pltpu.CompilerParams(dimension_semantics=semantics,
                                vmem_limit_bytes=VMEM_LIMIT_BYTES)


ADA_STREAMS = 4


def _ada_kernel(c_ref, b_ref, *refs):
    w_refs, o_ref = refs[:-1], refs[-1]
    c = c_ref[...]
    batch = c.shape[0]
    sc = c * jax.nn.sigmoid(c)
    if batch % SUBLANES:
        pad = SUBLANES - batch % SUBLANES
        sc = jnp.concatenate([sc, jnp.zeros((pad, sc.shape[1]), F32)], axis=0)
    sc = sc.astype(BF16)
    rows = w_refs[0].shape[0]
    acc = b_ref[...]
    for s, w_ref in enumerate(w_refs):
        acc = acc + jnp.dot(sc[:, s * rows:(s + 1) * rows], w_ref[...].astype(BF16),
                            preferred_element_type=F32)[:batch]
    o_ref[0] = acc


def _ada(c, w_ada, b_ada):
    batch, d = c.shape
    n = w_ada.shape[1]
    tn = 1024
    assert n == N_MOD * d and d % tn == 0 and d % ADA_STREAMS == 0
    slab = d // ADA_STREAMS
    per_chunk = d // tn
    return pl.pallas_call(
        _ada_kernel,
        grid=(n // tn,),
        in_specs=[
            pl.BlockSpec((batch, d), lambda j: (0, 0)),
            pl.BlockSpec((1, tn), lambda j: (0, j)),
        ] + [pl.BlockSpec((slab, tn), functools.partial(lambda j, s: (s, j), s=s))
             for s in range(ADA_STREAMS)],
        out_specs=pl.BlockSpec((1, batch, tn), lambda j: (j // per_chunk, 0, j % per_chunk)),
        out_shape=jax.ShapeDtypeStruct((N_MOD, batch, d), F32),
        compiler_params=_params("arbitrary"),
        name="ada",
    )(c, b_ada.reshape(1, n), *([w_ada] * ADA_STREAMS))


SHIFT1, SCALE1, GATE1, SHIFT2, SCALE2, GATE2 = range(N_MOD)


def _mod_row(mod_ref, k, b):
    return mod_ref[k, pl.ds(b, 1), :]


NORM_ROWS = 256


def _norm_modulate(x, gain, shift):
    y = x * lax.rsqrt(jnp.mean(x * x, axis=-1, keepdims=True) + EPS)
    return (y * gain + shift).astype(BF16)


def _norm_modulate_rows(x_ref, o_ref, norm_w, scale, shift):
    gain = norm_w * (1.0 + scale)

    def body(r, carry):
        rows = pl.ds(pl.multiple_of(r * NORM_ROWS, NORM_ROWS), NORM_ROWS)
        o_ref[rows, :] = _norm_modulate(x_ref[rows, :], gain, shift)
        return carry

    lax.fori_loop(0, x_ref.shape[0] // NORM_ROWS, body, 0)


Q_LOGIT_SCALE = math.log2(math.e) / math.sqrt(HEAD_DIM)


def _sigmoid(x):
    return 0.5 * jnp.tanh(0.5 * x) + 0.5


def _head_norm(blk, gain):
    ms = jnp.mean(blk * blk, axis=-1, keepdims=True)
    return (blk * lax.rsqrt(ms + EPS) * gain).astype(BF16)


IN_PROJ_ROWS = 1024
X_SLOTS = 4


def _in_proj_kernel(x_hbm, mod_ref, nw_ref, wu_ref, wq_ref, wk_ref, wv_ref, wga_ref, wgb_ref,
                    qw_ref, kw_ref, u_ref, q_ref, k_ref, v_ref, ga_ref, gb_ref,
                    h_scr, xbuf, sems, *, tm):
    i = pl.program_id(0)
    j = pl.program_id(1)
    chunks = tm // NORM_ROWS

    def x_copy(tile, r):
        slot = r % X_SLOTS
        return pltpu.make_async_copy(
            x_hbm.at[pl.ds(tile * tm + r * NORM_ROWS, NORM_ROWS), :], xbuf.at[slot],
            sems.at[slot])

    @pl.when(jnp.logical_and(j == 0, i == 0))
    def _():
        for r in range(X_SLOTS):
            x_copy(0, r).start()

    @pl.when(j == 0)
    def _():
        b = i * tm // SEQ
        gain = nw_ref[...] * (1.0 + _mod_row(mod_ref, SCALE1, b))
        shift = _mod_row(mod_ref, SHIFT1, b)

        def body(r, carry):
            x_copy(i, r).wait()
            rows = pl.ds(pl.multiple_of(r * NORM_ROWS, NORM_ROWS), NORM_ROWS)
            h_scr[rows, :] = _norm_modulate(xbuf[r % X_SLOTS], gain, shift)

            @pl.when(r + X_SLOTS < chunks)
            def _():
                x_copy(i, r + X_SLOTS).start()

            return carry

        lax.fori_loop(0, chunks, body, 0)

    @pl.when(jnp.logical_and(j == pl.num_programs(1) - 1, i + 1 < pl.num_programs(0)))
    def _():
        for r in range(X_SLOTS):
            x_copy(i + 1, r).start()

    def bf16_cols(*w_refs):
        return jnp.concatenate([w_ref[...].astype(BF16) for w_ref in w_refs], axis=1)

    w_qk, w_ga, w_gb, w_uv = (bf16_cols(wq_ref, wk_ref), bf16_cols(wga_ref),
                              bf16_cols(wgb_ref), bf16_cols(wu_ref, wv_ref))
    q_gain = qw_ref[...] * Q_LOGIT_SCALE
    k_gain = kw_ref[...]
    for r0 in range(0, tm, IN_PROJ_ROWS):
        rows = slice(r0, r0 + IN_PROJ_ROWS)
        h = h_scr[rows, :]
        qk = jnp.dot(h, w_qk, preferred_element_type=F32)
        q_ref[rows, :] = _head_norm(qk[:, :HEAD_DIM], q_gain)
        k_ref[rows, :] = _head_norm(qk[:, HEAD_DIM:], k_gain)
        ga = jnp.dot(h, w_ga, preferred_element_type=F32)
        ga_ref[rows, :] = _sigmoid(ga).astype(BF16)
        gb = jnp.dot(h, w_gb, preferred_element_type=F32)
        gb_ref[rows, :] = _sigmoid(gb).astype(BF16)
        uv = jnp.dot(h, w_uv, preferred_element_type=F32)
        u_ref[rows, :] = uv[:, :HEAD_DIM].astype(BF16)
        v_ref[rows, :] = uv[:, HEAD_DIM:].astype(BF16)


def _in_proj(x2, mod, norm_w, w_in, q_norm_w, k_norm_w, *, tm):
    m, d = x2.shape
    hd = HEAD_DIM
    gw = D_MODEL // SB_HEADS
    assert POOL_WIDTH == SB_HEADS * hd and D_MODEL == SB_HEADS * gw
    assert m % tm == 0 and SEQ % tm == 0 and tm % IN_PROJ_ROWS == 0
    assert tm % NORM_ROWS == 0 and tm // NORM_ROWS >= X_SLOTS
    w_spec = lambda width, off: pl.BlockSpec((d, width), lambda i, j: (0, off // width + j))
    out_spec = lambda width: pl.BlockSpec((tm, width), lambda i, j: (i, j))
    return pl.pallas_call(
        functools.partial(_in_proj_kernel, tm=tm),
        grid=(m // tm, SB_HEADS),
        in_specs=[
            pl.BlockSpec(memory_space=pl.ANY),
            pl.BlockSpec(mod.shape, lambda i, j: (0, 0, 0)),
            pl.BlockSpec((1, d), lambda i, j: (0, 0)),
            w_spec(hd, 0), w_spec(hd, Q_OFF), w_spec(hd, K_OFF), w_spec(hd, V_OFF),
            w_spec(gw, GA_OFF), w_spec(gw, GB_OFF),
            pl.BlockSpec((1, hd), lambda i, j: (0, 0)),
            pl.BlockSpec((1, hd), lambda i, j: (0, 0)),
        ],
        out_specs=[out_spec(hd), out_spec(hd), out_spec(hd), out_spec(hd),
                   out_spec(gw), out_spec(gw)],
        out_shape=[jax.ShapeDtypeStruct((m, POOL_WIDTH), BF16),
                   jax.ShapeDtypeStruct((m, SB_WIDTH), BF16),
                   jax.ShapeDtypeStruct((m, SB_WIDTH), BF16),
                   jax.ShapeDtypeStruct((m, SB_WIDTH), BF16),
                   jax.ShapeDtypeStruct((m, D_MODEL), BF16),
                   jax.ShapeDtypeStruct((m, D_MODEL), BF16)],
        scratch_shapes=[pltpu.VMEM((tm, d), BF16), pltpu.VMEM((X_SLOTS, NORM_ROWS, d), F32),
                        pltpu.SemaphoreType.DMA((X_SLOTS,))],
        compiler_params=_params("arbitrary", "arbitrary"),
        name="in_proj",
    )(x2, mod, norm_w.reshape(1, d), w_in, w_in, w_in, w_in, w_in, w_in,
      q_norm_w.reshape(1, hd), k_norm_w.reshape(1, hd))


FFN_OUT_CHUNK = 512


def _ffn_kernel(x1_hbm, mod_ref, nw_ref, w1_ref, w2_ref, o_ref, h_scr, xbuf, sems, seed_sems,
                *, tm):
    i = pl.program_id(0)
    c = pl.program_id(1)
    b = i * tm // SEQ
    chunks = tm // NORM_ROWS

    def x1_copy(tile, r):
        return pltpu.make_async_copy(
            x1_hbm.at[pl.ds(tile * tm + r * NORM_ROWS, NORM_ROWS), :], xbuf.at[r], sems.at[r])

    @pl.when(jnp.logical_and(i == 0, c == 0))
    def _():
        for r in range(chunks):
            x1_copy(0, r).start()

    @pl.when(c == 0)
    def _():
        gain = nw_ref[...] * (1.0 + _mod_row(mod_ref, SCALE2, b))
        shift = _mod_row(mod_ref, SHIFT2, b)

        def seed_copy(r):
            rows = pl.ds(pl.multiple_of(r * NORM_ROWS, NORM_ROWS), NORM_ROWS)
            return pltpu.make_async_copy(xbuf.at[r], o_ref.at[rows, :], seed_sems.at[r])

        def body(r, carry):
            rows = pl.ds(pl.multiple_of(r * NORM_ROWS, NORM_ROWS), NORM_ROWS)
            x1_copy(i, r).wait()
            seed_copy(r).start()
            h_scr[rows, :] = _norm_modulate(xbuf[r], gain, shift)
            return carry

        lax.fori_loop(0, chunks, body, 0)
        for r in range(chunks):
            seed_copy(r).wait()

    @pl.when(jnp.logical_and(c == pl.num_programs(1) - 1, i + 1 < pl.num_programs(0)))
    def _():
        for r in range(chunks):
            x1_copy(i + 1, r).start()

    a = jnp.dot(h_scr[...], w1_ref[...], preferred_element_type=F32)
    r = jnp.maximum(a, 0.0)
    act = (r * r).astype(BF16)
    gate = _mod_row(mod_ref, GATE2, b)
    for n0 in range(0, o_ref.shape[1], FFN_OUT_CHUNK):
        cols = slice(n0, n0 + FFN_OUT_CHUNK)
        y = jnp.dot(act, w2_ref[:, cols], preferred_element_type=F32)
        o_ref[:, cols] += gate[:, cols] * y


def _ffn(x1, mod, norm_w, w1, w2, *, tm, tf):
    m, d = x1.shape
    f = w1.shape[1]
    assert m % tm == 0 and SEQ % tm == 0 and f % tf == 0 and tm % NORM_ROWS == 0
    assert d % FFN_OUT_CHUNK == 0
    chunks = tm // NORM_ROWS
    return pl.pallas_call(
        functools.partial(_ffn_kernel, tm=tm),
        grid=(m // tm, f // tf),
        in_specs=[
            pl.BlockSpec(memory_space=pl.ANY),
            pl.BlockSpec(mod.shape, lambda i, c: (0, 0, 0)),
            pl.BlockSpec((1, d), lambda i, c: (0, 0)),
            pl.BlockSpec((d, tf), lambda i, c: (0, c)),
            pl.BlockSpec((tf, d), lambda i, c: (c, 0)),
        ],
        out_specs=pl.BlockSpec((tm, d), lambda i, c: (i, 0)),
        out_shape=jax.ShapeDtypeStruct((m, d), F32),
        scratch_shapes=[pltpu.VMEM((tm, d), BF16), pltpu.VMEM((chunks, NORM_ROWS, d), F32),
                        pltpu.SemaphoreType.DMA((chunks,)), pltpu.SemaphoreType.DMA((chunks,))],
        compiler_params=_params("arbitrary", "arbitrary"),
        name="ffn",
    )(x1, mod, norm_w.reshape(1, d), w1, w2)


POOL_ROWS = 256
POOL_HEAD = max(POOL_WINDOWS)
assert POOL_HEAD % (2 * SUBLANES) == 0 and all(w & (w - 1) == 0 for w in POOL_WINDOWS)


def _pool_head_exact(u_head, w):
    pos = lax.broadcasted_iota(jnp.int32, u_head.shape, 0)
    win_sum = u_head
    for k in range(1, w):
        win_sum = win_sum + jnp.where(pos >= k, pltpu.roll(u_head, k, 0), 0.0)
    count = jnp.minimum(pos + 1, w).astype(F32)
    return win_sum / count - u_head


def _pool_kernel(u_ref, w_ref, s_ref, o_ref):
    gd = POOL_GROUP_DIM
    tt = lax.broadcasted_iota(jnp.int32, (POOL_ROWS + POOL_HEAD, POOL_ROWS), 0)
    jj = lax.broadcasted_iota(jnp.int32, (POOL_ROWS + POOL_HEAD, POOL_ROWS), 1)
    for g, w in enumerate(POOL_WINDOWS):
        cols = slice(g * gd, (g + 1) * gd)
        inside = jnp.where(jnp.logical_and(jj <= tt, jj > tt - w), 1.0 / w, 0.0)
        inside = inside - jnp.where(jj == tt, 1.0, 0.0)
        spill = jnp.where(jj > tt - w, 1.0 / w, 0.0)
        band = jnp.where(tt < POOL_ROWS, inside, spill).astype(BF16)
        carry = None
        pooled = []
        for r0 in range(0, SEQ, POOL_ROWS):
            res = jnp.dot(band, u_ref[r0:r0 + POOL_ROWS, cols], preferred_element_type=F32)
            if r0 == 0:
                head = _pool_head_exact(u_ref[0:POOL_HEAD, cols].astype(F32), w)
            else:
                head = res[:POOL_HEAD] + carry
            pooled += [head.astype(BF16), res[POOL_HEAD:POOL_ROWS].astype(BF16)]
            carry = res[POOL_ROWS:]
        mixed = jnp.dot(jnp.concatenate(pooled, axis=0), w_ref[g].astype(BF16),
                        preferred_element_type=F32)
        o_ref[:, cols] = (mixed * s_ref[:, cols]).astype(o_ref.dtype)


def _pool(u, w_pool, pool_scale, batch):
    groups = len(POOL_WINDOWS)
    gd = POOL_GROUP_DIM
    assert SEQ % POOL_ROWS == 0
    return pl.pallas_call(
        _pool_kernel,
        grid=(batch,),
        in_specs=[
            pl.BlockSpec((SEQ, POOL_WIDTH), lambda b: (b, 0)),
            pl.BlockSpec((groups, gd, gd), lambda b: (0, 0, 0)),
            pl.BlockSpec((1, POOL_WIDTH), lambda b: (0, 0)),
        ],
        out_specs=pl.BlockSpec((SEQ, POOL_WIDTH), lambda b: (b, 0)),
        out_shape=jax.ShapeDtypeStruct((batch * SEQ, POOL_WIDTH), BF16),
        compiler_params=_params("arbitrary"),
        name="pool",
    )(u, w_pool, pool_scale.reshape(1, POOL_WIDTH))


EXP2_UNDERFLOW = -151.0


def _attn_kernel(q_ref, k_ref, v_ref, *refs, tb, hp, n_cast):
    w_refs, o_ref, wbf_refs = refs[:n_cast], refs[n_cast], refs[n_cast + 1:2 * n_cast + 1]
    carry_ref, acc_ref = refs[2 * n_cast + 1:]
    for w_ref, wbf_ref in zip(w_refs, wbf_refs):
        wbf_ref[...] = w_ref[...].astype(BF16)

    qi = pl.program_id(2)
    nseg = tb // LANES

    r = lax.broadcasted_iota(jnp.int32, (LANES, 2 * LANES), 0)
    c = lax.broadcasted_iota(jnp.int32, (LANES, 2 * LANES), 1)
    cum_op = jnp.where(jnp.logical_or(c >= LANES, r > c), 1.0, 0.0).astype(BF16)
    tri_r = lax.broadcasted_iota(jnp.int32, (LANES, LANES), 0)
    tri_c = lax.broadcasted_iota(jnp.int32, (LANES, LANES), 1)
    causal = tri_c < tri_r
    segs = [slice(sg * LANES, (sg + 1) * LANES) for sg in range(nseg)]

    def block(kb, diagonal):
        start = pl.multiple_of(kb * tb, tb)
        if diagonal:
            parts = [(segs[g], g + 1) for g in range(nseg)]
        else:
            parts = [(slice(0, tb), nseg)]
        work = [(p, slice(p * HEAD_DIM, (p + 1) * HEAD_DIM), rows, nk)
                for rows, nk in parts for p in range(hp)]
        zs = [lax.dot_general(q_ref[rows, cols], k_ref[pl.ds(start, nk * LANES), cols],
                              (((1,), (1,)), ((), ())), preferred_element_type=F32)
              for _, cols, rows, nk in work]
        log_betas, sums = [], []
        for z, (_, _, _, nk) in zip(zs, work):
            log_beta = jnp.minimum(z, 0.0) - jnp.log2(1.0 + jnp.exp2(-jnp.abs(z)))
            l = log_beta - z
            cs = []
            for sg in range(nk):
                l_seg = l[:, segs[sg]]
                if diagonal and sg == nk - 1:
                    l_seg = jnp.where(causal, l_seg, 0.0)
                cs.append(jnp.dot(l_seg.astype(BF16), cum_op, preferred_element_type=F32))
            log_betas.append(log_beta)
            sums.append(cs)
        top = None
        for (p, cols, rows, nk), log_beta, cs in zip(work, log_betas, sums):
            carry = carry_ref[p, rows, :]
            a_parts = [None] * nk
            for sg in range(nk - 1, -1, -1):
                a_seg = jnp.exp2(log_beta[:, segs[sg]] + (cs[sg][:, :LANES] + carry))
                if diagonal and sg == nk - 1:
                    a_seg = jnp.where(causal, a_seg, 0.0)
                a_parts[sg] = a_seg
                carry = carry + cs[sg][:, LANES:]
            a = jnp.concatenate(a_parts, axis=1) if nk > 1 else a_parts[0]
            acc_ref[p, rows, :] += jnp.dot(
                a.astype(BF16), v_ref[pl.ds(start, nk * LANES), cols],
                preferred_element_type=F32)
            carry_ref[p, rows, :] = carry
            top = jnp.max(carry) if top is None else jnp.maximum(top, jnp.max(carry))
        return top

    carry_ref[...] = jnp.zeros_like(carry_ref)
    acc_ref[...] = jnp.zeros_like(acc_ref)
    top = block(qi, True)

    def cond(state):
        kb, top = state
        return jnp.logical_and(kb >= 0, top > EXP2_UNDERFLOW)

    def body(state):
        kb, _ = state
        return kb - 1, block(kb, False)

    lax.while_loop(cond, body, (qi - 1, top))
    for p in range(hp):
        o_ref[:, p * HEAD_DIM:(p + 1) * HEAD_DIM] = acc_ref[p].astype(o_ref.dtype)


def _attn(q, k, v, weights, batch, *, tb, hp):
    nq = SEQ // tb
    groups = SB_HEADS // hp
    width = hp * HEAD_DIM
    steps = batch * groups * nq
    assert all(w.shape[0] % steps == 0 for w in weights)
    cast_specs = [pl.BlockSpec((w.shape[0] // steps, w.shape[1]),
                               lambda b, h, i: ((b * groups + h) * nq + i, 0))
                  for w in weights]
    outs = pl.pallas_call(
        functools.partial(_attn_kernel, tb=tb, hp=hp, n_cast=len(weights)),
        grid=(batch, groups, nq),
        in_specs=[
            pl.BlockSpec((tb, width), lambda b, h, i: (b * nq + i, h)),
            pl.BlockSpec((SEQ, width), lambda b, h, i: (b, h)),
            pl.BlockSpec((SEQ, width), lambda b, h, i: (b, h)),
        ] + cast_specs,
        out_specs=[pl.BlockSpec((tb, width), lambda b, h, i: (b * nq + i, h))] + cast_specs,
        out_shape=[jax.ShapeDtypeStruct((batch * SEQ, SB_WIDTH), BF16)]
        + [jax.ShapeDtypeStruct(w.shape, BF16) for w in weights],
        scratch_shapes=[pltpu.VMEM((hp, tb, LANES), F32),
                        pltpu.VMEM((hp, tb, HEAD_DIM), F32)],
        compiler_params=_params("arbitrary", "arbitrary", "arbitrary"),
        name="attn",
    )(q, k, v, *weights)
    return outs[0], outs[1:]


MIX_CHUNK = 256


def _mix_kernel(pa_ref, at_ref, sa_ref, sb_ref, x_ref, mod_ref, wa_ref, wb_ref, wo_ref,
                x1_ref, merged_scr):
    d = x_ref.shape[1]
    pa = pa_ref[...]
    at = at_ref[...]
    for n0 in range(0, d, MIX_CHUNK):
        cols = slice(n0, n0 + MIX_CHUNK)
        ya = jnp.dot(pa, wa_ref[:, cols], preferred_element_type=F32)
        yb = jnp.dot(at, wb_ref[:, cols], preferred_element_type=F32)
        merged = sa_ref[:, cols].astype(F32) * ya + sb_ref[:, cols].astype(F32) * yb
        merged_scr[:, cols] = merged.astype(BF16)
    merged = merged_scr[...]
    b = pl.program_id(0) * x_ref.shape[0] // SEQ
    gate = _mod_row(mod_ref, GATE1, b)
    for n0 in range(0, d, MIX_CHUNK):
        cols = slice(n0, n0 + MIX_CHUNK)
        o = jnp.dot(merged, wo_ref[:, cols], preferred_element_type=F32)
        x1_ref[:, cols] = x_ref[:, cols] + gate[:, cols] * o


def _mix(pa, at, sa, sb, x2, mod, wa_bf, wb_bf, wo_bf, *, tm):
    m, d = x2.shape
    kdim = pa.shape[1]
    resident = pl.Buffered(1)
    return pl.pallas_call(
        _mix_kernel,
        grid=(m // tm,),
        in_specs=[
            pl.BlockSpec((tm, kdim), lambda i: (i, 0)),
            pl.BlockSpec((tm, kdim), lambda i: (i, 0)),
            pl.BlockSpec((tm, d), lambda i: (i, 0)),
            pl.BlockSpec((tm, d), lambda i: (i, 0)),
            pl.BlockSpec((tm, d), lambda i: (i, 0)),
            pl.BlockSpec(mod.shape, lambda i: (0, 0, 0)),
            pl.BlockSpec((kdim, d), lambda i: (0, 0), pipeline_mode=resident),
            pl.BlockSpec((kdim, d), lambda i: (0, 0), pipeline_mode=resident),
            pl.BlockSpec((d, d), lambda i: (0, 0), pipeline_mode=resident),
        ],
        out_specs=pl.BlockSpec((tm, d), lambda i: (i, 0)),
        out_shape=jax.ShapeDtypeStruct((m, d), F32),
        scratch_shapes=[pltpu.VMEM((tm, d), BF16)],
        compiler_params=_params("arbitrary"),
        name="mix",
    )(pa, at, sa, sb, x2, mod, wa_bf, wb_bf, wo_bf)


def kernel(x, c, w_ada, b_ada, norm1_w, w_in, q_norm_w, k_norm_w, w_pool, pool_scale,
           w_a_up, w_b_up, w_o, norm2_w, w_ff1, w_ff2):
    batch, seq, d = x.shape
    assert (seq, d) == (SEQ, D_MODEL) and w_ada.shape[0] == 1
    x2 = x.reshape(batch * seq, d)

    mod = _ada(c, w_ada[0], b_ada[0])
    u, q, k, v, sa, sb = _in_proj(x2, mod, norm1_w[0], w_in[0], q_norm_w[0], k_norm_w[0],
                                  tm=2048)
    pa = _pool(u, w_pool[0], pool_scale[0], batch)
    at, (wa_bf, wb_bf, wo_bf, w1_bf, w2_bf) = _attn(
        q, k, v, (w_a_up[0], w_b_up[0], w_o[0], w_ff1[0], w_ff2[0]), batch,
        tb=256, hp=8)
    x1 = _mix(pa, at, sa, sb, x2, mod, wa_bf, wb_bf, wo_bf, tm=256)
    out = _ffn(x1, mod, norm2_w[0], w1_bf, w2_bf, tm=1024, tf=1024)
    return out.reshape(batch, seq, d)
```

```python
import functools
import math

import jax
import jax.numpy as jnp
from jax import lax
from jax.experimental import pallas as pl
from jax.experimental.pallas import tpu as pltpu

D_MODEL = 2048
SEQ = 2048
POOL_WIDTH = D_MODEL // 2
POOL_WINDOWS = (2, 4, 8, 16)
POOL_GROUP_DIM = POOL_WIDTH // len(POOL_WINDOWS)
HEAD_DIM = 128
SB_WIDTH = D_MODEL // 2
SB_HEADS = SB_WIDTH // HEAD_DIM
N_MOD = 6
EPS = 1e-6

Q_OFF = POOL_WIDTH
K_OFF = Q_OFF + SB_WIDTH
V_OFF = K_OFF + SB_WIDTH
GA_OFF = V_OFF + SB_WIDTH
GB_OFF = GA_OFF + D_MODEL

LANES = 128
SUBLANES = 8
V7X_VMEM_BYTES = 64 * 1024 * 1024
VMEM_LIMIT_BYTES = V7X_VMEM_BYTES * 7 // 8

BF16 = jnp.bfloat16
F32 = jnp.float32


def _params(*semantics):
    return pltpu.CompilerParams(dimension_semantics=semantics,
                                vmem_limit_bytes=VMEM_LIMIT_BYTES)


ADA_STREAMS = 4


def _ada_kernel(c_ref, b_ref, *refs):
    w_refs, o_ref = refs[:-1], refs[-1]
    c = c_ref[...]
    batch = c.shape[0]
    sc = c * jax.nn.sigmoid(c)
    if batch % SUBLANES:
        pad = SUBLANES - batch % SUBLANES
        sc = jnp.concatenate([sc, jnp.zeros((pad, sc.shape[1]), F32)], axis=0)
    sc = sc.astype(BF16)
    rows = w_refs[0].shape[0]
    acc = b_ref[...]
    for s, w_ref in enumerate(w_refs):
        acc = acc + jnp.dot(sc[:, s * rows:(s + 1) * rows], w_ref[...].astype(BF16),
                            preferred_element_type=F32)[:batch]
    o_ref[0] = acc


def _ada(c, w_ada, b_ada):
    batch, d = c.shape
    n = w_ada.shape[1]
    tn = 1024
    assert n == N_MOD * d and d % tn == 0 and d % ADA_STREAMS == 0
    slab = d // ADA_STREAMS
    per_chunk = d // tn
    return pl.pallas_call(
        _ada_kernel,
        grid=(n // tn,),
        in_specs=[
            pl.BlockSpec((batch, d), lambda j: (0, 0)),
            pl.BlockSpec((1, tn), lambda j: (0, j)),
        ] + [pl.BlockSpec((slab, tn), functools.partial(lambda j, s: (s, j), s=s))
             for s in range(ADA_STREAMS)],
        out_specs=pl.BlockSpec((1, batch, tn), lambda j: (j // per_chunk, 0, j % per_chunk)),
        out_shape=jax.ShapeDtypeStruct((N_MOD, batch, d), F32),
        compiler_params=_params("arbitrary"),
        name="ada",
    )(c, b_ada.reshape(1, n), *([w_ada] * ADA_STREAMS))


SHIFT1, SCALE1, GATE1, SHIFT2, SCALE2, GATE2 = range(N_MOD)


def _mod_row(mod_ref, k, b):
    return mod_ref[k, pl.ds(b, 1), :]


NORM_ROWS = 256


def _norm_modulate(x, gain, shift):
    y = x * lax.rsqrt(jnp.mean(x * x, axis=-1, keepdims=True) + EPS)
    return (y * gain + shift).astype(BF16)


def _norm_modulate_rows(x_ref, o_ref, norm_w, scale, shift):
    gain = norm_w * (1.0 + scale)

    def body(r, carry):
        rows = pl.ds(pl.multiple_of(r * NORM_ROWS, NORM_ROWS), NORM_ROWS)
        o_ref[rows, :] = _norm_modulate(x_ref[rows, :], gain, shift)
        return carry

    lax.fori_loop(0, x_ref.shape[0] // NORM_ROWS, body, 0)


Q_LOGIT_SCALE = math.log2(math.e) / math.sqrt(HEAD_DIM)


def _sigmoid(x):
    return 0.5 * jnp.tanh(0.5 * x) + 0.5


def _head_norm(blk, gain):
    ms = jnp.mean(blk * blk, axis=-1, keepdims=True)
    return (blk * lax.rsqrt(ms + EPS) * gain).astype(BF16)


IN_PROJ_ROWS = 1024
X_SLOTS = 4


def _in_proj_kernel(x_hbm, mod_ref, nw_ref, wu_ref, wq_ref, wk_ref, wv_ref, wga_ref, wgb_ref,
                    qw_ref, kw_ref, cast_ref, u_ref, q_ref, k_ref, v_ref, ga_ref, gb_ref,
                    cast_out_ref, h_scr, xbuf, sems, *, tm):
    cast_out_ref[...] = cast_ref[...].astype(BF16)
    i = pl.program_id(0)
    j = pl.program_id(1)
    chunks = tm // NORM_ROWS

    def x_copy(tile, r):
        slot = r % X_SLOTS
        return pltpu.make_async_copy(
            x_hbm.at[pl.ds(tile * tm + r * NORM_ROWS, NORM_ROWS), :], xbuf.at[slot],
            sems.at[slot])

    @pl.when(jnp.logical_and(j == 0, i == 0))
    def _():
        for r in range(X_SLOTS):
            x_copy(0, r).start()

    @pl.when(j == 0)
    def _():
        b = i * tm // SEQ
        gain = nw_ref[...] * (1.0 + _mod_row(mod_ref, SCALE1, b))
        shift = _mod_row(mod_ref, SHIFT1, b)

        def body(r, carry):
            x_copy(i, r).wait()
            rows = pl.ds(pl.multiple_of(r * NORM_ROWS, NORM_ROWS), NORM_ROWS)
            h_scr[rows, :] = _norm_modulate(xbuf[r % X_SLOTS], gain, shift)

            @pl.when(r + X_SLOTS < chunks)
            def _():
                x_copy(i, r + X_SLOTS).start()

            return carry

        lax.fori_loop(0, chunks, body, 0)

    @pl.when(jnp.logical_and(j == pl.num_programs(1) - 1, i + 1 < pl.num_programs(0)))
    def _():
        for r in range(X_SLOTS):
            x_copy(i + 1, r).start()

    def bf16_cols(*w_refs):
        return jnp.concatenate([w_ref[...].astype(BF16) for w_ref in w_refs], axis=1)

    w_qk, w_ga, w_gb, w_uv = (bf16_cols(wq_ref, wk_ref), bf16_cols(wga_ref),
                              bf16_cols(wgb_ref), bf16_cols(wu_ref, wv_ref))
    q_gain = qw_ref[...] * Q_LOGIT_SCALE
    k_gain = kw_ref[...]
    for r0 in range(0, tm, IN_PROJ_ROWS):
        rows = slice(r0, r0 + IN_PROJ_ROWS)
        h = h_scr[rows, :]
        qk = jnp.dot(h, w_qk, preferred_element_type=F32)
        q_ref[rows, :] = _head_norm(qk[:, :HEAD_DIM], q_gain)
        k_ref[rows, :] = _head_norm(qk[:, HEAD_DIM:], k_gain)
        ga = jnp.dot(h, w_ga, preferred_element_type=F32)
        ga_ref[rows, :] = _sigmoid(ga).astype(BF16)
        gb = jnp.dot(h, w_gb, preferred_element_type=F32)
        gb_ref[rows, :] = _sigmoid(gb).astype(BF16)
        uv = jnp.dot(h, w_uv, preferred_element_type=F32)
        u_ref[rows, :] = uv[:, :HEAD_DIM].astype(BF16)
        v_ref[rows, :] = uv[:, HEAD_DIM:].astype(BF16)


def _in_proj(x2, mod, norm_w, w_in, q_norm_w, k_norm_w, w_cast, *, tm):
    m, d = x2.shape
    steps = (m // tm) * SB_HEADS
    assert w_cast.shape[0] % steps == 0
    cast_spec = pl.BlockSpec((w_cast.shape[0] // steps, w_cast.shape[1]),
                             lambda i, j: (i * SB_HEADS + j, 0))
    hd = HEAD_DIM
    gw = D_MODEL // SB_HEADS
    assert POOL_WIDTH == SB_HEADS * hd and D_MODEL == SB_HEADS * gw
    assert m % tm == 0 and SEQ % tm == 0 and tm % IN_PROJ_ROWS == 0
    assert tm % NORM_ROWS == 0 and tm // NORM_ROWS >= X_SLOTS
    w_spec = lambda width, off: pl.BlockSpec((d, width), lambda i, j: (0, off // width + j))
    out_spec = lambda width: pl.BlockSpec((tm, width), lambda i, j: (i, j))
    return pl.pallas_call(
        functools.partial(_in_proj_kernel, tm=tm),
        grid=(m // tm, SB_HEADS),
        in_specs=[
            pl.BlockSpec(memory_space=pl.ANY),
            pl.BlockSpec(mod.shape, lambda i, j: (0, 0, 0)),
            pl.BlockSpec((1, d), lambda i, j: (0, 0)),
            w_spec(hd, 0), w_spec(hd, Q_OFF), w_spec(hd, K_OFF), w_spec(hd, V_OFF),
            w_spec(gw, GA_OFF), w_spec(gw, GB_OFF),
            pl.BlockSpec((1, hd), lambda i, j: (0, 0)),
            pl.BlockSpec((1, hd), lambda i, j: (0, 0)),
            cast_spec,
        ],
        out_specs=[out_spec(hd), out_spec(hd), out_spec(hd), out_spec(hd),
                   out_spec(gw), out_spec(gw), cast_spec],
        out_shape=[jax.ShapeDtypeStruct((m, POOL_WIDTH), BF16),
                   jax.ShapeDtypeStruct((m, SB_WIDTH), BF16),
                   jax.ShapeDtypeStruct((m, SB_WIDTH), BF16),
                   jax.ShapeDtypeStruct((m, SB_WIDTH), BF16),
                   jax.ShapeDtypeStruct((m, D_MODEL), BF16),
                   jax.ShapeDtypeStruct((m, D_MODEL), BF16),
                   jax.ShapeDtypeStruct(w_cast.shape, BF16)],
        scratch_shapes=[pltpu.VMEM((tm, d), BF16), pltpu.VMEM((X_SLOTS, NORM_ROWS, d), F32),
                        pltpu.SemaphoreType.DMA((X_SLOTS,))],
        compiler_params=_params("arbitrary", "arbitrary"),
        name="in_proj",
    )(x2, mod, norm_w.reshape(1, d), w_in, w_in, w_in, w_in, w_in, w_in,
      q_norm_w.reshape(1, hd), k_norm_w.reshape(1, hd), w_cast)


FFN_OUT_CHUNK = 512


def _ffn_kernel(h_ref, x1_hbm, mod_ref, w1_ref, w2_ref, o_ref, xbuf, sem, *, tm, xrows):
    i = pl.program_id(0)
    c = pl.program_id(1)
    x1_copy = pltpu.make_async_copy(x1_hbm.at[pl.ds(i * tm + c * xrows, xrows), :], xbuf, sem)
    x1_copy.start()

    @pl.when(c == 0)
    def _():
        o_ref[...] = jnp.zeros_like(o_ref)

    a = jnp.dot(h_ref[...], w1_ref[...], preferred_element_type=F32)
    r = jnp.maximum(a, 0.0)
    act = (r * r).astype(BF16)
    gate = _mod_row(mod_ref, GATE2, i * tm // SEQ)
    for n0 in range(0, o_ref.shape[1], FFN_OUT_CHUNK):
        cols = slice(n0, n0 + FFN_OUT_CHUNK)
        y = jnp.dot(act, w2_ref[:, cols], preferred_element_type=F32)
        o_ref[:, cols] += gate[:, cols] * y
    x1_copy.wait()
    rows = pl.ds(pl.multiple_of(c * xrows, xrows), xrows)
    o_ref[rows, :] += xbuf[...]


def _ffn(h2, x1, mod, w1, w2, *, tm, tf):
    m, d = h2.shape
    f = w1.shape[1]
    assert m % tm == 0 and SEQ % tm == 0 and f % tf == 0 and tm % (f // tf) == 0
    assert d % FFN_OUT_CHUNK == 0
    xrows = tm // (f // tf)
    return pl.pallas_call(
        functools.partial(_ffn_kernel, tm=tm, xrows=xrows),
        grid=(m // tm, f // tf),
        in_specs=[
            pl.BlockSpec((tm, d), lambda i, c: (i, 0)),
            pl.BlockSpec(memory_space=pl.ANY),
            pl.BlockSpec(mod.shape, lambda i, c: (0, 0, 0)),
            pl.BlockSpec((d, tf), lambda i, c: (0, c)),
            pl.BlockSpec((tf, d), lambda i, c: (c, 0)),
        ],
        out_specs=pl.BlockSpec((tm, d), lambda i, c: (i, 0)),
        out_shape=jax.ShapeDtypeStruct((m, d), F32),
        scratch_shapes=[pltpu.VMEM((xrows, d), F32), pltpu.SemaphoreType.DMA(())],
        compiler_params=_params("arbitrary", "arbitrary"),
        name="ffn",
    )(h2, x1, mod, w1, w2)


POOL_ROWS = 256
POOL_HEAD = max(POOL_WINDOWS)
assert POOL_HEAD % (2 * SUBLANES) == 0 and all(w & (w - 1) == 0 for w in POOL_WINDOWS)


def _pool_head_exact(u_head, w):
    pos = lax.broadcasted_iota(jnp.int32, u_head.shape, 0)
    win_sum = u_head
    for k in range(1, w):
        win_sum = win_sum + jnp.where(pos >= k, pltpu.roll(u_head, k, 0), 0.0)
    count = jnp.minimum(pos + 1, w).astype(F32)
    return win_sum / count - u_head


def _pool_kernel(u_ref, w_ref, s_ref, o_ref):
    gd = POOL_GROUP_DIM
    tt = lax.broadcasted_iota(jnp.int32, (POOL_ROWS + POOL_HEAD, POOL_ROWS), 0)
    jj = lax.broadcasted_iota(jnp.int32, (POOL_ROWS + POOL_HEAD, POOL_ROWS), 1)
    for g, w in enumerate(POOL_WINDOWS):
        cols = slice(g * gd, (g + 1) * gd)
        inside = jnp.where(jnp.logical_and(jj <= tt, jj > tt - w), 1.0 / w, 0.0)
        inside = inside - jnp.where(jj == tt, 1.0, 0.0)
        spill = jnp.where(jj > tt - w, 1.0 / w, 0.0)
        band = jnp.where(tt < POOL_ROWS, inside, spill).astype(BF16)
        carry = None
        pooled = []
        for r0 in range(0, SEQ, POOL_ROWS):
            res = jnp.dot(band, u_ref[r0:r0 + POOL_ROWS, cols], preferred_element_type=F32)
            if r0 == 0:
                head = _pool_head_exact(u_ref[0:POOL_HEAD, cols].astype(F32), w)
            else:
                head = res[:POOL_HEAD] + carry
            pooled += [head.astype(BF16), res[POOL_HEAD:POOL_ROWS].astype(BF16)]
            carry = res[POOL_ROWS:]
        mixed = jnp.dot(jnp.concatenate(pooled, axis=0), w_ref[g].astype(BF16),
                        preferred_element_type=F32)
        o_ref[:, cols] = (mixed * s_ref[:, cols]).astype(o_ref.dtype)


def _pool(u, w_pool, pool_scale, batch):
    groups = len(POOL_WINDOWS)
    gd = POOL_GROUP_DIM
    assert SEQ % POOL_ROWS == 0
    return pl.pallas_call(
        _pool_kernel,
        grid=(batch,),
        in_specs=[
            pl.BlockSpec((SEQ, POOL_WIDTH), lambda b: (b, 0)),
            pl.BlockSpec((groups, gd, gd), lambda b: (0, 0, 0)),
            pl.BlockSpec((1, POOL_WIDTH), lambda b: (0, 0)),
        ],
        out_specs=pl.BlockSpec((SEQ, POOL_WIDTH), lambda b: (b, 0)),
        out_shape=jax.ShapeDtypeStruct((batch * SEQ, POOL_WIDTH), BF16),
        compiler_params=_params("arbitrary"),
        name="pool",
    )(u, w_pool, pool_scale.reshape(1, POOL_WIDTH))


EXP2_UNDERFLOW = -151.0


def _attn_kernel(q_ref, k_ref, v_ref, *refs, tb, hp, n_cast):
    w_refs, o_ref, wbf_refs = refs[:n_cast], refs[n_cast], refs[n_cast + 1:2 * n_cast + 1]
    carry_ref, acc_ref = refs[2 * n_cast + 1:]
    for w_ref, wbf_ref in zip(w_refs, wbf_refs):
        wbf_ref[...] = w_ref[...].astype(BF16)

    qi = pl.program_id(2)
    nseg = tb // LANES

    r = lax.broadcasted_iota(jnp.int32, (LANES, 2 * LANES), 0)
    c = lax.broadcasted_iota(jnp.int32, (LANES, 2 * LANES), 1)
    cum_op = jnp.where(jnp.logical_or(c >= LANES, r > c), 1.0, 0.0).astype(BF16)
    tri_r = lax.broadcasted_iota(jnp.int32, (LANES, LANES), 0)
    tri_c = lax.broadcasted_iota(jnp.int32, (LANES, LANES), 1)
    causal = tri_c < tri_r
    segs = [slice(sg * LANES, (sg + 1) * LANES) for sg in range(nseg)]

    def block(kb, diagonal):
        start = pl.multiple_of(kb * tb, tb)
        if diagonal:
            parts = [(segs[g], g + 1) for g in range(nseg)]
        else:
            parts = [(slice(0, tb), nseg)]
        work = [(p, slice(p * HEAD_DIM, (p + 1) * HEAD_DIM), rows, nk)
                for rows, nk in parts for p in range(hp)]
        zs = [lax.dot_general(q_ref[rows, cols], k_ref[pl.ds(start, nk * LANES), cols],
                              (((1,), (1,)), ((), ())), preferred_element_type=F32)
              for _, cols, rows, nk in work]
        log_betas, sums = [], []
        for z, (_, _, _, nk) in zip(zs, work):
            log_beta = jnp.minimum(z, 0.0) - jnp.log2(1.0 + jnp.exp2(-jnp.abs(z)))
            l = log_beta - z
            cs = []
            for sg in range(nk):
                l_seg = l[:, segs[sg]]
                if diagonal and sg == nk - 1:
                    l_seg = jnp.where(causal, l_seg, 0.0)
                cs.append(jnp.dot(l_seg.astype(BF16), cum_op, preferred_element_type=F32))
            log_betas.append(log_beta)
            sums.append(cs)
        top = None
        for (p, cols, rows, nk), log_beta, cs in zip(work, log_betas, sums):
            carry = carry_ref[p, rows, :]
            a_parts = [None] * nk
            for sg in range(nk - 1, -1, -1):
                a_seg = jnp.exp2(log_beta[:, segs[sg]] + (cs[sg][:, :LANES] + carry))
                if diagonal and sg == nk - 1:
                    a_seg = jnp.where(causal, a_seg, 0.0)
                a_parts[sg] = a_seg
                carry = carry + cs[sg][:, LANES:]
            a = jnp.concatenate(a_parts, axis=1) if nk > 1 else a_parts[0]
            acc_ref[p, rows, :] += jnp.dot(
                a.astype(BF16), v_ref[pl.ds(start, nk * LANES), cols],
                preferred_element_type=F32)
            carry_ref[p, rows, :] = carry
            top = jnp.max(carry) if top is None else jnp.maximum(top, jnp.max(carry))
        return top

    carry_ref[...] = jnp.zeros_like(carry_ref)
    acc_ref[...] = jnp.zeros_like(acc_ref)
    top = block(qi, True)

    def cond(state):
        kb, top = state
        return jnp.logical_and(kb >= 0, top > EXP2_UNDERFLOW)

    def body(state):
        kb, _ = state
        return kb - 1, block(kb, False)

    lax.while_loop(cond, body, (qi - 1, top))
    for p in range(hp):
        o_ref[:, p * HEAD_DIM:(p + 1) * HEAD_DIM] = acc_ref[p].astype(o_ref.dtype)


def _attn(q, k, v, weights, batch, *, tb, hp):
    nq = SEQ // tb
    groups = SB_HEADS // hp
    width = hp * HEAD_DIM
    steps = batch * groups * nq
    assert all(w.shape[0] % steps == 0 for w in weights)
    cast_specs = [pl.BlockSpec((w.shape[0] // steps, w.shape[1]),
                               lambda b, h, i: ((b * groups + h) * nq + i, 0))
                  for w in weights]
    outs = pl.pallas_call(
        functools.partial(_attn_kernel, tb=tb, hp=hp, n_cast=len(weights)),
        grid=(batch, groups, nq),
        in_specs=[
            pl.BlockSpec((tb, width), lambda b, h, i: (b * nq + i, h)),
            pl.BlockSpec((SEQ, width), lambda b, h, i: (b, h)),
            pl.BlockSpec((SEQ, width), lambda b, h, i: (b, h)),
        ] + cast_specs,
        out_specs=[pl.BlockSpec((tb, width), lambda b, h, i: (b * nq + i, h))] + cast_specs,
        out_shape=[jax.ShapeDtypeStruct((batch * SEQ, SB_WIDTH), BF16)]
        + [jax.ShapeDtypeStruct(w.shape, BF16) for w in weights],
        scratch_shapes=[pltpu.VMEM((hp, tb, LANES), F32),
                        pltpu.VMEM((hp, tb, HEAD_DIM), F32)],
        compiler_params=_params("arbitrary", "arbitrary", "arbitrary"),
        name="attn",
    )(q, k, v, *weights)
    return outs[0], outs[1:]


MIX_CHUNK = 256


def _mix_kernel(pa_ref, at_ref, sa_ref, sb_ref, x_ref, mod_ref, nw_ref, wa_ref, wb_ref, wo_ref,
                x1_ref, h2_ref, merged_scr):
    d = x_ref.shape[1]
    pa = pa_ref[...]
    at = at_ref[...]
    for n0 in range(0, d, MIX_CHUNK):
        cols = slice(n0, n0 + MIX_CHUNK)
        ya = jnp.dot(pa, wa_ref[:, cols], preferred_element_type=F32)
        yb = jnp.dot(at, wb_ref[:, cols], preferred_element_type=F32)
        merged = sa_ref[:, cols].astype(F32) * ya + sb_ref[:, cols].astype(F32) * yb
        merged_scr[:, cols] = merged.astype(BF16)
    merged = merged_scr[...]
    b = pl.program_id(0) * x_ref.shape[0] // SEQ
    gate = _mod_row(mod_ref, GATE1, b)
    for n0 in range(0, d, MIX_CHUNK):
        cols = slice(n0, n0 + MIX_CHUNK)
        o = jnp.dot(merged, wo_ref[:, cols], preferred_element_type=F32)
        x1_ref[:, cols] = x_ref[:, cols] + gate[:, cols] * o
    _norm_modulate_rows(x1_ref, h2_ref, nw_ref[...], _mod_row(mod_ref, SCALE2, b),
                        _mod_row(mod_ref, SHIFT2, b))


def _mix(pa, at, sa, sb, x2, mod, norm_w, wa_bf, wb_bf, wo_bf, *, tm):
    m, d = x2.shape
    kdim = pa.shape[1]
    resident = pl.Buffered(1)
    return pl.pallas_call(
        _mix_kernel,
        grid=(m // tm,),
        in_specs=[
            pl.BlockSpec((tm, kdim), lambda i: (i, 0)),
            pl.BlockSpec((tm, kdim), lambda i: (i, 0)),
            pl.BlockSpec((tm, d), lambda i: (i, 0)),
            pl.BlockSpec((tm, d), lambda i: (i, 0)),
            pl.BlockSpec((tm, d), lambda i: (i, 0)),
            pl.BlockSpec(mod.shape, lambda i: (0, 0, 0)),
            pl.BlockSpec((1, d), lambda i: (0, 0)),
            pl.BlockSpec((kdim, d), lambda i: (0, 0), pipeline_mode=resident),
            pl.BlockSpec((kdim, d), lambda i: (0, 0), pipeline_mode=resident),
            pl.BlockSpec((d, d), lambda i: (0, 0), pipeline_mode=resident),
        ],
        out_specs=[pl.BlockSpec((tm, d), lambda i: (i, 0)),
                   pl.BlockSpec((tm, d), lambda i: (i, 0))],
        out_shape=[jax.ShapeDtypeStruct((m, d), F32),
                   jax.ShapeDtypeStruct((m, d), BF16)],
        scratch_shapes=[pltpu.VMEM((tm, d), BF16)],
        compiler_params=_params("arbitrary"),
        name="mix",
    )(pa, at, sa, sb, x2, mod, norm_w.reshape(1, d), wa_bf, wb_bf, wo_bf)


def kernel(x, c, w_ada, b_ada, norm1_w, w_in, q_norm_w, k_norm_w, w_pool, pool_scale,
           w_a_up, w_b_up, w_o, norm2_w, w_ff1, w_ff2):
    batch, seq, d = x.shape
    assert (seq, d) == (SEQ, D_MODEL) and w_ada.shape[0] == 1
    x2 = x.reshape(batch * seq, d)

    mod = _ada(c, w_ada[0], b_ada[0])
    u, q, k, v, sa, sb, w1_bf = _in_proj(x2, mod, norm1_w[0], w_in[0], q_norm_w[0],
                                         k_norm_w[0], w_ff1[0], tm=2048)
    pa = _pool(u, w_pool[0], pool_scale[0], batch)
    at, (wa_bf, wb_bf, wo_bf, w2_bf) = _attn(
        q, k, v, (w_a_up[0], w_b_up[0], w_o[0], w_ff2[0]), batch,
        tb=256, hp=8)
    x1, h2 = _mix(pa, at, sa, sb, x2, mod, norm2_w[0], wa_bf, wb_bf, wo_bf, tm=256)
    out = _ffn(h2, x1, mod, w1_bf, w2_bf, tm=1024, tf=1024)
    return out.reshape(batch, seq, d)
```

```python
import functools
import math

import jax
import jax.numpy as jnp
from jax import lax
from jax.experimental import pallas as pl
from jax.experimental.pallas import tpu as pltpu

D_MODEL = 2048
SEQ = 2048
POOL_WIDTH = D_MODEL // 2
POOL_WINDOWS = (2, 4, 8, 16)
POOL_GROUP_DIM = POOL_WIDTH // len(POOL_WINDOWS)
HEAD_DIM = 128
SB_WIDTH = D_MODEL // 2
SB_HEADS = SB_WIDTH // HEAD_DIM
N_MOD = 6
EPS = 1e-6

Q_OFF = POOL_WIDTH
K_OFF = Q_OFF + SB_WIDTH
V_OFF = K_OFF + SB_WIDTH
GA_OFF = V_OFF + SB_WIDTH
GB_OFF = GA_OFF + D_MODEL

LANES = 128
SUBLANES = 8
V7X_VMEM_BYTES = 64 * 1024 * 1024
VMEM_LIMIT_BYTES = V7X_VMEM_BYTES * 7 // 8

BF16 = jnp.bfloat16
F32 = jnp.float32


def _params(*semantics):
    return pltpu.CompilerParams(dimension_semantics=semantics,
                                vmem_limit_bytes=VMEM_LIMIT_BYTES)


ADA_STREAMS = 4


def _ada_kernel(c_ref, b_ref, *refs):
    w_refs, o_ref = refs[:-1], refs[-1]
    c = c_ref[...]
    batch = c.shape[0]
    sc = c * jax.nn.sigmoid(c)
    if batch % SUBLANES:
        pad = SUBLANES - batch % SUBLANES
        sc = jnp.concatenate([sc, jnp.zeros((pad, sc.shape[1]), F32)], axis=0)
    sc = sc.astype(BF16)
    rows = w_refs[0].shape[0]
    acc = b_ref[...]
    for s, w_ref in enumerate(w_refs):
        acc = acc + jnp.dot(sc[:, s * rows:(s + 1) * rows], w_ref[...].astype(BF16),
                            preferred_element_type=F32)[:batch]
    o_ref[0] = acc


def _ada(c, w_ada, b_ada):
    batch, d = c.shape
    n = w_ada.shape[1]
    tn = 1024
    assert n == N_MOD * d and d % tn == 0 and d % ADA_STREAMS == 0
    slab = d // ADA_STREAMS
    per_chunk = d // tn
    return pl.pallas_call(
        _ada_kernel,
        grid=(n // tn,),
        in_specs=[
            pl.BlockSpec((batch, d), lambda j: (0, 0)),
            pl.BlockSpec((1, tn), lambda j: (0, j)),
        ] + [pl.BlockSpec((slab, tn), functools.partial(lambda j, s: (s, j), s=s))
             for s in range(ADA_STREAMS)],
        out_specs=pl.BlockSpec((1, batch, tn), lambda j: (j // per_chunk, 0, j % per_chunk)),
        out_shape=jax.ShapeDtypeStruct((N_MOD, batch, d), F32),
        compiler_params=_params("arbitrary"),
        name="ada",
    )(c, b_ada.reshape(1, n), *([w_ada] * ADA_STREAMS))


SHIFT1, SCALE1, GATE1, SHIFT2, SCALE2, GATE2 = range(N_MOD)


def _mod_row(mod_ref, k, b):
    return mod_ref[k, pl.ds(b, 1), :]


NORM_ROWS = 256


def _norm_modulate(x, gain, shift):
    y = x * lax.rsqrt(jnp.mean(x * x, axis=-1, keepdims=True) + EPS)
    return (y * gain + shift).astype(BF16)


def _norm_modulate_rows(x_ref, o_ref, norm_w, scale, shift):
    gain = norm_w * (1.0 + scale)

    def body(r, carry):
        rows = pl.ds(pl.multiple_of(r * NORM_ROWS, NORM_ROWS), NORM_ROWS)
        o_ref[rows, :] = _norm_modulate(x_ref[rows, :], gain, shift)
        return carry

    lax.fori_loop(0, x_ref.shape[0] // NORM_ROWS, body, 0)


Q_LOGIT_SCALE = math.log2(math.e) / math.sqrt(HEAD_DIM)


def _sigmoid(x):
    return 0.5 * jnp.tanh(0.5 * x) + 0.5


def _head_norm(blk, gain):
    ms = jnp.mean(blk * blk, axis=-1, keepdims=True)
    return (blk * lax.rsqrt(ms + EPS) * gain).astype(BF16)


IN_PROJ_ROWS = 1024
X_SLOTS = 4


def _in_proj_kernel(x_hbm, mod_ref, nw_ref, wu_ref, wq_ref, wk_ref, wv_ref, wga_ref, wgb_ref,
                    qw_ref, kw_ref, cast_ref, u_ref, q_ref, k_ref, v_ref, ga_ref, gb_ref,
                    cast_out_ref, h_scr, xbuf, sems, *, tm):
    cast_out_ref[...] = cast_ref[...].astype(BF16)
    i = pl.program_id(0)
    j = pl.program_id(1)
    chunks = tm // NORM_ROWS

    def x_copy(tile, r):
        slot = r % X_SLOTS
        return pltpu.make_async_copy(
            x_hbm.at[pl.ds(tile * tm + r * NORM_ROWS, NORM_ROWS), :], xbuf.at[slot],
            sems.at[slot])

    @pl.when(jnp.logical_and(j == 0, i == 0))
    def _():
        for r in range(X_SLOTS):
            x_copy(0, r).start()

    @pl.when(j == 0)
    def _():
        b = i * tm // SEQ
        gain = nw_ref[...] * (1.0 + _mod_row(mod_ref, SCALE1, b))
        shift = _mod_row(mod_ref, SHIFT1, b)

        def body(r, carry):
            x_copy(i, r).wait()
            rows = pl.ds(pl.multiple_of(r * NORM_ROWS, NORM_ROWS), NORM_ROWS)
            h_scr[rows, :] = _norm_modulate(xbuf[r % X_SLOTS], gain, shift)

            @pl.when(r + X_SLOTS < chunks)
            def _():
                x_copy(i, r + X_SLOTS).start()

            return carry

        lax.fori_loop(0, chunks, body, 0)

    @pl.when(jnp.logical_and(j == pl.num_programs(1) - 1, i + 1 < pl.num_programs(0)))
    def _():
        for r in range(X_SLOTS):
            x_copy(i + 1, r).start()

    def bf16_cols(*w_refs):
        return jnp.concatenate([w_ref[...].astype(BF16) for w_ref in w_refs], axis=1)

    w_qk, w_ga, w_gb, w_uv = (bf16_cols(wq_ref, wk_ref), bf16_cols(wga_ref),
                              bf16_cols(wgb_ref), bf16_cols(wu_ref, wv_ref))
    q_gain = qw_ref[...] * Q_LOGIT_SCALE
    k_gain = kw_ref[...]
    for r0 in range(0, tm, IN_PROJ_ROWS):
        rows = slice(r0, r0 + IN_PROJ_ROWS)
        h = h_scr[rows, :]
        qk = jnp.dot(h, w_qk, preferred_element_type=F32)
        q_ref[rows, :] = _head_norm(qk[:, :HEAD_DIM], q_gain)
        k_ref[rows, :] = _head_norm(qk[:, HEAD_DIM:], k_gain)
        ga = jnp.dot(h, w_ga, preferred_element_type=F32)
        ga_ref[rows, :] = _sigmoid(ga).astype(BF16)
        gb = jnp.dot(h, w_gb, preferred_element_type=F32)
        gb_ref[rows, :] = _sigmoid(gb).astype(BF16)
        uv = jnp.dot(h, w_uv, preferred_element_type=F32)
        u_ref[rows, :] = uv[:, :HEAD_DIM].astype(BF16)
        v_ref[rows, :] = uv[:, HEAD_DIM:].astype(BF16)


def _in_proj(x2, mod, norm_w, w_in, q_norm_w, k_norm_w, w_cast, *, tm):
    m, d = x2.shape
    steps = (m // tm) * SB_HEADS
    assert w_cast.shape[0] % steps == 0
    cast_spec = pl.BlockSpec((w_cast.shape[0] // steps, w_cast.shape[1]),
                             lambda i, j: (i * SB_HEADS + j, 0))
    hd = HEAD_DIM
    gw = D_MODEL // SB_HEADS
    assert POOL_WIDTH == SB_HEADS * hd and D_MODEL == SB_HEADS * gw
    assert m % tm == 0 and SEQ % tm == 0 and tm % IN_PROJ_ROWS == 0
    assert tm % NORM_ROWS == 0 and tm // NORM_ROWS >= X_SLOTS
    w_spec = lambda width, off: pl.BlockSpec((d, width), lambda i, j: (0, off // width + j))
    out_spec = lambda width: pl.BlockSpec((tm, width), lambda i, j: (i, j))
    return pl.pallas_call(
        functools.partial(_in_proj_kernel, tm=tm),
        grid=(m // tm, SB_HEADS),
        in_specs=[
            pl.BlockSpec(memory_space=pl.ANY),
            pl.BlockSpec(mod.shape, lambda i, j: (0, 0, 0)),
            pl.BlockSpec((1, d), lambda i, j: (0, 0)),
            w_spec(hd, 0), w_spec(hd, Q_OFF), w_spec(hd, K_OFF), w_spec(hd, V_OFF),
            w_spec(gw, GA_OFF), w_spec(gw, GB_OFF),
            pl.BlockSpec((1, hd), lambda i, j: (0, 0)),
            pl.BlockSpec((1, hd), lambda i, j: (0, 0)),
            cast_spec,
        ],
        out_specs=[out_spec(hd), out_spec(hd), out_spec(hd), out_spec(hd),
                   out_spec(gw), out_spec(gw), cast_spec],
        out_shape=[jax.ShapeDtypeStruct((m, POOL_WIDTH), BF16),
                   jax.ShapeDtypeStruct((m, SB_WIDTH), BF16),
                   jax.ShapeDtypeStruct((m, SB_WIDTH), BF16),
                   jax.ShapeDtypeStruct((m, SB_WIDTH), BF16),
                   jax.ShapeDtypeStruct((m, D_MODEL), BF16),
                   jax.ShapeDtypeStruct((m, D_MODEL), BF16),
                   jax.ShapeDtypeStruct(w_cast.shape, BF16)],
        scratch_shapes=[pltpu.VMEM((tm, d), BF16), pltpu.VMEM((X_SLOTS, NORM_ROWS, d), F32),
                        pltpu.SemaphoreType.DMA((X_SLOTS,))],
        compiler_params=_params("arbitrary", "arbitrary"),
        name="in_proj",
    )(x2, mod, norm_w.reshape(1, d), w_in, w_in, w_in, w_in, w_in, w_in,
      q_norm_w.reshape(1, hd), k_norm_w.reshape(1, hd), w_cast)


FFN_OUT_CHUNK = 512


def _ffn_kernel(h_ref, x1_hbm, mod_ref, w1_ref, w2_ref, o_ref, xbuf, sem, *, tm, xrows):
    i = pl.program_id(0)
    c = pl.program_id(1)
    x1_copy = pltpu.make_async_copy(x1_hbm.at[pl.ds(i * tm + c * xrows, xrows), :], xbuf, sem)
    x1_copy.start()

    @pl.when(c == 0)
    def _():
        o_ref[...] = jnp.zeros_like(o_ref)

    a = jnp.dot(h_ref[...], w1_ref[...], preferred_element_type=F32)
    r = jnp.maximum(a, 0.0)
    act = (r * r).astype(BF16)
    gate = _mod_row(mod_ref, GATE2, i * tm // SEQ)
    for n0 in range(0, o_ref.shape[1], FFN_OUT_CHUNK):
        cols = slice(n0, n0 + FFN_OUT_CHUNK)
        y = jnp.dot(act, w2_ref[:, cols], preferred_element_type=F32)
        o_ref[:, cols] += gate[:, cols] * y
    x1_copy.wait()
    rows = pl.ds(pl.multiple_of(c * xrows, xrows), xrows)
    o_ref[rows, :] += xbuf[...]


def _ffn(h2, x1, mod, w1, w2, *, tm, tf):
    m, d = h2.shape
    f = w1.shape[1]
    assert m % tm == 0 and SEQ % tm == 0 and f % tf == 0 and tm % (f // tf) == 0
    assert d % FFN_OUT_CHUNK == 0
    xrows = tm // (f // tf)
    return pl.pallas_call(
        functools.partial(_ffn_kernel, tm=tm, xrows=xrows),
        grid=(m // tm, f // tf),
        in_specs=[
            pl.BlockSpec((tm, d), lambda i, c: (i, 0)),
            pl.BlockSpec(memory_space=pl.ANY),
            pl.BlockSpec(mod.shape, lambda i, c: (0, 0, 0)),
            pl.BlockSpec((d, tf), lambda i, c: (0, c)),
            pl.BlockSpec((tf, d), lambda i, c: (c, 0)),
        ],
        out_specs=pl.BlockSpec((tm, d), lambda i, c: (i, 0)),
        out_shape=jax.ShapeDtypeStruct((m, d), F32),
        scratch_shapes=[pltpu.VMEM((xrows, d), F32), pltpu.SemaphoreType.DMA(())],
        compiler_params=_params("arbitrary", "arbitrary"),
        name="ffn",
    )(h2, x1, mod, w1, w2)


POOL_ROWS = 256
POOL_HEAD = max(POOL_WINDOWS)
assert POOL_HEAD % (2 * SUBLANES) == 0 and all(w & (w - 1) == 0 for w in POOL_WINDOWS)


def _pool_head_exact(u_head, w):
    pos = lax.broadcasted_iota(jnp.int32, u_head.shape, 0)
    win_sum = u_head
    for k in range(1, w):
        win_sum = win_sum + jnp.where(pos >= k, pltpu.roll(u_head, k, 0), 0.0)
    count = jnp.minimum(pos + 1, w).astype(F32)
    return win_sum / count - u_head


def _pool_kernel(u_ref, w_ref, s_ref, o_ref):
    gd = POOL_GROUP_DIM
    tt = lax.broadcasted_iota(jnp.int32, (POOL_ROWS + POOL_HEAD, POOL_ROWS), 0)
    jj = lax.broadcasted_iota(jnp.int32, (POOL_ROWS + POOL_HEAD, POOL_ROWS), 1)
    for g, w in enumerate(POOL_WINDOWS):
        cols = slice(g * gd, (g + 1) * gd)
        inside = jnp.where(jnp.logical_and(jj <= tt, jj > tt - w), 1.0 / w, 0.0)
        inside = inside - jnp.where(jj == tt, 1.0, 0.0)
        spill = jnp.where(jj > tt - w, 1.0 / w, 0.0)
        band = jnp.where(tt < POOL_ROWS, inside, spill).astype(BF16)
        carry = None
        pooled = []
        for r0 in range(0, SEQ, POOL_ROWS):
            res = jnp.dot(band, u_ref[r0:r0 + POOL_ROWS, cols], preferred_element_type=F32)
            if r0 == 0:
                head = _pool_head_exact(u_ref[0:POOL_HEAD, cols].astype(F32), w)
            else:
                head = res[:POOL_HEAD] + carry
            pooled += [head.astype(BF16), res[POOL_HEAD:POOL_ROWS].astype(BF16)]
            carry = res[POOL_ROWS:]
        mixed = jnp.dot(jnp.concatenate(pooled, axis=0), w_ref[g].astype(BF16),
                        preferred_element_type=F32)
        o_ref[:, cols] = (mixed * s_ref[:, cols]).astype(o_ref.dtype)


def _pool(u, w_pool, pool_scale, batch):
    groups = len(POOL_WINDOWS)
    gd = POOL_GROUP_DIM
    assert SEQ % POOL_ROWS == 0
    return pl.pallas_call(
        _pool_kernel,
        grid=(batch,),
        in_specs=[
            pl.BlockSpec((SEQ, POOL_WIDTH), lambda b: (b, 0)),
            pl.BlockSpec((groups, gd, gd), lambda b: (0, 0, 0)),
            pl.BlockSpec((1, POOL_WIDTH), lambda b: (0, 0)),
        ],
        out_specs=pl.BlockSpec((SEQ, POOL_WIDTH), lambda b: (b, 0)),
        out_shape=jax.ShapeDtypeStruct((batch * SEQ, POOL_WIDTH), BF16),
        compiler_params=_params("arbitrary"),
        name="pool",
    )(u, w_pool, pool_scale.reshape(1, POOL_WIDTH))


EXP2_UNDERFLOW = -151.0


def _attn_kernel(q_ref, k_ref, v_ref, *refs, tb, hp, n_cast):
    w_refs, o_ref, wbf_refs = refs[:n_cast], refs[n_cast], refs[n_cast + 1:2 * n_cast + 1]
    carry_ref, acc_ref = refs[2 * n_cast + 1:]
    for w_ref, wbf_ref in zip(w_refs, wbf_refs):
        wbf_ref[...] = w_ref[...].astype(BF16)

    qi = pl.program_id(2)
    nseg = tb // LANES

    r = lax.broadcasted_iota(jnp.int32, (LANES, 2 * LANES), 0)
    c = lax.broadcasted_iota(jnp.int32, (LANES, 2 * LANES), 1)
    cum_op = jnp.where(jnp.logical_or(c >= LANES, r > c), 1.0, 0.0).astype(BF16)
    tri_r = lax.broadcasted_iota(jnp.int32, (LANES, LANES), 0)
    tri_c = lax.broadcasted_iota(jnp.int32, (LANES, LANES), 1)
    causal = tri_c < tri_r
    segs = [slice(sg * LANES, (sg + 1) * LANES) for sg in range(nseg)]

    def block(kb, diagonal):
        start = pl.multiple_of(kb * tb, tb)
        if diagonal:
            parts = [(segs[g], g + 1) for g in range(nseg)]
        else:
            parts = [(slice(0, tb), nseg)]
        work = [(p, slice(p * HEAD_DIM, (p + 1) * HEAD_DIM), rows, nk)
                for rows, nk in parts for p in range(hp)]
        zs = [lax.dot_general(q_ref[rows, cols], k_ref[pl.ds(start, nk * LANES), cols],
                              (((1,), (1,)), ((), ())), preferred_element_type=F32)
              for _, cols, rows, nk in work]
        log_betas, sums = [], []
        for z, (_, _, _, nk) in zip(zs, work):
            log_beta = jnp.minimum(z, 0.0) - jnp.log2(1.0 + jnp.exp2(-jnp.abs(z)))
            l = log_beta - z
            cs = []
            for sg in range(nk):
                l_seg = l[:, segs[sg]]
                if diagonal and sg == nk - 1:
                    l_seg = jnp.where(causal, l_seg, 0.0)
                cs.append(jnp.dot(l_seg.astype(BF16), cum_op, preferred_element_type=F32))
            log_betas.append(log_beta)
            sums.append(cs)
        top = None
        for (p, cols, rows, nk), log_beta, cs in zip(work, log_betas, sums):
            carry = carry_ref[p, rows, :]
            a_parts = [None] * nk
            for sg in range(nk - 1, -1, -1):
                a_seg = jnp.exp2(log_beta[:, segs[sg]] + (cs[sg][:, :LANES] + carry))
                if diagonal and sg == nk - 1:
                    a_seg = jnp.where(causal, a_seg, 0.0)
                a_parts[sg] = a_seg
                carry = carry + cs[sg][:, LANES:]
            a = jnp.concatenate(a_parts, axis=1) if nk > 1 else a_parts[0]
            acc_ref[p, rows, :] += jnp.dot(
                a.astype(BF16), v_ref[pl.ds(start, nk * LANES), cols],
                preferred_element_type=F32)
            carry_ref[p, rows, :] = carry
            top = jnp.max(carry) if top is None else jnp.maximum(top, jnp.max(carry))
        return top

    carry_ref[...] = jnp.zeros_like(carry_ref)
    acc_ref[...] = jnp.zeros_like(acc_ref)
    top = block(qi, True)

    def cond(state):
        kb, top = state
        return jnp.logical_and(kb >= 0, top > EXP2_UNDERFLOW)

    def body(state):
        kb, _ = state
        return kb - 1, block(kb, False)

    lax.while_loop(cond, body, (qi - 1, top))
    for p in range(hp):
        o_ref[:, p * HEAD_DIM:(p + 1) * HEAD_DIM] = acc_ref[p].astype(o_ref.dtype)


def _attn(q, k, v, weights, batch, *, tb, hp):
    nq = SEQ // tb
    groups = SB_HEADS // hp
    width = hp * HEAD_DIM
    steps = batch * groups * nq
    assert all(w.shape[0] % steps == 0 for w in weights)
    cast_specs = [pl.BlockSpec((w.shape[0] // steps, w.shape[1]),
                               lambda b, h, i: ((b * groups + h) * nq + i, 0))
                  for w in weights]
    outs = pl.pallas_call(
        functools.partial(_attn_kernel, tb=tb, hp=hp, n_cast=len(weights)),
        grid=(batch, groups, nq),
        in_specs=[
            pl.BlockSpec((tb, width), lambda b, h, i: (b * nq + i, h)),
            pl.BlockSpec((SEQ, width), lambda b, h, i: (b, h)),
            pl.BlockSpec((SEQ, width), lambda b, h, i: (b, h)),
        ] + cast_specs,
        out_specs=[pl.BlockSpec((tb, width), lambda b, h, i: (b * nq + i, h))] + cast_specs,
        out_shape=[jax.ShapeDtypeStruct((batch * SEQ, SB_WIDTH), BF16)]
        + [jax.ShapeDtypeStruct(w.shape, BF16) for w in weights],
        scratch_shapes=[pltpu.VMEM((hp, tb, LANES), F32),
                        pltpu.VMEM((hp, tb, HEAD_DIM), F32)],
        compiler_params=_params("arbitrary", "arbitrary", "arbitrary"),
        name="attn",
    )(q, k, v, *weights)
    return outs[0], outs[1:]


MIX_CHUNK = 256


def _mix_kernel(pa_ref, at_ref, sa_ref, sb_ref, x_ref, mod_ref, nw_ref, wa_ref, wb_ref, wo_ref,
                cast_ref, x1_ref, h2_ref, cast_out_ref, merged_scr):
    cast_out_ref[...] = cast_ref[...].astype(BF16)
    d = x_ref.shape[1]
    pa = pa_ref[...]
    at = at_ref[...]
    for n0 in range(0, d, MIX_CHUNK):
        cols = slice(n0, n0 + MIX_CHUNK)
        ya = jnp.dot(pa, wa_ref[:, cols], preferred_element_type=F32)
        yb = jnp.dot(at, wb_ref[:, cols], preferred_element_type=F32)
        merged = sa_ref[:, cols].astype(F32) * ya + sb_ref[:, cols].astype(F32) * yb
        merged_scr[:, cols] = merged.astype(BF16)
    merged = merged_scr[...]
    b = pl.program_id(0) * x_ref.shape[0] // SEQ
    gate = _mod_row(mod_ref, GATE1, b)
    for n0 in range(0, d, MIX_CHUNK):
        cols = slice(n0, n0 + MIX_CHUNK)
        o = jnp.dot(merged, wo_ref[:, cols], preferred_element_type=F32)
        x1_ref[:, cols] = x_ref[:, cols] + gate[:, cols] * o
    _norm_modulate_rows(x1_ref, h2_ref, nw_ref[...], _mod_row(mod_ref, SCALE2, b),
                        _mod_row(mod_ref, SHIFT2, b))


def _mix(pa, at, sa, sb, x2, mod, norm_w, wa_bf, wb_bf, wo_bf, w_cast, *, tm):
    m, d = x2.shape
    kdim = pa.shape[1]
    assert m % tm == 0 and w_cast.shape[0] % (m // tm) == 0
    cast_spec = pl.BlockSpec((w_cast.shape[0] // (m // tm), w_cast.shape[1]), lambda i: (i, 0))
    resident = pl.Buffered(1)
    return pl.pallas_call(
        _mix_kernel,
        grid=(m // tm,),
        in_specs=[
            pl.BlockSpec((tm, kdim), lambda i: (i, 0)),
            pl.BlockSpec((tm, kdim), lambda i: (i, 0)),
            pl.BlockSpec((tm, d), lambda i: (i, 0)),
            pl.BlockSpec((tm, d), lambda i: (i, 0)),
            pl.BlockSpec((tm, d), lambda i: (i, 0)),
            pl.BlockSpec(mod.shape, lambda i: (0, 0, 0)),
            pl.BlockSpec((1, d), lambda i: (0, 0)),
            pl.BlockSpec((kdim, d), lambda i: (0, 0), pipeline_mode=resident),
            pl.BlockSpec((kdim, d), lambda i: (0, 0), pipeline_mode=resident),
            pl.BlockSpec((d, d), lambda i: (0, 0), pipeline_mode=resident),
            cast_spec,
        ],
        out_specs=[pl.BlockSpec((tm, d), lambda i: (i, 0)),
                   pl.BlockSpec((tm, d), lambda i: (i, 0)), cast_spec],
        out_shape=[jax.ShapeDtypeStruct((m, d), F32),
                   jax.ShapeDtypeStruct((m, d), BF16),
                   jax.ShapeDtypeStruct(w_cast.shape, BF16)],
        scratch_shapes=[pltpu.VMEM((tm, d), BF16)],
        compiler_params=_params("arbitrary"),
        name="mix",
    )(pa, at, sa, sb, x2, mod, norm_w.reshape(1, d), wa_bf, wb_bf, wo_bf, w_cast)


def kernel(x, c, w_ada, b_ada, norm1_w, w_in, q_norm_w, k_norm_w, w_pool, pool_scale,
           w_a_up, w_b_up, w_o, norm2_w, w_ff1, w_ff2):
    batch, seq, d = x.shape
    assert (seq, d) == (SEQ, D_MODEL) and w_ada.shape[0] == 1
    x2 = x.reshape(batch * seq, d)

    mod = _ada(c, w_ada[0], b_ada[0])
    u, q, k, v, sa, sb, w1_bf = _in_proj(x2, mod, norm1_w[0], w_in[0], q_norm_w[0],
                                         k_norm_w[0], w_ff1[0], tm=2048)
    pa = _pool(u, w_pool[0], pool_scale[0], batch)
    at, (wa_bf, wb_bf, wo_bf) = _attn(
        q, k, v, (w_a_up[0], w_b_up[0], w_o[0]), batch,
        tb=256, hp=8)
    x1, h2, w2_bf = _mix(pa, at, sa, sb, x2, mod, norm2_w[0], wa_bf, wb_bf, wo_bf, w_ff2[0],
                         tm=256)
    out = _ffn(h2, x1, mod, w1_bf, w2_bf, tm=1024, tf=1024)
    return out.reshape(batch, seq, d)
```

```python
import functools
import math

import jax
import jax.numpy as jnp
from jax import lax
from jax.experimental import pallas as pl
from jax.experimental.pallas import tpu as pltpu

D_MODEL = 2048
SEQ = 2048
POOL_WIDTH = D_MODEL // 2
POOL_WINDOWS = (2, 4, 8, 16)
POOL_GROUP_DIM = POOL_WIDTH // len(POOL_WINDOWS)
HEAD_DIM = 128
SB_WIDTH = D_MODEL // 2
SB_HEADS = SB_WIDTH // HEAD_DIM
N_MOD = 6
EPS = 1e-6

Q_OFF = POOL_WIDTH
K_OFF = Q_OFF + SB_WIDTH
V_OFF = K_OFF + SB_WIDTH
GA_OFF = V_OFF + SB_WIDTH
GB_OFF = GA_OFF + D_MODEL

LANES = 128
SUBLANES = 8
V7X_VMEM_BYTES = 64 * 1024 * 1024
VMEM_LIMIT_BYTES = V7X_VMEM_BYTES * 7 // 8

BF16 = jnp.bfloat16
F32 = jnp.float32


def _params(*semantics):
    return pltpu.CompilerParams(dimension_semantics=semantics,
                                vmem_limit_bytes=VMEM_LIMIT_BYTES)


ADA_STREAMS = 4


def _ada_kernel(c_ref, b_ref, *refs):
    w_refs, o_ref = refs[:-1], refs[-1]
    c = c_ref[...]
    batch = c.shape[0]
    sc = c * jax.nn.sigmoid(c)
    if batch % SUBLANES:
        pad = SUBLANES - batch % SUBLANES
        sc = jnp.concatenate([sc, jnp.zeros((pad, sc.shape[1]), F32)], axis=0)
    sc = sc.astype(BF16)
    rows = w_refs[0].shape[0]
    acc = b_ref[...]
    for s, w_ref in enumerate(w_refs):
        acc = acc + jnp.dot(sc[:, s * rows:(s + 1) * rows], w_ref[...].astype(BF16),
                            preferred_element_type=F32)[:batch]
    o_ref[0] = acc


def _ada(c, w_ada, b_ada):
    batch, d = c.shape
    n = w_ada.shape[1]
    tn = 1024
    assert n == N_MOD * d and d % tn == 0 and d % ADA_STREAMS == 0
    slab = d // ADA_STREAMS
    per_chunk = d // tn
    return pl.pallas_call(
        _ada_kernel,
        grid=(n // tn,),
        in_specs=[
            pl.BlockSpec((batch, d), lambda j: (0, 0)),
            pl.BlockSpec((1, tn), lambda j: (0, j)),
        ] + [pl.BlockSpec((slab, tn), functools.partial(lambda j, s: (s, j), s=s))
             for s in range(ADA_STREAMS)],
        out_specs=pl.BlockSpec((1, batch, tn), lambda j: (j // per_chunk, 0, j % per_chunk)),
        out_shape=jax.ShapeDtypeStruct((N_MOD, batch, d), F32),
        compiler_params=_params("arbitrary"),
        name="ada",
    )(c, b_ada.reshape(1, n), *([w_ada] * ADA_STREAMS))


SHIFT1, SCALE1, GATE1, SHIFT2, SCALE2, GATE2 = range(N_MOD)


def _mod_row(mod_ref, k, b):
    return mod_ref[k, pl.ds(b, 1), :]


NORM_ROWS = 256


def _norm_modulate(x, gain, shift):
    y = x * lax.rsqrt(jnp.mean(x * x, axis=-1, keepdims=True) + EPS)
    return (y * gain + shift).astype(BF16)


def _norm_modulate_rows(x_ref, o_ref, norm_w, scale, shift):
    gain = norm_w * (1.0 + scale)

    def body(r, carry):
        rows = pl.ds(pl.multiple_of(r * NORM_ROWS, NORM_ROWS), NORM_ROWS)
        o_ref[rows, :] = _norm_modulate(x_ref[rows, :], gain, shift)
        return carry

    lax.fori_loop(0, x_ref.shape[0] // NORM_ROWS, body, 0)


Q_LOGIT_SCALE = math.log2(math.e) / math.sqrt(HEAD_DIM)


def _sigmoid(x):
    return 0.5 * jnp.tanh(0.5 * x) + 0.5


def _head_norm(blk, gain):
    ms = jnp.mean(blk * blk, axis=-1, keepdims=True)
    return (blk * lax.rsqrt(ms + EPS) * gain).astype(BF16)


IN_PROJ_ROWS = 1024
X_SLOTS = 4


def _in_proj_kernel(x_hbm, mod_ref, nw_ref, wu_ref, wq_ref, wk_ref, wv_ref, wga_ref, wgb_ref,
                    qw_ref, kw_ref, cast_ref, u_ref, q_ref, k_ref, v_ref, ga_ref, gb_ref,
                    cast_out_ref, h_scr, xbuf, sems, *, tm):
    cast_out_ref[...] = cast_ref[...].astype(BF16)
    i = pl.program_id(0)
    j = pl.program_id(1)
    chunks = tm // NORM_ROWS

    def x_copy(tile, r):
        slot = r % X_SLOTS
        return pltpu.make_async_copy(
            x_hbm.at[pl.ds(tile * tm + r * NORM_ROWS, NORM_ROWS), :], xbuf.at[slot],
            sems.at[slot])

    @pl.when(jnp.logical_and(j == 0, i == 0))
    def _():
        for r in range(X_SLOTS):
            x_copy(0, r).start()

    @pl.when(j == 0)
    def _():
        b = i * tm // SEQ
        gain = nw_ref[...] * (1.0 + _mod_row(mod_ref, SCALE1, b))
        shift = _mod_row(mod_ref, SHIFT1, b)

        def body(r, carry):
            x_copy(i, r).wait()
            rows = pl.ds(pl.multiple_of(r * NORM_ROWS, NORM_ROWS), NORM_ROWS)
            h_scr[rows, :] = _norm_modulate(xbuf[r % X_SLOTS], gain, shift)

            @pl.when(r + X_SLOTS < chunks)
            def _():
                x_copy(i, r + X_SLOTS).start()

            return carry

        lax.fori_loop(0, chunks, body, 0)

    @pl.when(jnp.logical_and(j == pl.num_programs(1) - 1, i + 1 < pl.num_programs(0)))
    def _():
        for r in range(X_SLOTS):
            x_copy(i + 1, r).start()

    def bf16_cols(*w_refs):
        return jnp.concatenate([w_ref[...].astype(BF16) for w_ref in w_refs], axis=1)

    w_qk, w_ga, w_gb, w_uv = (bf16_cols(wq_ref, wk_ref), bf16_cols(wga_ref),
                              bf16_cols(wgb_ref), bf16_cols(wu_ref, wv_ref))
    q_gain = qw_ref[...] * Q_LOGIT_SCALE
    k_gain = kw_ref[...]
    for r0 in range(0, tm, IN_PROJ_ROWS):
        rows = slice(r0, r0 + IN_PROJ_ROWS)
        h = h_scr[rows, :]
        qk = jnp.dot(h, w_qk, preferred_element_type=F32)
        q_ref[rows, :] = _head_norm(qk[:, :HEAD_DIM], q_gain)
        k_ref[rows, :] = _head_norm(qk[:, HEAD_DIM:], k_gain)
        ga = jnp.dot(h, w_ga, preferred_element_type=F32)
        ga_ref[rows, :] = _sigmoid(ga).astype(BF16)
        gb = jnp.dot(h, w_gb, preferred_element_type=F32)
        gb_ref[rows, :] = _sigmoid(gb).astype(BF16)
        uv = jnp.dot(h, w_uv, preferred_element_type=F32)
        u_ref[rows, :] = uv[:, :HEAD_DIM].astype(BF16)
        v_ref[rows, :] = uv[:, HEAD_DIM:].astype(BF16)


def _in_proj(x2, mod, norm_w, w_in, q_norm_w, k_norm_w, w_cast, *, tm):
    m, d = x2.shape
    steps = (m // tm) * SB_HEADS
    assert w_cast.shape[0] % steps == 0
    cast_spec = pl.BlockSpec((w_cast.shape[0] // steps, w_cast.shape[1]),
                             lambda i, j: (i * SB_HEADS + j, 0))
    hd = HEAD_DIM
    gw = D_MODEL // SB_HEADS
    assert POOL_WIDTH == SB_HEADS * hd and D_MODEL == SB_HEADS * gw
    assert m % tm == 0 and SEQ % tm == 0 and tm % IN_PROJ_ROWS == 0
    assert tm % NORM_ROWS == 0 and tm // NORM_ROWS >= X_SLOTS
    w_spec = lambda width, off: pl.BlockSpec((d, width), lambda i, j: (0, off // width + j))
    out_spec = lambda width: pl.BlockSpec((tm, width), lambda i, j: (i, j))
    return pl.pallas_call(
        functools.partial(_in_proj_kernel, tm=tm),
        grid=(m // tm, SB_HEADS),
        in_specs=[
            pl.BlockSpec(memory_space=pl.ANY),
            pl.BlockSpec(mod.shape, lambda i, j: (0, 0, 0)),
            pl.BlockSpec((1, d), lambda i, j: (0, 0)),
            w_spec(hd, 0), w_spec(hd, Q_OFF), w_spec(hd, K_OFF), w_spec(hd, V_OFF),
            w_spec(gw, GA_OFF), w_spec(gw, GB_OFF),
            pl.BlockSpec((1, hd), lambda i, j: (0, 0)),
            pl.BlockSpec((1, hd), lambda i, j: (0, 0)),
            cast_spec,
        ],
        out_specs=[out_spec(hd), out_spec(hd), out_spec(hd), out_spec(hd),
                   out_spec(gw), out_spec(gw), cast_spec],
        out_shape=[jax.ShapeDtypeStruct((m, POOL_WIDTH), BF16),
                   jax.ShapeDtypeStruct((m, SB_WIDTH), BF16),
                   jax.ShapeDtypeStruct((m, SB_WIDTH), BF16),
                   jax.ShapeDtypeStruct((m, SB_WIDTH), BF16),
                   jax.ShapeDtypeStruct((m, D_MODEL), BF16),
                   jax.ShapeDtypeStruct((m, D_MODEL), BF16),
                   jax.ShapeDtypeStruct(w_cast.shape, BF16)],
        scratch_shapes=[pltpu.VMEM((tm, d), BF16), pltpu.VMEM((X_SLOTS, NORM_ROWS, d), F32),
                        pltpu.SemaphoreType.DMA((X_SLOTS,))],
        compiler_params=_params("arbitrary", "arbitrary"),
        name="in_proj",
    )(x2, mod, norm_w.reshape(1, d), w_in, w_in, w_in, w_in, w_in, w_in,
      q_norm_w.reshape(1, hd), k_norm_w.reshape(1, hd), w_cast)


FFN_OUT_CHUNK = 512


def _ffn_kernel(h_ref, x1_ref, mod_ref, w1_ref, w2_ref, o_ref, *, tm):
    i = pl.program_id(0)
    c = pl.program_id(1)
    xrows = x1_ref.shape[0]

    def step(first):
        rows = pl.ds(pl.multiple_of(c * xrows, xrows), xrows)
        if not first:
            o_ref[rows, :] += x1_ref[...]
        a = jnp.dot(h_ref[...], w1_ref[...], preferred_element_type=F32)
        r = jnp.maximum(a, 0.0)
        act = (r * r).astype(BF16)
        gate = _mod_row(mod_ref, GATE2, i * tm // SEQ)
        for n0 in range(0, o_ref.shape[1], FFN_OUT_CHUNK):
            cols = slice(n0, n0 + FFN_OUT_CHUNK)
            y = gate[:, cols] * jnp.dot(act, w2_ref[:, cols], preferred_element_type=F32)
            o_ref[:, cols] = y if first else o_ref[:, cols] + y
        if first:
            o_ref[rows, :] += x1_ref[...]

    pl.when(c == 0)(functools.partial(step, True))
    pl.when(c != 0)(functools.partial(step, False))


def _ffn(h2, x1, mod, w1, w2, *, tm, tf):
    m, d = h2.shape
    f = w1.shape[1]
    assert m % tm == 0 and SEQ % tm == 0 and f % tf == 0 and tm % (f // tf) == 0
    assert d % FFN_OUT_CHUNK == 0
    chunks = f // tf
    xrows = tm // chunks
    return pl.pallas_call(
        functools.partial(_ffn_kernel, tm=tm),
        grid=(m // tm, chunks),
        in_specs=[
            pl.BlockSpec((tm, d), lambda i, c: (i, 0)),
            pl.BlockSpec((xrows, d), lambda i, c: (i * chunks + c, 0)),
            pl.BlockSpec(mod.shape, lambda i, c: (0, 0, 0)),
            pl.BlockSpec((d, tf), lambda i, c: (0, c)),
            pl.BlockSpec((tf, d), lambda i, c: (c, 0)),
        ],
        out_specs=pl.BlockSpec((tm, d), lambda i, c: (i, 0)),
        out_shape=jax.ShapeDtypeStruct((m, d), F32),
        compiler_params=_params("arbitrary", "arbitrary"),
        name="ffn",
    )(h2, x1, mod, w1, w2)


POOL_ROWS = 256
POOL_HEAD = max(POOL_WINDOWS)
assert POOL_HEAD % (2 * SUBLANES) == 0 and all(w & (w - 1) == 0 for w in POOL_WINDOWS)


def _pool_head_exact(u_head, w):
    pos = lax.broadcasted_iota(jnp.int32, u_head.shape, 0)
    win_sum = u_head
    for k in range(1, w):
        win_sum = win_sum + jnp.where(pos >= k, pltpu.roll(u_head, k, 0), 0.0)
    count = jnp.minimum(pos + 1, w).astype(F32)
    return win_sum / count - u_head


def _pool_kernel(u_ref, w_ref, s_ref, o_ref):
    gd = POOL_GROUP_DIM
    tt = lax.broadcasted_iota(jnp.int32, (POOL_ROWS + POOL_HEAD, POOL_ROWS), 0)
    jj = lax.broadcasted_iota(jnp.int32, (POOL_ROWS + POOL_HEAD, POOL_ROWS), 1)
    for g, w in enumerate(POOL_WINDOWS):
        cols = slice(g * gd, (g + 1) * gd)
        inside = jnp.where(jnp.logical_and(jj <= tt, jj > tt - w), 1.0 / w, 0.0)
        inside = inside - jnp.where(jj == tt, 1.0, 0.0)
        spill = jnp.where(jj > tt - w, 1.0 / w, 0.0)
        band = jnp.where(tt < POOL_ROWS, inside, spill).astype(BF16)
        carry = None
        pooled = []
        for r0 in range(0, SEQ, POOL_ROWS):
            res = jnp.dot(band, u_ref[r0:r0 + POOL_ROWS, cols], preferred_element_type=F32)
            if r0 == 0:
                head = _pool_head_exact(u_ref[0:POOL_HEAD, cols].astype(F32), w)
            else:
                head = res[:POOL_HEAD] + carry
            pooled += [head.astype(BF16), res[POOL_HEAD:POOL_ROWS].astype(BF16)]
            carry = res[POOL_ROWS:]
        mixed = jnp.dot(jnp.concatenate(pooled, axis=0), w_ref[g].astype(BF16),
                        preferred_element_type=F32)
        o_ref[:, cols] = (mixed * s_ref[:, cols]).astype(o_ref.dtype)


def _pool(u, w_pool, pool_scale, batch):
    groups = len(POOL_WINDOWS)
    gd = POOL_GROUP_DIM
    assert SEQ % POOL_ROWS == 0
    return pl.pallas_call(
        _pool_kernel,
        grid=(batch,),
        in_specs=[
            pl.BlockSpec((SEQ, POOL_WIDTH), lambda b: (b, 0)),
            pl.BlockSpec((groups, gd, gd), lambda b: (0, 0, 0)),
            pl.BlockSpec((1, POOL_WIDTH), lambda b: (0, 0)),
        ],
        out_specs=pl.BlockSpec((SEQ, POOL_WIDTH), lambda b: (b, 0)),
        out_shape=jax.ShapeDtypeStruct((batch * SEQ, POOL_WIDTH), BF16),
        compiler_params=_params("arbitrary"),
        name="pool",
    )(u, w_pool, pool_scale.reshape(1, POOL_WIDTH))


EXP2_UNDERFLOW = -151.0


def _attn_kernel(q_ref, k_ref, v_ref, *refs, tb, hp, n_cast):
    w_refs, o_ref, wbf_refs = refs[:n_cast], refs[n_cast], refs[n_cast + 1:2 * n_cast + 1]
    carry_ref, acc_ref = refs[2 * n_cast + 1:]
    for w_ref, wbf_ref in zip(w_refs, wbf_refs):
        wbf_ref[...] = w_ref[...].astype(BF16)

    qi = pl.program_id(2)
    nseg = tb // LANES

    r = lax.broadcasted_iota(jnp.int32, (LANES, 2 * LANES), 0)
    c = lax.broadcasted_iota(jnp.int32, (LANES, 2 * LANES), 1)
    cum_op = jnp.where(jnp.logical_or(c >= LANES, r > c), 1.0, 0.0).astype(BF16)
    tri_r = lax.broadcasted_iota(jnp.int32, (LANES, LANES), 0)
    tri_c = lax.broadcasted_iota(jnp.int32, (LANES, LANES), 1)
    causal = tri_c < tri_r
    segs = [slice(sg * LANES, (sg + 1) * LANES) for sg in range(nseg)]

    def block(kb, diagonal):
        start = pl.multiple_of(kb * tb, tb)
        if diagonal:
            parts = [(segs[g], g + 1) for g in range(nseg)]
        else:
            parts = [(slice(0, tb), nseg)]
        work = [(p, slice(p * HEAD_DIM, (p + 1) * HEAD_DIM), rows, nk)
                for rows, nk in parts for p in range(hp)]
        zs = [lax.dot_general(q_ref[rows, cols], k_ref[pl.ds(start, nk * LANES), cols],
                              (((1,), (1,)), ((), ())), preferred_element_type=F32)
              for _, cols, rows, nk in work]
        log_betas, sums = [], []
        for z, (_, _, _, nk) in zip(zs, work):
            log_beta = jnp.minimum(z, 0.0) - jnp.log2(1.0 + jnp.exp2(-jnp.abs(z)))
            l = log_beta - z
            cs = []
            for sg in range(nk):
                l_seg = l[:, segs[sg]]
                if diagonal and sg == nk - 1:
                    l_seg = jnp.where(causal, l_seg, 0.0)
                cs.append(jnp.dot(l_seg.astype(BF16), cum_op, preferred_element_type=F32))
            log_betas.append(log_beta)
            sums.append(cs)
        top = None
        for (p, cols, rows, nk), log_beta, cs in zip(work, log_betas, sums):
            carry = carry_ref[p, rows, :]
            a_parts = [None] * nk
            for sg in range(nk - 1, -1, -1):
                a_seg = jnp.exp2(log_beta[:, segs[sg]] + (cs[sg][:, :LANES] + carry))
                if diagonal and sg == nk - 1:
                    a_seg = jnp.where(causal, a_seg, 0.0)
                a_parts[sg] = a_seg
                carry = carry + cs[sg][:, LANES:]
            a = jnp.concatenate(a_parts, axis=1) if nk > 1 else a_parts[0]
            acc_ref[p, rows, :] += jnp.dot(
                a.astype(BF16), v_ref[pl.ds(start, nk * LANES), cols],
                preferred_element_type=F32)
            carry_ref[p, rows, :] = carry
            top = jnp.max(carry) if top is None else jnp.maximum(top, jnp.max(carry))
        return top

    carry_ref[...] = jnp.zeros_like(carry_ref)
    acc_ref[...] = jnp.zeros_like(acc_ref)
    top = block(qi, True)

    def cond(state):
        kb, top = state
        return jnp.logical_and(kb >= 0, top > EXP2_UNDERFLOW)

    def body(state):
        kb, _ = state
        return kb - 1, block(kb, False)

    lax.while_loop(cond, body, (qi - 1, top))
    for p in range(hp):
        o_ref[:, p * HEAD_DIM:(p + 1) * HEAD_DIM] = acc_ref[p].astype(o_ref.dtype)


def _attn(q, k, v, weights, batch, *, tb, hp):
    nq = SEQ // tb
    groups = SB_HEADS // hp
    width = hp * HEAD_DIM
    steps = batch * groups * nq
    assert all(w.shape[0] % steps == 0 for w in weights)
    cast_specs = [pl.BlockSpec((w.shape[0] // steps, w.shape[1]),
                               lambda b, h, i: ((b * groups + h) * nq + i, 0))
                  for w in weights]
    outs = pl.pallas_call(
        functools.partial(_attn_kernel, tb=tb, hp=hp, n_cast=len(weights)),
        grid=(batch, groups, nq),
        in_specs=[
            pl.BlockSpec((tb, width), lambda b, h, i: (b * nq + i, h)),
            pl.BlockSpec((SEQ, width), lambda b, h, i: (b, h)),
            pl.BlockSpec((SEQ, width), lambda b, h, i: (b, h)),
        ] + cast_specs,
        out_specs=[pl.BlockSpec((tb, width), lambda b, h, i: (b * nq + i, h))] + cast_specs,
        out_shape=[jax.ShapeDtypeStruct((batch * SEQ, SB_WIDTH), BF16)]
        + [jax.ShapeDtypeStruct(w.shape, BF16) for w in weights],
        scratch_shapes=[pltpu.VMEM((hp, tb, LANES), F32),
                        pltpu.VMEM((hp, tb, HEAD_DIM), F32)],
        compiler_params=_params("arbitrary", "arbitrary", "arbitrary"),
        name="attn",
    )(q, k, v, *weights)
    return outs[0], outs[1:]


MIX_CHUNK = 256


def _mix_kernel(pa_ref, at_ref, sa_ref, sb_ref, x_ref, mod_ref, nw_ref, wa_ref, wb_ref, wo_ref,
                cast_ref, x1_ref, h2_ref, cast_out_ref, merged_scr):
    cast_out_ref[...] = cast_ref[...].astype(BF16)
    d = x_ref.shape[1]
    pa = pa_ref[...]
    at = at_ref[...]
    for n0 in range(0, d, MIX_CHUNK):
        cols = slice(n0, n0 + MIX_CHUNK)
        ya = jnp.dot(pa, wa_ref[:, cols], preferred_element_type=F32)
        yb = jnp.dot(at, wb_ref[:, cols], preferred_element_type=F32)
        merged = sa_ref[:, cols].astype(F32) * ya + sb_ref[:, cols].astype(F32) * yb
        merged_scr[:, cols] = merged.astype(BF16)
    merged = merged_scr[...]
    b = pl.program_id(0) * x_ref.shape[0] // SEQ
    gate = _mod_row(mod_ref, GATE1, b)
    for n0 in range(0, d, MIX_CHUNK):
        cols = slice(n0, n0 + MIX_CHUNK)
        o = jnp.dot(merged, wo_ref[:, cols], preferred_element_type=F32)
        x1_ref[:, cols] = x_ref[:, cols] + gate[:, cols] * o
    _norm_modulate_rows(x1_ref, h2_ref, nw_ref[...], _mod_row(mod_ref, SCALE2, b),
                        _mod_row(mod_ref, SHIFT2, b))


def _mix(pa, at, sa, sb, x2, mod, norm_w, wa_bf, wb_bf, wo_bf, w_cast, *, tm):
    m, d = x2.shape
    kdim = pa.shape[1]
    assert m % tm == 0 and w_cast.shape[0] % (m // tm) == 0
    cast_spec = pl.BlockSpec((w_cast.shape[0] // (m // tm), w_cast.shape[1]), lambda i: (i, 0))
    resident = pl.Buffered(1)
    return pl.pallas_call(
        _mix_kernel,
        grid=(m // tm,),
        in_specs=[
            pl.BlockSpec((tm, kdim), lambda i: (i, 0)),
            pl.BlockSpec((tm, kdim), lambda i: (i, 0)),
            pl.BlockSpec((tm, d), lambda i: (i, 0)),
            pl.BlockSpec((tm, d), lambda i: (i, 0)),
            pl.BlockSpec((tm, d), lambda i: (i, 0)),
            pl.BlockSpec(mod.shape, lambda i: (0, 0, 0)),
            pl.BlockSpec((1, d), lambda i: (0, 0)),
            pl.BlockSpec((kdim, d), lambda i: (0, 0), pipeline_mode=resident),
            pl.BlockSpec((kdim, d), lambda i: (0, 0), pipeline_mode=resident),
            pl.BlockSpec((d, d), lambda i: (0, 0), pipeline_mode=resident),
            cast_spec,
        ],
        out_specs=[pl.BlockSpec((tm, d), lambda i: (i, 0)),
                   pl.BlockSpec((tm, d), lambda i: (i, 0)), cast_spec],
        out_shape=[jax.ShapeDtypeStruct((m, d), F32),
                   jax.ShapeDtypeStruct((m, d), BF16),
                   jax.ShapeDtypeStruct(w_cast.shape, BF16)],
        scratch_shapes=[pltpu.VMEM((tm, d), BF16)],
        compiler_params=_params("arbitrary"),
        name="mix",
    )(pa, at, sa, sb, x2, mod, norm_w.reshape(1, d), wa_bf, wb_bf, wo_bf, w_cast)


def kernel(x, c, w_ada, b_ada, norm1_w, w_in, q_norm_w, k_norm_w, w_pool, pool_scale,
           w_a_up, w_b_up, w_o, norm2_w, w_ff1, w_ff2):
    batch, seq, d = x.shape
    assert (seq, d) == (SEQ, D_MODEL) and w_ada.shape[0] == 1
    x2 = x.reshape(batch * seq, d)

    mod = _ada(c, w_ada[0], b_ada[0])
    u, q, k, v, sa, sb, w1_bf = _in_proj(x2, mod, norm1_w[0], w_in[0], q_norm_w[0],
                                         k_norm_w[0], w_ff1[0], tm=2048)
    pa = _pool(u, w_pool[0], pool_scale[0], batch)
    at, (wa_bf, wb_bf, wo_bf) = _attn(
        q, k, v, (w_a_up[0], w_b_up[0], w_o[0]), batch,
        tb=256, hp=8)
    x1, h2, w2_bf = _mix(pa, at, sa, sb, x2, mod, norm2_w[0], wa_bf, wb_bf, wo_bf, w_ff2[0],
                         tm=256)
    out = _ffn(h2, x1, mod, w1_bf, w2_bf, tm=1024, tf=1024)
    return out.reshape(batch, seq, d)
```

```python
import functools
import math

import jax
import jax.numpy as jnp
from jax import lax
from jax.experimental import pallas as pl
from jax.experimental.pallas import tpu as pltpu

D_MODEL = 2048
SEQ = 2048
POOL_WIDTH = D_MODEL // 2
POOL_WINDOWS = (2, 4, 8, 16)
POOL_GROUP_DIM = POOL_WIDTH // len(POOL_WINDOWS)
HEAD_DIM = 128
SB_WIDTH = D_MODEL // 2
SB_HEADS = SB_WIDTH // HEAD_DIM
N_MOD = 6
EPS = 1e-6

Q_OFF = POOL_WIDTH
K_OFF = Q_OFF + SB_WIDTH
V_OFF = K_OFF + SB_WIDTH
GA_OFF = V_OFF + SB_WIDTH
GB_OFF = GA_OFF + D_MODEL

LANES = 128
SUBLANES = 8
V7X_VMEM_BYTES = 64 * 1024 * 1024
VMEM_LIMIT_BYTES = V7X_VMEM_BYTES * 7 // 8

BF16 = jnp.bfloat16
F32 = jnp.float32


def _params(*semantics):
    return pltpu.CompilerParams(dimension_semantics=semantics,
                                vmem_limit_bytes=VMEM_LIMIT_BYTES)


ADA_STREAMS = 4


def _ada_kernel(c_ref, b_ref, *refs):
    w_refs, o_ref = refs[:-1], refs[-1]
    c = c_ref[...]
    batch = c.shape[0]
    sc = c * jax.nn.sigmoid(c)
    if batch % SUBLANES:
        pad = SUBLANES - batch % SUBLANES
        sc = jnp.concatenate([sc, jnp.zeros((pad, sc.shape[1]), F32)], axis=0)
    sc = sc.astype(BF16)
    rows = w_refs[0].shape[0]
    acc = b_ref[...]
    for s, w_ref in enumerate(w_refs):
        acc = acc + jnp.dot(sc[:, s * rows:(s + 1) * rows], w_ref[...].astype(BF16),
                            preferred_element_type=F32)[:batch]
    o_ref[0] = acc


def _ada(c, w_ada, b_ada):
    batch, d = c.shape
    n = w_ada.shape[1]
    tn = 1024
    assert n == N_MOD * d and d % tn == 0 and d % ADA_STREAMS == 0
    slab = d // ADA_STREAMS
    per_chunk = d // tn
    return pl.pallas_call(
        _ada_kernel,
        grid=(n // tn,),
        in_specs=[
            pl.BlockSpec((batch, d), lambda j: (0, 0)),
            pl.BlockSpec((1, tn), lambda j: (0, j)),
        ] + [pl.BlockSpec((slab, tn), functools.partial(lambda j, s: (s, j), s=s))
             for s in range(ADA_STREAMS)],
        out_specs=pl.BlockSpec((1, batch, tn), lambda j: (j // per_chunk, 0, j % per_chunk)),
        out_shape=jax.ShapeDtypeStruct((N_MOD, batch, d), F32),
        compiler_params=_params("arbitrary"),
        name="ada",
    )(c, b_ada.reshape(1, n), *([w_ada] * ADA_STREAMS))


SHIFT1, SCALE1, GATE1, SHIFT2, SCALE2, GATE2 = range(N_MOD)


def _mod_row(mod_ref, k, b):
    return mod_ref[k, pl.ds(b, 1), :]


NORM_ROWS = 256


def _norm_modulate(x, gain, shift):
    y = x * lax.rsqrt(jnp.mean(x * x, axis=-1, keepdims=True) + EPS)
    return (y * gain + shift).astype(BF16)


def _norm_modulate_rows(x_ref, o_ref, norm_w, scale, shift):
    gain = norm_w * (1.0 + scale)

    def body(r, carry):
        rows = pl.ds(pl.multiple_of(r * NORM_ROWS, NORM_ROWS), NORM_ROWS)
        o_ref[rows, :] = _norm_modulate(x_ref[rows, :], gain, shift)
        return carry

    lax.fori_loop(0, x_ref.shape[0] // NORM_ROWS, body, 0)


Q_LOGIT_SCALE = math.log2(math.e) / math.sqrt(HEAD_DIM)


def _sigmoid(x):
    return 0.5 * jnp.tanh(0.5 * x) + 0.5


def _head_norm(blk, gain):
    ms = jnp.mean(blk * blk, axis=-1, keepdims=True)
    return (blk * lax.rsqrt(ms + EPS) * gain).astype(BF16)


IN_PROJ_ROWS = 1024
X_SLOTS = 4


def _in_proj_kernel(x_hbm, mod_ref, nw_ref, wu_ref, wq_ref, wk_ref, wv_ref, wga_ref, wgb_ref,
                    qw_ref, kw_ref, cast_ref, u_ref, q_ref, k_ref, v_ref, ga_ref, gb_ref,
                    cast_out_ref, h_scr, xbuf, sems, *, tm):
    cast_out_ref[...] = cast_ref[...].astype(BF16)
    i = pl.program_id(0)
    j = pl.program_id(1)
    chunks = tm // NORM_ROWS

    def x_copy(tile, r):
        slot = r % X_SLOTS
        return pltpu.make_async_copy(
            x_hbm.at[pl.ds(tile * tm + r * NORM_ROWS, NORM_ROWS), :], xbuf.at[slot],
            sems.at[slot])

    @pl.when(jnp.logical_and(j == 0, i == 0))
    def _():
        for r in range(X_SLOTS):
            x_copy(0, r).start()

    @pl.when(j == 0)
    def _():
        b = i * tm // SEQ
        gain = nw_ref[...] * (1.0 + _mod_row(mod_ref, SCALE1, b))
        shift = _mod_row(mod_ref, SHIFT1, b)

        def body(r, carry):
            x_copy(i, r).wait()
            rows = pl.ds(pl.multiple_of(r * NORM_ROWS, NORM_ROWS), NORM_ROWS)
            h_scr[rows, :] = _norm_modulate(xbuf[r % X_SLOTS], gain, shift)

            @pl.when(r + X_SLOTS < chunks)
            def _():
                x_copy(i, r + X_SLOTS).start()

            return carry

        lax.fori_loop(0, chunks, body, 0)

    @pl.when(jnp.logical_and(j == pl.num_programs(1) - 1, i + 1 < pl.num_programs(0)))
    def _():
        for r in range(X_SLOTS):
            x_copy(i + 1, r).start()

    def bf16_cols(*w_refs):
        return jnp.concatenate([w_ref[...].astype(BF16) for w_ref in w_refs], axis=1)

    w_qk, w_ga, w_gb, w_uv = (bf16_cols(wq_ref, wk_ref), bf16_cols(wga_ref),
                              bf16_cols(wgb_ref), bf16_cols(wu_ref, wv_ref))
    q_gain = qw_ref[...] * Q_LOGIT_SCALE
    k_gain = kw_ref[...]
    for r0 in range(0, tm, IN_PROJ_ROWS):
        rows = slice(r0, r0 + IN_PROJ_ROWS)
        h = h_scr[rows, :]
        qk = jnp.dot(h, w_qk, preferred_element_type=F32)
        q_ref[rows, :] = _head_norm(qk[:, :HEAD_DIM], q_gain)
        k_ref[rows, :] = _head_norm(qk[:, HEAD_DIM:], k_gain)
        ga = jnp.dot(h, w_ga, preferred_element_type=F32)
        ga_ref[rows, :] = _sigmoid(ga).astype(BF16)
        gb = jnp.dot(h, w_gb, preferred_element_type=F32)
        gb_ref[rows, :] = _sigmoid(gb).astype(BF16)
        uv = jnp.dot(h, w_uv, preferred_element_type=F32)
        u_ref[rows, :] = uv[:, :HEAD_DIM].astype(BF16)
        v_ref[rows, :] = uv[:, HEAD_DIM:].astype(BF16)


def _in_proj(x2, mod, norm_w, w_in, q_norm_w, k_norm_w, w_cast, *, tm):
    m, d = x2.shape
    steps = (m // tm) * SB_HEADS
    assert w_cast.shape[0] % steps == 0
    cast_spec = pl.BlockSpec((w_cast.shape[0] // steps, w_cast.shape[1]),
                             lambda i, j: (i * SB_HEADS + j, 0))
    hd = HEAD_DIM
    gw = D_MODEL // SB_HEADS
    assert POOL_WIDTH == SB_HEADS * hd and D_MODEL == SB_HEADS * gw
    assert m % tm == 0 and SEQ % tm == 0 and tm % IN_PROJ_ROWS == 0
    assert tm % NORM_ROWS == 0 and tm // NORM_ROWS >= X_SLOTS
    w_spec = lambda width, off: pl.BlockSpec((d, width), lambda i, j: (0, off // width + j))
    out_spec = lambda width: pl.BlockSpec((tm, width), lambda i, j: (i, j))
    return pl.pallas_call(
        functools.partial(_in_proj_kernel, tm=tm),
        grid=(m // tm, SB_HEADS),
        in_specs=[
            pl.BlockSpec(memory_space=pl.ANY),
            pl.BlockSpec(mod.shape, lambda i, j: (0, 0, 0)),
            pl.BlockSpec((1, d), lambda i, j: (0, 0)),
            w_spec(hd, 0), w_spec(hd, Q_OFF), w_spec(hd, K_OFF), w_spec(hd, V_OFF),
            w_spec(gw, GA_OFF), w_spec(gw, GB_OFF),
            pl.BlockSpec((1, hd), lambda i, j: (0, 0)),
            pl.BlockSpec((1, hd), lambda i, j: (0, 0)),
            cast_spec,
        ],
        out_specs=[out_spec(hd), out_spec(hd), out_spec(hd), out_spec(hd),
                   out_spec(gw), out_spec(gw), cast_spec],
        out_shape=[jax.ShapeDtypeStruct((m, POOL_WIDTH), BF16),
                   jax.ShapeDtypeStruct((m, SB_WIDTH), BF16),
                   jax.ShapeDtypeStruct((m, SB_WIDTH), BF16),
                   jax.ShapeDtypeStruct((m, SB_WIDTH), BF16),
                   jax.ShapeDtypeStruct((m, D_MODEL), BF16),
                   jax.ShapeDtypeStruct((m, D_MODEL), BF16),
                   jax.ShapeDtypeStruct(w_cast.shape, BF16)],
        scratch_shapes=[pltpu.VMEM((tm, d), BF16), pltpu.VMEM((X_SLOTS, NORM_ROWS, d), F32),
                        pltpu.SemaphoreType.DMA((X_SLOTS,))],
        compiler_params=_params("arbitrary", "arbitrary"),
        name="in_proj",
    )(x2, mod, norm_w.reshape(1, d), w_in, w_in, w_in, w_in, w_in, w_in,
      q_norm_w.reshape(1, hd), k_norm_w.reshape(1, hd), w_cast)


FFN_OUT_CHUNK = 512


def _ffn_kernel(h_ref, x1_ref, mod_ref, w1_ref, w2_ref, o_ref, *, tm):
    i = pl.program_id(0)
    c = pl.program_id(1)
    xrows = x1_ref.shape[0]

    def step(first):
        rows = pl.ds(pl.multiple_of(c * xrows, xrows), xrows)
        if not first:
            o_ref[rows, :] += x1_ref[...]
        a = jnp.dot(h_ref[...], w1_ref[...], preferred_element_type=F32)
        r = jnp.maximum(a, 0.0)
        act = (r * r).astype(BF16)
        gate = _mod_row(mod_ref, GATE2, i * tm // SEQ)
        for n0 in range(0, o_ref.shape[1], FFN_OUT_CHUNK):
            cols = slice(n0, n0 + FFN_OUT_CHUNK)
            y = gate[:, cols] * jnp.dot(act, w2_ref[:, cols], preferred_element_type=F32)
            o_ref[:, cols] = y if first else o_ref[:, cols] + y
        if first:
            o_ref[rows, :] += x1_ref[...]

    pl.when(c == 0)(functools.partial(step, True))
    pl.when(c != 0)(functools.partial(step, False))


def _ffn(h2, x1, mod, w1, w2, *, tm, tf):
    m, d = h2.shape
    f = w1.shape[1]
    assert m % tm == 0 and SEQ % tm == 0 and f % tf == 0 and tm % (f // tf) == 0
    assert d % FFN_OUT_CHUNK == 0
    chunks = f // tf
    xrows = tm // chunks
    return pl.pallas_call(
        functools.partial(_ffn_kernel, tm=tm),
        grid=(m // tm, chunks),
        in_specs=[
            pl.BlockSpec((tm, d), lambda i, c: (i, 0)),
            pl.BlockSpec((xrows, d), lambda i, c: (i * chunks + c, 0)),
            pl.BlockSpec(mod.shape, lambda i, c: (0, 0, 0)),
            pl.BlockSpec((d, tf), lambda i, c: (0, c)),
            pl.BlockSpec((tf, d), lambda i, c: (c, 0)),
        ],
        out_specs=pl.BlockSpec((tm, d), lambda i, c: (i, 0)),
        out_shape=jax.ShapeDtypeStruct((m, d), F32),
        compiler_params=_params("arbitrary", "arbitrary"),
        name="ffn",
    )(h2, x1, mod, w1, w2)


POOL_ROWS = 256
POOL_HEAD = max(POOL_WINDOWS)
assert POOL_HEAD % (2 * SUBLANES) == 0 and all(w & (w - 1) == 0 for w in POOL_WINDOWS)


def _pool_head_exact(u_head, w):
    pos = lax.broadcasted_iota(jnp.int32, u_head.shape, 0)
    win_sum = u_head
    for k in range(1, w):
        win_sum = win_sum + jnp.where(pos >= k, pltpu.roll(u_head, k, 0), 0.0)
    count = jnp.minimum(pos + 1, w).astype(F32)
    return win_sum / count - u_head


def _pool_kernel(u_ref, w_ref, s_ref, o_ref):
    gd = POOL_GROUP_DIM
    tt = lax.broadcasted_iota(jnp.int32, (POOL_ROWS + POOL_HEAD, POOL_ROWS), 0)
    jj = lax.broadcasted_iota(jnp.int32, (POOL_ROWS + POOL_HEAD, POOL_ROWS), 1)
    for g, w in enumerate(POOL_WINDOWS):
        cols = slice(g * gd, (g + 1) * gd)
        inside = jnp.where(jnp.logical_and(jj <= tt, jj > tt - w), 1.0 / w, 0.0)
        inside = inside - jnp.where(jj == tt, 1.0, 0.0)
        spill = jnp.where(jj > tt - w, 1.0 / w, 0.0)
        band = jnp.where(tt < POOL_ROWS, inside, spill).astype(BF16)
        carry = None
        pooled = []
        for r0 in range(0, SEQ, POOL_ROWS):
            res = jnp.dot(band, u_ref[r0:r0 + POOL_ROWS, cols], preferred_element_type=F32)
            if r0 == 0:
                head = _pool_head_exact(u_ref[0:POOL_HEAD, cols].astype(F32), w)
            else:
                head = res[:POOL_HEAD] + carry
            pooled += [head.astype(BF16), res[POOL_HEAD:POOL_ROWS].astype(BF16)]
            carry = res[POOL_ROWS:]
        mixed = jnp.dot(jnp.concatenate(pooled, axis=0), w_ref[g].astype(BF16),
                        preferred_element_type=F32)
        o_ref[:, cols] = (mixed * s_ref[:, cols]).astype(o_ref.dtype)


def _pool(u, w_pool, pool_scale, batch):
    groups = len(POOL_WINDOWS)
    gd = POOL_GROUP_DIM
    assert SEQ % POOL_ROWS == 0
    return pl.pallas_call(
        _pool_kernel,
        grid=(batch,),
        in_specs=[
            pl.BlockSpec((SEQ, POOL_WIDTH), lambda b: (b, 0)),
            pl.BlockSpec((groups, gd, gd), lambda b: (0, 0, 0)),
            pl.BlockSpec((1, POOL_WIDTH), lambda b: (0, 0)),
        ],
        out_specs=pl.BlockSpec((SEQ, POOL_WIDTH), lambda b: (b, 0)),
        out_shape=jax.ShapeDtypeStruct((batch * SEQ, POOL_WIDTH), BF16),
        compiler_params=_params("arbitrary"),
        name="pool",
    )(u, w_pool, pool_scale.reshape(1, POOL_WIDTH))


EXP2_UNDERFLOW = -151.0


def _attn_kernel(q_ref, k_ref, v_ref, *refs, tb, hp, n_cast):
    w_refs, o_ref, wbf_refs = refs[:n_cast], refs[n_cast], refs[n_cast + 1:2 * n_cast + 1]
    carry_ref, acc_ref = refs[2 * n_cast + 1:]
    for w_ref, wbf_ref in zip(w_refs, wbf_refs):
        wbf_ref[...] = w_ref[...].astype(BF16)

    qi = pl.program_id(2)
    nseg = tb // LANES

    r = lax.broadcasted_iota(jnp.int32, (LANES, 2 * LANES), 0)
    c = lax.broadcasted_iota(jnp.int32, (LANES, 2 * LANES), 1)
    cum_op = jnp.where(jnp.logical_or(c >= LANES, r > c), 1.0, 0.0).astype(BF16)
    tri_r = lax.broadcasted_iota(jnp.int32, (LANES, LANES), 0)
    tri_c = lax.broadcasted_iota(jnp.int32, (LANES, LANES), 1)
    causal = tri_c < tri_r
    segs = [slice(sg * LANES, (sg + 1) * LANES) for sg in range(nseg)]

    def block(kb, diagonal):
        start = pl.multiple_of(kb * tb, tb)
        if diagonal:
            parts = [(segs[g], g + 1) for g in range(nseg)]
        else:
            parts = [(slice(0, tb), nseg)]
        work = [(p, slice(p * HEAD_DIM, (p + 1) * HEAD_DIM), rows, nk)
                for rows, nk in parts for p in range(hp)]
        zs = [lax.dot_general(q_ref[rows, cols], k_ref[pl.ds(start, nk * LANES), cols],
                              (((1,), (1,)), ((), ())), preferred_element_type=F32)
              for _, cols, rows, nk in work]
        log_betas, sums = [], []
        for z, (_, _, _, nk) in zip(zs, work):
            log_beta = jnp.minimum(z, 0.0) - jnp.log2(1.0 + jnp.exp2(-jnp.abs(z)))
            l = log_beta - z
            cs = []
            for sg in range(nk):
                l_seg = l[:, segs[sg]]
                if diagonal and sg == nk - 1:
                    l_seg = jnp.where(causal, l_seg, 0.0)
                cs.append(jnp.dot(l_seg.astype(BF16), cum_op, preferred_element_type=F32))
            log_betas.append(log_beta)
            sums.append(cs)
        top = None
        for (p, cols, rows, nk), log_beta, cs in zip(work, log_betas, sums):
            carry = carry_ref[p, rows, :]
            a_parts = [None] * nk
            for sg in range(nk - 1, -1, -1):
                a_seg = jnp.exp2(log_beta[:, segs[sg]] + (cs[sg][:, :LANES] + carry))
                if diagonal and sg == nk - 1:
                    a_seg = jnp.where(causal, a_seg, 0.0)
                a_parts[sg] = a_seg
                carry = carry + cs[sg][:, LANES:]
            a = jnp.concatenate(a_parts, axis=1) if nk > 1 else a_parts[0]
            acc_ref[p, rows, :] += jnp.dot(
                a.astype(BF16), v_ref[pl.ds(start, nk * LANES), cols],
                preferred_element_type=F32)
            carry_ref[p, rows, :] = carry
            top = jnp.max(carry) if top is None else jnp.maximum(top, jnp.max(carry))
        return top

    carry_ref[...] = jnp.zeros_like(carry_ref)
    acc_ref[...] = jnp.zeros_like(acc_ref)
    top = block(qi, True)

    def cond(state):
        kb, top = state
        return jnp.logical_and(kb >= 0, top > EXP2_UNDERFLOW)

    def body(state):
        kb, _ = state
        return kb - 1, block(kb, False)

    lax.while_loop(cond, body, (qi - 1, top))
    for p in range(hp):
        o_ref[:, p * HEAD_DIM:(p + 1) * HEAD_DIM] = acc_ref[p].astype(o_ref.dtype)


def _attn(q, k, v, weights, batch, *, tb, hp):
    nq = SEQ // tb
    groups = SB_HEADS // hp
    width = hp * HEAD_DIM
    steps = batch * groups * nq
    assert all(w.shape[0] % steps == 0 for w in weights)
    cast_specs = [pl.BlockSpec((w.shape[0] // steps, w.shape[1]),
                               lambda b, h, i: ((b * groups + h) * nq + i, 0))
                  for w in weights]
    outs = pl.pallas_call(
        functools.partial(_attn_kernel, tb=tb, hp=hp, n_cast=len(weights)),
        grid=(batch, groups, nq),
        in_specs=[
            pl.BlockSpec((tb, width), lambda b, h, i: (b * nq + i, h)),
            pl.BlockSpec((SEQ, width), lambda b, h, i: (b, h)),
            pl.BlockSpec((SEQ, width), lambda b, h, i: (b, h)),
        ] + cast_specs,
        out_specs=[pl.BlockSpec((tb, width), lambda b, h, i: (b * nq + i, h))] + cast_specs,
        out_shape=[jax.ShapeDtypeStruct((batch * SEQ, SB_WIDTH), BF16)]
        + [jax.ShapeDtypeStruct(w.shape, BF16) for w in weights],
        scratch_shapes=[pltpu.VMEM((hp, tb, LANES), F32),
                        pltpu.VMEM((hp, tb, HEAD_DIM), F32)],
        compiler_params=_params("arbitrary", "arbitrary", "arbitrary"),
        name="attn",
    )(q, k, v, *weights)
    return outs[0], outs[1:]


MIX_CHUNK = 256


def _mix_kernel(pa_ref, at_ref, sa_ref, sb_ref, x_ref, mod_ref, nw_ref, wa_ref, wb_ref, wo_ref,
                cast_ref, x1_ref, h2_ref, cast_out_ref, merged_scr, x1_scr):
    i = pl.program_id(0)
    tiles = pl.num_programs(0) - 1
    tm, d = x_ref.shape

    def step(norm_previous, matmuls):
        if norm_previous:
            b = (i - 1) * tm // SEQ
            gain = nw_ref[...] * (1.0 + _mod_row(mod_ref, SCALE2, b))
            h2 = _norm_modulate(x1_scr[...], gain, _mod_row(mod_ref, SHIFT2, b))
            if not matmuls:
                h2_ref[...] = h2
                return
            merged_scr[...] = h2
        cast_out_ref[...] = cast_ref[...].astype(BF16)
        pa = pa_ref[...]
        at = at_ref[...]
        for n0 in range(0, d, MIX_CHUNK):
            cols = slice(n0, n0 + MIX_CHUNK)
            if norm_previous:
                h2_ref[:, cols] = merged_scr[:, cols]
            ya = jnp.dot(pa, wa_ref[:, cols], preferred_element_type=F32)
            yb = jnp.dot(at, wb_ref[:, cols], preferred_element_type=F32)
            merged = sa_ref[:, cols].astype(F32) * ya + sb_ref[:, cols].astype(F32) * yb
            merged_scr[:, cols] = merged.astype(BF16)
        merged = merged_scr[...]
        gate = _mod_row(mod_ref, GATE1, i * tm // SEQ)
        for n0 in range(0, d, MIX_CHUNK):
            cols = slice(n0, n0 + MIX_CHUNK)
            o = jnp.dot(merged, wo_ref[:, cols], preferred_element_type=F32)
            x1 = x_ref[:, cols] + gate[:, cols] * o
            x1_ref[:, cols] = x1
            x1_scr[:, cols] = x1

    pl.when(i == 0)(functools.partial(step, False, True))
    pl.when(jnp.logical_and(i > 0, i < tiles))(functools.partial(step, True, True))
    pl.when(i == tiles)(functools.partial(step, True, False))


def _mix(pa, at, sa, sb, x2, mod, norm_w, wa_bf, wb_bf, wo_bf, w_cast, *, tm):
    m, d = x2.shape
    kdim = pa.shape[1]
    tiles = m // tm
    assert m % tm == 0 and SEQ % tm == 0 and w_cast.shape[0] % tiles == 0

    def tile(i):
        return jnp.minimum(i, tiles - 1)

    cast_spec = pl.BlockSpec((w_cast.shape[0] // tiles, w_cast.shape[1]),
                             lambda i: (tile(i), 0))
    resident = pl.Buffered(1)
    return pl.pallas_call(
        _mix_kernel,
        grid=(tiles + 1,),
        in_specs=[
            pl.BlockSpec((tm, kdim), lambda i: (tile(i), 0)),
            pl.BlockSpec((tm, kdim), lambda i: (tile(i), 0)),
            pl.BlockSpec((tm, d), lambda i: (tile(i), 0)),
            pl.BlockSpec((tm, d), lambda i: (tile(i), 0)),
            pl.BlockSpec((tm, d), lambda i: (tile(i), 0)),
            pl.BlockSpec(mod.shape, lambda i: (0, 0, 0)),
            pl.BlockSpec((1, d), lambda i: (0, 0)),
            pl.BlockSpec((kdim, d), lambda i: (0, 0), pipeline_mode=resident),
            pl.BlockSpec((kdim, d), lambda i: (0, 0), pipeline_mode=resident),
            pl.BlockSpec((d, d), lambda i: (0, 0), pipeline_mode=resident),
            cast_spec,
        ],
        out_specs=[pl.BlockSpec((tm, d), lambda i: (tile(i), 0)),
                   pl.BlockSpec((tm, d), lambda i: (jnp.maximum(i - 1, 0), 0)), cast_spec],
        out_shape=[jax.ShapeDtypeStruct((m, d), F32),
                   jax.ShapeDtypeStruct((m, d), BF16),
                   jax.ShapeDtypeStruct(w_cast.shape, BF16)],
        scratch_shapes=[pltpu.VMEM((tm, d), BF16), pltpu.VMEM((tm, d), F32)],
        compiler_params=_params("arbitrary"),
        name="mix",
    )(pa, at, sa, sb, x2, mod, norm_w.reshape(1, d), wa_bf, wb_bf, wo_bf, w_cast)


def kernel(x, c, w_ada, b_ada, norm1_w, w_in, q_norm_w, k_norm_w, w_pool, pool_scale,
           w_a_up, w_b_up, w_o, norm2_w, w_ff1, w_ff2):
    batch, seq, d = x.shape
    assert (seq, d) == (SEQ, D_MODEL) and w_ada.shape[0] == 1
    x2 = x.reshape(batch * seq, d)

    mod = _ada(c, w_ada[0], b_ada[0])
    u, q, k, v, sa, sb, w1_bf = _in_proj(x2, mod, norm1_w[0], w_in[0], q_norm_w[0],
                                         k_norm_w[0], w_ff1[0], tm=2048)
    pa = _pool(u, w_pool[0], pool_scale[0], batch)
    at, (wa_bf, wb_bf, wo_bf) = _attn(
        q, k, v, (w_a_up[0], w_b_up[0], w_o[0]), batch,
        tb=256, hp=8)
    x1, h2, w2_bf = _mix(pa, at, sa, sb, x2, mod, norm2_w[0], wa_bf, wb_bf, wo_bf, w_ff2[0],
                         tm=256)
    out = _ffn(h2, x1, mod, w1_bf, w2_bf, tm=1024, tf=1024)
    return out.reshape(batch, seq, d)
```

```python
import functools
import math

import jax
import jax.numpy as jnp
from jax import lax
from jax.experimental import pallas as pl
from jax.experimental.pallas import tpu as pltpu

D_MODEL = 2048
SEQ = 2048
POOL_WIDTH = D_MODEL // 2
POOL_WINDOWS = (2, 4, 8, 16)
POOL_GROUP_DIM = POOL_WIDTH // len(POOL_WINDOWS)
HEAD_DIM = 128
SB_WIDTH = D_MODEL // 2
SB_HEADS = SB_WIDTH // HEAD_DIM
N_MOD = 6
EPS = 1e-6

Q_OFF = POOL_WIDTH
K_OFF = Q_OFF + SB_WIDTH
V_OFF = K_OFF + SB_WIDTH
GA_OFF = V_OFF + SB_WIDTH
GB_OFF = GA_OFF + D_MODEL

LANES = 128
SUBLANES = 8
V7X_VMEM_BYTES = 64 * 1024 * 1024
VMEM_LIMIT_BYTES = V7X_VMEM_BYTES * 7 // 8

BF16 = jnp.bfloat16
F32 = jnp.float32


def _params(*semantics):
    return pltpu.CompilerParams(dimension_semantics=semantics,
                                vmem_limit_bytes=VMEM_LIMIT_BYTES)


ADA_STREAMS = 4


def _ada_kernel(c_ref, b_ref, *refs):
    w_refs, o_ref = refs[:-1], refs[-1]
    c = c_ref[...]
    batch = c.shape[0]
    sc = c * jax.nn.sigmoid(c)
    if batch % SUBLANES:
        pad = SUBLANES - batch % SUBLANES
        sc = jnp.concatenate([sc, jnp.zeros((pad, sc.shape[1]), F32)], axis=0)
    sc = sc.astype(BF16)
    rows = w_refs[0].shape[0]
    acc = b_ref[...]
    for s, w_ref in enumerate(w_refs):
        acc = acc + jnp.dot(sc[:, s * rows:(s + 1) * rows], w_ref[...].astype(BF16),
                            preferred_element_type=F32)[:batch]
    o_ref[0] = acc


def _ada(c, w_ada, b_ada):
    batch, d = c.shape
    n = w_ada.shape[1]
    tn = 1024
    assert n == N_MOD * d and d % tn == 0 and d % ADA_STREAMS == 0
    slab = d // ADA_STREAMS
    per_chunk = d // tn
    return pl.pallas_call(
        _ada_kernel,
        grid=(n // tn,),
        in_specs=[
            pl.BlockSpec((batch, d), lambda j: (0, 0)),
            pl.BlockSpec((1, tn), lambda j: (0, j)),
        ] + [pl.BlockSpec((slab, tn), functools.partial(lambda j, s: (s, j), s=s))
             for s in range(ADA_STREAMS)],
        out_specs=pl.BlockSpec((1, batch, tn), lambda j: (j // per_chunk, 0, j % per_chunk)),
        out_shape=jax.ShapeDtypeStruct((N_MOD, batch, d), F32),
        compiler_params=_params("arbitrary"),
        name="ada",
    )(c, b_ada.reshape(1, n), *([w_ada] * ADA_STREAMS))


SHIFT1, SCALE1, GATE1, SHIFT2, SCALE2, GATE2 = range(N_MOD)


def _mod_row(mod_ref, k, b):
    return mod_ref[k, pl.ds(b, 1), :]


NORM_ROWS = 256


def _norm_modulate(x, gain, shift):
    y = x * lax.rsqrt(jnp.mean(x * x, axis=-1, keepdims=True) + EPS)
    return (y * gain + shift).astype(BF16)


Q_LOGIT_SCALE = math.log2(math.e) / math.sqrt(HEAD_DIM)


def _sigmoid(x):
    return 0.5 * jnp.tanh(0.5 * x) + 0.5


def _head_norm(blk, gain):
    ms = jnp.mean(blk * blk, axis=-1, keepdims=True)
    return (blk * lax.rsqrt(ms + EPS) * gain).astype(BF16)


IN_PROJ_ROWS = 1024
X_SLOTS = 4


def _in_proj_kernel(x_hbm, mod_ref, nw_ref, wu_ref, wq_ref, wk_ref, wv_ref, wga_ref, wgb_ref,
                    qw_ref, kw_ref, cast_ref, u_ref, q_ref, k_ref, v_ref, ga_ref, gb_ref,
                    cast_out_ref, h_scr, xbuf, sems, *, tm):
    cast_out_ref[...] = cast_ref[...].astype(BF16)
    i = pl.program_id(0)
    j = pl.program_id(1)
    chunks = tm // NORM_ROWS

    def x_copy(tile, r):
        slot = r % X_SLOTS
        return pltpu.make_async_copy(
            x_hbm.at[pl.ds(tile * tm + r * NORM_ROWS, NORM_ROWS), :], xbuf.at[slot],
            sems.at[slot])

    @pl.when(jnp.logical_and(j == 0, i == 0))
    def _():
        for r in range(X_SLOTS):
            x_copy(0, r).start()

    @pl.when(j == 0)
    def _():
        b = i * tm // SEQ
        gain = nw_ref[...] * (1.0 + _mod_row(mod_ref, SCALE1, b))
        shift = _mod_row(mod_ref, SHIFT1, b)

        def body(r, carry):
            x_copy(i, r).wait()
            rows = pl.ds(pl.multiple_of(r * NORM_ROWS, NORM_ROWS), NORM_ROWS)
            h_scr[rows, :] = _norm_modulate(xbuf[r % X_SLOTS], gain, shift)

            @pl.when(r + X_SLOTS < chunks)
            def _():
                x_copy(i, r + X_SLOTS).start()

            return carry

        lax.fori_loop(0, chunks, body, 0)

    @pl.when(jnp.logical_and(j == pl.num_programs(1) - 1, i + 1 < pl.num_programs(0)))
    def _():
        for r in range(X_SLOTS):
            x_copy(i + 1, r).start()

    def bf16_cols(*w_refs):
        return jnp.concatenate([w_ref[...].astype(BF16) for w_ref in w_refs], axis=1)

    w_qk, w_ga, w_gb, w_uv = (bf16_cols(wq_ref, wk_ref), bf16_cols(wga_ref),
                              bf16_cols(wgb_ref), bf16_cols(wu_ref, wv_ref))
    q_gain = qw_ref[...] * Q_LOGIT_SCALE
    k_gain = kw_ref[...]
    for r0 in range(0, tm, IN_PROJ_ROWS):
        rows = slice(r0, r0 + IN_PROJ_ROWS)
        h = h_scr[rows, :]
        qk = jnp.dot(h, w_qk, preferred_element_type=F32)
        q_ref[rows, :] = _head_norm(qk[:, :HEAD_DIM], q_gain)
        k_ref[rows, :] = _head_norm(qk[:, HEAD_DIM:], k_gain)
        ga = jnp.dot(h, w_ga, preferred_element_type=F32)
        ga_ref[rows, :] = _sigmoid(ga).astype(BF16)
        gb = jnp.dot(h, w_gb, preferred_element_type=F32)
        gb_ref[rows, :] = _sigmoid(gb).astype(BF16)
        uv = jnp.dot(h, w_uv, preferred_element_type=F32)
        u_ref[rows, :] = uv[:, :HEAD_DIM].astype(BF16)
        v_ref[rows, :] = uv[:, HEAD_DIM:].astype(BF16)


def _in_proj(x2, mod, norm_w, w_in, q_norm_w, k_norm_w, w_cast, *, tm):
    m, d = x2.shape
    steps = (m // tm) * SB_HEADS
    assert w_cast.shape[0] % steps == 0
    cast_spec = pl.BlockSpec((w_cast.shape[0] // steps, w_cast.shape[1]),
                             lambda i, j: (i * SB_HEADS + j, 0))
    hd = HEAD_DIM
    gw = D_MODEL // SB_HEADS
    assert POOL_WIDTH == SB_HEADS * hd and D_MODEL == SB_HEADS * gw
    assert m % tm == 0 and SEQ % tm == 0 and tm % IN_PROJ_ROWS == 0
    assert tm % NORM_ROWS == 0 and tm // NORM_ROWS >= X_SLOTS
    w_spec = lambda width, off: pl.BlockSpec((d, width), lambda i, j: (0, off // width + j))
    out_spec = lambda width: pl.BlockSpec((tm, width), lambda i, j: (i, j))
    return pl.pallas_call(
        functools.partial(_in_proj_kernel, tm=tm),
        grid=(m // tm, SB_HEADS),
        in_specs=[
            pl.BlockSpec(memory_space=pl.ANY),
            pl.BlockSpec(mod.shape, lambda i, j: (0, 0, 0)),
            pl.BlockSpec((1, d), lambda i, j: (0, 0)),
            w_spec(hd, 0), w_spec(hd, Q_OFF), w_spec(hd, K_OFF), w_spec(hd, V_OFF),
            w_spec(gw, GA_OFF), w_spec(gw, GB_OFF),
            pl.BlockSpec((1, hd), lambda i, j: (0, 0)),
            pl.BlockSpec((1, hd), lambda i, j: (0, 0)),
            cast_spec,
        ],
        out_specs=[out_spec(hd), out_spec(hd), out_spec(hd), out_spec(hd),
                   out_spec(gw), out_spec(gw), cast_spec],
        out_shape=[jax.ShapeDtypeStruct((m, POOL_WIDTH), BF16),
                   jax.ShapeDtypeStruct((m, SB_WIDTH), BF16),
                   jax.ShapeDtypeStruct((m, SB_WIDTH), BF16),
                   jax.ShapeDtypeStruct((m, SB_WIDTH), BF16),
                   jax.ShapeDtypeStruct((m, D_MODEL), BF16),
                   jax.ShapeDtypeStruct((m, D_MODEL), BF16),
                   jax.ShapeDtypeStruct(w_cast.shape, BF16)],
        scratch_shapes=[pltpu.VMEM((tm, d), BF16), pltpu.VMEM((X_SLOTS, NORM_ROWS, d), F32),
                        pltpu.SemaphoreType.DMA((X_SLOTS,))],
        compiler_params=_params("arbitrary", "arbitrary"),
        name="in_proj",
    )(x2, mod, norm_w.reshape(1, d), w_in, w_in, w_in, w_in, w_in, w_in,
      q_norm_w.reshape(1, hd), k_norm_w.reshape(1, hd), w_cast)


FFN_OUT_CHUNK = 512


def _ffn_kernel(h_ref, x1_ref, mod_ref, w1_ref, w2_ref, o_ref, *, tm):
    i = pl.program_id(0)
    c = pl.program_id(1)
    xrows = x1_ref.shape[0]

    def step(first):
        rows = pl.ds(pl.multiple_of(c * xrows, xrows), xrows)
        if not first:
            o_ref[rows, :] += x1_ref[...]
        a = jnp.dot(h_ref[...], w1_ref[...], preferred_element_type=F32)
        r = jnp.maximum(a, 0.0)
        act = (r * r).astype(BF16)
        gate = _mod_row(mod_ref, GATE2, i * tm // SEQ)
        for n0 in range(0, o_ref.shape[1], FFN_OUT_CHUNK):
            cols = slice(n0, n0 + FFN_OUT_CHUNK)
            y = gate[:, cols] * jnp.dot(act, w2_ref[:, cols], preferred_element_type=F32)
            o_ref[:, cols] = y if first else o_ref[:, cols] + y
        if first:
            o_ref[rows, :] += x1_ref[...]

    pl.when(c == 0)(functools.partial(step, True))
    pl.when(c != 0)(functools.partial(step, False))


def _ffn(h2, x1, mod, w1, w2, *, tm, tf):
    m, d = h2.shape
    f = w1.shape[1]
    assert m % tm == 0 and SEQ % tm == 0 and f % tf == 0 and tm % (f // tf) == 0
    assert d % FFN_OUT_CHUNK == 0
    chunks = f // tf
    xrows = tm // chunks
    return pl.pallas_call(
        functools.partial(_ffn_kernel, tm=tm),
        grid=(m // tm, chunks),
        in_specs=[
            pl.BlockSpec((tm, d), lambda i, c: (i, 0)),
            pl.BlockSpec((xrows, d), lambda i, c: (i * chunks + c, 0)),
            pl.BlockSpec(mod.shape, lambda i, c: (0, 0, 0)),
            pl.BlockSpec((d, tf), lambda i, c: (0, c)),
            pl.BlockSpec((tf, d), lambda i, c: (c, 0)),
        ],
        out_specs=pl.BlockSpec((tm, d), lambda i, c: (i, 0)),
        out_shape=jax.ShapeDtypeStruct((m, d), F32),
        compiler_params=_params("arbitrary", "arbitrary"),
        name="ffn",
    )(h2, x1, mod, w1, w2)


POOL_ROWS = 256
POOL_HEAD = max(POOL_WINDOWS)
assert POOL_HEAD % (2 * SUBLANES) == 0 and all(w & (w - 1) == 0 for w in POOL_WINDOWS)


def _pool_head_exact(u_head, w):
    pos = lax.broadcasted_iota(jnp.int32, u_head.shape, 0)
    win_sum = u_head
    for k in range(1, w):
        win_sum = win_sum + jnp.where(pos >= k, pltpu.roll(u_head, k, 0), 0.0)
    count = jnp.minimum(pos + 1, w).astype(F32)
    return win_sum / count - u_head


def _pool_kernel(u_ref, w_ref, s_ref, o_ref):
    gd = POOL_GROUP_DIM
    tt = lax.broadcasted_iota(jnp.int32, (POOL_ROWS + POOL_HEAD, POOL_ROWS), 0)
    jj = lax.broadcasted_iota(jnp.int32, (POOL_ROWS + POOL_HEAD, POOL_ROWS), 1)
    for g, w in enumerate(POOL_WINDOWS):
        cols = slice(g * gd, (g + 1) * gd)
        inside = jnp.where(jnp.logical_and(jj <= tt, jj > tt - w), 1.0 / w, 0.0)
        inside = inside - jnp.where(jj == tt, 1.0, 0.0)
        spill = jnp.where(jj > tt - w, 1.0 / w, 0.0)
        band = jnp.where(tt < POOL_ROWS, inside, spill).astype(BF16)
        carry = None
        pooled = []
        for r0 in range(0, SEQ, POOL_ROWS):
            res = jnp.dot(band, u_ref[r0:r0 + POOL_ROWS, cols], preferred_element_type=F32)
            if r0 == 0:
                head = _pool_head_exact(u_ref[0:POOL_HEAD, cols].astype(F32), w)
            else:
                head = res[:POOL_HEAD] + carry
            pooled += [head.astype(BF16), res[POOL_HEAD:POOL_ROWS].astype(BF16)]
            carry = res[POOL_ROWS:]
        mixed = jnp.dot(jnp.concatenate(pooled, axis=0), w_ref[g].astype(BF16),
                        preferred_element_type=F32)
        o_ref[:, cols] = (mixed * s_ref[:, cols]).astype(o_ref.dtype)


def _pool(u, w_pool, pool_scale, batch):
    groups = len(POOL_WINDOWS)
    gd = POOL_GROUP_DIM
    assert SEQ % POOL_ROWS == 0
    return pl.pallas_call(
        _pool_kernel,
        grid=(batch,),
        in_specs=[
            pl.BlockSpec((SEQ, POOL_WIDTH), lambda b: (b, 0)),
            pl.BlockSpec((groups, gd, gd), lambda b: (0, 0, 0)),
            pl.BlockSpec((1, POOL_WIDTH), lambda b: (0, 0)),
        ],
        out_specs=pl.BlockSpec((SEQ, POOL_WIDTH), lambda b: (b, 0)),
        out_shape=jax.ShapeDtypeStruct((batch * SEQ, POOL_WIDTH), BF16),
        compiler_params=_params("arbitrary"),
        name="pool",
    )(u, w_pool, pool_scale.reshape(1, POOL_WIDTH))


EXP2_UNDERFLOW = -151.0


def _attn_kernel(q_ref, k_ref, v_ref, *refs, tb, hp, n_cast):
    w_refs, o_ref, wbf_refs = refs[:n_cast], refs[n_cast], refs[n_cast + 1:2 * n_cast + 1]
    carry_ref, acc_ref = refs[2 * n_cast + 1:]
    for w_ref, wbf_ref in zip(w_refs, wbf_refs):
        wbf_ref[...] = w_ref[...].astype(BF16)

    qi = pl.program_id(2)
    nseg = tb // LANES

    r = lax.broadcasted_iota(jnp.int32, (LANES, 2 * LANES), 0)
    c = lax.broadcasted_iota(jnp.int32, (LANES, 2 * LANES), 1)
    cum_op = jnp.where(jnp.logical_or(c >= LANES, r > c), 1.0, 0.0).astype(BF16)
    tri_r = lax.broadcasted_iota(jnp.int32, (LANES, LANES), 0)
    tri_c = lax.broadcasted_iota(jnp.int32, (LANES, LANES), 1)
    causal = tri_c < tri_r
    segs = [slice(sg * LANES, (sg + 1) * LANES) for sg in range(nseg)]

    def block(kb, diagonal):
        start = pl.multiple_of(kb * tb, tb)
        if diagonal:
            parts = [(segs[g], g + 1) for g in range(nseg)]
        else:
            parts = [(slice(0, tb), nseg)]
        work = [(p, slice(p * HEAD_DIM, (p + 1) * HEAD_DIM), rows, nk)
                for rows, nk in parts for p in range(hp)]
        zs = [lax.dot_general(q_ref[rows, cols], k_ref[pl.ds(start, nk * LANES), cols],
                              (((1,), (1,)), ((), ())), preferred_element_type=F32)
              for _, cols, rows, nk in work]
        log_betas, sums = [], []
        for z, (_, _, _, nk) in zip(zs, work):
            log_beta = jnp.minimum(z, 0.0) - jnp.log2(1.0 + jnp.exp2(-jnp.abs(z)))
            l = log_beta - z
            cs = []
            for sg in range(nk):
                l_seg = l[:, segs[sg]]
                if diagonal and sg == nk - 1:
                    l_seg = jnp.where(causal, l_seg, 0.0)
                cs.append(jnp.dot(l_seg.astype(BF16), cum_op, preferred_element_type=F32))
            log_betas.append(log_beta)
            sums.append(cs)
        top = None
        for (p, cols, rows, nk), log_beta, cs in zip(work, log_betas, sums):
            carry = carry_ref[p, rows, :]
            a_parts = [None] * nk
            for sg in range(nk - 1, -1, -1):
                a_seg = jnp.exp2(log_beta[:, segs[sg]] + (cs[sg][:, :LANES] + carry))
                if diagonal and sg == nk - 1:
                    a_seg = jnp.where(causal, a_seg, 0.0)
                a_parts[sg] = a_seg
                carry = carry + cs[sg][:, LANES:]
            a = jnp.concatenate(a_parts, axis=1) if nk > 1 else a_parts[0]
            acc_ref[p, rows, :] += jnp.dot(
                a.astype(BF16), v_ref[pl.ds(start, nk * LANES), cols],
                preferred_element_type=F32)
            carry_ref[p, rows, :] = carry
            top = jnp.max(carry) if top is None else jnp.maximum(top, jnp.max(carry))
        return top

    carry_ref[...] = jnp.zeros_like(carry_ref)
    acc_ref[...] = jnp.zeros_like(acc_ref)
    top = block(qi, True)

    def cond(state):
        kb, top = state
        return jnp.logical_and(kb >= 0, top > EXP2_UNDERFLOW)

    def body(state):
        kb, _ = state
        return kb - 1, block(kb, False)

    lax.while_loop(cond, body, (qi - 1, top))
    for p in range(hp):
        o_ref[:, p * HEAD_DIM:(p + 1) * HEAD_DIM] = acc_ref[p].astype(o_ref.dtype)


def _attn(q, k, v, weights, batch, *, tb, hp):
    nq = SEQ // tb
    groups = SB_HEADS // hp
    width = hp * HEAD_DIM
    steps = batch * groups * nq
    assert all(w.shape[0] % steps == 0 for w in weights)
    cast_specs = [pl.BlockSpec((w.shape[0] // steps, w.shape[1]),
                               lambda b, h, i: ((b * groups + h) * nq + i, 0))
                  for w in weights]
    outs = pl.pallas_call(
        functools.partial(_attn_kernel, tb=tb, hp=hp, n_cast=len(weights)),
        grid=(batch, groups, nq),
        in_specs=[
            pl.BlockSpec((tb, width), lambda b, h, i: (b * nq + i, h)),
            pl.BlockSpec((SEQ, width), lambda b, h, i: (b, h)),
            pl.BlockSpec((SEQ, width), lambda b, h, i: (b, h)),
        ] + cast_specs,
        out_specs=[pl.BlockSpec((tb, width), lambda b, h, i: (b * nq + i, h))] + cast_specs,
        out_shape=[jax.ShapeDtypeStruct((batch * SEQ, SB_WIDTH), BF16)]
        + [jax.ShapeDtypeStruct(w.shape, BF16) for w in weights],
        scratch_shapes=[pltpu.VMEM((hp, tb, LANES), F32),
                        pltpu.VMEM((hp, tb, HEAD_DIM), F32)],
        compiler_params=_params("arbitrary", "arbitrary", "arbitrary"),
        name="attn",
    )(q, k, v, *weights)
    return outs[0], outs[1:]


MIX_CHUNK = 256


def _mix_kernel(pa_ref, at_ref, sa_ref, sb_ref, x_ref, mod_ref, nw_ref, wa_ref, wb_ref, wo_ref,
                cast_ref, x1_ref, h2_ref, cast_out_ref, merged_scr, x1_scr):
    i = pl.program_id(0)
    tiles = pl.num_programs(0) - 1
    tm, d = x_ref.shape

    def step(norm_previous, matmuls):
        if norm_previous:
            b = (i - 1) * tm // SEQ
            gain = nw_ref[...] * (1.0 + _mod_row(mod_ref, SCALE2, b))
            h2 = _norm_modulate(x1_scr[...], gain, _mod_row(mod_ref, SHIFT2, b))
            if not matmuls:
                h2_ref[...] = h2
                return
            merged_scr[...] = h2
        cast_out_ref[...] = cast_ref[...].astype(BF16)
        pa = pa_ref[...]
        at = at_ref[...]
        for n0 in range(0, d, MIX_CHUNK):
            cols = slice(n0, n0 + MIX_CHUNK)
            if norm_previous:
                h2_ref[:, cols] = merged_scr[:, cols]
            ya = jnp.dot(pa, wa_ref[:, cols], preferred_element_type=F32)
            yb = jnp.dot(at, wb_ref[:, cols], preferred_element_type=F32)
            merged = sa_ref[:, cols].astype(F32) * ya + sb_ref[:, cols].astype(F32) * yb
            merged_scr[:, cols] = merged.astype(BF16)
        merged = merged_scr[...]
        gate = _mod_row(mod_ref, GATE1, i * tm // SEQ)
        for n0 in range(0, d, MIX_CHUNK):
            cols = slice(n0, n0 + MIX_CHUNK)
            o = jnp.dot(merged, wo_ref[:, cols], preferred_element_type=F32)
            x1 = x_ref[:, cols] + gate[:, cols] * o
            x1_ref[:, cols] = x1
            x1_scr[:, cols] = x1

    pl.when(i == 0)(functools.partial(step, False, True))
    pl.when(jnp.logical_and(i > 0, i < tiles))(functools.partial(step, True, True))
    pl.when(i == tiles)(functools.partial(step, True, False))


def _mix(pa, at, sa, sb, x2, mod, norm_w, wa_bf, wb_bf, wo_bf, w_cast, *, tm):
    m, d = x2.shape
    kdim = pa.shape[1]
    tiles = m // tm
    assert m % tm == 0 and SEQ % tm == 0 and w_cast.shape[0] % tiles == 0

    def tile(i):
        return jnp.minimum(i, tiles - 1)

    cast_spec = pl.BlockSpec((w_cast.shape[0] // tiles, w_cast.shape[1]),
                             lambda i: (tile(i), 0))
    resident = pl.Buffered(1)
    return pl.pallas_call(
        _mix_kernel,
        grid=(tiles + 1,),
        in_specs=[
            pl.BlockSpec((tm, kdim), lambda i: (tile(i), 0)),
            pl.BlockSpec((tm, kdim), lambda i: (tile(i), 0)),
            pl.BlockSpec((tm, d), lambda i: (tile(i), 0)),
            pl.BlockSpec((tm, d), lambda i: (tile(i), 0)),
            pl.BlockSpec((tm, d), lambda i: (tile(i), 0)),
            pl.BlockSpec(mod.shape, lambda i: (0, 0, 0)),
            pl.BlockSpec((1, d), lambda i: (0, 0)),
            pl.BlockSpec((kdim, d), lambda i: (0, 0), pipeline_mode=resident),
            pl.BlockSpec((kdim, d), lambda i: (0, 0), pipeline_mode=resident),
            pl.BlockSpec((d, d), lambda i: (0, 0), pipeline_mode=resident),
            cast_spec,
        ],
        out_specs=[pl.BlockSpec((tm, d), lambda i: (tile(i), 0)),
                   pl.BlockSpec((tm, d), lambda i: (jnp.maximum(i - 1, 0), 0)), cast_spec],
        out_shape=[jax.ShapeDtypeStruct((m, d), F32),
                   jax.ShapeDtypeStruct((m, d), BF16),
                   jax.ShapeDtypeStruct(w_cast.shape, BF16)],
        scratch_shapes=[pltpu.VMEM((tm, d), BF16), pltpu.VMEM((tm, d), F32)],
        compiler_params=_params("arbitrary"),
        name="mix",
    )(pa, at, sa, sb, x2, mod, norm_w.reshape(1, d), wa_bf, wb_bf, wo_bf, w_cast)


def kernel(x, c, w_ada, b_ada, norm1_w, w_in, q_norm_w, k_norm_w, w_pool, pool_scale,
           w_a_up, w_b_up, w_o, norm2_w, w_ff1, w_ff2):
    batch, seq, d = x.shape
    assert (seq, d) == (SEQ, D_MODEL) and w_ada.shape[0] == 1
    x2 = x.reshape(batch * seq, d)

    mod = _ada(c, w_ada[0], b_ada[0])
    u, q, k, v, sa, sb, w1_bf = _in_proj(x2, mod, norm1_w[0], w_in[0], q_norm_w[0],
                                         k_norm_w[0], w_ff1[0], tm=2048)
    pa = _pool(u, w_pool[0], pool_scale[0], batch)
    at, (wa_bf, wb_bf, wo_bf) = _attn(
        q, k, v, (w_a_up[0], w_b_up[0], w_o[0]), batch,
        tb=256, hp=8)
    x1, h2, w2_bf = _mix(pa, at, sa, sb, x2, mod, norm2_w[0], wa_bf, wb_bf, wo_bf, w_ff2[0],
                         tm=256)
    out = _ffn(h2, x1, mod, w1_bf, w2_bf, tm=1024, tf=1024)
    return out.reshape(batch, seq, d)
```

```python
import functools
import math

import jax
import jax.numpy as jnp
from jax import lax
from jax.experimental import pallas as pl
from jax.experimental.pallas import tpu as pltpu

D_MODEL = 2048
SEQ = 2048
POOL_WIDTH = D_MODEL // 2
POOL_WINDOWS = (2, 4, 8, 16)
POOL_GROUP_DIM = POOL_WIDTH // len(POOL_WINDOWS)
HEAD_DIM = 128
SB_WIDTH = D_MODEL // 2
SB_HEADS = SB_WIDTH // HEAD_DIM
N_MOD = 6
EPS = 1e-6

Q_OFF = POOL_WIDTH
K_OFF = Q_OFF + SB_WIDTH
V_OFF = K_OFF + SB_WIDTH
GA_OFF = V_OFF + SB_WIDTH
GB_OFF = GA_OFF + D_MODEL

LANES = 128
SUBLANES = 8
V7X_VMEM_BYTES = 64 * 1024 * 1024
VMEM_LIMIT_BYTES = V7X_VMEM_BYTES * 7 // 8

BF16 = jnp.bfloat16
F32 = jnp.float32


def _params(*semantics):
    return pltpu.CompilerParams(dimension_semantics=semantics,
                                vmem_limit_bytes=VMEM_LIMIT_BYTES)


ADA_STREAMS = 4


def _ada_kernel(c_ref, b_ref, *refs):
    w_refs, o_ref = refs[:-1], refs[-1]
    c = c_ref[...]
    batch = c.shape[0]
    sc = c * jax.nn.sigmoid(c)
    if batch % SUBLANES:
        pad = SUBLANES - batch % SUBLANES
        sc = jnp.concatenate([sc, jnp.zeros((pad, sc.shape[1]), F32)], axis=0)
    sc = sc.astype(BF16)
    rows = w_refs[0].shape[0]
    acc = b_ref[...]
    for s, w_ref in enumerate(w_refs):
        acc = acc + jnp.dot(sc[:, s * rows:(s + 1) * rows], w_ref[...].astype(BF16),
                            preferred_element_type=F32)[:batch]
    o_ref[0] = acc


def _ada(c, w_ada, b_ada):
    batch, d = c.shape
    n = w_ada.shape[1]
    tn = 1024
    assert n == N_MOD * d and d % tn == 0 and d % ADA_STREAMS == 0
    slab = d // ADA_STREAMS
    per_chunk = d // tn
    return pl.pallas_call(
        _ada_kernel,
        grid=(n // tn,),
        in_specs=[
            pl.BlockSpec((batch, d), lambda j: (0, 0)),
            pl.BlockSpec((1, tn), lambda j: (0, j)),
        ] + [pl.BlockSpec((slab, tn), functools.partial(lambda j, s: (s, j), s=s))
             for s in range(ADA_STREAMS)],
        out_specs=pl.BlockSpec((1, batch, tn), lambda j: (j // per_chunk, 0, j % per_chunk)),
        out_shape=jax.ShapeDtypeStruct((N_MOD, batch, d), F32),
        compiler_params=_params("arbitrary"),
        name="ada",
    )(c, b_ada.reshape(1, n), *([w_ada] * ADA_STREAMS))


SHIFT1, SCALE1, GATE1, SHIFT2, SCALE2, GATE2 = range(N_MOD)


def _mod_row(mod_ref, k, b):
    return mod_ref[k, pl.ds(b, 1), :]


NORM_ROWS = 256


def _norm_modulate(x, gain, shift):
    y = x * lax.rsqrt(jnp.mean(x * x, axis=-1, keepdims=True) + EPS)
    return (y * gain + shift).astype(BF16)


Q_LOGIT_SCALE = math.log2(math.e) / math.sqrt(HEAD_DIM)


def _sigmoid(x):
    return 0.5 * jnp.tanh(0.5 * x) + 0.5


def _head_norm(blk, gain):
    ms = jnp.mean(blk * blk, axis=-1, keepdims=True)
    return (blk * lax.rsqrt(ms + EPS) * gain).astype(BF16)


IN_PROJ_ROWS = 1024
X_SLOTS = 4
GATE_BLOCK = D_MODEL // SB_HEADS


def _in_proj_kernel(x_hbm, mod_ref, nw_ref, wu_ref, wq_ref, wk_ref, wv_ref, wga_ref, wgb_ref,
                    qw_ref, kw_ref, cast_ref, u_ref, q_ref, k_ref, v_ref, g_ref,
                    cast_out_ref, h_scr, xbuf, sems, *, tm):
    cast_out_ref[...] = cast_ref[...].astype(BF16)
    i = pl.program_id(0)
    j = pl.program_id(1)
    chunks = tm // NORM_ROWS

    def x_copy(tile, r):
        slot = r % X_SLOTS
        return pltpu.make_async_copy(
            x_hbm.at[pl.ds(tile * tm + r * NORM_ROWS, NORM_ROWS), :], xbuf.at[slot],
            sems.at[slot])

    @pl.when(jnp.logical_and(j == 0, i == 0))
    def _():
        for r in range(X_SLOTS):
            x_copy(0, r).start()

    @pl.when(j == 0)
    def _():
        b = i * tm // SEQ
        gain = nw_ref[...] * (1.0 + _mod_row(mod_ref, SCALE1, b))
        shift = _mod_row(mod_ref, SHIFT1, b)

        def body(r, carry):
            x_copy(i, r).wait()
            rows = pl.ds(pl.multiple_of(r * NORM_ROWS, NORM_ROWS), NORM_ROWS)
            h_scr[rows, :] = _norm_modulate(xbuf[r % X_SLOTS], gain, shift)

            @pl.when(r + X_SLOTS < chunks)
            def _():
                x_copy(i, r + X_SLOTS).start()

            return carry

        lax.fori_loop(0, chunks, body, 0)

    @pl.when(jnp.logical_and(j == pl.num_programs(1) - 1, i + 1 < pl.num_programs(0)))
    def _():
        for r in range(X_SLOTS):
            x_copy(i + 1, r).start()

    def bf16_cols(*w_refs):
        return jnp.concatenate([w_ref[...].astype(BF16) for w_ref in w_refs], axis=1)

    w_qk, w_ga, w_gb, w_uv = (bf16_cols(wq_ref, wk_ref), bf16_cols(wga_ref),
                              bf16_cols(wgb_ref), bf16_cols(wu_ref, wv_ref))
    q_gain = qw_ref[...] * Q_LOGIT_SCALE
    k_gain = kw_ref[...]
    for r0 in range(0, tm, IN_PROJ_ROWS):
        rows = slice(r0, r0 + IN_PROJ_ROWS)
        h = h_scr[rows, :]
        qk = jnp.dot(h, w_qk, preferred_element_type=F32)
        q_ref[rows, :] = _head_norm(qk[:, :HEAD_DIM], q_gain)
        k_ref[rows, :] = _head_norm(qk[:, HEAD_DIM:], k_gain)
        ga = jnp.dot(h, w_ga, preferred_element_type=F32)
        g_ref[rows, :GATE_BLOCK] = _sigmoid(ga).astype(BF16)
        gb = jnp.dot(h, w_gb, preferred_element_type=F32)
        g_ref[rows, GATE_BLOCK:] = _sigmoid(gb).astype(BF16)
        uv = jnp.dot(h, w_uv, preferred_element_type=F32)
        u_ref[rows, :] = uv[:, :HEAD_DIM].astype(BF16)
        v_ref[rows, :] = uv[:, HEAD_DIM:].astype(BF16)


def _in_proj(x2, mod, norm_w, w_in, q_norm_w, k_norm_w, w_cast, *, tm):
    m, d = x2.shape
    steps = (m // tm) * SB_HEADS
    assert w_cast.shape[0] % steps == 0
    cast_spec = pl.BlockSpec((w_cast.shape[0] // steps, w_cast.shape[1]),
                             lambda i, j: (i * SB_HEADS + j, 0))
    hd = HEAD_DIM
    gw = GATE_BLOCK
    assert POOL_WIDTH == SB_HEADS * hd and D_MODEL == SB_HEADS * gw
    assert m % tm == 0 and SEQ % tm == 0 and tm % IN_PROJ_ROWS == 0
    assert tm % NORM_ROWS == 0 and tm // NORM_ROWS >= X_SLOTS
    w_spec = lambda width, off: pl.BlockSpec((d, width), lambda i, j: (0, off // width + j))
    out_spec = lambda width: pl.BlockSpec((tm, width), lambda i, j: (i, j))
    return pl.pallas_call(
        functools.partial(_in_proj_kernel, tm=tm),
        grid=(m // tm, SB_HEADS),
        in_specs=[
            pl.BlockSpec(memory_space=pl.ANY),
            pl.BlockSpec(mod.shape, lambda i, j: (0, 0, 0)),
            pl.BlockSpec((1, d), lambda i, j: (0, 0)),
            w_spec(hd, 0), w_spec(hd, Q_OFF), w_spec(hd, K_OFF), w_spec(hd, V_OFF),
            w_spec(gw, GA_OFF), w_spec(gw, GB_OFF),
            pl.BlockSpec((1, hd), lambda i, j: (0, 0)),
            pl.BlockSpec((1, hd), lambda i, j: (0, 0)),
            cast_spec,
        ],
        out_specs=[out_spec(hd), out_spec(hd), out_spec(hd), out_spec(hd),
                   out_spec(2 * gw), cast_spec],
        out_shape=[jax.ShapeDtypeStruct((m, POOL_WIDTH), BF16),
                   jax.ShapeDtypeStruct((m, SB_WIDTH), BF16),
                   jax.ShapeDtypeStruct((m, SB_WIDTH), BF16),
                   jax.ShapeDtypeStruct((m, SB_WIDTH), BF16),
                   jax.ShapeDtypeStruct((m, 2 * D_MODEL), BF16),
                   jax.ShapeDtypeStruct(w_cast.shape, BF16)],
        scratch_shapes=[pltpu.VMEM((tm, d), BF16), pltpu.VMEM((X_SLOTS, NORM_ROWS, d), F32),
                        pltpu.SemaphoreType.DMA((X_SLOTS,))],
        compiler_params=_params("arbitrary", "arbitrary"),
        name="in_proj",
    )(x2, mod, norm_w.reshape(1, d), w_in, w_in, w_in, w_in, w_in, w_in,
      q_norm_w.reshape(1, hd), k_norm_w.reshape(1, hd), w_cast)


FFN_OUT_CHUNK = 512


def _ffn_kernel(h_ref, x1_ref, mod_ref, w1_ref, w2_ref, o_ref, *, tm):
    i = pl.program_id(0)
    c = pl.program_id(1)
    xrows = x1_ref.shape[0]

    def step(first):
        rows = pl.ds(pl.multiple_of(c * xrows, xrows), xrows)
        if not first:
            o_ref[rows, :] += x1_ref[...]
        a = jnp.dot(h_ref[...], w1_ref[...], preferred_element_type=F32)
        r = jnp.maximum(a, 0.0)
        act = (r * r).astype(BF16)
        gate = _mod_row(mod_ref, GATE2, i * tm // SEQ)
        for n0 in range(0, o_ref.shape[1], FFN_OUT_CHUNK):
            cols = slice(n0, n0 + FFN_OUT_CHUNK)
            y = gate[:, cols] * jnp.dot(act, w2_ref[:, cols], preferred_element_type=F32)
            o_ref[:, cols] = y if first else o_ref[:, cols] + y
        if first:
            o_ref[rows, :] += x1_ref[...]

    pl.when(c == 0)(functools.partial(step, True))
    pl.when(c != 0)(functools.partial(step, False))


def _ffn(h2, x1, mod, w1, w2, *, tm, tf):
    m, d = h2.shape
    f = w1.shape[1]
    assert m % tm == 0 and SEQ % tm == 0 and f % tf == 0 and tm % (f // tf) == 0
    assert d % FFN_OUT_CHUNK == 0
    chunks = f // tf
    xrows = tm // chunks
    return pl.pallas_call(
        functools.partial(_ffn_kernel, tm=tm),
        grid=(m // tm, chunks),
        in_specs=[
            pl.BlockSpec((tm, d), lambda i, c: (i, 0)),
            pl.BlockSpec((xrows, d), lambda i, c: (i * chunks + c, 0)),
            pl.BlockSpec(mod.shape, lambda i, c: (0, 0, 0)),
            pl.BlockSpec((d, tf), lambda i, c: (0, c)),
            pl.BlockSpec((tf, d), lambda i, c: (c, 0)),
        ],
        out_specs=pl.BlockSpec((tm, d), lambda i, c: (i, 0)),
        out_shape=jax.ShapeDtypeStruct((m, d), F32),
        compiler_params=_params("arbitrary", "arbitrary"),
        name="ffn",
    )(h2, x1, mod, w1, w2)


POOL_ROWS = 256
POOL_HEAD = max(POOL_WINDOWS)
assert POOL_HEAD % (2 * SUBLANES) == 0 and all(w & (w - 1) == 0 for w in POOL_WINDOWS)


def _pool_head_exact(u_head, w):
    pos = lax.broadcasted_iota(jnp.int32, u_head.shape, 0)
    win_sum = u_head
    for k in range(1, w):
        win_sum = win_sum + jnp.where(pos >= k, pltpu.roll(u_head, k, 0), 0.0)
    count = jnp.minimum(pos + 1, w).astype(F32)
    return win_sum / count - u_head


def _pool_kernel(u_ref, w_ref, s_ref, o_ref):
    gd = POOL_GROUP_DIM
    tt = lax.broadcasted_iota(jnp.int32, (POOL_ROWS + POOL_HEAD, POOL_ROWS), 0)
    jj = lax.broadcasted_iota(jnp.int32, (POOL_ROWS + POOL_HEAD, POOL_ROWS), 1)
    for g, w in enumerate(POOL_WINDOWS):
        cols = slice(g * gd, (g + 1) * gd)
        inside = jnp.where(jnp.logical_and(jj <= tt, jj > tt - w), 1.0 / w, 0.0)
        inside = inside - jnp.where(jj == tt, 1.0, 0.0)
        spill = jnp.where(jj > tt - w, 1.0 / w, 0.0)
        band = jnp.where(tt < POOL_ROWS, inside, spill).astype(BF16)
        carry = None
        pooled = []
        for r0 in range(0, SEQ, POOL_ROWS):
            res = jnp.dot(band, u_ref[r0:r0 + POOL_ROWS, cols], preferred_element_type=F32)
            if r0 == 0:
                head = _pool_head_exact(u_ref[0:POOL_HEAD, cols].astype(F32), w)
            else:
                head = res[:POOL_HEAD] + carry
            pooled += [head.astype(BF16), res[POOL_HEAD:POOL_ROWS].astype(BF16)]
            carry = res[POOL_ROWS:]
        mixed = jnp.dot(jnp.concatenate(pooled, axis=0), w_ref[g].astype(BF16),
                        preferred_element_type=F32)
        o_ref[:, cols] = (mixed * s_ref[:, cols]).astype(o_ref.dtype)


def _pool(u, w_pool, pool_scale, batch):
    groups = len(POOL_WINDOWS)
    gd = POOL_GROUP_DIM
    assert SEQ % POOL_ROWS == 0
    return pl.pallas_call(
        _pool_kernel,
        grid=(batch,),
        in_specs=[
            pl.BlockSpec((SEQ, POOL_WIDTH), lambda b: (b, 0)),
            pl.BlockSpec((groups, gd, gd), lambda b: (0, 0, 0)),
            pl.BlockSpec((1, POOL_WIDTH), lambda b: (0, 0)),
        ],
        out_specs=pl.BlockSpec((SEQ, POOL_WIDTH), lambda b: (b, 0)),
        out_shape=jax.ShapeDtypeStruct((batch * SEQ, POOL_WIDTH), BF16),
        compiler_params=_params("arbitrary"),
        name="pool",
    )(u, w_pool, pool_scale.reshape(1, POOL_WIDTH))


EXP2_UNDERFLOW = -151.0


def _attn_kernel(q_ref, k_ref, v_ref, *refs, tb, hp, cast_steps):
    n_cast = len(cast_steps)
    w_refs, o_ref, wbf_refs = refs[:n_cast], refs[n_cast], refs[n_cast + 1:2 * n_cast + 1]
    carry_ref, acc_ref = refs[2 * n_cast + 1:]
    qi = pl.program_id(2)
    step = (pl.program_id(0) * pl.num_programs(1) + pl.program_id(1)) * pl.num_programs(2) + qi
    for (first, count), w_ref, wbf_ref in zip(cast_steps, w_refs, wbf_refs):
        @pl.when(jnp.logical_and(step >= first, step < first + count))
        def _(w_ref=w_ref, wbf_ref=wbf_ref):
            wbf_ref[...] = w_ref[...].astype(BF16)

    nseg = tb // LANES

    r = lax.broadcasted_iota(jnp.int32, (LANES, 2 * LANES), 0)
    c = lax.broadcasted_iota(jnp.int32, (LANES, 2 * LANES), 1)
    cum_op = jnp.where(jnp.logical_or(c >= LANES, r > c), 1.0, 0.0).astype(BF16)
    tri_r = lax.broadcasted_iota(jnp.int32, (LANES, LANES), 0)
    tri_c = lax.broadcasted_iota(jnp.int32, (LANES, LANES), 1)
    causal = tri_c < tri_r
    segs = [slice(sg * LANES, (sg + 1) * LANES) for sg in range(nseg)]

    def block(kb, diagonal):
        start = pl.multiple_of(kb * tb, tb)
        if diagonal:
            parts = [(segs[g], g + 1) for g in range(nseg)]
        else:
            parts = [(slice(0, tb), nseg)]
        work = [(p, slice(p * HEAD_DIM, (p + 1) * HEAD_DIM), rows, nk)
                for rows, nk in parts for p in range(hp)]
        zs = [lax.dot_general(q_ref[rows, cols], k_ref[pl.ds(start, nk * LANES), cols],
                              (((1,), (1,)), ((), ())), preferred_element_type=F32)
              for _, cols, rows, nk in work]
        log_betas, sums = [], []
        for z, (_, _, _, nk) in zip(zs, work):
            log_beta = jnp.minimum(z, 0.0) - jnp.log2(1.0 + jnp.exp2(-jnp.abs(z)))
            l = log_beta - z
            cs = []
            for sg in range(nk):
                l_seg = l[:, segs[sg]]
                if diagonal and sg == nk - 1:
                    l_seg = jnp.where(causal, l_seg, 0.0)
                cs.append(jnp.dot(l_seg.astype(BF16), cum_op, preferred_element_type=F32))
            log_betas.append(log_beta)
            sums.append(cs)
        top = None
        for (p, cols, rows, nk), log_beta, cs in zip(work, log_betas, sums):
            carry = carry_ref[p, rows, :]
            a_parts = [None] * nk
            for sg in range(nk - 1, -1, -1):
                a_seg = jnp.exp2(log_beta[:, segs[sg]] + (cs[sg][:, :LANES] + carry))
                if diagonal and sg == nk - 1:
                    a_seg = jnp.where(causal, a_seg, 0.0)
                a_parts[sg] = a_seg
                carry = carry + cs[sg][:, LANES:]
            a = jnp.concatenate(a_parts, axis=1) if nk > 1 else a_parts[0]
            acc_ref[p, rows, :] += jnp.dot(
                a.astype(BF16), v_ref[pl.ds(start, nk * LANES), cols],
                preferred_element_type=F32)
            carry_ref[p, rows, :] = carry
            top = jnp.max(carry) if top is None else jnp.maximum(top, jnp.max(carry))
        return top

    carry_ref[...] = jnp.zeros_like(carry_ref)
    acc_ref[...] = jnp.zeros_like(acc_ref)
    top = block(qi, True)

    def cond(state):
        kb, top = state
        return jnp.logical_and(kb >= 0, top > EXP2_UNDERFLOW)

    def body(state):
        kb, _ = state
        return kb - 1, block(kb, False)

    lax.while_loop(cond, body, (qi - 1, top))
    for p in range(hp):
        o_ref[:, p * HEAD_DIM:(p + 1) * HEAD_DIM] = acc_ref[p].astype(o_ref.dtype)


def _attn(q, k, v, weights, batch, *, tb, hp):
    nq = SEQ // tb
    groups = SB_HEADS // hp
    width = hp * HEAD_DIM
    steps = batch * groups * nq
    rows = sum(w.shape[0] for w in weights) // steps
    assert rows * steps == sum(w.shape[0] for w in weights)
    assert all(w.shape[0] % rows == 0 for w in weights)
    counts = [w.shape[0] // rows for w in weights]
    starts = [sum(counts[:n]) for n in range(len(weights))]

    def cast_spec(w, start, count):
        return pl.BlockSpec(
            (rows, w.shape[1]),
            lambda b, h, i: (jnp.clip((b * groups + h) * nq + i - start, 0, count - 1), 0))

    cast_specs = [cast_spec(w, s, n) for w, s, n in zip(weights, starts, counts)]
    outs = pl.pallas_call(
        functools.partial(_attn_kernel, tb=tb, hp=hp, cast_steps=tuple(zip(starts, counts))),
        grid=(batch, groups, nq),
        in_specs=[
            pl.BlockSpec((tb, width), lambda b, h, i: (b * nq + i, h)),
            pl.BlockSpec((SEQ, width), lambda b, h, i: (b, h)),
            pl.BlockSpec((SEQ, width), lambda b, h, i: (b, h)),
        ] + cast_specs,
        out_specs=[pl.BlockSpec((tb, width), lambda b, h, i: (b * nq + i, h))] + cast_specs,
        out_shape=[jax.ShapeDtypeStruct((batch * SEQ, SB_WIDTH), BF16)]
        + [jax.ShapeDtypeStruct(w.shape, BF16) for w in weights],
        scratch_shapes=[pltpu.VMEM((hp, tb, LANES), F32),
                        pltpu.VMEM((hp, tb, HEAD_DIM), F32)],
        compiler_params=_params("arbitrary", "arbitrary", "arbitrary"),
        name="attn",
    )(q, k, v, *weights)
    return outs[0], outs[1:]


MIX_CHUNK = 256


def _mix_kernel(pa_ref, at_ref, g_ref, x_ref, mod_ref, nw_ref, wa_ref, wb_ref, wo_ref,
                cast_ref, x1_ref, h2_ref, cast_out_ref, merged_scr, x1_scr):
    i = pl.program_id(0)
    tiles = pl.num_programs(0) - 1
    tm, d = x_ref.shape

    def step(norm_previous, matmuls):
        if norm_previous:
            b = (i - 1) * tm // SEQ
            gain = nw_ref[...] * (1.0 + _mod_row(mod_ref, SCALE2, b))
            h2 = _norm_modulate(x1_scr[...], gain, _mod_row(mod_ref, SHIFT2, b))
            if not matmuls:
                h2_ref[...] = h2
                return
            merged_scr[...] = h2
        cast_out_ref[...] = cast_ref[...].astype(BF16)
        pa = pa_ref[...]
        at = at_ref[...]
        for n0 in range(0, d, MIX_CHUNK):
            cols = slice(n0, n0 + MIX_CHUNK)
            if norm_previous:
                h2_ref[:, cols] = merged_scr[:, cols]
            ya = jnp.dot(pa, wa_ref[:, cols], preferred_element_type=F32)
            yb = jnp.dot(at, wb_ref[:, cols], preferred_element_type=F32)
            sa = g_ref[:, 2 * n0:2 * n0 + MIX_CHUNK].astype(F32)
            sb = g_ref[:, 2 * n0 + MIX_CHUNK:2 * n0 + 2 * MIX_CHUNK].astype(F32)
            merged = sa * ya + sb * yb
            merged_scr[:, cols] = merged.astype(BF16)
        merged = merged_scr[...]
        gate = _mod_row(mod_ref, GATE1, i * tm // SEQ)
        for n0 in range(0, d, MIX_CHUNK):
            cols = slice(n0, n0 + MIX_CHUNK)
            o = jnp.dot(merged, wo_ref[:, cols], preferred_element_type=F32)
            x1 = x_ref[:, cols] + gate[:, cols] * o
            x1_ref[:, cols] = x1
            x1_scr[:, cols] = x1

    pl.when(i == 0)(functools.partial(step, False, True))
    pl.when(jnp.logical_and(i > 0, i < tiles))(functools.partial(step, True, True))
    pl.when(i == tiles)(functools.partial(step, True, False))


def _mix(pa, at, gates, x2, mod, norm_w, wa_bf, wb_bf, wo_bf, w_cast, *, tm):
    m, d = x2.shape
    kdim = pa.shape[1]
    tiles = m // tm
    assert m % tm == 0 and SEQ % tm == 0 and w_cast.shape[0] % tiles == 0
    assert gates.shape == (m, 2 * d) and MIX_CHUNK == GATE_BLOCK

    def tile(i):
        return jnp.minimum(i, tiles - 1)

    cast_spec = pl.BlockSpec((w_cast.shape[0] // tiles, w_cast.shape[1]),
                             lambda i: (tile(i), 0))
    resident = pl.Buffered(1)
    return pl.pallas_call(
        _mix_kernel,
        grid=(tiles + 1,),
        in_specs=[
            pl.BlockSpec((tm, kdim), lambda i: (tile(i), 0)),
            pl.BlockSpec((tm, kdim), lambda i: (tile(i), 0)),
            pl.BlockSpec((tm, 2 * d), lambda i: (tile(i), 0)),
            pl.BlockSpec((tm, d), lambda i: (tile(i), 0)),
            pl.BlockSpec(mod.shape, lambda i: (0, 0, 0)),
            pl.BlockSpec((1, d), lambda i: (0, 0)),
            pl.BlockSpec((kdim, d), lambda i: (0, 0), pipeline_mode=resident),
            pl.BlockSpec((kdim, d), lambda i: (0, 0), pipeline_mode=resident),
            pl.BlockSpec((d, d), lambda i: (0, 0), pipeline_mode=resident),
            cast_spec,
        ],
        out_specs=[pl.BlockSpec((tm, d), lambda i: (tile(i), 0)),
                   pl.BlockSpec((tm, d), lambda i: (jnp.maximum(i - 1, 0), 0)), cast_spec],
        out_shape=[jax.ShapeDtypeStruct((m, d), F32),
                   jax.ShapeDtypeStruct((m, d), BF16),
                   jax.ShapeDtypeStruct(w_cast.shape, BF16)],
        scratch_shapes=[pltpu.VMEM((tm, d), BF16), pltpu.VMEM((tm, d), F32)],
        compiler_params=_params("arbitrary"),
        name="mix",
    )(pa, at, gates, x2, mod, norm_w.reshape(1, d), wa_bf, wb_bf, wo_bf, w_cast)


def kernel(x, c, w_ada, b_ada, norm1_w, w_in, q_norm_w, k_norm_w, w_pool, pool_scale,
           w_a_up, w_b_up, w_o, norm2_w, w_ff1, w_ff2):
    batch, seq, d = x.shape
    assert (seq, d) == (SEQ, D_MODEL) and w_ada.shape[0] == 1
    x2 = x.reshape(batch * seq, d)

    mod = _ada(c, w_ada[0], b_ada[0])
    u, q, k, v, gates, w1_bf = _in_proj(x2, mod, norm1_w[0], w_in[0], q_norm_w[0],
                                        k_norm_w[0], w_ff1[0], tm=2048)
    pa = _pool(u, w_pool[0], pool_scale[0], batch)
    at, (wa_bf, wb_bf, wo_bf) = _attn(
        q, k, v, (w_a_up[0], w_b_up[0], w_o[0]), batch,
        tb=256, hp=8)
    x1, h2, w2_bf = _mix(pa, at, gates, x2, mod, norm2_w[0], wa_bf, wb_bf, wo_bf, w_ff2[0],
                         tm=256)
    out = _ffn(h2, x1, mod, w1_bf, w2_bf, tm=1024, tf=1024)
    return out.reshape(batch, seq, d)
```

```python
import functools
import math

import jax
import jax.numpy as jnp
from jax import lax
from jax.experimental import pallas as pl
from jax.experimental.pallas import tpu as pltpu

D_MODEL = 2048
SEQ = 2048
POOL_WIDTH = D_MODEL // 2
POOL_WINDOWS = (2, 4, 8, 16)
POOL_GROUP_DIM = POOL_WIDTH // len(POOL_WINDOWS)
HEAD_DIM = 128
SB_WIDTH = D_MODEL // 2
SB_HEADS = SB_WIDTH // HEAD_DIM
N_MOD = 6
EPS = 1e-6

Q_OFF = POOL_WIDTH
K_OFF = Q_OFF + SB_WIDTH
V_OFF = K_OFF + SB_WIDTH
GA_OFF = V_OFF + SB_WIDTH
GB_OFF = GA_OFF + D_MODEL

LANES = 128
SUBLANES = 8
V7X_VMEM_BYTES = 64 * 1024 * 1024
VMEM_LIMIT_BYTES = V7X_VMEM_BYTES * 7 // 8

BF16 = jnp.bfloat16
F32 = jnp.float32


def _params(*semantics):
    return pltpu.CompilerParams(dimension_semantics=semantics,
                                vmem_limit_bytes=VMEM_LIMIT_BYTES)


ADA_STREAMS = 4


def _ada_kernel(c_ref, b_ref, *refs):
    w_refs, o_ref = refs[:-1], refs[-1]
    c = c_ref[...]
    batch = c.shape[0]
    sc = c * jax.nn.sigmoid(c)
    if batch % SUBLANES:
        pad = SUBLANES - batch % SUBLANES
        sc = jnp.concatenate([sc, jnp.zeros((pad, sc.shape[1]), F32)], axis=0)
    sc = sc.astype(BF16)
    rows = w_refs[0].shape[0]
    acc = b_ref[...]
    for s, w_ref in enumerate(w_refs):
        acc = acc + jnp.dot(sc[:, s * rows:(s + 1) * rows], w_ref[...].astype(BF16),
                            preferred_element_type=F32)[:batch]
    o_ref[0] = acc


def _ada(c, w_ada, b_ada):
    batch, d = c.shape
    n = w_ada.shape[1]
    tn = 1024
    assert n == N_MOD * d and d % tn == 0 and d % ADA_STREAMS == 0
    slab = d // ADA_STREAMS
    per_chunk = d // tn
    return pl.pallas_call(
        _ada_kernel,
        grid=(n // tn,),
        in_specs=[
            pl.BlockSpec((batch, d), lambda j: (0, 0)),
            pl.BlockSpec((1, tn), lambda j: (0, j)),
        ] + [pl.BlockSpec((slab, tn), functools.partial(lambda j, s: (s, j), s=s))
             for s in range(ADA_STREAMS)],
        out_specs=pl.BlockSpec((1, batch, tn), lambda j: (j // per_chunk, 0, j % per_chunk)),
        out_shape=jax.ShapeDtypeStruct((N_MOD, batch, d), F32),
        compiler_params=_params("arbitrary"),
        name="ada",
    )(c, b_ada.reshape(1, n), *([w_ada] * ADA_STREAMS))


SHIFT1, SCALE1, GATE1, SHIFT2, SCALE2, GATE2 = range(N_MOD)


def _mod_row(mod_ref, k, b):
    return mod_ref[k, pl.ds(b, 1), :]


NORM_ROWS = 256


def _norm_modulate(x, gain, shift):
    y = x * lax.rsqrt(jnp.mean(x * x, axis=-1, keepdims=True) + EPS)
    return (y * gain + shift).astype(BF16)


Q_LOGIT_SCALE = math.log2(math.e) / math.sqrt(HEAD_DIM)


def _sigmoid(x):
    return 0.5 * jnp.tanh(0.5 * x) + 0.5


def _head_norm(blk, gain):
    ms = jnp.mean(blk * blk, axis=-1, keepdims=True)
    return (blk * lax.rsqrt(ms + EPS) * gain).astype(BF16)


IN_PROJ_ROWS = 1024
X_SLOTS = 4
X_DMA_PRIORITY = 1
GATE_BLOCK = D_MODEL // SB_HEADS


def _in_proj_kernel(x_hbm, mod_ref, nw_ref, wu_ref, wq_ref, wk_ref, wv_ref, wga_ref, wgb_ref,
                    qw_ref, kw_ref, cast_ref, u_ref, q_ref, k_ref, v_ref, g_ref,
                    cast_out_ref, h_scr, xbuf, sems, *, tm):
    cast_out_ref[...] = cast_ref[...].astype(BF16)
    i = pl.program_id(0)
    j = pl.program_id(1)
    chunks = tm // NORM_ROWS

    def x_copy(tile, r):
        slot = r % X_SLOTS
        return pltpu.make_async_copy(
            x_hbm.at[pl.ds(tile * tm + r * NORM_ROWS, NORM_ROWS), :], xbuf.at[slot],
            sems.at[slot])

    @pl.when(jnp.logical_and(j == 0, i == 0))
    def _():
        for r in range(X_SLOTS):
            x_copy(0, r).start(priority=X_DMA_PRIORITY)

    @pl.when(j == 0)
    def _():
        b = i * tm // SEQ
        gain = nw_ref[...] * (1.0 + _mod_row(mod_ref, SCALE1, b))
        shift = _mod_row(mod_ref, SHIFT1, b)

        def body(r, carry):
            x_copy(i, r).wait()
            rows = pl.ds(pl.multiple_of(r * NORM_ROWS, NORM_ROWS), NORM_ROWS)
            h_scr[rows, :] = _norm_modulate(xbuf[r % X_SLOTS], gain, shift)

            @pl.when(r + X_SLOTS < chunks)
            def _():
                x_copy(i, r + X_SLOTS).start(priority=X_DMA_PRIORITY)

            return carry

        lax.fori_loop(0, chunks, body, 0)

    @pl.when(jnp.logical_and(j == pl.num_programs(1) - 1, i + 1 < pl.num_programs(0)))
    def _():
        for r in range(X_SLOTS):
            x_copy(i + 1, r).start(priority=X_DMA_PRIORITY)

    def bf16_cols(*w_refs):
        return jnp.concatenate([w_ref[...].astype(BF16) for w_ref in w_refs], axis=1)

    w_qk, w_ga, w_gb, w_uv = (bf16_cols(wq_ref, wk_ref), bf16_cols(wga_ref),
                              bf16_cols(wgb_ref), bf16_cols(wu_ref, wv_ref))
    q_gain = qw_ref[...] * Q_LOGIT_SCALE
    k_gain = kw_ref[...]
    for r0 in range(0, tm, IN_PROJ_ROWS):
        rows = slice(r0, r0 + IN_PROJ_ROWS)
        h = h_scr[rows, :]
        qk = jnp.dot(h, w_qk, preferred_element_type=F32)
        q_ref[rows, :] = _head_norm(qk[:, :HEAD_DIM], q_gain)
        k_ref[rows, :] = _head_norm(qk[:, HEAD_DIM:], k_gain)
        ga = jnp.dot(h, w_ga, preferred_element_type=F32)
        g_ref[rows, :GATE_BLOCK] = _sigmoid(ga).astype(BF16)
        gb = jnp.dot(h, w_gb, preferred_element_type=F32)
        g_ref[rows, GATE_BLOCK:] = _sigmoid(gb).astype(BF16)
        uv = jnp.dot(h, w_uv, preferred_element_type=F32)
        u_ref[rows, :] = uv[:, :HEAD_DIM].astype(BF16)
        v_ref[rows, :] = uv[:, HEAD_DIM:].astype(BF16)


def _in_proj(x2, mod, norm_w, w_in, q_norm_w, k_norm_w, w_cast, *, tm):
    m, d = x2.shape
    steps = (m // tm) * SB_HEADS
    assert w_cast.shape[0] % steps == 0
    cast_spec = pl.BlockSpec((w_cast.shape[0] // steps, w_cast.shape[1]),
                             lambda i, j: (i * SB_HEADS + j, 0))
    hd = HEAD_DIM
    gw = GATE_BLOCK
    assert POOL_WIDTH == SB_HEADS * hd and D_MODEL == SB_HEADS * gw
    assert m % tm == 0 and SEQ % tm == 0 and tm % IN_PROJ_ROWS == 0
    assert tm % NORM_ROWS == 0 and tm // NORM_ROWS >= X_SLOTS
    w_spec = lambda width, off: pl.BlockSpec((d, width), lambda i, j: (0, off // width + j))
    out_spec = lambda width: pl.BlockSpec((tm, width), lambda i, j: (i, j))
    return pl.pallas_call(
        functools.partial(_in_proj_kernel, tm=tm),
        grid=(m // tm, SB_HEADS),
        in_specs=[
            pl.BlockSpec(memory_space=pl.ANY),
            pl.BlockSpec(mod.shape, lambda i, j: (0, 0, 0)),
            pl.BlockSpec((1, d), lambda i, j: (0, 0)),
            w_spec(hd, 0), w_spec(hd, Q_OFF), w_spec(hd, K_OFF), w_spec(hd, V_OFF),
            w_spec(gw, GA_OFF), w_spec(gw, GB_OFF),
            pl.BlockSpec((1, hd), lambda i, j: (0, 0)),
            pl.BlockSpec((1, hd), lambda i, j: (0, 0)),
            cast_spec,
        ],
        out_specs=[out_spec(hd), out_spec(hd), out_spec(hd), out_spec(hd),
                   out_spec(2 * gw), cast_spec],
        out_shape=[jax.ShapeDtypeStruct((m, POOL_WIDTH), BF16),
                   jax.ShapeDtypeStruct((m, SB_WIDTH), BF16),
                   jax.ShapeDtypeStruct((m, SB_WIDTH), BF16),
                   jax.ShapeDtypeStruct((m, SB_WIDTH), BF16),
                   jax.ShapeDtypeStruct((m, 2 * D_MODEL), BF16),
                   jax.ShapeDtypeStruct(w_cast.shape, BF16)],
        scratch_shapes=[pltpu.VMEM((tm, d), BF16), pltpu.VMEM((X_SLOTS, NORM_ROWS, d), F32),
                        pltpu.SemaphoreType.DMA((X_SLOTS,))],
        compiler_params=_params("arbitrary", "arbitrary"),
        name="in_proj",
    )(x2, mod, norm_w.reshape(1, d), w_in, w_in, w_in, w_in, w_in, w_in,
      q_norm_w.reshape(1, hd), k_norm_w.reshape(1, hd), w_cast)


FFN_OUT_CHUNK = 512


def _ffn_kernel(h_ref, x1_ref, mod_ref, w1_ref, w2_ref, o_ref, *, tm):
    i = pl.program_id(0)
    c = pl.program_id(1)
    xrows = x1_ref.shape[0]

    def step(first):
        rows = pl.ds(pl.multiple_of(c * xrows, xrows), xrows)
        if not first:
            o_ref[rows, :] += x1_ref[...]
        a = jnp.dot(h_ref[...], w1_ref[...], preferred_element_type=F32)
        r = jnp.maximum(a, 0.0)
        act = (r * r).astype(BF16)
        gate = _mod_row(mod_ref, GATE2, i * tm // SEQ)
        for n0 in range(0, o_ref.shape[1], FFN_OUT_CHUNK):
            cols = slice(n0, n0 + FFN_OUT_CHUNK)
            y = gate[:, cols] * jnp.dot(act, w2_ref[:, cols], preferred_element_type=F32)
            o_ref[:, cols] = y if first else o_ref[:, cols] + y
        if first:
            o_ref[rows, :] += x1_ref[...]

    pl.when(c == 0)(functools.partial(step, True))
    pl.when(c != 0)(functools.partial(step, False))


def _ffn(h2, x1, mod, w1, w2, *, tm, tf):
    m, d = h2.shape
    f = w1.shape[1]
    assert m % tm == 0 and SEQ % tm == 0 and f % tf == 0 and tm % (f // tf) == 0
    assert d % FFN_OUT_CHUNK == 0
    chunks = f // tf
    xrows = tm // chunks
    return pl.pallas_call(
        functools.partial(_ffn_kernel, tm=tm),
        grid=(m // tm, chunks),
        in_specs=[
            pl.BlockSpec((tm, d), lambda i, c: (i, 0)),
            pl.BlockSpec((xrows, d), lambda i, c: (i * chunks + c, 0)),
            pl.BlockSpec(mod.shape, lambda i, c: (0, 0, 0)),
            pl.BlockSpec((d, tf), lambda i, c: (0, c)),
            pl.BlockSpec((tf, d), lambda i, c: (c, 0)),
        ],
        out_specs=pl.BlockSpec((tm, d), lambda i, c: (i, 0)),
        out_shape=jax.ShapeDtypeStruct((m, d), F32),
        compiler_params=_params("arbitrary", "arbitrary"),
        name="ffn",
    )(h2, x1, mod, w1, w2)


POOL_ROWS = 256
POOL_HEAD = max(POOL_WINDOWS)
assert POOL_HEAD % (2 * SUBLANES) == 0 and all(w & (w - 1) == 0 for w in POOL_WINDOWS)


def _pool_head_exact(u_head, w):
    pos = lax.broadcasted_iota(jnp.int32, u_head.shape, 0)
    win_sum = u_head
    for k in range(1, w):
        win_sum = win_sum + jnp.where(pos >= k, pltpu.roll(u_head, k, 0), 0.0)
    count = jnp.minimum(pos + 1, w).astype(F32)
    return win_sum / count - u_head


def _pool_kernel(u_ref, w_ref, s_ref, o_ref):
    gd = POOL_GROUP_DIM
    tt = lax.broadcasted_iota(jnp.int32, (POOL_ROWS + POOL_HEAD, POOL_ROWS), 0)
    jj = lax.broadcasted_iota(jnp.int32, (POOL_ROWS + POOL_HEAD, POOL_ROWS), 1)
    for g, w in enumerate(POOL_WINDOWS):
        cols = slice(g * gd, (g + 1) * gd)
        inside = jnp.where(jnp.logical_and(jj <= tt, jj > tt - w), 1.0 / w, 0.0)
        inside = inside - jnp.where(jj == tt, 1.0, 0.0)
        spill = jnp.where(jj > tt - w, 1.0 / w, 0.0)
        band = jnp.where(tt < POOL_ROWS, inside, spill).astype(BF16)
        carry = None
        pooled = []
        for r0 in range(0, SEQ, POOL_ROWS):
            res = jnp.dot(band, u_ref[r0:r0 + POOL_ROWS, cols], preferred_element_type=F32)
            if r0 == 0:
                head = _pool_head_exact(u_ref[0:POOL_HEAD, cols].astype(F32), w)
            else:
                head = res[:POOL_HEAD] + carry
            pooled += [head.astype(BF16), res[POOL_HEAD:POOL_ROWS].astype(BF16)]
            carry = res[POOL_ROWS:]
        mixed = jnp.dot(jnp.concatenate(pooled, axis=0), w_ref[g].astype(BF16),
                        preferred_element_type=F32)
        o_ref[:, cols] = (mixed * s_ref[:, cols]).astype(o_ref.dtype)


def _pool(u, w_pool, pool_scale, batch):
    groups = len(POOL_WINDOWS)
    gd = POOL_GROUP_DIM
    assert SEQ % POOL_ROWS == 0
    return pl.pallas_call(
        _pool_kernel,
        grid=(batch,),
        in_specs=[
            pl.BlockSpec((SEQ, POOL_WIDTH), lambda b: (b, 0)),
            pl.BlockSpec((groups, gd, gd), lambda b: (0, 0, 0)),
            pl.BlockSpec((1, POOL_WIDTH), lambda b: (0, 0)),
        ],
        out_specs=pl.BlockSpec((SEQ, POOL_WIDTH), lambda b: (b, 0)),
        out_shape=jax.ShapeDtypeStruct((batch * SEQ, POOL_WIDTH), BF16),
        compiler_params=_params("arbitrary"),
        name="pool",
    )(u, w_pool, pool_scale.reshape(1, POOL_WIDTH))


EXP2_UNDERFLOW = -151.0


def _attn_kernel(q_ref, k_ref, v_ref, *refs, tb, hp, n_cast):
    w_refs, o_ref, wbf_refs = refs[:n_cast], refs[n_cast], refs[n_cast + 1:2 * n_cast + 1]
    carry_ref, acc_ref = refs[2 * n_cast + 1:]
    for w_ref, wbf_ref in zip(w_refs, wbf_refs):
        wbf_ref[...] = w_ref[...].astype(BF16)

    qi = pl.program_id(2)
    nseg = tb // LANES

    r = lax.broadcasted_iota(jnp.int32, (LANES, 2 * LANES), 0)
    c = lax.broadcasted_iota(jnp.int32, (LANES, 2 * LANES), 1)
    cum_op = jnp.where(jnp.logical_or(c >= LANES, r > c), 1.0, 0.0).astype(BF16)
    tri_r = lax.broadcasted_iota(jnp.int32, (LANES, LANES), 0)
    tri_c = lax.broadcasted_iota(jnp.int32, (LANES, LANES), 1)
    causal = tri_c < tri_r
    segs = [slice(sg * LANES, (sg + 1) * LANES) for sg in range(nseg)]

    def block(kb, diagonal):
        start = pl.multiple_of(kb * tb, tb)
        if diagonal:
            parts = [(segs[g], g + 1) for g in range(nseg)]
        else:
            parts = [(slice(0, tb), nseg)]
        work = [(p, slice(p * HEAD_DIM, (p + 1) * HEAD_DIM), rows, nk)
                for rows, nk in parts for p in range(hp)]
        zs = [lax.dot_general(q_ref[rows, cols], k_ref[pl.ds(start, nk * LANES), cols],
                              (((1,), (1,)), ((), ())), preferred_element_type=F32)
              for _, cols, rows, nk in work]
        log_betas, sums = [], []
        for z, (_, _, _, nk) in zip(zs, work):
            log_beta = jnp.minimum(z, 0.0) - jnp.log2(1.0 + jnp.exp2(-jnp.abs(z)))
            l = log_beta - z
            cs = []
            for sg in range(nk):
                l_seg = l[:, segs[sg]]
                if diagonal and sg == nk - 1:
                    l_seg = jnp.where(causal, l_seg, 0.0)
                cs.append(jnp.dot(l_seg.astype(BF16), cum_op, preferred_element_type=F32))
            log_betas.append(log_beta)
            sums.append(cs)
        top = None
        for (p, cols, rows, nk), log_beta, cs in zip(work, log_betas, sums):
            carry = carry_ref[p, rows, :]
            a_parts = [None] * nk
            for sg in range(nk - 1, -1, -1):
                a_seg = jnp.exp2(log_beta[:, segs[sg]] + (cs[sg][:, :LANES] + carry))
                if diagonal and sg == nk - 1:
                    a_seg = jnp.where(causal, a_seg, 0.0)
                a_parts[sg] = a_seg
                carry = carry + cs[sg][:, LANES:]
            a = jnp.concatenate(a_parts, axis=1) if nk > 1 else a_parts[0]
            acc_ref[p, rows, :] += jnp.dot(
                a.astype(BF16), v_ref[pl.ds(start, nk * LANES), cols],
                preferred_element_type=F32)
            carry_ref[p, rows, :] = carry
            top = jnp.max(carry) if top is None else jnp.maximum(top, jnp.max(carry))
        return top

    carry_ref[...] = jnp.zeros_like(carry_ref)
    acc_ref[...] = jnp.zeros_like(acc_ref)
    top = block(qi, True)

    def cond(state):
        kb, top = state
        return jnp.logical_and(kb >= 0, top > EXP2_UNDERFLOW)

    def body(state):
        kb, _ = state
        return kb - 1, block(kb, False)

    lax.while_loop(cond, body, (qi - 1, top))
    for p in range(hp):
        o_ref[:, p * HEAD_DIM:(p + 1) * HEAD_DIM] = acc_ref[p].astype(o_ref.dtype)


def _attn(q, k, v, weights, batch, *, tb, hp):
    nq = SEQ // tb
    groups = SB_HEADS // hp
    width = hp * HEAD_DIM
    steps = batch * groups * nq
    assert all(w.shape[0] % steps == 0 for w in weights)
    cast_specs = [pl.BlockSpec((w.shape[0] // steps, w.shape[1]),
                               lambda b, h, i: ((b * groups + h) * nq + i, 0))
                  for w in weights]
    outs = pl.pallas_call(
        functools.partial(_attn_kernel, tb=tb, hp=hp, n_cast=len(weights)),
        grid=(batch, groups, nq),
        in_specs=[
            pl.BlockSpec((tb, width), lambda b, h, i: (b * nq + i, h)),
            pl.BlockSpec((SEQ, width), lambda b, h, i: (b, h)),
            pl.BlockSpec((SEQ, width), lambda b, h, i: (b, h)),
        ] + cast_specs,
        out_specs=[pl.BlockSpec((tb, width), lambda b, h, i: (b * nq + i, h))] + cast_specs,
        out_shape=[jax.ShapeDtypeStruct((batch * SEQ, SB_WIDTH), BF16)]
        + [jax.ShapeDtypeStruct(w.shape, BF16) for w in weights],
        scratch_shapes=[pltpu.VMEM((hp, tb, LANES), F32),
                        pltpu.VMEM((hp, tb, HEAD_DIM), F32)],
        compiler_params=_params("arbitrary", "arbitrary", "arbitrary"),
        name="attn",
    )(q, k, v, *weights)
    return outs[0], outs[1:]


MIX_CHUNK = 256


def _mix_kernel(pa_ref, at_ref, g_ref, x_ref, mod_ref, nw_ref, wa_ref, wb_ref, wo_ref,
                cast_ref, x1_ref, h2_ref, cast_out_ref, merged_scr, x1_scr):
    i = pl.program_id(0)
    tiles = pl.num_programs(0) - 1
    tm, d = x_ref.shape

    def step(norm_previous, matmuls):
        if norm_previous:
            b = (i - 1) * tm // SEQ
            gain = nw_ref[...] * (1.0 + _mod_row(mod_ref, SCALE2, b))
            h2 = _norm_modulate(x1_scr[...], gain, _mod_row(mod_ref, SHIFT2, b))
            if not matmuls:
                h2_ref[...] = h2
                return
            merged_scr[...] = h2
        cast_out_ref[...] = cast_ref[...].astype(BF16)
        pa = pa_ref[...]
        at = at_ref[...]
        for n0 in range(0, d, MIX_CHUNK):
            cols = slice(n0, n0 + MIX_CHUNK)
            if norm_previous:
                h2_ref[:, cols] = merged_scr[:, cols]
            ya = jnp.dot(pa, wa_ref[:, cols], preferred_element_type=F32)
            yb = jnp.dot(at, wb_ref[:, cols], preferred_element_type=F32)
            sa = g_ref[:, 2 * n0:2 * n0 + MIX_CHUNK].astype(F32)
            sb = g_ref[:, 2 * n0 + MIX_CHUNK:2 * n0 + 2 * MIX_CHUNK].astype(F32)
            merged = sa * ya + sb * yb
            merged_scr[:, cols] = merged.astype(BF16)
        merged = merged_scr[...]
        gate = _mod_row(mod_ref, GATE1, i * tm // SEQ)
        for n0 in range(0, d, MIX_CHUNK):
            cols = slice(n0, n0 + MIX_CHUNK)
            o = jnp.dot(merged, wo_ref[:, cols], preferred_element_type=F32)
            x1 = x_ref[:, cols] + gate[:, cols] * o
            x1_ref[:, cols] = x1
            x1_scr[:, cols] = x1

    pl.when(i == 0)(functools.partial(step, False, True))
    pl.when(jnp.logical_and(i > 0, i < tiles))(functools.partial(step, True, True))
    pl.when(i == tiles)(functools.partial(step, True, False))


def _mix(pa, at, gates, x2, mod, norm_w, wa_bf, wb_bf, wo_bf, w_cast, *, tm):
    m, d = x2.shape
    kdim = pa.shape[1]
    tiles = m // tm
    assert m % tm == 0 and SEQ % tm == 0 and w_cast.shape[0] % tiles == 0
    assert gates.shape == (m, 2 * d) and MIX_CHUNK == GATE_BLOCK

    def tile(i):
        return jnp.minimum(i, tiles - 1)

    cast_spec = pl.BlockSpec((w_cast.shape[0] // tiles, w_cast.shape[1]),
                             lambda i: (tile(i), 0))
    resident = pl.Buffered(1)
    return pl.pallas_call(
        _mix_kernel,
        grid=(tiles + 1,),
        in_specs=[
            pl.BlockSpec((tm, kdim), lambda i: (tile(i), 0)),
            pl.BlockSpec((tm, kdim), lambda i: (tile(i), 0)),
            pl.BlockSpec((tm, 2 * d), lambda i: (tile(i), 0)),
            pl.BlockSpec((tm, d), lambda i: (tile(i), 0)),
            pl.BlockSpec(mod.shape, lambda i: (0, 0, 0)),
            pl.BlockSpec((1, d), lambda i: (0, 0)),
            pl.BlockSpec((kdim, d), lambda i: (0, 0), pipeline_mode=resident),
            pl.BlockSpec((kdim, d), lambda i: (0, 0), pipeline_mode=resident),
            pl.BlockSpec((d, d), lambda i: (0, 0), pipeline_mode=resident),
            cast_spec,
        ],
        out_specs=[pl.BlockSpec((tm, d), lambda i: (tile(i), 0)),
                   pl.BlockSpec((tm, d), lambda i: (jnp.maximum(i - 1, 0), 0)), cast_spec],
        out_shape=[jax.ShapeDtypeStruct((m, d), F32),
                   jax.ShapeDtypeStruct((m, d), BF16),
                   jax.ShapeDtypeStruct(w_cast.shape, BF16)],
        scratch_shapes=[pltpu.VMEM((tm, d), BF16), pltpu.VMEM((tm, d), F32)],
        compiler_params=_params("arbitrary"),
        name="mix",
    )(pa, at, gates, x2, mod, norm_w.reshape(1, d), wa_bf, wb_bf, wo_bf, w_cast)


def kernel(x, c, w_ada, b_ada, norm1_w, w_in, q_norm_w, k_norm_w, w_pool, pool_scale,
           w_a_up, w_b_up, w_o, norm2_w, w_ff1, w_ff2):
    batch, seq, d = x.shape
    assert (seq, d) == (SEQ, D_MODEL) and w_ada.shape[0] == 1
    x2 = x.reshape(batch * seq, d)

    mod = _ada(c, w_ada[0], b_ada[0])
    u, q, k, v, gates, w1_bf = _in_proj(x2, mod, norm1_w[0], w_in[0], q_norm_w[0],
                                        k_norm_w[0], w_ff1[0], tm=2048)
    pa = _pool(u, w_pool[0], pool_scale[0], batch)
    at, (wa_bf, wb_bf, wo_bf) = _attn(
        q, k, v, (w_a_up[0], w_b_up[0], w_o[0]), batch,
        tb=256, hp=8)
    x1, h2, w2_bf = _mix(pa, at, gates, x2, mod, norm2_w[0], wa_bf, wb_bf, wo_bf, w_ff2[0],
                         tm=256)
    out = _ffn(h2, x1, mod, w1_bf, w2_bf, tm=1024, tf=1024)
    return out.reshape(batch, seq, d)
```
